```python
import math
import jax
import jax.numpy as jnp
from jax import lax
import numpy as np

D_MODEL = 1024
BATCH = 8
SEQ = 16384
DEPTH = 2

N_HEADS = 16
HEAD_DIM = D_MODEL // N_HEADS
D_FF = 4 * D_MODEL
ROPE_THETA = 500000.0
ROPE_DIM = HEAD_DIM // 4
BLOCK = 128
DILATED_BRANCHES = ((128, 1), (512, 4), (2048, 16))
N_MIXERS = 2
DEEPNORM_ALPHA = (2 * DEPTH) ** 0.25
DEEPNORM_BETA = (8 * DEPTH) ** -0.25
LN_EPS = 1e-5

kernel_name = "hybrid_stickbreak_dilated_deepnorm"


def layer_norm(x, g, b):
    xf = x.astype(jnp.float32)
    mu = jnp.mean(xf, axis=-1, keepdims=True)
    var = jnp.mean(jnp.square(xf - mu), axis=-1, keepdims=True)
    y = (xf - mu) * lax.rsqrt(var + LN_EPS)
    return (y * g.astype(jnp.float32) + b.astype(jnp.float32)).astype(x.dtype)


def rotary_partial(x, positions):
    half = ROPE_DIM // 2
    inv_freq = ROPE_THETA ** (-jnp.arange(half, dtype=jnp.float32) / half)
    ang = positions.astype(jnp.float32)[:, None] * inv_freq[None, :]
    cos = jnp.cos(ang)[None, :, None, :]
    sin = jnp.sin(ang)[None, :, None, :]
    xf = x.astype(jnp.float32)
    x1 = xf[..., :half]
    x2 = xf[..., half:ROPE_DIM]
    return jnp.concatenate([x1 * cos - x2 * sin, x2 * cos + x1 * sin, xf[..., ROPE_DIM:]], axis=-1)


def stick_breaking_attention(q, k, v):
    B, H, S, hd = q.shape
    nq = S // BLOCK
    scale = 1.0 / math.sqrt(hd)
    ar = jnp.arange(BLOCK)
    tri_incl = (ar[:, None] >= ar[None, :]).astype(jnp.float32)
    outs = []
    for blk in range(nq):
        n_k = blk + 1
        end = n_k * BLOCK
        q_blk = q[:, :, blk * BLOCK:end]
        k_blk = k[:, :, :end]
        v_blk = v[:, :, :end]
        z = jnp.einsum('bhqd,bhkd->bhqk', q_blk, k_blk) * scale
        causal = jnp.arange(end)[None, :] < (blk * BLOCK + ar)[:, None]
        log_fail = jnp.where(causal, -jax.nn.softplus(z), 0.0)
        lf = log_fail.reshape(B, H, BLOCK, n_k, BLOCK)
        r_in = jnp.einsum('bhqcj,js->bhqcs', lf, tri_incl)
        totals = jnp.sum(lf, axis=-1)
        ac = jnp.arange(n_k)
        later = (ac[:, None] > ac[None, :]).astype(jnp.float32)
        r_cross = jnp.einsum('bhqe,ec->bhqc', totals, later)
        log_w = z.reshape(B, H, BLOCK, n_k, BLOCK) + r_in + r_cross[..., None]
        w = jnp.where(causal, jnp.exp(log_w).reshape(B, H, BLOCK, end), 0.0)
        outs.append(jnp.einsum('bhqk,bhkd->bhqd', w, v_blk))
    out = jnp.concatenate(outs, axis=2)
    return out.transpose(0, 2, 1, 3)


def dilated_branch(q, k, v, window, dilation):
    B, H, S, hd = q.shape
    w_sub = window // dilation
    span = dilation * BLOCK
    s_pad = -(-S // span) * span
    L = s_pad // dilation
    nb = L // BLOCK

    def to_sub(t):
        t = jnp.pad(t, ((0, 0), (0, 0), (0, s_pad - S), (0, 0)))
        t = t.reshape(B, H, L, dilation, hd).transpose(0, 1, 3, 2, 4)
        return t.reshape(B, H, dilation, nb, BLOCK, hd)

    def with_prev(t):
        prev = jnp.pad(t, ((0, 0), (0, 0), (0, 0), (1, 0), (0, 0), (0, 0)))[:, :, :, :-1]
        return jnp.concatenate([prev, t], axis=4)

    qs = to_sub(q)
    kk = with_prev(to_sub(k))
    vv = with_prev(to_sub(v))
    s = jnp.einsum('bhrnqd,bhrnkd->bhrnqk', qs, kk)
    qi = jnp.arange(BLOCK)[:, None]
    kj = jnp.arange(2 * BLOCK)[None, :]
    dist = BLOCK + qi - kj
    blk = jnp.arange(nb)[:, None, None]
    valid = (dist >= 0) & (dist <= w_sub) & ((blk > 0) | (kj >= BLOCK))
    s = jnp.where(valid, s, -jnp.inf)
    m = jnp.max(s, axis=-1, keepdims=True)
    p = jnp.exp(s - m)
    den = jnp.sum(p, axis=-1, keepdims=True)
    num = jnp.einsum('bhrnqk,bhrnkd->bhrnqd', p, vv)

    def from_sub(t):
        c = t.shape[-1]
        t = t.reshape(B, H, dilation, L, c).transpose(0, 1, 3, 2, 4).reshape(B, H, s_pad, c)
        return t[:, :, :S]

    return from_sub(m), from_sub(den), from_sub(num)


def dilated_attention(q, k, v):
    hd = q.shape[-1]
    qs = q * (1.0 / math.sqrt(hd))
    parts = [dilated_branch(qs, k, v, w, d) for (w, d) in DILATED_BRANCHES]
    m_all = jnp.max(jnp.concatenate([p[0] for p in parts], axis=-1), axis=-1, keepdims=True)
    num = sum(jnp.exp(p[0] - m_all) * p[2] for p in parts)
    den = sum(jnp.exp(p[0] - m_all) * p[1] for p in parts)
    out = num / den
    return out.transpose(0, 2, 1, 3)


def _fwd_setup_inputs(seed: int = 0) -> dict:
    key = jax.random.key(seed)
    keys = jax.random.split(key, 1 + 9 * DEPTH)
    inputs = {"x": jax.random.normal(keys[0], (BATCH, SEQ, D_MODEL), jnp.float32)}
    for i in range(DEPTH):
        k = keys[1 + 9 * i: 10 + 9 * i]
        w_qk = jax.random.normal(k[0], (D_MODEL, 2 * D_MODEL), jnp.float32) * D_MODEL ** -0.5
        w_v = jax.random.normal(k[1], (D_MODEL, D_MODEL), jnp.float32) * (D_MODEL ** -0.5 * DEEPNORM_BETA)
        inputs[f"w_qkv_{i}"] = jnp.concatenate([w_qk, w_v], axis=1)
        inputs[f"w_o_{i}"] = jax.random.normal(k[2], (D_MODEL, D_MODEL), jnp.float32) * (D_MODEL ** -0.5 * DEEPNORM_BETA)
        inputs[f"ln1_g_{i}"] = 1.0 + 0.02 * jax.random.normal(k[3], (D_MODEL,), jnp.float32)
        inputs[f"ln1_b_{i}"] = 0.02 * jax.random.normal(k[4], (D_MODEL,), jnp.float32)
        inputs[f"w_ff1_{i}"] = jax.random.normal(k[5], (D_MODEL, D_FF), jnp.float32) * (D_MODEL ** -0.5 * DEEPNORM_BETA)
        inputs[f"w_ff2_{i}"] = jax.random.normal(k[6], (D_FF, D_MODEL), jnp.float32) * (D_FF ** -0.5 * DEEPNORM_BETA)
        inputs[f"ln2_g_{i}"] = 1.0 + 0.02 * jax.random.normal(k[7], (D_MODEL,), jnp.float32)
        inputs[f"ln2_b_{i}"] = 0.02 * jax.random.normal(k[8], (D_MODEL,), jnp.float32)
    return inputs


def _fwd_reference(x,
              w_qkv_0, w_o_0, ln1_g_0, ln1_b_0, w_ff1_0, w_ff2_0, ln2_g_0, ln2_b_0,
              w_qkv_1, w_o_1, ln1_g_1, ln1_b_1, w_ff1_1, w_ff2_1, ln2_g_1, ln2_b_1):
    layers = (
        (w_qkv_0, w_o_0, ln1_g_0, ln1_b_0, w_ff1_0, w_ff2_0, ln2_g_0, ln2_b_0),
        (w_qkv_1, w_o_1, ln1_g_1, ln1_b_1, w_ff1_1, w_ff2_1, ln2_g_1, ln2_b_1),
    )
    B, S, D = x.shape
    positions = jnp.arange(S, dtype=jnp.int32)
    for i in range(DEPTH):
        w_qkv, w_o, g1, b1, w1, w2, g2, b2 = layers[i]
        qkv = jnp.einsum('bsd,de->bse', x, w_qkv).reshape(B, S, 3, N_HEADS, HEAD_DIM)
        q, k, v = qkv[:, :, 0], qkv[:, :, 1], qkv[:, :, 2]
        if i % N_MIXERS == 0:
            qh = q.astype(jnp.float32).transpose(0, 2, 1, 3)
            kh = k.astype(jnp.float32).transpose(0, 2, 1, 3)
            vh = v.astype(jnp.float32).transpose(0, 2, 1, 3)
            o = stick_breaking_attention(qh, kh, vh)
        else:
            qh = rotary_partial(q, positions).transpose(0, 2, 1, 3)
            kh = rotary_partial(k, positions).transpose(0, 2, 1, 3)
            vh = v.astype(jnp.float32).transpose(0, 2, 1, 3)
            o = dilated_attention(qh, kh, vh)
        mix = jnp.einsum('bse,ed->bsd', o.reshape(B, S, D).astype(x.dtype), w_o)
        x = layer_norm(DEEPNORM_ALPHA * x + mix, g1, b1)
        h = jnp.square(jax.nn.relu(jnp.einsum('bsd,df->bsf', x, w1)))
        x = layer_norm(DEEPNORM_ALPHA * x + jnp.einsum('bsf,fd->bsd', h, w2), g2, b2)
    return x


import jax as _jax
import jax.numpy as _jnp

TWIN_FORMAT = 'train_step'
FWD_PARAMS = ['x', 'w_qkv_0', 'w_o_0', 'ln1_g_0', 'ln1_b_0', 'w_ff1_0', 'w_ff2_0', 'ln2_g_0', 'ln2_b_0', 'w_qkv_1', 'w_o_1', 'ln1_g_1', 'ln1_b_1', 'w_ff1_1', 'w_ff2_1', 'ln2_g_1', 'ln2_b_1']
TWIN_WEIGHTS = ['w_qkv_0', 'w_o_0', 'ln1_g_0', 'ln1_b_0', 'w_ff1_0', 'w_ff2_0', 'ln2_g_0', 'ln2_b_0', 'w_qkv_1', 'w_o_1', 'ln1_g_1', 'ln1_b_1', 'w_ff1_1', 'w_ff2_1', 'ln2_g_1', 'ln2_b_1']
TWIN_DIFF_INPUT = 'x'
TWIN_INPUTS = ['x', 'w_qkv_0', 'w_o_0', 'ln1_g_0', 'ln1_b_0', 'w_ff1_0', 'w_ff2_0', 'ln2_g_0', 'ln2_b_0', 'w_qkv_1', 'w_o_1', 'ln1_g_1', 'ln1_b_1', 'w_ff1_1', 'w_ff2_1', 'ln2_g_1', 'ln2_b_1', 'loss_target', 'm_w_qkv_0', 'm_w_o_0', 'm_ln1_g_0', 'm_ln1_b_0', 'm_w_ff1_0', 'm_w_ff2_0', 'm_ln2_g_0', 'm_ln2_b_0', 'm_w_qkv_1', 'm_w_o_1', 'm_ln1_g_1', 'm_ln1_b_1', 'm_w_ff1_1', 'm_w_ff2_1', 'm_ln2_g_1', 'm_ln2_b_1', 'v_w_qkv_0', 'v_w_o_0', 'v_ln1_g_0', 'v_ln1_b_0', 'v_w_ff1_0', 'v_w_ff2_0', 'v_ln2_g_0', 'v_ln2_b_0', 'v_w_qkv_1', 'v_w_o_1', 'v_ln1_g_1', 'v_ln1_b_1', 'v_w_ff1_1', 'v_w_ff2_1', 'v_ln2_g_1', 'v_ln2_b_1']
TWIN_OUTPUTS = ['loss', 'grad_x', 'grad_w_qkv_0', 'grad_w_o_0', 'grad_ln1_g_0', 'grad_ln1_b_0', 'grad_w_ff1_0', 'grad_w_ff2_0', 'grad_ln2_g_0', 'grad_ln2_b_0', 'grad_w_qkv_1', 'grad_w_o_1', 'grad_ln1_g_1', 'grad_ln1_b_1', 'grad_w_ff1_1', 'grad_w_ff2_1', 'grad_ln2_g_1', 'grad_ln2_b_1', 'delta_w_qkv_0', 'delta_w_o_0', 'delta_ln1_g_0', 'delta_ln1_b_0', 'delta_w_ff1_0', 'delta_w_ff2_0', 'delta_ln2_g_0', 'delta_ln2_b_0', 'delta_w_qkv_1', 'delta_w_o_1', 'delta_ln1_g_1', 'delta_ln1_b_1', 'delta_w_ff1_1', 'delta_w_ff2_1', 'delta_ln2_g_1', 'delta_ln2_b_1', 'new_m_w_qkv_0', 'new_m_w_o_0', 'new_m_ln1_g_0', 'new_m_ln1_b_0', 'new_m_w_ff1_0', 'new_m_w_ff2_0', 'new_m_ln2_g_0', 'new_m_ln2_b_0', 'new_m_w_qkv_1', 'new_m_w_o_1', 'new_m_ln1_g_1', 'new_m_ln1_b_1', 'new_m_w_ff1_1', 'new_m_w_ff2_1', 'new_m_ln2_g_1', 'new_m_ln2_b_1', 'new_v_w_qkv_0', 'new_v_w_o_0', 'new_v_ln1_g_0', 'new_v_ln1_b_0', 'new_v_w_ff1_0', 'new_v_w_ff2_0', 'new_v_ln2_g_0', 'new_v_ln2_b_0', 'new_v_w_qkv_1', 'new_v_w_o_1', 'new_v_ln1_g_1', 'new_v_ln1_b_1', 'new_v_w_ff1_1', 'new_v_w_ff2_1', 'new_v_ln2_g_1', 'new_v_ln2_b_1']
TWIN_LEAF_KINDS = {'loss': 'loss', 'grad_x': 'grad_x', 'grad_w_qkv_0': 'grad_w', 'grad_w_o_0': 'grad_w', 'grad_ln1_g_0': 'grad_w', 'grad_ln1_b_0': 'grad_w', 'grad_w_ff1_0': 'grad_w', 'grad_w_ff2_0': 'grad_w', 'grad_ln2_g_0': 'grad_w', 'grad_ln2_b_0': 'grad_w', 'grad_w_qkv_1': 'grad_w', 'grad_w_o_1': 'grad_w', 'grad_ln1_g_1': 'grad_w', 'grad_ln1_b_1': 'grad_w', 'grad_w_ff1_1': 'grad_w', 'grad_w_ff2_1': 'grad_w', 'grad_ln2_g_1': 'grad_w', 'grad_ln2_b_1': 'grad_w', 'delta_w_qkv_0': 'delta_w', 'delta_w_o_0': 'delta_w', 'delta_ln1_g_0': 'delta_w', 'delta_ln1_b_0': 'delta_w', 'delta_w_ff1_0': 'delta_w', 'delta_w_ff2_0': 'delta_w', 'delta_ln2_g_0': 'delta_w', 'delta_ln2_b_0': 'delta_w', 'delta_w_qkv_1': 'delta_w', 'delta_w_o_1': 'delta_w', 'delta_ln1_g_1': 'delta_w', 'delta_ln1_b_1': 'delta_w', 'delta_w_ff1_1': 'delta_w', 'delta_w_ff2_1': 'delta_w', 'delta_ln2_g_1': 'delta_w', 'delta_ln2_b_1': 'delta_w', 'new_m_w_qkv_0': 'new_m', 'new_m_w_o_0': 'new_m', 'new_m_ln1_g_0': 'new_m', 'new_m_ln1_b_0': 'new_m', 'new_m_w_ff1_0': 'new_m', 'new_m_w_ff2_0': 'new_m', 'new_m_ln2_g_0': 'new_m', 'new_m_ln2_b_0': 'new_m', 'new_m_w_qkv_1': 'new_m', 'new_m_w_o_1': 'new_m', 'new_m_ln1_g_1': 'new_m', 'new_m_ln1_b_1': 'new_m', 'new_m_w_ff1_1': 'new_m', 'new_m_w_ff2_1': 'new_m', 'new_m_ln2_g_1': 'new_m', 'new_m_ln2_b_1': 'new_m', 'new_v_w_qkv_0': 'new_v', 'new_v_w_o_0': 'new_v', 'new_v_ln1_g_0': 'new_v', 'new_v_ln1_b_0': 'new_v', 'new_v_w_ff1_0': 'new_v', 'new_v_w_ff2_0': 'new_v', 'new_v_ln2_g_0': 'new_v', 'new_v_ln2_b_0': 'new_v', 'new_v_w_qkv_1': 'new_v', 'new_v_w_o_1': 'new_v', 'new_v_ln1_g_1': 'new_v', 'new_v_ln1_b_1': 'new_v', 'new_v_w_ff1_1': 'new_v', 'new_v_w_ff2_1': 'new_v', 'new_v_ln2_g_1': 'new_v', 'new_v_ln2_b_1': 'new_v'}


def _forward(args):
    return _fwd_reference(*[args[k] for k in FWD_PARAMS])


def _output_shape():
    def fwd():
        inp = _fwd_setup_inputs(0)
        return _fwd_reference(*[inp[k] for k in FWD_PARAMS])
    out = _jax.eval_shape(fwd)
    return out.shape, out.dtype

N_MICROBATCH = 1
ADAM_LR = 0.001
ADAM_B1 = 0.9
ADAM_B2 = 0.999
ADAM_EPS = 1e-08
ADAM_WD = 0.01
ADAM_STEP = 10
PER_EXAMPLE_BATCH_AXIS = {'x': 0, 'loss_target': 0}
SHARED_INPUTS = []
_WEIGHT_DTYPES = {'w_qkv_0': _jnp.float32, 'w_o_0': _jnp.float32, 'ln1_g_0': _jnp.float32, 'ln1_b_0': _jnp.float32, 'w_ff1_0': _jnp.float32, 'w_ff2_0': _jnp.float32, 'ln2_g_0': _jnp.float32, 'ln2_b_0': _jnp.float32, 'w_qkv_1': _jnp.float32, 'w_o_1': _jnp.float32, 'ln1_g_1': _jnp.float32, 'ln1_b_1': _jnp.float32, 'w_ff1_1': _jnp.float32, 'w_ff2_1': _jnp.float32, 'ln2_g_1': _jnp.float32, 'ln2_b_1': _jnp.float32}
MOMENT_SCALE = {'w_qkv_0': 4.881466e-02, 'w_o_0': 8.059950e-02, 'ln1_g_0': 4.232909e+00, 'ln1_b_0': 1.881110e+00, 'w_ff1_0': 4.426195e-02, 'w_ff2_0': 1.340055e-01, 'ln2_g_0': 4.303001e+00, 'ln2_b_0': 1.874823e+00, 'w_qkv_1': 2.015346e-02, 'w_o_1': 3.084944e-02, 'ln1_g_1': 4.304326e+00, 'ln1_b_1': 1.864182e+00, 'w_ff1_1': 4.463618e-02, 'w_ff2_1': 1.339882e-01, 'ln2_g_1': 1.282946e+02, 'ln2_b_1': 8.802159e+00}


def _to_microbatches(a, axis):
    t = _jnp.moveaxis(a, axis, 0)
    t = t.reshape((N_MICROBATCH, t.shape[0] // N_MICROBATCH) + t.shape[1:])
    return _jnp.moveaxis(t, 1, axis + 1)


def setup_inputs(seed: int = 0) -> dict:
    inp = _fwd_setup_inputs(seed)
    key = _jax.random.fold_in(_jax.random.key(seed), 7919)
    shape, _ = _output_shape()
    out = dict(inp)
    out["loss_target"] = _jax.random.normal(_jax.random.fold_in(key, 0), shape, _jnp.float32)
    for i, name in enumerate(TWIN_WEIGHTS):
        w = inp[name].astype(_jnp.float32)
        if MOMENT_SCALE is None:
            s = _jnp.sqrt(_jnp.mean(_jnp.square(w)) + 1e-30)
        else:
            s = MOMENT_SCALE[name]
        km, kv = _jax.random.split(_jax.random.fold_in(key, i + 1))
        out[name] = w
        out["m_" + name] = s * _jax.random.normal(km, w.shape, _jnp.float32)
        out["v_" + name] = (s * s) * _jax.random.uniform(kv, w.shape, _jnp.float32, 0.5, 1.5)
    if N_MICROBATCH > 1:
        for name, axis in PER_EXAMPLE_BATCH_AXIS.items():
            out[name] = _to_microbatches(out[name], axis)
    return {'x': out['x'], 'w_qkv_0': out['w_qkv_0'], 'w_o_0': out['w_o_0'], 'ln1_g_0': out['ln1_g_0'], 'ln1_b_0': out['ln1_b_0'], 'w_ff1_0': out['w_ff1_0'], 'w_ff2_0': out['w_ff2_0'], 'ln2_g_0': out['ln2_g_0'], 'ln2_b_0': out['ln2_b_0'], 'w_qkv_1': out['w_qkv_1'], 'w_o_1': out['w_o_1'], 'ln1_g_1': out['ln1_g_1'], 'ln1_b_1': out['ln1_b_1'], 'w_ff1_1': out['w_ff1_1'], 'w_ff2_1': out['w_ff2_1'], 'ln2_g_1': out['ln2_g_1'], 'ln2_b_1': out['ln2_b_1'], 'loss_target': out['loss_target'], 'm_w_qkv_0': out['m_w_qkv_0'], 'm_w_o_0': out['m_w_o_0'], 'm_ln1_g_0': out['m_ln1_g_0'], 'm_ln1_b_0': out['m_ln1_b_0'], 'm_w_ff1_0': out['m_w_ff1_0'], 'm_w_ff2_0': out['m_w_ff2_0'], 'm_ln2_g_0': out['m_ln2_g_0'], 'm_ln2_b_0': out['m_ln2_b_0'], 'm_w_qkv_1': out['m_w_qkv_1'], 'm_w_o_1': out['m_w_o_1'], 'm_ln1_g_1': out['m_ln1_g_1'], 'm_ln1_b_1': out['m_ln1_b_1'], 'm_w_ff1_1': out['m_w_ff1_1'], 'm_w_ff2_1': out['m_w_ff2_1'], 'm_ln2_g_1': out['m_ln2_g_1'], 'm_ln2_b_1': out['m_ln2_b_1'], 'v_w_qkv_0': out['v_w_qkv_0'], 'v_w_o_0': out['v_w_o_0'], 'v_ln1_g_0': out['v_ln1_g_0'], 'v_ln1_b_0': out['v_ln1_b_0'], 'v_w_ff1_0': out['v_w_ff1_0'], 'v_w_ff2_0': out['v_w_ff2_0'], 'v_ln2_g_0': out['v_ln2_g_0'], 'v_ln2_b_0': out['v_ln2_b_0'], 'v_w_qkv_1': out['v_w_qkv_1'], 'v_w_o_1': out['v_w_o_1'], 'v_ln1_g_1': out['v_ln1_g_1'], 'v_ln1_b_1': out['v_ln1_b_1'], 'v_w_ff1_1': out['v_w_ff1_1'], 'v_w_ff2_1': out['v_w_ff2_1'], 'v_ln2_g_1': out['v_ln2_g_1'], 'v_ln2_b_1': out['v_ln2_b_1']}


def _loss(weights, diff, rest, loss_target):
    with _jax.named_scope("forward"):
        args = {**rest, TWIN_DIFF_INPUT: diff, **{k: w.astype(_WEIGHT_DTYPES[k]) for k, w in weights.items()}}
        y = _forward(args)
    with _jax.named_scope("loss_head"):
        err = _jnp.square(y.astype(_jnp.float32) - loss_target)
        return 0.5 * _jnp.sum(_jnp.mean(err, axis=-1)) if err.ndim else 0.5 * err


def _adamw(w, g, m, v):
    m = ADAM_B1 * m + (1.0 - ADAM_B1) * g
    v = ADAM_B2 * v + (1.0 - ADAM_B2) * _jnp.square(g)
    m_hat = m / (1.0 - ADAM_B1 ** ADAM_STEP)
    v_hat = v / (1.0 - ADAM_B2 ** ADAM_STEP)
    delta = -ADAM_LR * (m_hat / (_jnp.sqrt(v_hat) + ADAM_EPS) + ADAM_WD * w)
    return delta, m, v


def reference(x, w_qkv_0, w_o_0, ln1_g_0, ln1_b_0, w_ff1_0, w_ff2_0, ln2_g_0, ln2_b_0, w_qkv_1, w_o_1, ln1_g_1, ln1_b_1, w_ff1_1, w_ff2_1, ln2_g_1, ln2_b_1, loss_target, m_w_qkv_0, m_w_o_0, m_ln1_g_0, m_ln1_b_0, m_w_ff1_0, m_w_ff2_0, m_ln2_g_0, m_ln2_b_0, m_w_qkv_1, m_w_o_1, m_ln1_g_1, m_ln1_b_1, m_w_ff1_1, m_w_ff2_1, m_ln2_g_1, m_ln2_b_1, v_w_qkv_0, v_w_o_0, v_ln1_g_0, v_ln1_b_0, v_w_ff1_0, v_w_ff2_0, v_ln2_g_0, v_ln2_b_0, v_w_qkv_1, v_w_o_1, v_ln1_g_1, v_ln1_b_1, v_w_ff1_1, v_w_ff2_1, v_ln2_g_1, v_ln2_b_1):
    given = dict(x=x, w_qkv_0=w_qkv_0, w_o_0=w_o_0, ln1_g_0=ln1_g_0, ln1_b_0=ln1_b_0, w_ff1_0=w_ff1_0, w_ff2_0=w_ff2_0, ln2_g_0=ln2_g_0, ln2_b_0=ln2_b_0, w_qkv_1=w_qkv_1, w_o_1=w_o_1, ln1_g_1=ln1_g_1, ln1_b_1=ln1_b_1, w_ff1_1=w_ff1_1, w_ff2_1=w_ff2_1, ln2_g_1=ln2_g_1, ln2_b_1=ln2_b_1, loss_target=loss_target, m_w_qkv_0=m_w_qkv_0, m_w_o_0=m_w_o_0, m_ln1_g_0=m_ln1_g_0, m_ln1_b_0=m_ln1_b_0, m_w_ff1_0=m_w_ff1_0, m_w_ff2_0=m_w_ff2_0, m_ln2_g_0=m_ln2_g_0, m_ln2_b_0=m_ln2_b_0, m_w_qkv_1=m_w_qkv_1, m_w_o_1=m_w_o_1, m_ln1_g_1=m_ln1_g_1, m_ln1_b_1=m_ln1_b_1, m_w_ff1_1=m_w_ff1_1, m_w_ff2_1=m_w_ff2_1, m_ln2_g_1=m_ln2_g_1, m_ln2_b_1=m_ln2_b_1, v_w_qkv_0=v_w_qkv_0, v_w_o_0=v_w_o_0, v_ln1_g_0=v_ln1_g_0, v_ln1_b_0=v_ln1_b_0, v_w_ff1_0=v_w_ff1_0, v_w_ff2_0=v_w_ff2_0, v_ln2_g_0=v_ln2_g_0, v_ln2_b_0=v_ln2_b_0, v_w_qkv_1=v_w_qkv_1, v_w_o_1=v_w_o_1, v_ln1_g_1=v_ln1_g_1, v_ln1_b_1=v_ln1_b_1, v_w_ff1_1=v_w_ff1_1, v_w_ff2_1=v_w_ff2_1, v_ln2_g_1=v_ln2_g_1, v_ln2_b_1=v_ln2_b_1)
    weights = {n: given[n] for n in TWIN_WEIGHTS}
    shared = {n: given[n] for n in SHARED_INPUTS}
    per_example = {n: given[n] for n in ['x']}
    grad_fn = _jax.value_and_grad(_loss, argnums=(0, 1))

    def one_microbatch(ex, loss_target):
        ex = dict(ex)
        diff = ex.pop(TWIN_DIFF_INPUT)
        return grad_fn(weights, diff, {**shared, **ex}, loss_target)

    if N_MICROBATCH == 1:
        loss, (grad_w, grad_x) = one_microbatch(per_example, given["loss_target"])
    else:
        def body(carry, xs):
            loss_sum, grad_sum = carry
            l_k, (gw_k, gx_k) = one_microbatch(xs[0], xs[1])
            with _jax.named_scope("update"):
                return (loss_sum + l_k, _jax.tree.map(_jnp.add, grad_sum, gw_k)), gx_k

        init = (_jnp.zeros((), _jnp.float32), _jax.tree.map(_jnp.zeros_like, weights))
        (loss, grad_w), grad_x = _jax.lax.scan(body, init, (per_example, given["loss_target"]))
    with _jax.named_scope("update"):
        delta_w, new_m, new_v = {}, {}, {}
        for n in TWIN_WEIGHTS:
            delta_w[n], new_m[n], new_v[n] = _adamw(weights[n], grad_w[n], given["m_" + n], given["v_" + n])
    return (loss, grad_x, *[grad_w[n] for n in TWIN_WEIGHTS], *[delta_w[n] for n in TWIN_WEIGHTS],
            *[new_m[n] for n in TWIN_WEIGHTS], *[new_v[n] for n in TWIN_WEIGHTS])
```

```python
import functools

import jax
import jax.numpy as jnp
from jax import lax
from jax.experimental import pallas as pl
from jax.experimental.pallas import tpu as pltpu

F32 = jnp.float32
MXU = jnp.bfloat16

HEAD_DIM = 64
LANES = 128
CHUNK = 128
ROPE_THETA = 500000.0
ROPE_HALF = 8
DILATIONS = (1, 4, 16)
DEPTH = 2
ALPHA = (2 * DEPTH) ** 0.25
LN_EPS = 1e-5
QK_SCALE = 0.125
ADAM_LR, ADAM_B1, ADAM_B2, ADAM_EPS, ADAM_WD, ADAM_STEP = 0.001, 0.9, 0.999, 1e-08, 0.01, 10
NEG = -1e30
V7X_VMEM_LIMIT = 56 * 1024 * 1024
MESH = pl.DeviceIdType.MESH
NT = (((1,), (1,)), ((), ()))
TN = (((0,), (0,)), ((), ()))
N_CHIPS = 4


def _params(*sem):
    return pltpu.CompilerParams(dimension_semantics=sem, vmem_limit_bytes=V7X_VMEM_LIMIT)


def _pick(n, cands=(1024, 768, 512, 384, 256, 128)):
    for c in cands:
        if n % c == 0:
            return c
    raise ValueError(f"no tile for {n}")


def _mm_nn(a, b, *, a_op, out_dtype, name):
    M, K = a.shape
    G, Kb, Nc = b.shape
    assert K == Kb
    N = G * Nc
    tm, tn, tk = _pick(M, (512, 256, 128)), _pick(Nc), _pick(K)
    nbs, nk = Nc // tn, K // tk

    def body(a_ref, b_ref, o_ref, acc_ref):
        kk = pl.program_id(2)
        av = a_ref[...]
        if a_op == "relu2":
            av = jnp.square(jnp.maximum(av, 0.0))
        part = jnp.dot(av.astype(MXU), b_ref[...], preferred_element_type=F32)

        @pl.when(kk == 0)
        def _():
            acc_ref[...] = part

        @pl.when(kk > 0)
        def _():
            acc_ref[...] += part

        @pl.when(kk == nk - 1)
        def _():
            o_ref[...] = acc_ref[...].astype(out_dtype)

    return pl.pallas_call(
        body,
        grid=(N // tn, M // tm, nk),
        in_specs=[
            pl.BlockSpec((tm, tk), lambda j, i, k: (i, k)),
            pl.BlockSpec((None, tk, tn), lambda j, i, k: (j // nbs, k, j % nbs)),
        ],
        out_specs=pl.BlockSpec((tm, tn), lambda j, i, k: (i, j)),
        out_shape=jax.ShapeDtypeStruct((M, N), out_dtype),
        scratch_shapes=[pltpu.VMEM((tm, tn), F32)],
        compiler_params=_params("parallel", "parallel", "arbitrary"),
        name=name,
    )(a, b)


def _mm_nt(a, b, *, epi, extra, out_dtype, name):
    M, N = a.shape
    G, K, Nc = b.shape
    assert N == G * Nc
    tm, tko, tn = _pick(M, (512, 256, 128)), _pick(K), _pick(Nc)
    nbs, nn = Nc // tn, N // tn

    def body(*refs):
        if epi is None:
            a_ref, b_ref, o_ref, acc_ref = refs
        else:
            a_ref, b_ref, e_ref, o_ref, acc_ref = refs
        kk = pl.program_id(2)
        part = lax.dot_general(a_ref[...].astype(MXU), b_ref[...], NT, preferred_element_type=F32)

        @pl.when(kk == 0)
        def _():
            acc_ref[...] = part

        @pl.when(kk > 0)
        def _():
            acc_ref[...] += part

        @pl.when(kk == nn - 1)
        def _():
            r = acc_ref[...]
            if epi == "relu2grad":
                r = r * (2.0 * jnp.maximum(e_ref[...], 0.0))
            elif epi == "residual":
                r = ALPHA * e_ref[...] + r
            o_ref[...] = r.astype(out_dtype)

    in_specs = [
        pl.BlockSpec((tm, tn), lambda i, j, k: (i, k)),
        pl.BlockSpec((None, tko, tn), lambda i, j, k: (k // nbs, j, k % nbs)),
    ]
    ops = [a, b]
    if epi is not None:
        in_specs.append(pl.BlockSpec((tm, tko), lambda i, j, k: (i, j)))
        ops.append(extra)
    return pl.pallas_call(
        body,
        grid=(M // tm, K // tko, nn),
        in_specs=in_specs,
        out_specs=pl.BlockSpec((tm, tko), lambda i, j, k: (i, j)),
        out_shape=jax.ShapeDtypeStruct((M, K), out_dtype),
        scratch_shapes=[pltpu.VMEM((tm, tko), F32)],
        compiler_params=_params("parallel", "parallel", "arbitrary"),
        name=name,
    )(*ops)


def _mm_tn(a, b, *, a_op, groups, name):
    M, Ka = a.shape
    Mb, N = b.shape
    assert M == Mb
    Nc = N // groups
    tka, tn, tm = _pick(Ka, (512, 256, 128)), _pick(Nc), _pick(M, (512, 256, 128))
    nbs, nm = Nc // tn, M // tm

    def body(a_ref, b_ref, o_ref, acc_ref):
        kk = pl.program_id(2)
        av = a_ref[...]
        if a_op == "relu2":
            av = jnp.square(jnp.maximum(av, 0.0))
        part = lax.dot_general(av.astype(MXU), b_ref[...].astype(MXU), TN, preferred_element_type=F32)

        @pl.when(kk == 0)
        def _():
            acc_ref[...] = part

        @pl.when(kk > 0)
        def _():
            acc_ref[...] += part

        @pl.when(kk == nm - 1)
        def _():
            o_ref[...] = acc_ref[...]

    return pl.pallas_call(
        body,
        grid=(Ka // tka, N // tn, nm),
        in_specs=[
            pl.BlockSpec((tm, tka), lambda i, j, k: (k, i)),
            pl.BlockSpec((tm, tn), lambda i, j, k: (k, j)),
        ],
        out_specs=pl.BlockSpec((None, tka, tn), lambda i, j, k: (j // nbs, i, j % nbs)),
        out_shape=jax.ShapeDtypeStruct((groups, Ka, Nc), F32),
        scratch_shapes=[pltpu.VMEM((tka, tn), F32)],
        compiler_params=_params("parallel", "parallel", "arbitrary"),
        name=name,
    )(a, b)


def _ln_stats(y):
    mu = jnp.mean(y, axis=-1, keepdims=True)
    yc = y - mu
    var = jnp.mean(yc * yc, axis=-1, keepdims=True)
    rstd = lax.rsqrt(var + LN_EPS)
    return yc * rstd, rstd


def _ln_fwd(x, mix, g, b, *, name):
    S, D = x.shape
    tm = _pick(S, (256, 128))

    def body(x_ref, m_ref, g_ref, b_ref, y_ref, o_ref):
        y = ALPHA * x_ref[...] + m_ref[...]
        xhat, _ = _ln_stats(y)
        y_ref[...] = y
        o_ref[...] = xhat * g_ref[...] + b_ref[...]

    row = pl.BlockSpec((tm, D), lambda i: (i, 0))
    vec = pl.BlockSpec((1, D), lambda i: (0, 0))
    return pl.pallas_call(
        body,
        grid=(S // tm,),
        in_specs=[row, row, vec, vec],
        out_specs=[row, row],
        out_shape=[jax.ShapeDtypeStruct((S, D), F32)] * 2,
        compiler_params=_params("parallel"),
        name=name,
    )(x, mix, g.reshape(1, D), b.reshape(1, D))


def _ln_bwd(dx, y, g, *, name):
    S, D = y.shape
    tm = _pick(S, (256, 128))

    def body(dx_ref, y_ref, g_ref, dy_ref, dg_ref, db_ref):
        xhat, rstd = _ln_stats(y_ref[...])
        dx_v = dx_ref[...]
        dxh = dx_v * g_ref[...]
        m1 = jnp.mean(dxh, axis=-1, keepdims=True)
        m2 = jnp.mean(dxh * xhat, axis=-1, keepdims=True)
        dy_ref[...] = rstd * (dxh - m1 - xhat * m2)

        @pl.when(pl.program_id(0) == 0)
        def _():
            dg_ref[...] = jnp.zeros_like(dg_ref)
            db_ref[...] = jnp.zeros_like(db_ref)

        dg_ref[...] += jnp.sum(dx_v * xhat, axis=0, keepdims=True)
        db_ref[...] += jnp.sum(dx_v, axis=0, keepdims=True)

    row = pl.BlockSpec((tm, D), lambda i: (i, 0))
    vec = pl.BlockSpec((1, D), lambda i: (0, 0))
    return pl.pallas_call(
        body,
        grid=(S // tm,),
        in_specs=[row, row, vec],
        out_specs=[row, vec, vec],
        out_shape=[jax.ShapeDtypeStruct((S, D), F32)] + [jax.ShapeDtypeStruct((1, D), F32)] * 2,
        compiler_params=_params("arbitrary"),
        name=name,
    )(dx, y, g.reshape(1, D))


def _ln_bwd_loss(y, g, b, target, *, name):
    S, D = y.shape
    tm = _pick(S, (256, 128))

    def body(y_ref, g_ref, b_ref, t_ref, dy_ref, dg_ref, db_ref, ls_ref):
        xhat, rstd = _ln_stats(y_ref[...])
        err = xhat * g_ref[...] + b_ref[...] - t_ref[...]
        dx_v = err * (1.0 / D)
        dxh = dx_v * g_ref[...]
        m1 = jnp.mean(dxh, axis=-1, keepdims=True)
        m2 = jnp.mean(dxh * xhat, axis=-1, keepdims=True)
        dy_ref[...] = rstd * (dxh - m1 - xhat * m2)

        @pl.when(pl.program_id(0) == 0)
        def _():
            dg_ref[...] = jnp.zeros_like(dg_ref)
            db_ref[...] = jnp.zeros_like(db_ref)
            ls_ref[...] = jnp.zeros_like(ls_ref)

        dg_ref[...] += jnp.sum(dx_v * xhat, axis=0, keepdims=True)
        db_ref[...] += jnp.sum(dx_v, axis=0, keepdims=True)
        ls_ref[...] += jnp.sum(err * err, axis=0, keepdims=True)

    row = pl.BlockSpec((tm, D), lambda i: (i, 0))
    vec = pl.BlockSpec((1, D), lambda i: (0, 0))
    return pl.pallas_call(
        body,
        grid=(S // tm,),
        in_specs=[row, vec, vec, row],
        out_specs=[row, vec, vec, vec],
        out_shape=[jax.ShapeDtypeStruct((S, D), F32)] + [jax.ShapeDtypeStruct((1, D), F32)] * 3,
        compiler_params=_params("arbitrary"),
        name=name,
    )(y, g.reshape(1, D), b.reshape(1, D), target)


def _rope_tables(S):
    inv_freq = ROPE_THETA ** (-jnp.arange(ROPE_HALF, dtype=F32) / ROPE_HALF)
    ang = jnp.arange(S, dtype=jnp.int32).astype(F32)[:, None] * inv_freq[None, :]
    cos, sin = jnp.cos(ang), jnp.sin(ang)
    ones = jnp.ones((S, HEAD_DIM - 2 * ROPE_HALF), F32)
    zeros8 = jnp.zeros((S, ROPE_HALF), F32)
    c = jnp.concatenate([cos, cos, ones], axis=1)
    s1 = jnp.concatenate([zeros8, sin, 0.0 * ones], axis=1)
    s2 = jnp.concatenate([-sin, zeros8, 0.0 * ones], axis=1)
    return tuple(jnp.concatenate([t, t], axis=1) for t in (c, s1, s2))


def _rope(xb, c, s1, s2):
    return xb * c + pltpu.roll(xb, ROPE_HALF, 1) * s1 + pltpu.roll(xb, LANES - ROPE_HALF, 1) * s2


def _rope_t(dyb, c, s1, s2):
    return dyb * c + pltpu.roll(dyb * s1, LANES - ROPE_HALF, 1) + pltpu.roll(dyb * s2, ROPE_HALF, 1)


def _qkv_post(qkv, tables, *, name):
    S, D3 = qkv.shape
    D = D3 // 3
    tm = _pick(S, (256, 128))
    rope = tables is not None

    def body(*refs):
        if rope:
            x_ref, c_ref, s1_ref, s2_ref, o_ref = refs
            tabs = (c_ref[...], s1_ref[...], s2_ref[...])
        else:
            x_ref, o_ref = refs
        j = pl.program_id(1)

        def write(rotate, scale):
            for blk in range(D // LANES):
                cols = slice(blk * LANES, (blk + 1) * LANES)
                xb = x_ref[:, cols]
                if rotate:
                    xb = _rope(xb, *tabs)
                if scale:
                    xb = xb * QK_SCALE
                o_ref[:, cols] = xb.astype(MXU)

        pl.when(j == 0)(lambda: write(rope, True))
        pl.when(j == 1)(lambda: write(rope, False))
        pl.when(j == 2)(lambda: write(False, False))

    tab = pl.BlockSpec((tm, LANES), lambda i, j: (i, 0))
    return pl.pallas_call(
        body,
        grid=(S // tm, 3),
        in_specs=[pl.BlockSpec((tm, D), lambda i, j: (i, j))] + ([tab] * 3 if rope else []),
        out_specs=pl.BlockSpec((None, tm, D), lambda i, j: (j, i, 0)),
        out_shape=jax.ShapeDtypeStruct((3, S, D), MXU),
        compiler_params=_params("parallel", "parallel"),
        name=name,
    )(qkv, *(tables if rope else ()))


def _qkv_bwd_post(dqs, dks, dvs, tables, *, name):
    S, D = dqs[0].shape
    tm = _pick(S, (256, 128))
    nb = len(dqs)
    rope = tables is not None

    def body(*refs):
        ins, o_ref = refs[:-1], refs[-1]
        groups = [ins[0:nb], ins[nb : 2 * nb], ins[2 * nb : 3 * nb]]
        tabs = [r[...] for r in ins[3 * nb :]]
        for j, grp in enumerate(groups):
            for blk in range(D // LANES):
                cols = slice(blk * LANES, (blk + 1) * LANES)
                v = grp[0][:, cols]
                for r in grp[1:]:
                    v = v + r[:, cols]
                if rope and j < 2:
                    v = _rope_t(v, *tabs)
                if j == 0:
                    v = v * QK_SCALE
                o_ref[:, j * D + blk * LANES : j * D + (blk + 1) * LANES] = v.astype(MXU)

    row = pl.BlockSpec((tm, D), lambda i: (i, 0))
    tab = pl.BlockSpec((tm, LANES), lambda i: (i, 0))
    return pl.pallas_call(
        body,
        grid=(S // tm,),
        in_specs=[row] * (3 * nb) + ([tab] * 3 if rope else []),
        out_specs=pl.BlockSpec((tm, 3 * D), lambda i: (i, 0)),
        out_shape=jax.ShapeDtypeStruct((S, 3 * D), MXU),
        compiler_params=_params("parallel"),
        name=name,
    )(*dqs, *dks, *dvs, *(tables if rope else ()))


def _log_fail(z):
    return -(jnp.maximum(z, 0.0) + jnp.log1p(jnp.exp(-jnp.abs(z))))


def _sb_fwd(q, k, v, *, name):
    S, D = q.shape
    HP, H = D // LANES, D // HEAD_DIM
    tq = _pick(S, (256, 128))
    cpb = tq // CHUNK
    assert S // CHUNK <= LANES

    def body(q_ref, k_ref, v_ref, o_ref, car_ref, acc_ref, carry_ref):
        i = pl.program_id(1)
        lane = lax.broadcasted_iota(jnp.int32, (tq, LANES), 1)
        row = lax.broadcasted_iota(jnp.int32, (tq, LANES), 0)
        kl = lax.broadcasted_iota(jnp.int32, (CHUNK, LANES), 1)
        kr = lax.broadcasted_iota(jnp.int32, (CHUNK, LANES), 0)
        tri = (kr >= kl).astype(MXU)
        qv = q_ref[...]
        acc_ref[...] = jnp.zeros_like(acc_ref)
        for hh in range(2):
            hm = (lane >= HEAD_DIM) if hh else (lane < HEAD_DIM)
            hmk = (kl >= HEAD_DIM) if hh else (kl < HEAD_DIM)
            qh = jnp.where(hm, qv, jnp.zeros_like(qv))
            carry_ref[...] = jnp.zeros_like(carry_ref)
            car_ref[hh] = jnp.zeros((tq, LANES), F32)

            def step(c, masked, hh=hh, hmk=hmk, qh=qh):
                off = pl.multiple_of(c * CHUNK, CHUNK)
                kc = k_ref[pl.ds(off, CHUNK), :]
                vc = v_ref[pl.ds(off, CHUNK), :]
                vc = jnp.where(hmk, vc, jnp.zeros_like(vc))
                z = lax.dot_general(qh, kc, NT, preferred_element_type=F32)
                lf = _log_fail(z)
                if masked:
                    cm = (off + lane) < (i * tq + row)
                    lf = jnp.where(cm, lf, 0.0)
                carry = carry_ref[...]
                logw = z + jnp.dot(lf.astype(MXU), tri, preferred_element_type=F32) + carry
                w = jnp.exp(logw)
                if masked:
                    w = jnp.where(cm, w, 0.0)
                acc_ref[...] += jnp.dot(w.astype(MXU), vc, preferred_element_type=F32)
                car_ref[hh] = jnp.where(lane == c, carry, car_ref[hh])
                carry_ref[...] = carry + jnp.sum(lf, axis=-1, keepdims=True)

            for u in reversed(range(cpb)):
                step(i * cpb + u, True)

            def loop_body(t, carry, step=step):
                step(i * cpb - 1 - t, False)
                return carry

            lax.fori_loop(0, i * cpb, loop_body, 0)
        o_ref[...] = acc_ref[...].astype(o_ref.dtype)

    blk = pl.BlockSpec((tq, LANES), lambda h, i: (i, h))
    full = pl.BlockSpec((S, LANES), lambda h, i: (0, h))
    return pl.pallas_call(
        body,
        grid=(HP, S // tq),
        in_specs=[blk, full, full],
        out_specs=[blk, pl.BlockSpec((2, tq, LANES), lambda h, i: (h, i, 0))],
        out_shape=[jax.ShapeDtypeStruct((S, D), MXU), jax.ShapeDtypeStruct((H, S, LANES), F32)],
        scratch_shapes=[pltpu.VMEM((tq, LANES), F32), pltpu.VMEM((tq, 1), F32)],
        compiler_params=_params("parallel", "arbitrary"),
        name=name,
    )(q, k, v)


def _sb_bwd(q, k, v, do, car, *, name):
    S, D = q.shape
    HP = D // LANES
    tq = _pick(S, (256, 128))
    cpb = tq // CHUNK
    nq = S // tq

    def body(q_ref, k_ref, v_ref, do_ref, car_ref, dq_ref, dk_hbm, dv_hbm, dq_acc, dk_acc, dv_acc, gcar_ref, sem):
        hp = pl.program_id(0)
        i = pl.program_id(1)
        lane = lax.broadcasted_iota(jnp.int32, (tq, LANES), 1)
        row = lax.broadcasted_iota(jnp.int32, (tq, LANES), 0)
        kl = lax.broadcasted_iota(jnp.int32, (CHUNK, LANES), 1)
        kr = lax.broadcasted_iota(jnp.int32, (CHUNK, LANES), 0)
        tri = (kr >= kl).astype(MXU)
        tri_prefix = (kr <= kl).astype(MXU)

        @pl.when(i == 0)
        def _():
            dk_acc[...] = jnp.zeros_like(dk_acc)
            dv_acc[...] = jnp.zeros_like(dv_acc)

        qv = q_ref[...]
        dov = do_ref[...]
        dq_acc[...] = jnp.zeros_like(dq_acc)
        for hh in range(2):
            hm = (lane >= HEAD_DIM) if hh else (lane < HEAD_DIM)
            hmk = (kl >= HEAD_DIM) if hh else (kl < HEAD_DIM)
            qh = jnp.where(hm, qv, jnp.zeros_like(qv))
            doh = jnp.where(hm, dov, jnp.zeros_like(dov))
            gcar_ref[...] = jnp.zeros_like(gcar_ref)

            def step(c, masked, hh=hh, hmk=hmk, qh=qh, doh=doh):
                off = pl.multiple_of(c * CHUNK, CHUNK)
                kc = k_ref[pl.ds(off, CHUNK), :]
                vc = v_ref[pl.ds(off, CHUNK), :]
                z = lax.dot_general(qh, kc, NT, preferred_element_type=F32)
                lf = _log_fail(z)
                sig = jnp.exp(z + lf)
                if masked:
                    cm = (off + lane) < (i * tq + row)
                    lf = jnp.where(cm, lf, 0.0)
                carry = jnp.sum(jnp.where(lane == c, car_ref[hh], 0.0), axis=-1, keepdims=True)
                logw = z + jnp.dot(lf.astype(MXU), tri, preferred_element_type=F32) + carry
                w = jnp.exp(logw)
                if masked:
                    w = jnp.where(cm, w, 0.0)
                dw = lax.dot_general(doh, vc, NT, preferred_element_type=F32)
                g = w * dw
                big_g = jnp.dot(g.astype(MXU), tri_prefix, preferred_element_type=F32) + gcar_ref[...]
                dz = g - sig * big_g
                if masked:
                    dz = jnp.where(cm, dz, 0.0)
                gcar_ref[...] += jnp.sum(g, axis=-1, keepdims=True)
                dzb = dz.astype(MXU)
                kch = jnp.where(hmk, kc, jnp.zeros_like(kc))
                dq_acc[...] += jnp.dot(dzb, kch, preferred_element_type=F32)
                dk_acc[pl.ds(off, CHUNK), :] += lax.dot_general(dzb, qh, TN, preferred_element_type=F32)
                dv_acc[pl.ds(off, CHUNK), :] += lax.dot_general(w.astype(MXU), doh, TN, preferred_element_type=F32)

            def loop_body(c, carry, step=step):
                step(c, False)
                return carry

            lax.fori_loop(0, i * cpb, loop_body, 0)
            for u in range(cpb):
                step(i * cpb + u, True)
        dq_ref[...] = dq_acc[...]

        @pl.when(i == nq - 1)
        def _():
            cols = pl.ds(pl.multiple_of(hp * LANES, LANES), LANES)
            for src, dst in ((dk_acc, dk_hbm), (dv_acc, dv_hbm)):
                cp = pltpu.make_async_copy(src, dst.at[:, cols], sem)
                cp.start()
                cp.wait()

    blk = pl.BlockSpec((tq, LANES), lambda h, i: (i, h))
    full = pl.BlockSpec((S, LANES), lambda h, i: (0, h))
    hbm = pl.BlockSpec(memory_space=pl.ANY)
    return pl.pallas_call(
        body,
        grid=(HP, nq),
        in_specs=[blk, full, full, blk, pl.BlockSpec((2, tq, LANES), lambda h, i: (h, i, 0))],
        out_specs=[blk, hbm, hbm],
        out_shape=[jax.ShapeDtypeStruct((S, D), F32)] * 3,
        scratch_shapes=[
            pltpu.VMEM((tq, LANES), F32),
            pltpu.VMEM((S, LANES), F32),
            pltpu.VMEM((S, LANES), F32),
            pltpu.VMEM((tq, 1), F32),
            pltpu.SemaphoreType.DMA,
        ],
        compiler_params=_params("arbitrary", "arbitrary"),
        name=name,
    )(q, k, v, do, car)


def _perm(x, d):
    if d == 1:
        return x
    S, D = x.shape
    return x.reshape(S // d, d, D).transpose(1, 0, 2).reshape(S, D)


def _unperm(x, d):
    if d == 1:
        return x
    S, D = x.shape
    return x.reshape(d, S // d, D).transpose(1, 0, 2).reshape(S, D)


def _perm3(x):
    return jnp.stack([_perm(x, d) for d in DILATIONS])


def _dil_tb(S):
    return min(512, S // DILATIONS[-1])


def _dil_masks():
    qi = lax.broadcasted_iota(jnp.int32, (CHUNK, CHUNK), 0)
    kj = lax.broadcasted_iota(jnp.int32, (CHUNK, CHUNK), 1)
    return qi, kj


def _dil_fwd(q3, k3, v3, *, name):
    _, S, D = q3.shape
    HP = D // LANES
    tb = _dil_tb(S)
    nb, nsub = S // tb, tb // CHUNK

    def body(q_ref, k_ref, v_ref, kp_ref, vp_ref, o_ref, l_ref):
        b = pl.program_id(0)
        n = pl.program_id(2)
        bps = jnp.right_shift(nb, 2 * b)
        first = jnp.bitwise_and(n, bps - 1) == 0
        qi, kj = _dil_masks()
        hl = lax.broadcasted_iota(jnp.int32, (CHUNK, LANES), 1)
        m_cur = kj <= qi
        m_prev = (kj - qi) >= 0
        m_prev0 = (kj - qi) >= jnp.where(first, 2 * CHUNK, 0)
        for u in range(nsub):
            rows = slice(u * CHUNK, (u + 1) * CHUNK)
            qs, kc, vc = q_ref[rows, :], k_ref[rows, :], v_ref[rows, :]
            if u == 0:
                kp, vp, pm = kp_ref[...], vp_ref[...], m_prev0
            else:
                prev = slice((u - 1) * CHUNK, u * CHUNK)
                kp, vp, pm = k_ref[prev, :], v_ref[prev, :], m_prev
            o_t = jnp.zeros((CHUNK, LANES), F32)
            l_t = jnp.zeros((CHUNK, LANES), F32)
            for hh in range(2):
                hm = (hl >= HEAD_DIM) if hh else (hl < HEAD_DIM)
                qh = jnp.where(hm, qs, jnp.zeros_like(qs))
                s_p = jnp.where(pm, lax.dot_general(qh, kp, NT, preferred_element_type=F32), NEG)
                s_c = jnp.where(m_cur, lax.dot_general(qh, kc, NT, preferred_element_type=F32), NEG)
                m = jnp.maximum(jnp.max(s_p, axis=-1, keepdims=True), jnp.max(s_c, axis=-1, keepdims=True))
                p_p, p_c = jnp.exp(s_p - m), jnp.exp(s_c - m)
                den = jnp.sum(p_p, axis=-1, keepdims=True) + jnp.sum(p_c, axis=-1, keepdims=True)
                num = jnp.dot(p_p.astype(MXU), vp, preferred_element_type=F32) + jnp.dot(
                    p_c.astype(MXU), vc, preferred_element_type=F32
                )
                o_t = jnp.where(hm, num / den, o_t)
                l_t = jnp.where(hm, m + jnp.log(den), l_t)
            o_ref[rows, :] = o_t
            l_ref[rows, :] = l_t

    blk = pl.BlockSpec((None, tb, LANES), lambda b, h, n: (b, n, h))
    tail = pl.BlockSpec((None, CHUNK, LANES), lambda b, h, n: (b, jnp.maximum(n * nsub - 1, 0), h))
    return pl.pallas_call(
        body,
        grid=(3, HP, nb),
        in_specs=[blk, blk, blk, tail, tail],
        out_specs=[blk, blk],
        out_shape=[jax.ShapeDtypeStruct((3, S, D), F32)] * 2,
        compiler_params=_params("parallel", "parallel", "parallel"),
        name=name,
    )(q3, k3, v3, k3, v3)


def _dil_combine(o3, l3, *, name):
    _, S, D = o3.shape
    tm = _pick(S, (256, 128))

    def body(o_ref, l_ref, out_ref, lse_ref):
        l0, l1, l2 = l_ref[0], l_ref[1], l_ref[2]
        m = jnp.maximum(jnp.maximum(l0, l1), l2)
        a0, a1, a2 = jnp.exp(l0 - m), jnp.exp(l1 - m), jnp.exp(l2 - m)
        den = a0 + a1 + a2
        out_ref[...] = (a0 * o_ref[0] + a1 * o_ref[1] + a2 * o_ref[2]) / den
        lse_ref[...] = m + jnp.log(den)

    blk3 = pl.BlockSpec((3, tm, D), lambda i: (0, i, 0))
    row = pl.BlockSpec((tm, D), lambda i: (i, 0))
    return pl.pallas_call(
        body,
        grid=(S // tm,),
        in_specs=[blk3, blk3],
        out_specs=[row, row],
        out_shape=[jax.ShapeDtypeStruct((S, D), F32)] * 2,
        compiler_params=_params("parallel"),
        name=name,
    )(o3, l3)


def _head_col(tile, hl, hh):
    return jnp.sum(jnp.where(hl == hh * HEAD_DIM, tile, 0.0), axis=-1, keepdims=True)


def _dil_bwd_dq(q3, k3, v3, do3, out3, lse3, *, name):
    _, S, D = q3.shape
    HP = D // LANES
    tb = _dil_tb(S)
    nb, nsub = S // tb, tb // CHUNK

    def body(q_ref, k_ref, v_ref, kp_ref, vp_ref, do_ref, out_ref, lse_ref, dq_ref):
        b = pl.program_id(0)
        n = pl.program_id(2)
        bps = jnp.right_shift(nb, 2 * b)
        first = jnp.bitwise_and(n, bps - 1) == 0
        qi, kj = _dil_masks()
        hl = lax.broadcasted_iota(jnp.int32, (CHUNK, LANES), 1)
        m_cur = kj <= qi
        m_prev = (kj - qi) >= 0
        m_prev0 = (kj - qi) >= jnp.where(first, 2 * CHUNK, 0)
        for u in range(nsub):
            rows = slice(u * CHUNK, (u + 1) * CHUNK)
            qs, kc, vc = q_ref[rows, :], k_ref[rows, :], v_ref[rows, :]
            if u == 0:
                kp, vp, pm = kp_ref[...], vp_ref[...], m_prev0
            else:
                prev = slice((u - 1) * CHUNK, u * CHUNK)
                kp, vp, pm = k_ref[prev, :], v_ref[prev, :], m_prev
            do_t, out_t, lse_t = do_ref[rows, :], out_ref[rows, :], lse_ref[rows, :]
            dq_t = jnp.zeros((CHUNK, LANES), F32)
            for hh in range(2):
                hm = (hl >= HEAD_DIM) if hh else (hl < HEAD_DIM)
                qh = jnp.where(hm, qs, jnp.zeros_like(qs))
                doh = jnp.where(hm, do_t, 0.0)
                delta = jnp.sum(doh * out_t, axis=-1, keepdims=True)
                lse = _head_col(lse_t, hl, hh)
                dob = doh.astype(MXU)
                for kk, vv, msk in ((kp, vp, pm), (kc, vc, m_cur)):
                    s = jnp.where(msk, lax.dot_general(qh, kk, NT, preferred_element_type=F32), NEG)
                    p = jnp.exp(s - lse)
                    dp = lax.dot_general(dob, vv, NT, preferred_element_type=F32)
                    ds = p * (dp - delta)
                    kh = jnp.where(hm, kk, jnp.zeros_like(kk))
                    dq_t = dq_t + jnp.dot(ds.astype(MXU), kh, preferred_element_type=F32)
            dq_ref[rows, :] = dq_t

    blk = pl.BlockSpec((None, tb, LANES), lambda b, h, n: (b, n, h))
    tail = pl.BlockSpec((None, CHUNK, LANES), lambda b, h, n: (b, jnp.maximum(n * nsub - 1, 0), h))
    return pl.pallas_call(
        body,
        grid=(3, HP, nb),
        in_specs=[blk, blk, blk, tail, tail, blk, blk, blk],
        out_specs=blk,
        out_shape=jax.ShapeDtypeStruct((3, S, D), F32),
        compiler_params=_params("parallel", "parallel", "parallel"),
        name=name,
    )(q3, k3, v3, k3, v3, do3, out3, lse3)


def _dil_bwd_dkv(q3, k3, v3, do3, out3, lse3, *, name):
    _, S, D = q3.shape
    HP = D // LANES
    tb = _dil_tb(S)
    nb, nsub = S // tb, tb // CHUNK
    last_chunk = S // CHUNK - 1

    def body(q_ref, k_ref, v_ref, do_ref, out_ref, lse_ref, qn_ref, don_ref, outn_ref, lsen_ref, dk_ref, dv_ref):
        b = pl.program_id(0)
        n = pl.program_id(2)
        bps = jnp.right_shift(nb, 2 * b)
        last = jnp.bitwise_and(n + 1, bps - 1) == 0
        qi, kj = _dil_masks()
        hl = lax.broadcasted_iota(jnp.int32, (CHUNK, LANES), 1)
        m_cur = kj <= qi
        m_next = (kj - qi) >= 0
        m_next_last = (kj - qi) >= jnp.where(last, 2 * CHUNK, 0)
        for u in range(nsub):
            rows = slice(u * CHUNK, (u + 1) * CHUNK)
            kc, vc = k_ref[rows, :], v_ref[rows, :]
            cur = (q_ref[rows, :], do_ref[rows, :], out_ref[rows, :], lse_ref[rows, :], m_cur)
            if u == nsub - 1:
                nxt = (qn_ref[...], don_ref[...], outn_ref[...], lsen_ref[...], m_next_last)
            else:
                nr = slice((u + 1) * CHUNK, (u + 2) * CHUNK)
                nxt = (q_ref[nr, :], do_ref[nr, :], out_ref[nr, :], lse_ref[nr, :], m_next)
            dk_t = jnp.zeros((CHUNK, LANES), F32)
            dv_t = jnp.zeros((CHUNK, LANES), F32)
            for hh in range(2):
                hm = (hl >= HEAD_DIM) if hh else (hl < HEAD_DIM)
                for qs, do_t, out_t, lse_t, msk in (cur, nxt):
                    qh = jnp.where(hm, qs, jnp.zeros_like(qs))
                    doh = jnp.where(hm, do_t, 0.0)
                    delta = jnp.sum(doh * out_t, axis=-1, keepdims=True)
                    lse = _head_col(lse_t, hl, hh)
                    dob = doh.astype(MXU)
                    s = jnp.where(msk, lax.dot_general(qh, kc, NT, preferred_element_type=F32), NEG)
                    p = jnp.exp(s - lse)
                    dp = lax.dot_general(dob, vc, NT, preferred_element_type=F32)
                    ds = p * (dp - delta)
                    dv_t = dv_t + lax.dot_general(p.astype(MXU), dob, TN, preferred_element_type=F32)
                    dk_t = dk_t + lax.dot_general(ds.astype(MXU), qh, TN, preferred_element_type=F32)
            dk_ref[rows, :] = dk_t
            dv_ref[rows, :] = dv_t

    blk = pl.BlockSpec((None, tb, LANES), lambda b, h, n: (b, n, h))
    head = pl.BlockSpec((None, CHUNK, LANES), lambda b, h, n: (b, jnp.minimum((n + 1) * nsub, last_chunk), h))
    return pl.pallas_call(
        body,
        grid=(3, HP, nb),
        in_specs=[blk] * 6 + [head] * 4,
        out_specs=[blk, blk],
        out_shape=[jax.ShapeDtypeStruct((3, S, D), F32)] * 2,
        compiler_params=_params("parallel", "parallel", "parallel"),
        name=name,
    )(q3, k3, v3, do3, out3, lse3, q3, do3, out3, lse3)


def _chip_peers():
    x, y, c = lax.axis_index("x"), lax.axis_index("y"), lax.axis_index("c")
    peers = [(1 - x, y, c), (x, 1 - y, c), (1 - x, 1 - y, c)]
    return x, y, c, peers


def _allgather_weights(wsh, *, name):
    R, D = wsh.shape

    def body(w_ref, out_ref, send_sems, recv_sems, local_sem):
        x, y, c, peers = _chip_peers()
        me = 2 * x + y
        mine = pltpu.make_async_copy(w_ref, out_ref.at[me], local_sem)
        mine.start()
        sends = [
            pltpu.make_async_remote_copy(
                src_ref=w_ref, dst_ref=out_ref.at[me], send_sem=send_sems.at[j], recv_sem=recv_sems.at[j],
                device_id=p, device_id_type=MESH,
            )
            for j, p in enumerate(peers)
        ]
        for cp in sends:
            cp.start()
        for j, p in enumerate(peers):
            theirs = out_ref.at[2 * p[0] + p[1]]
            pltpu.make_async_remote_copy(
                src_ref=theirs, dst_ref=theirs, send_sem=send_sems.at[j], recv_sem=recv_sems.at[j],
                device_id=p, device_id_type=MESH,
            ).wait_recv()
        for cp in sends:
            cp.wait_send()
        mine.wait()

    hbm = pl.BlockSpec(memory_space=pl.ANY)
    return pl.pallas_call(
        body,
        in_specs=[hbm],
        out_specs=hbm,
        out_shape=jax.ShapeDtypeStruct((N_CHIPS, R, D), wsh.dtype),
        scratch_shapes=[pltpu.SemaphoreType.DMA((3,)), pltpu.SemaphoreType.DMA((3,)), pltpu.SemaphoreType.DMA],
        name=name,
    )(wsh)


def _scatter_grads(gflat, *, name):
    _, R, D = gflat.shape

    def body(g_ref, out_ref, send_sems, recv_sems):
        x, y, c, peers = _chip_peers()
        sends = [
            pltpu.make_async_remote_copy(
                src_ref=g_ref.at[2 * p[0] + p[1]], dst_ref=out_ref.at[j], send_sem=send_sems.at[j],
                recv_sem=recv_sems.at[j], device_id=p, device_id_type=MESH,
            )
            for j, p in enumerate(peers)
        ]
        for cp in sends:
            cp.start()
        for j, p in enumerate(peers):
            pltpu.make_async_remote_copy(
                src_ref=out_ref.at[j], dst_ref=out_ref.at[j], send_sem=send_sems.at[j], recv_sem=recv_sems.at[j],
                device_id=p, device_id_type=MESH,
            ).wait_recv()
        for cp in sends:
            cp.wait_send()

    hbm = pl.BlockSpec(memory_space=pl.ANY)
    return pl.pallas_call(
        body,
        in_specs=[hbm],
        out_specs=hbm,
        out_shape=jax.ShapeDtypeStruct((3, R, D), gflat.dtype),
        scratch_shapes=[pltpu.SemaphoreType.DMA((3,)), pltpu.SemaphoreType.DMA((3,))],
        name=name,
    )(gflat)


def _swap_with_sibling(part, *, name):
    def body(p_ref, out_ref, send_sem, recv_sem):
        x, y, c = lax.axis_index("x"), lax.axis_index("y"), lax.axis_index("c")
        cp = pltpu.make_async_remote_copy(
            src_ref=p_ref, dst_ref=out_ref, send_sem=send_sem, recv_sem=recv_sem,
            device_id=(x, y, 1 - c), device_id_type=MESH,
        )
        cp.start()
        cp.wait()

    hbm = pl.BlockSpec(memory_space=pl.ANY)
    return pl.pallas_call(
        body,
        in_specs=[hbm],
        out_specs=hbm,
        out_shape=jax.ShapeDtypeStruct(part.shape, part.dtype),
        scratch_shapes=[pltpu.SemaphoreType.DMA, pltpu.SemaphoreType.DMA],
        name=name,
    )(part)


def _gather_from_all(vec, *, name):
    R, D = vec.shape

    def body(v_ref, out_ref, send_sems, recv_sems):
        x, y, c = lax.axis_index("x"), lax.axis_index("y"), lax.axis_index("c")
        me = 4 * x + 2 * y + c
        out_ref[me] = v_ref[...]
        rel = [(j >> 2 & 1, j >> 1 & 1, j & 1) for j in range(1, 8)]
        peers = [((1 - x) if fx else x, (1 - y) if fy else y, (1 - c) if fc else c) for fx, fy, fc in rel]
        sends = [
            pltpu.make_async_remote_copy(
                src_ref=v_ref, dst_ref=out_ref.at[me], send_sem=send_sems.at[j], recv_sem=recv_sems.at[j],
                device_id=p, device_id_type=MESH,
            )
            for j, p in enumerate(peers)
        ]
        for cp in sends:
            cp.start()
        for j, p in enumerate(peers):
            theirs = out_ref.at[4 * p[0] + 2 * p[1] + p[2]]
            pltpu.make_async_remote_copy(
                src_ref=theirs, dst_ref=theirs, send_sem=send_sems.at[j], recv_sem=recv_sems.at[j],
                device_id=p, device_id_type=MESH,
            ).wait_recv()
        for cp in sends:
            cp.wait_send()

    vmem = pl.BlockSpec(memory_space=pltpu.VMEM)
    return pl.pallas_call(
        body,
        in_specs=[vmem],
        out_specs=vmem,
        out_shape=jax.ShapeDtypeStruct((8, R, D), vec.dtype),
        scratch_shapes=[pltpu.SemaphoreType.DMA((7,)), pltpu.SemaphoreType.DMA((7,))],
        name=name,
    )(vec)


def _sum4(a, rest, *, name):
    R, D = a.shape
    tm = _pick(R, (256, 128, 8))

    def body(a_ref, r_ref, o_ref):
        o_ref[...] = ((a_ref[...] + r_ref[0]) + r_ref[1]) + r_ref[2]

    row = pl.BlockSpec((tm, D), lambda i: (i, 0))
    return pl.pallas_call(
        body,
        grid=(R // tm,),
        in_specs=[row, pl.BlockSpec((3, tm, D), lambda i: (0, i, 0))],
        out_specs=row,
        out_shape=jax.ShapeDtypeStruct((R, D), F32),
        compiler_params=_params("parallel"),
        name=name,
    )(a, rest)


def _adamw(parts, w, m, v, *, name):
    P, R, D = parts.shape
    tm = _pick(R, (256, 128, 8))

    def body(p_ref, w_ref, m_ref, v_ref, g_ref, d_ref, nm_ref, nv_ref):
        g = p_ref[0]
        for k in range(1, P):
            g = g + p_ref[k]
        nm = ADAM_B1 * m_ref[...] + (1.0 - ADAM_B1) * g
        nv = ADAM_B2 * v_ref[...] + (1.0 - ADAM_B2) * jnp.square(g)
        m_hat = nm / (1.0 - ADAM_B1**ADAM_STEP)
        v_hat = nv / (1.0 - ADAM_B2**ADAM_STEP)
        g_ref[...] = g
        d_ref[...] = -ADAM_LR * (m_hat / (jnp.sqrt(v_hat) + ADAM_EPS) + ADAM_WD * w_ref[...])
        nm_ref[...] = nm
        nv_ref[...] = nv

    row = pl.BlockSpec((tm, D), lambda i: (i, 0))
    return pl.pallas_call(
        body,
        grid=(R // tm,),
        in_specs=[pl.BlockSpec((P, tm, D), lambda i: (0, i, 0)), row, row, row],
        out_specs=[row] * 4,
        out_shape=[jax.ShapeDtypeStruct((R, D), F32)] * 4,
        compiler_params=_params("parallel"),
        name=name,
    )(parts, w, m, v)


MATS = ("w_qkv", "w_o", "w_ff1", "w_ff2")
VECS = ("ln1_g", "ln1_b", "ln2_g", "ln2_b")
PARAM_ORDER = ("w_qkv", "w_o", "ln1_g", "ln1_b", "w_ff1", "w_ff2", "ln2_g", "ln2_b")


def _pack_rows(arrs, D):
    return jnp.concatenate([a.reshape(-1, D) for a in arrs], axis=0)


def kernel(x, w_qkv_0, w_o_0, ln1_g_0, ln1_b_0, w_ff1_0, w_ff2_0, ln2_g_0, ln2_b_0, w_qkv_1, w_o_1, ln1_g_1, ln1_b_1, w_ff1_1, w_ff2_1, ln2_g_1, ln2_b_1, loss_target, m_w_qkv_0, m_w_o_0, m_ln1_g_0, m_ln1_b_0, m_w_ff1_0, m_w_ff2_0, m_ln2_g_0, m_ln2_b_0, m_w_qkv_1, m_w_o_1, m_ln1_g_1, m_ln1_b_1, m_w_ff1_1, m_w_ff2_1, m_ln2_g_1, m_ln2_b_1, v_w_qkv_0, v_w_o_0, v_ln1_g_0, v_ln1_b_0, v_w_ff1_0, v_w_ff2_0, v_ln2_g_0, v_ln2_b_0, v_w_qkv_1, v_w_o_1, v_ln1_g_1, v_ln1_b_1, v_w_ff1_1, v_w_ff2_1, v_ln2_g_1, v_ln2_b_1):
    given = dict(locals())
    xs = x[0]
    target = loss_target[0]
    S, D = xs.shape
    C3 = 3 * D // N_CHIPS
    names = [f"{p}_{l}" for l in range(DEPTH) for p in PARAM_ORDER]
    mat_names = [f"{p}_{l}" for l in range(DEPTH) for p in MATS]
    vec_names = [f"{p}_{l}" for l in range(DEPTH) for p in VECS]
    rows = {"w_qkv": C3, "w_o": D // N_CHIPS, "w_ff1": D, "w_ff2": D}
    layer_rows = sum(rows.values())

    gathered = _allgather_weights(_pack_rows([given[n] for n in mat_names], D).astype(MXU), name="allgather_weights")
    W = []
    for l in range(DEPTH):
        r0 = l * layer_rows
        seg = {}
        for p in MATS:
            seg[p] = gathered[:, r0 : r0 + rows[p], :]
            r0 += rows[p]
        W.append(
            dict(
                qkv=seg["w_qkv"].reshape(N_CHIPS, D, C3),
                o=seg["w_o"].reshape(1, D, D),
                ff1=seg["w_ff1"],
                ff2=seg["w_ff2"].reshape(1, 4 * D, D),
            )
        )
    tables = _rope_tables(S)

    h = xs
    saved = []
    for l in range(DEPTH):
        g1, b1, g2, b2 = (given[f"{p}_{l}"] for p in VECS)
        qkv = _mm_nn(h, W[l]["qkv"], a_op=None, out_dtype=F32, name=f"qkv_{l}")
        qkv3 = _qkv_post(qkv, tables if l == 1 else None, name=f"qkv_post_{l}")
        st = dict(h=h, qkv3=qkv3)
        if l == 0:
            o, car = _sb_fwd(qkv3[0], qkv3[1], qkv3[2], name="stickbreak_fwd")
            st.update(car=car)
        else:
            q3, k3, v3 = _perm3(qkv3[0]), _perm3(qkv3[1]), _perm3(qkv3[2])
            ob, lb = _dil_fwd(q3, k3, v3, name="dilated_fwd")
            ob = jnp.stack([_unperm(ob[i], d) for i, d in enumerate(DILATIONS)])
            lb = jnp.stack([_unperm(lb[i], d) for i, d in enumerate(DILATIONS)])
            o, lse = _dil_combine(ob, lb, name="dilated_combine")
            st.update(q3=q3, k3=k3, v3=v3, lse=lse)
        mix = _mm_nn(o, W[l]["o"], a_op=None, out_dtype=F32, name=f"attn_out_{l}")
        y1, x1 = _ln_fwd(h, mix, g1, b1, name=f"ln1_{l}")
        hp = _mm_nn(x1, W[l]["ff1"], a_op=None, out_dtype=F32, name=f"ff1_{l}")
        mlp = _mm_nn(hp, W[l]["ff2"], a_op="relu2", out_dtype=F32, name=f"ff2_{l}")
        y2, x2 = _ln_fwd(x1, mlp, g2, b2, name=f"ln2_{l}")
        st.update(o=o, y1=y1, x1=x1, hp=hp, y2=y2)
        saved.append(st)
        h = x2

    grads = {}
    top = saved[-1]
    dy2, dg, db, loss_lanes = _ln_bwd_loss(
        top["y2"], given[f"ln2_g_{DEPTH - 1}"], given[f"ln2_b_{DEPTH - 1}"], target, name="loss_ln2_bwd"
    )
    loss = lax.psum(jnp.sum(loss_lanes) * (0.5 / D), ("x", "y", "c"))
    grads[f"ln2_g_{DEPTH - 1}"], grads[f"ln2_b_{DEPTH - 1}"] = dg, db
    grad_x = None
    for l in reversed(range(DEPTH)):
        st = saved[l]
        dhp = _mm_nt(dy2, W[l]["ff2"], epi="relu2grad", extra=st["hp"], out_dtype=MXU, name=f"d_ff2_in_{l}")
        grads[f"w_ff2_{l}"] = _mm_tn(st["hp"], dy2, a_op="relu2", groups=1, name=f"d_w_ff2_{l}")
        dx1 = _mm_nt(dhp, W[l]["ff1"], epi="residual", extra=dy2, out_dtype=F32, name=f"d_ff1_in_{l}")
        grads[f"w_ff1_{l}"] = _mm_tn(st["x1"], dhp, a_op=None, groups=N_CHIPS, name=f"d_w_ff1_{l}")
        dy1, grads[f"ln1_g_{l}"], grads[f"ln1_b_{l}"] = _ln_bwd(dx1, st["y1"], given[f"ln1_g_{l}"], name=f"ln1_bwd_{l}")
        grads[f"w_o_{l}"] = _mm_tn(st["o"], dy1, a_op=None, groups=1, name=f"d_w_o_{l}")
        q, k, v = st["qkv3"][0], st["qkv3"][1], st["qkv3"][2]
        if l == 0:
            do = _mm_nt(dy1, W[l]["o"], epi=None, extra=None, out_dtype=MXU, name=f"d_attn_out_{l}")
            dq, dk, dv = _sb_bwd(q, k, v, do, st["car"], name="stickbreak_bwd")
            dqkv = _qkv_bwd_post([dq], [dk], [dv], None, name=f"qkv_bwd_post_{l}")
        else:
            do = _mm_nt(dy1, W[l]["o"], epi=None, extra=None, out_dtype=F32, name=f"d_attn_out_{l}")
            do3, out3, lse3 = _perm3(do), _perm3(st["o"]), _perm3(st["lse"])
            dq3 = _dil_bwd_dq(st["q3"], st["k3"], st["v3"], do3, out3, lse3, name="dilated_bwd_dq")
            dk3, dv3 = _dil_bwd_dkv(st["q3"], st["k3"], st["v3"], do3, out3, lse3, name="dilated_bwd_dkv")
            un = lambda t: [_unperm(t[i], d) for i, d in enumerate(DILATIONS)]
            dqkv = _qkv_bwd_post(un(dq3), un(dk3), un(dv3), tables, name=f"qkv_bwd_post_{l}")
        dh = _mm_nt(dqkv, W[l]["qkv"], epi="residual", extra=dy1, out_dtype=F32, name=f"d_qkv_in_{l}")
        grads[f"w_qkv_{l}"] = _mm_tn(st["h"], dqkv, a_op=None, groups=N_CHIPS, name=f"d_w_qkv_{l}")
        if l > 0:
            prev = saved[l - 1]
            dy2, grads[f"ln2_g_{l - 1}"], grads[f"ln2_b_{l - 1}"] = _ln_bwd(
                dh, prev["y2"], given[f"ln2_g_{l - 1}"], name=f"ln2_bwd_{l - 1}"
            )
        else:
            grad_x = dh

    gflat = jnp.concatenate([grads[n].reshape(N_CHIPS, -1, D) for n in mat_names], axis=1)
    me = 2 * lax.axis_index("x") + lax.axis_index("y")
    own = lax.dynamic_index_in_dim(gflat, me, 0, keepdims=False)
    from_chips = _scatter_grads(gflat, name="scatter_grads")
    part = _sum4(own, from_chips, name="sum_chips")
    sib = _swap_with_sibling(part, name="swap_sibling")
    pack = lambda prefix: _pack_rows([given[prefix + n] for n in mat_names], D)
    outs_m = _adamw(jnp.stack([part, sib]), pack(""), pack("m_"), pack("v_"), name="adamw_matrices")
    vflat = _pack_rows([grads[n] for n in vec_names], D)
    packv = lambda prefix: _pack_rows([given[prefix + n] for n in vec_names], D)
    outs_v = _adamw(_gather_from_all(vflat, name="gather_ln_grads"), packv(""), packv("m_"), packv("v_"), name="adamw_vectors")

    def unpack(flat_m, flat_v):
        out, r0 = {}, 0
        for n in mat_names:
            r = given[n].size // D
            out[n] = flat_m[r0 : r0 + r].reshape(given[n].shape)
            r0 += r
        for i, n in enumerate(vec_names):
            out[n] = flat_v[i]
        return [out[n] for n in names]

    result = [loss, grad_x[None]]
    for k in range(4):
        result += unpack(outs_m[k], outs_v[k])
    return tuple(result)
```

```python
import functools

import jax
import jax.numpy as jnp
from jax import lax
from jax.experimental import pallas as pl
from jax.experimental.pallas import tpu as pltpu

F32 = jnp.float32
MXU = jnp.bfloat16

HEAD_DIM = 64
LANES = 128
CHUNK = 128
ROPE_THETA = 500000.0
ROPE_HALF = 8
DILATIONS = (1, 4, 16)
DEPTH = 2
ALPHA = (2 * DEPTH) ** 0.25
LN_EPS = 1e-5
QK_SCALE = 0.125
ADAM_LR, ADAM_B1, ADAM_B2, ADAM_EPS, ADAM_WD, ADAM_STEP = 0.001, 0.9, 0.999, 1e-08, 0.01, 10
NEG = -1e30
UNDERFLOW = -104.0
SB_TQ = 2 * CHUNK
V7X_VMEM_LIMIT = 56 * 1024 * 1024
MESH = pl.DeviceIdType.MESH
NT = (((1,), (1,)), ((), ()))
TN = (((0,), (0,)), ((), ()))
N_CHIPS = 4


def _params(*sem):
    return pltpu.CompilerParams(dimension_semantics=sem, vmem_limit_bytes=V7X_VMEM_LIMIT)


def _pick(n, cands=(1024, 768, 512, 384, 256, 128)):
    for c in cands:
        if n % c == 0:
            return c
    raise ValueError(f"no tile for {n}")


def _mm_nn(a, b, *, a_op, out_dtype, name):
    M, K = a.shape
    G, Kb, Nc = b.shape
    assert K == Kb
    N = G * Nc
    tm, tn, tk = _pick(M, (512, 256, 128)), _pick(Nc), _pick(K)
    nbs, nk = Nc // tn, K // tk

    def body(a_ref, b_ref, o_ref, acc_ref):
        kk = pl.program_id(2)
        av = a_ref[...]
        if a_op == "relu2":
            av = jnp.square(jnp.maximum(av, 0.0))
        part = jnp.dot(av.astype(MXU), b_ref[...], preferred_element_type=F32)

        @pl.when(kk == 0)
        def _():
            acc_ref[...] = part

        @pl.when(kk > 0)
        def _():
            acc_ref[...] += part

        @pl.when(kk == nk - 1)
        def _():
            o_ref[...] = acc_ref[...].astype(out_dtype)

    return pl.pallas_call(
        body,
        grid=(N // tn, M // tm, nk),
        in_specs=[
            pl.BlockSpec((tm, tk), lambda j, i, k: (i, k)),
            pl.BlockSpec((None, tk, tn), lambda j, i, k: (j // nbs, k, j % nbs)),
        ],
        out_specs=pl.BlockSpec((tm, tn), lambda j, i, k: (i, j)),
        out_shape=jax.ShapeDtypeStruct((M, N), out_dtype),
        scratch_shapes=[pltpu.VMEM((tm, tn), F32)],
        compiler_params=_params("parallel", "parallel", "arbitrary"),
        name=name,
    )(a, b)


def _mm_nt(a, b, *, epi, extra, out_dtype, name):
    M, N = a.shape
    G, K, Nc = b.shape
    assert N == G * Nc
    tm, tko, tn = _pick(M, (512, 256, 128)), _pick(K), _pick(Nc)
    nbs, nn = Nc // tn, N // tn

    def body(*refs):
        if epi is None:
            a_ref, b_ref, o_ref, acc_ref = refs
        else:
            a_ref, b_ref, e_ref, o_ref, acc_ref = refs
        kk = pl.program_id(2)
        part = lax.dot_general(a_ref[...].astype(MXU), b_ref[...], NT, preferred_element_type=F32)

        @pl.when(kk == 0)
        def _():
            acc_ref[...] = part

        @pl.when(kk > 0)
        def _():
            acc_ref[...] += part

        @pl.when(kk == nn - 1)
        def _():
            r = acc_ref[...]
            if epi == "relu2grad":
                r = r * (2.0 * jnp.maximum(e_ref[...], 0.0))
            elif epi == "residual":
                r = ALPHA * e_ref[...] + r
            o_ref[...] = r.astype(out_dtype)

    in_specs = [
        pl.BlockSpec((tm, tn), lambda i, j, k: (i, k)),
        pl.BlockSpec((None, tko, tn), lambda i, j, k: (k // nbs, j, k % nbs)),
    ]
    ops = [a, b]
    if epi is not None:
        in_specs.append(pl.BlockSpec((tm, tko), lambda i, j, k: (i, j)))
        ops.append(extra)
    return pl.pallas_call(
        body,
        grid=(M // tm, K // tko, nn),
        in_specs=in_specs,
        out_specs=pl.BlockSpec((tm, tko), lambda i, j, k: (i, j)),
        out_shape=jax.ShapeDtypeStruct((M, K), out_dtype),
        scratch_shapes=[pltpu.VMEM((tm, tko), F32)],
        compiler_params=_params("parallel", "parallel", "arbitrary"),
        name=name,
    )(*ops)


def _mm_tn(a, b, *, a_op, groups, name):
    M, Ka = a.shape
    Mb, N = b.shape
    assert M == Mb
    Nc = N // groups
    tka, tn, tm = _pick(Ka, (512, 256, 128)), _pick(Nc), _pick(M, (512, 256, 128))
    nbs, nm = Nc // tn, M // tm

    def body(a_ref, b_ref, o_ref, acc_ref):
        kk = pl.program_id(2)
        av = a_ref[...]
        if a_op == "relu2":
            av = jnp.square(jnp.maximum(av, 0.0))
        part = lax.dot_general(av.astype(MXU), b_ref[...].astype(MXU), TN, preferred_element_type=F32)

        @pl.when(kk == 0)
        def _():
            acc_ref[...] = part

        @pl.when(kk > 0)
        def _():
            acc_ref[...] += part

        @pl.when(kk == nm - 1)
        def _():
            o_ref[...] = acc_ref[...]

    return pl.pallas_call(
        body,
        grid=(Ka // tka, N // tn, nm),
        in_specs=[
            pl.BlockSpec((tm, tka), lambda i, j, k: (k, i)),
            pl.BlockSpec((tm, tn), lambda i, j, k: (k, j)),
        ],
        out_specs=pl.BlockSpec((None, tka, tn), lambda i, j, k: (j // nbs, i, j % nbs)),
        out_shape=jax.ShapeDtypeStruct((groups, Ka, Nc), F32),
        scratch_shapes=[pltpu.VMEM((tka, tn), F32)],
        compiler_params=_params("parallel", "parallel", "arbitrary"),
        name=name,
    )(a, b)


def _ln_stats(y):
    mu = jnp.mean(y, axis=-1, keepdims=True)
    yc = y - mu
    var = jnp.mean(yc * yc, axis=-1, keepdims=True)
    rstd = lax.rsqrt(var + LN_EPS)
    return yc * rstd, rstd


def _ln_fwd(x, mix, g, b, *, name):
    S, D = x.shape
    tm = _pick(S, (256, 128))

    def body(x_ref, m_ref, g_ref, b_ref, y_ref, o_ref):
        y = ALPHA * x_ref[...] + m_ref[...]
        xhat, _ = _ln_stats(y)
        y_ref[...] = y
        o_ref[...] = xhat * g_ref[...] + b_ref[...]

    row = pl.BlockSpec((tm, D), lambda i: (i, 0))
    vec = pl.BlockSpec((1, D), lambda i: (0, 0))
    return pl.pallas_call(
        body,
        grid=(S // tm,),
        in_specs=[row, row, vec, vec],
        out_specs=[row, row],
        out_shape=[jax.ShapeDtypeStruct((S, D), F32)] * 2,
        compiler_params=_params("parallel"),
        name=name,
    )(x, mix, g.reshape(1, D), b.reshape(1, D))


def _ln_bwd(dx, y, g, *, name):
    S, D = y.shape
    tm = _pick(S, (256, 128))

    def body(dx_ref, y_ref, g_ref, dy_ref, dg_ref, db_ref):
        xhat, rstd = _ln_stats(y_ref[...])
        dx_v = dx_ref[...]
        dxh = dx_v * g_ref[...]
        m1 = jnp.mean(dxh, axis=-1, keepdims=True)
        m2 = jnp.mean(dxh * xhat, axis=-1, keepdims=True)
        dy_ref[...] = rstd * (dxh - m1 - xhat * m2)

        @pl.when(pl.program_id(0) == 0)
        def _():
            dg_ref[...] = jnp.zeros_like(dg_ref)
            db_ref[...] = jnp.zeros_like(db_ref)

        dg_ref[...] += jnp.sum(dx_v * xhat, axis=0, keepdims=True)
        db_ref[...] += jnp.sum(dx_v, axis=0, keepdims=True)

    row = pl.BlockSpec((tm, D), lambda i: (i, 0))
    vec = pl.BlockSpec((1, D), lambda i: (0, 0))
    return pl.pallas_call(
        body,
        grid=(S // tm,),
        in_specs=[row, row, vec],
        out_specs=[row, vec, vec],
        out_shape=[jax.ShapeDtypeStruct((S, D), F32)] + [jax.ShapeDtypeStruct((1, D), F32)] * 2,
        compiler_params=_params("arbitrary"),
        name=name,
    )(dx, y, g.reshape(1, D))


def _ln_bwd_loss(y, g, b, target, *, name):
    S, D = y.shape
    tm = _pick(S, (256, 128))

    def body(y_ref, g_ref, b_ref, t_ref, dy_ref, dg_ref, db_ref, ls_ref):
        xhat, rstd = _ln_stats(y_ref[...])
        err = xhat * g_ref[...] + b_ref[...] - t_ref[...]
        dx_v = err * (1.0 / D)
        dxh = dx_v * g_ref[...]
        m1 = jnp.mean(dxh, axis=-1, keepdims=True)
        m2 = jnp.mean(dxh * xhat, axis=-1, keepdims=True)
        dy_ref[...] = rstd * (dxh - m1 - xhat * m2)

        @pl.when(pl.program_id(0) == 0)
        def _():
            dg_ref[...] = jnp.zeros_like(dg_ref)
            db_ref[...] = jnp.zeros_like(db_ref)
            ls_ref[...] = jnp.zeros_like(ls_ref)

        dg_ref[...] += jnp.sum(dx_v * xhat, axis=0, keepdims=True)
        db_ref[...] += jnp.sum(dx_v, axis=0, keepdims=True)
        ls_ref[...] += jnp.sum(err * err, axis=0, keepdims=True)

    row = pl.BlockSpec((tm, D), lambda i: (i, 0))
    vec = pl.BlockSpec((1, D), lambda i: (0, 0))
    return pl.pallas_call(
        body,
        grid=(S // tm,),
        in_specs=[row, vec, vec, row],
        out_specs=[row, vec, vec, vec],
        out_shape=[jax.ShapeDtypeStruct((S, D), F32)] + [jax.ShapeDtypeStruct((1, D), F32)] * 3,
        compiler_params=_params("arbitrary"),
        name=name,
    )(y, g.reshape(1, D), b.reshape(1, D), target)


def _rope_tables(S):
    inv_freq = ROPE_THETA ** (-jnp.arange(ROPE_HALF, dtype=F32) / ROPE_HALF)
    ang = jnp.arange(S, dtype=jnp.int32).astype(F32)[:, None] * inv_freq[None, :]
    cos, sin = jnp.cos(ang), jnp.sin(ang)
    ones = jnp.ones((S, HEAD_DIM - 2 * ROPE_HALF), F32)
    zeros8 = jnp.zeros((S, ROPE_HALF), F32)
    c = jnp.concatenate([cos, cos, ones], axis=1)
    s1 = jnp.concatenate([zeros8, sin, 0.0 * ones], axis=1)
    s2 = jnp.concatenate([-sin, zeros8, 0.0 * ones], axis=1)
    return tuple(jnp.concatenate([t, t], axis=1) for t in (c, s1, s2))


def _rope(xb, c, s1, s2):
    return xb * c + pltpu.roll(xb, ROPE_HALF, 1) * s1 + pltpu.roll(xb, LANES - ROPE_HALF, 1) * s2


def _rope_t(dyb, c, s1, s2):
    return dyb * c + pltpu.roll(dyb * s1, LANES - ROPE_HALF, 1) + pltpu.roll(dyb * s2, ROPE_HALF, 1)


def _qkv_post(qkv, tables, *, name):
    S, D3 = qkv.shape
    D = D3 // 3
    tm = _pick(S, (256, 128))
    rope = tables is not None

    def body(*refs):
        if rope:
            x_ref, c_ref, s1_ref, s2_ref, o_ref = refs
            tabs = (c_ref[...], s1_ref[...], s2_ref[...])
        else:
            x_ref, o_ref = refs
        j = pl.program_id(1)

        def write(rotate, scale):
            for blk in range(D // LANES):
                cols = slice(blk * LANES, (blk + 1) * LANES)
                xb = x_ref[:, cols]
                if rotate:
                    xb = _rope(xb, *tabs)
                if scale:
                    xb = xb * QK_SCALE
                o_ref[:, cols] = xb.astype(MXU)

        pl.when(j == 0)(lambda: write(rope, True))
        pl.when(j == 1)(lambda: write(rope, False))
        pl.when(j == 2)(lambda: write(False, False))

    tab = pl.BlockSpec((tm, LANES), lambda i, j: (i, 0))
    return pl.pallas_call(
        body,
        grid=(S // tm, 3),
        in_specs=[pl.BlockSpec((tm, D), lambda i, j: (i, j))] + ([tab] * 3 if rope else []),
        out_specs=pl.BlockSpec((None, tm, D), lambda i, j: (j, i, 0)),
        out_shape=jax.ShapeDtypeStruct((3, S, D), MXU),
        compiler_params=_params("parallel", "parallel"),
        name=name,
    )(qkv, *(tables if rope else ()))


def _qkv_bwd_post(dqs, dks, dvs, tables, *, name):
    S, D = dqs[0].shape
    tm = _pick(S, (256, 128))
    nb = len(dqs)
    rope = tables is not None

    def body(*refs):
        ins, o_ref = refs[:-1], refs[-1]
        groups = [ins[0:nb], ins[nb : 2 * nb], ins[2 * nb : 3 * nb]]
        tabs = [r[...] for r in ins[3 * nb :]]
        for j, grp in enumerate(groups):
            for blk in range(D // LANES):
                cols = slice(blk * LANES, (blk + 1) * LANES)
                v = grp[0][:, cols]
                for r in grp[1:]:
                    v = v + r[:, cols]
                if rope and j < 2:
                    v = _rope_t(v, *tabs)
                if j == 0:
                    v = v * QK_SCALE
                o_ref[:, j * D + blk * LANES : j * D + (blk + 1) * LANES] = v.astype(MXU)

    row = pl.BlockSpec((tm, D), lambda i: (i, 0))
    tab = pl.BlockSpec((tm, LANES), lambda i: (i, 0))
    return pl.pallas_call(
        body,
        grid=(S // tm,),
        in_specs=[row] * (3 * nb) + ([tab] * 3 if rope else []),
        out_specs=pl.BlockSpec((tm, 3 * D), lambda i: (i, 0)),
        out_shape=jax.ShapeDtypeStruct((S, 3 * D), MXU),
        compiler_params=_params("parallel"),
        name=name,
    )(*dqs, *dks, *dvs, *(tables if rope else ()))


def _log_fail(z):
    return -(jnp.maximum(z, 0.0) + jnp.log1p(jnp.exp(-jnp.abs(z))))


def _sb_fwd(q, k, v, *, name):
    S, D = q.shape
    HP, H = D // LANES, D // HEAD_DIM
    tq = SB_TQ
    assert S % tq == 0
    cpb = tq // CHUNK
    assert S // CHUNK <= LANES

    def body(q_ref, k_ref, v_ref, o_ref, car_ref, acc_ref, carry0_ref, carry1_ref):
        i = pl.program_id(1)
        lane = lax.broadcasted_iota(jnp.int32, (tq, LANES), 1)
        row = lax.broadcasted_iota(jnp.int32, (tq, LANES), 0)
        kl = lax.broadcasted_iota(jnp.int32, (CHUNK, LANES), 1)
        kr = lax.broadcasted_iota(jnp.int32, (CHUNK, LANES), 0)
        tri = (kr >= kl).astype(MXU)
        qv = q_ref[...]
        qh = (jnp.where(lane < HEAD_DIM, qv, jnp.zeros_like(qv)), jnp.where(lane >= HEAD_DIM, qv, jnp.zeros_like(qv)))
        carry_refs = (carry0_ref, carry1_ref)
        acc_ref[...] = jnp.zeros_like(acc_ref)
        car_ref[...] = jnp.full((2, tq, LANES), NEG, F32)
        for r in carry_refs:
            r[...] = jnp.zeros_like(r)

        def run(chunks, masked):
            offs = [pl.multiple_of(c * CHUNK, CHUNK) for c in chunks]
            kcs = [k_ref[pl.ds(o, CHUNK), :] for o in offs]
            vcs = [v_ref[pl.ds(o, CHUNK), :] for o in offs]
            jobs = [(n, hh) for n in range(len(chunks)) for hh in range(2)]
            zs = {j: lax.dot_general(qh[j[1]], kcs[j[0]], NT, preferred_element_type=F32) for j in jobs}
            if masked:
                cms = [(o + lane) < (i * tq + row) for o in offs]
            lfs = {}
            for j in jobs:
                lf = _log_fail(zs[j])
                lfs[j] = jnp.where(cms[j[0]], lf, 0.0) if masked else lf
            rs = {j: jnp.dot(lfs[j].astype(MXU), tri, preferred_element_type=F32) for j in jobs}
            ws = {}
            for hh in range(2):
                carry = carry_refs[hh][...]
                tile = car_ref[hh]
                for n, c in enumerate(chunks):
                    w = jnp.exp(zs[n, hh] + rs[n, hh] + carry)
                    if masked:
                        w = jnp.where(cms[n], w, 0.0)
                    ws[n, hh] = w.astype(MXU)
                    tile = jnp.where(lane == c, carry, tile)
                    carry = carry + jnp.sum(lfs[n, hh], axis=-1, keepdims=True)
                car_ref[hh] = tile
                carry_refs[hh][...] = carry
            for hh in range(2):
                pv = jnp.dot(ws[0, hh], vcs[0], preferred_element_type=F32)
                for n in range(1, len(chunks)):
                    pv = pv + jnp.dot(ws[n, hh], vcs[n], preferred_element_type=F32)
                acc_ref[hh] += pv

        def max_carry():
            return jnp.maximum(jnp.max(carry0_ref[...]), jnp.max(carry1_ref[...]))

        run([i * cpb + u for u in reversed(range(cpb))], True)

        def cond(st):
            return jnp.logical_and(st[0] >= 0, st[1] >= UNDERFLOW)

        def pair(st):
            c = st[0]
            run([c, c - 1], False)
            return c - 2, max_carry()

        lax.while_loop(cond, pair, (i * cpb - 1, max_carry()))
        o_ref[...] = jnp.where(lane < HEAD_DIM, acc_ref[0], acc_ref[1]).astype(o_ref.dtype)

    blk = pl.BlockSpec((tq, LANES), lambda h, i: (i, h))
    full = pl.BlockSpec((S, LANES), lambda h, i: (0, h))
    return pl.pallas_call(
        body,
        grid=(HP, S // tq),
        in_specs=[blk, full, full],
        out_specs=[blk, pl.BlockSpec((2, tq, LANES), lambda h, i: (h, i, 0))],
        out_shape=[jax.ShapeDtypeStruct((S, D), MXU), jax.ShapeDtypeStruct((H, S, LANES), F32)],
        scratch_shapes=[pltpu.VMEM((2, tq, LANES), F32), pltpu.VMEM((tq, 1), F32), pltpu.VMEM((tq, 1), F32)],
        compiler_params=_params("parallel", "arbitrary"),
        name=name,
    )(q, k, v)


def _sb_bwd(q, k, v, do, car, *, name):
    S, D = q.shape
    HP = D // LANES
    tq = SB_TQ
    assert S % tq == 0
    cpb = tq // CHUNK
    nq = S // tq

    def body(q_ref, k_ref, v_ref, do_ref, car_ref, dq_ref, dk_hbm, dv_hbm, dq_acc, dk_acc, dv_acc, gcar0_ref, gcar1_ref, sem):
        hp = pl.program_id(0)
        i = pl.program_id(1)
        lane = lax.broadcasted_iota(jnp.int32, (tq, LANES), 1)
        row = lax.broadcasted_iota(jnp.int32, (tq, LANES), 0)
        kl = lax.broadcasted_iota(jnp.int32, (CHUNK, LANES), 1)
        kr = lax.broadcasted_iota(jnp.int32, (CHUNK, LANES), 0)
        tri = (kr >= kl).astype(MXU)
        tri_prefix = (kr <= kl).astype(MXU)

        @pl.when(i == 0)
        def _():
            dk_acc[...] = jnp.zeros_like(dk_acc)
            dv_acc[...] = jnp.zeros_like(dv_acc)

        qv = q_ref[...]
        dov = do_ref[...]
        qh = (jnp.where(lane < HEAD_DIM, qv, jnp.zeros_like(qv)), jnp.where(lane >= HEAD_DIM, qv, jnp.zeros_like(qv)))
        doh = (jnp.where(lane < HEAD_DIM, dov, jnp.zeros_like(dov)), jnp.where(lane >= HEAD_DIM, dov, jnp.zeros_like(dov)))
        gcar_refs = (gcar0_ref, gcar1_ref)
        dq_acc[...] = jnp.zeros_like(dq_acc)
        for r in gcar_refs:
            r[...] = jnp.zeros_like(r)

        def step(c, masked):
            off = pl.multiple_of(c * CHUNK, CHUNK)
            kc = k_ref[pl.ds(off, CHUNK), :]
            vc = v_ref[pl.ds(off, CHUNK), :]
            if masked:
                cm = (off + lane) < (i * tq + row)
            heads = range(2)
            zs = [lax.dot_general(qh[hh], kc, NT, preferred_element_type=F32) for hh in heads]
            dws = [lax.dot_general(doh[hh], vc, NT, preferred_element_type=F32) for hh in heads]
            lfs, sigs = [], []
            for hh in heads:
                lf = _log_fail(zs[hh])
                sigs.append(jnp.exp(zs[hh] + lf))
                lfs.append(jnp.where(cm, lf, 0.0) if masked else lf)
            rs = [jnp.dot(lfs[hh].astype(MXU), tri, preferred_element_type=F32) for hh in heads]
            ws, gs = [], []
            for hh in heads:
                carry = jnp.sum(jnp.where(lane == c, car_ref[hh], 0.0), axis=-1, keepdims=True)
                w = jnp.exp(zs[hh] + rs[hh] + carry)
                if masked:
                    w = jnp.where(cm, w, 0.0)
                ws.append(w.astype(MXU))
                gs.append(w * dws[hh])
            big_gs = [jnp.dot(gs[hh].astype(MXU), tri_prefix, preferred_element_type=F32) for hh in heads]
            dzs = []
            for hh in heads:
                dz = gs[hh] - sigs[hh] * (big_gs[hh] + gcar_refs[hh][...])
                if masked:
                    dz = jnp.where(cm, dz, 0.0)
                gcar_refs[hh][...] += jnp.sum(gs[hh], axis=-1, keepdims=True)
                dzs.append(dz.astype(MXU))
            for hh in heads:
                dq_acc[hh] += jnp.dot(dzs[hh], kc, preferred_element_type=F32)
            dk_acc[pl.ds(off, CHUNK), :] += lax.dot_general(dzs[0], qh[0], TN, preferred_element_type=F32) + lax.dot_general(
                dzs[1], qh[1], TN, preferred_element_type=F32
            )
            dv_acc[pl.ds(off, CHUNK), :] += lax.dot_general(ws[0], doh[0], TN, preferred_element_type=F32) + lax.dot_general(
                ws[1], doh[1], TN, preferred_element_type=F32
            )

        colmax = jnp.maximum(jnp.max(car_ref[0], axis=0, keepdims=True), jnp.max(car_ref[1], axis=0, keepdims=True))
        lane1 = lax.broadcasted_iota(jnp.int32, (1, LANES), 1)
        dead = jnp.logical_and(colmax < UNDERFLOW, lane1 < i * cpb)
        first = jnp.sum(jnp.where(dead, 1.0, 0.0)).astype(jnp.int32)

        def loop_body(c, carry):
            step(c, False)
            return carry

        lax.fori_loop(first, i * cpb, loop_body, 0)
        for u in range(cpb):
            step(i * cpb + u, True)
        dq_ref[...] = jnp.where(lane < HEAD_DIM, dq_acc[0], dq_acc[1])

        @pl.when(i == nq - 1)
        def _():
            cols = pl.ds(pl.multiple_of(hp * LANES, LANES), LANES)
            for src, dst in ((dk_acc, dk_hbm), (dv_acc, dv_hbm)):
                cp = pltpu.make_async_copy(src, dst.at[:, cols], sem)
                cp.start()
                cp.wait()

    blk = pl.BlockSpec((tq, LANES), lambda h, i: (i, h))
    full = pl.BlockSpec((S, LANES), lambda h, i: (0, h))
    hbm = pl.BlockSpec(memory_space=pl.ANY)
    return pl.pallas_call(
        body,
        grid=(HP, nq),
        in_specs=[blk, full, full, blk, pl.BlockSpec((2, tq, LANES), lambda h, i: (h, i, 0))],
        out_specs=[blk, hbm, hbm],
        out_shape=[jax.ShapeDtypeStruct((S, D), F32)] * 3,
        scratch_shapes=[
            pltpu.VMEM((2, tq, LANES), F32),
            pltpu.VMEM((S, LANES), F32),
            pltpu.VMEM((S, LANES), F32),
            pltpu.VMEM((tq, 1), F32),
            pltpu.VMEM((tq, 1), F32),
            pltpu.SemaphoreType.DMA,
        ],
        compiler_params=_params("arbitrary", "arbitrary"),
        name=name,
    )(q, k, v, do, car)


def _perm(x, d):
    if d == 1:
        return x
    S, D = x.shape
    return x.reshape(S // d, d, D).transpose(1, 0, 2).reshape(S, D)


def _unperm(x, d):
    if d == 1:
        return x
    S, D = x.shape
    return x.reshape(d, S // d, D).transpose(1, 0, 2).reshape(S, D)


def _perm3(x):
    return jnp.stack([_perm(x, d) for d in DILATIONS])


def _dil_tb(S):
    return min(512, S // DILATIONS[-1])


def _dil_masks():
    qi = lax.broadcasted_iota(jnp.int32, (CHUNK, CHUNK), 0)
    kj = lax.broadcasted_iota(jnp.int32, (CHUNK, CHUNK), 1)
    return qi, kj


def _dil_fwd(q3, k3, v3, *, name):
    _, S, D = q3.shape
    HP = D // LANES
    tb = _dil_tb(S)
    nb, nsub = S // tb, tb // CHUNK

    def body(q_ref, k_ref, v_ref, kp_ref, vp_ref, o_ref, l_ref):
        b = pl.program_id(0)
        n = pl.program_id(2)
        bps = jnp.right_shift(nb, 2 * b)
        first = jnp.bitwise_and(n, bps - 1) == 0
        qi, kj = _dil_masks()
        hl = lax.broadcasted_iota(jnp.int32, (CHUNK, LANES), 1)
        m_cur = kj <= qi
        m_prev = (kj - qi) >= 0
        m_prev0 = (kj - qi) >= jnp.where(first, 2 * CHUNK, 0)
        hms = (hl < HEAD_DIM, hl >= HEAD_DIM)
        tiles, scores = [], {}
        for u in range(nsub):
            rows = slice(u * CHUNK, (u + 1) * CHUNK)
            qs, kc, vc = q_ref[rows, :], k_ref[rows, :], v_ref[rows, :]
            if u == 0:
                kp, vp, pm = kp_ref[...], vp_ref[...], m_prev0
            else:
                prev = slice((u - 1) * CHUNK, u * CHUNK)
                kp, vp, pm = k_ref[prev, :], v_ref[prev, :], m_prev
            tiles.append((rows, vp, vc, pm))
            for hh in range(2):
                qh = jnp.where(hms[hh], qs, jnp.zeros_like(qs))
                scores[u, hh] = (
                    lax.dot_general(qh, kp, NT, preferred_element_type=F32),
                    lax.dot_general(qh, kc, NT, preferred_element_type=F32),
                )
        probs = {}
        for (u, hh), (s_p, s_c) in scores.items():
            s_p = jnp.where(tiles[u][3], s_p, NEG)
            s_c = jnp.where(m_cur, s_c, NEG)
            m = jnp.max(jnp.maximum(s_p, s_c), axis=-1, keepdims=True)
            p_p, p_c = jnp.exp(s_p - m), jnp.exp(s_c - m)
            den = jnp.sum(p_p + p_c, axis=-1, keepdims=True)
            probs[u, hh] = (p_p.astype(MXU), p_c.astype(MXU), 1.0 / den, m + jnp.log(den))
        for u, (rows, vp, vc, _) in enumerate(tiles):
            o_t = l_t = None
            for hh in range(2):
                p_p, p_c, inv, lse = probs[u, hh]
                num = jnp.dot(p_p, vp, preferred_element_type=F32) + jnp.dot(p_c, vc, preferred_element_type=F32)
                o_h, l_h = num * inv, jnp.broadcast_to(lse, (CHUNK, LANES))
                o_t = o_h if hh == 0 else jnp.where(hms[1], o_h, o_t)
                l_t = l_h if hh == 0 else jnp.where(hms[1], l_h, l_t)
            o_ref[rows, :] = o_t
            l_ref[rows, :] = l_t

    blk = pl.BlockSpec((None, tb, LANES), lambda b, h, n: (b, n, h))
    tail = pl.BlockSpec((None, CHUNK, LANES), lambda b, h, n: (b, jnp.maximum(n * nsub - 1, 0), h))
    return pl.pallas_call(
        body,
        grid=(3, HP, nb),
        in_specs=[blk, blk, blk, tail, tail],
        out_specs=[blk, blk],
        out_shape=[jax.ShapeDtypeStruct((3, S, D), F32)] * 2,
        compiler_params=_params("parallel", "parallel", "parallel"),
        name=name,
    )(q3, k3, v3, k3, v3)


def _dil_combine(o3, l3, *, name):
    _, S, D = o3.shape
    tm = _pick(S, (256, 128))

    def body(o_ref, l_ref, out_ref, lse_ref):
        l0, l1, l2 = l_ref[0], l_ref[1], l_ref[2]
        m = jnp.maximum(jnp.maximum(l0, l1), l2)
        a0, a1, a2 = jnp.exp(l0 - m), jnp.exp(l1 - m), jnp.exp(l2 - m)
        den = a0 + a1 + a2
        out_ref[...] = (a0 * o_ref[0] + a1 * o_ref[1] + a2 * o_ref[2]) / den
        lse_ref[...] = m + jnp.log(den)

    blk3 = pl.BlockSpec((3, tm, D), lambda i: (0, i, 0))
    row = pl.BlockSpec((tm, D), lambda i: (i, 0))
    return pl.pallas_call(
        body,
        grid=(S // tm,),
        in_specs=[blk3, blk3],
        out_specs=[row, row],
        out_shape=[jax.ShapeDtypeStruct((S, D), F32)] * 2,
        compiler_params=_params("parallel"),
        name=name,
    )(o3, l3)


def _head_col(tile, hl, hh):
    return jnp.sum(jnp.where(hl == hh * HEAD_DIM, tile, 0.0), axis=-1, keepdims=True)


def _dil_bwd_dq(q3, k3, v3, do3, out3, lse3, *, name):
    _, S, D = q3.shape
    HP = D // LANES
    tb = _dil_tb(S)
    nb, nsub = S // tb, tb // CHUNK

    def body(q_ref, k_ref, v_ref, kp_ref, vp_ref, do_ref, out_ref, lse_ref, dq_ref):
        b = pl.program_id(0)
        n = pl.program_id(2)
        bps = jnp.right_shift(nb, 2 * b)
        first = jnp.bitwise_and(n, bps - 1) == 0
        qi, kj = _dil_masks()
        hl = lax.broadcasted_iota(jnp.int32, (CHUNK, LANES), 1)
        m_cur = kj <= qi
        m_prev = (kj - qi) >= 0
        m_prev0 = (kj - qi) >= jnp.where(first, 2 * CHUNK, 0)
        hms = (hl < HEAD_DIM, hl >= HEAD_DIM)
        jobs, cols = [], {}
        for u in range(nsub):
            rows = slice(u * CHUNK, (u + 1) * CHUNK)
            qs, kc, vc = q_ref[rows, :], k_ref[rows, :], v_ref[rows, :]
            if u == 0:
                kp, vp, pm = kp_ref[...], vp_ref[...], m_prev0
            else:
                prev = slice((u - 1) * CHUNK, u * CHUNK)
                kp, vp, pm = k_ref[prev, :], v_ref[prev, :], m_prev
            do_t, out_t, lse_t = do_ref[rows, :], out_ref[rows, :], lse_ref[rows, :]
            for hh in range(2):
                qh = jnp.where(hms[hh], qs, jnp.zeros_like(qs))
                doh = jnp.where(hms[hh], do_t, 0.0)
                cols[u, hh] = (jnp.sum(doh * out_t, axis=-1, keepdims=True), _head_col(lse_t, hl, hh))
                dob = doh.astype(MXU)
                for kk, vv, msk in ((kp, vp, pm), (kc, vc, m_cur)):
                    s = lax.dot_general(qh, kk, NT, preferred_element_type=F32)
                    dp = lax.dot_general(dob, vv, NT, preferred_element_type=F32)
                    jobs.append((u, hh, s, dp, kk, msk))
        dss = []
        for u, hh, s, dp, kk, msk in jobs:
            delta, lse = cols[u, hh]
            p = jnp.exp(jnp.where(msk, s, NEG) - lse)
            dss.append((p * (dp - delta)).astype(MXU))
        dq_h = {}
        for (u, hh, _, _, kk, _), ds in zip(jobs, dss):
            part = jnp.dot(ds, kk, preferred_element_type=F32)
            dq_h[u, hh] = part if (u, hh) not in dq_h else dq_h[u, hh] + part
        for u in range(nsub):
            dq_ref[u * CHUNK : (u + 1) * CHUNK, :] = jnp.where(hms[0], dq_h[u, 0], dq_h[u, 1])

    blk = pl.BlockSpec((None, tb, LANES), lambda b, h, n: (b, n, h))
    tail = pl.BlockSpec((None, CHUNK, LANES), lambda b, h, n: (b, jnp.maximum(n * nsub - 1, 0), h))
    return pl.pallas_call(
        body,
        grid=(3, HP, nb),
        in_specs=[blk, blk, blk, tail, tail, blk, blk, blk],
        out_specs=blk,
        out_shape=jax.ShapeDtypeStruct((3, S, D), F32),
        compiler_params=_params("parallel", "parallel", "parallel"),
        name=name,
    )(q3, k3, v3, k3, v3, do3, out3, lse3)


def _dil_bwd_dkv(q3, k3, v3, do3, out3, lse3, *, name):
    _, S, D = q3.shape
    HP = D // LANES
    tb = _dil_tb(S)
    nb, nsub = S // tb, tb // CHUNK
    last_chunk = S // CHUNK - 1

    def body(q_ref, k_ref, v_ref, do_ref, out_ref, lse_ref, qn_ref, don_ref, outn_ref, lsen_ref, dk_ref, dv_ref):
        b = pl.program_id(0)
        n = pl.program_id(2)
        bps = jnp.right_shift(nb, 2 * b)
        last = jnp.bitwise_and(n + 1, bps - 1) == 0
        qi, kj = _dil_masks()
        hl = lax.broadcasted_iota(jnp.int32, (CHUNK, LANES), 1)
        m_cur = kj <= qi
        m_next = (kj - qi) >= 0
        m_next_last = (kj - qi) >= jnp.where(last, 2 * CHUNK, 0)
        hms = (hl < HEAD_DIM, hl >= HEAD_DIM)
        qtiles = {}
        for t in range(nsub + 1):
            if t == nsub:
                qs, do_t, out_t, lse_t = qn_ref[...], don_ref[...], outn_ref[...], lsen_ref[...]
            else:
                rows = slice(t * CHUNK, (t + 1) * CHUNK)
                qs, do_t, out_t, lse_t = q_ref[rows, :], do_ref[rows, :], out_ref[rows, :], lse_ref[rows, :]
            for hh in range(2):
                doh = jnp.where(hms[hh], do_t, 0.0)
                qtiles[t, hh] = (
                    jnp.where(hms[hh], qs, jnp.zeros_like(qs)),
                    doh.astype(MXU),
                    jnp.sum(doh * out_t, axis=-1, keepdims=True),
                    _head_col(lse_t, hl, hh),
                )
        jobs = []
        for u in range(nsub):
            rows = slice(u * CHUNK, (u + 1) * CHUNK)
            kc, vc = k_ref[rows, :], v_ref[rows, :]
            for t, msk in ((u, m_cur), (u + 1, m_next_last if u == nsub - 1 else m_next)):
                for hh in range(2):
                    qh, dob, _, _ = qtiles[t, hh]
                    s = lax.dot_general(qh, kc, NT, preferred_element_type=F32)
                    dp = lax.dot_general(dob, vc, NT, preferred_element_type=F32)
                    jobs.append((u, t, hh, s, dp, msk))
        pds = []
        for u, t, hh, s, dp, msk in jobs:
            _, _, delta, lse = qtiles[t, hh]
            p = jnp.exp(jnp.where(msk, s, NEG) - lse)
            pds.append((p.astype(MXU), (p * (dp - delta)).astype(MXU)))
        dks, dvs = {}, {}
        for (u, t, hh, _, _, _), (pb, dsb) in zip(jobs, pds):
            qh, dob, _, _ = qtiles[t, hh]
            dv_p = lax.dot_general(pb, dob, TN, preferred_element_type=F32)
            dk_p = lax.dot_general(dsb, qh, TN, preferred_element_type=F32)
            dvs[u] = dv_p if u not in dvs else dvs[u] + dv_p
            dks[u] = dk_p if u not in dks else dks[u] + dk_p
        for u in range(nsub):
            dk_ref[u * CHUNK : (u + 1) * CHUNK, :] = dks[u]
            dv_ref[u * CHUNK : (u + 1) * CHUNK, :] = dvs[u]

    blk = pl.BlockSpec((None, tb, LANES), lambda b, h, n: (b, n, h))
    head = pl.BlockSpec((None, CHUNK, LANES), lambda b, h, n: (b, jnp.minimum((n + 1) * nsub, last_chunk), h))
    return pl.pallas_call(
        body,
        grid=(3, HP, nb),
        in_specs=[blk] * 6 + [head] * 4,
        out_specs=[blk, blk],
        out_shape=[jax.ShapeDtypeStruct((3, S, D), F32)] * 2,
        compiler_params=_params("parallel", "parallel", "parallel"),
        name=name,
    )(q3, k3, v3, do3, out3, lse3, q3, do3, out3, lse3)


def _chip_peers():
    x, y, c = lax.axis_index("x"), lax.axis_index("y"), lax.axis_index("c")
    peers = [(1 - x, y, c), (x, 1 - y, c), (1 - x, 1 - y, c)]
    return x, y, c, peers


def _allgather_weights(wsh, *, name):
    R, D = wsh.shape

    def body(w_ref, out_ref, send_sems, recv_sems, local_sem):
        x, y, c, peers = _chip_peers()
        me = 2 * x + y
        mine = pltpu.make_async_copy(w_ref, out_ref.at[me], local_sem)
        mine.start()
        sends = [
            pltpu.make_async_remote_copy(
                src_ref=w_ref, dst_ref=out_ref.at[me], send_sem=send_sems.at[j], recv_sem=recv_sems.at[j],
                device_id=p, device_id_type=MESH,
            )
            for j, p in enumerate(peers)
        ]
        for cp in sends:
            cp.start()
        for j, p in enumerate(peers):
            theirs = out_ref.at[2 * p[0] + p[1]]
            pltpu.make_async_remote_copy(
                src_ref=theirs, dst_ref=theirs, send_sem=send_sems.at[j], recv_sem=recv_sems.at[j],
                device_id=p, device_id_type=MESH,
            ).wait_recv()
        for cp in sends:
            cp.wait_send()
        mine.wait()

    hbm = pl.BlockSpec(memory_space=pl.ANY)
    return pl.pallas_call(
        body,
        in_specs=[hbm],
        out_specs=hbm,
        out_shape=jax.ShapeDtypeStruct((N_CHIPS, R, D), wsh.dtype),
        scratch_shapes=[pltpu.SemaphoreType.DMA((3,)), pltpu.SemaphoreType.DMA((3,)), pltpu.SemaphoreType.DMA],
        name=name,
    )(wsh)


def _scatter_grads(gflat, *, name):
    _, R, D = gflat.shape

    def body(g_ref, out_ref, send_sems, recv_sems):
        x, y, c, peers = _chip_peers()
        sends = [
            pltpu.make_async_remote_copy(
                src_ref=g_ref.at[2 * p[0] + p[1]], dst_ref=out_ref.at[j], send_sem=send_sems.at[j],
                recv_sem=recv_sems.at[j], device_id=p, device_id_type=MESH,
            )
            for j, p in enumerate(peers)
        ]
        for cp in sends:
            cp.start()
        for j, p in enumerate(peers):
            pltpu.make_async_remote_copy(
                src_ref=out_ref.at[j], dst_ref=out_ref.at[j], send_sem=send_sems.at[j], recv_sem=recv_sems.at[j],
                device_id=p, device_id_type=MESH,
            ).wait_recv()
        for cp in sends:
            cp.wait_send()

    hbm = pl.BlockSpec(memory_space=pl.ANY)
    return pl.pallas_call(
        body,
        in_specs=[hbm],
        out_specs=hbm,
        out_shape=jax.ShapeDtypeStruct((3, R, D), gflat.dtype),
        scratch_shapes=[pltpu.SemaphoreType.DMA((3,)), pltpu.SemaphoreType.DMA((3,))],
        name=name,
    )(gflat)


def _swap_with_sibling(part, *, name):
    def body(p_ref, out_ref, send_sem, recv_sem):
        x, y, c = lax.axis_index("x"), lax.axis_index("y"), lax.axis_index("c")
        cp = pltpu.make_async_remote_copy(
            src_ref=p_ref, dst_ref=out_ref, send_sem=send_sem, recv_sem=recv_sem,
            device_id=(x, y, 1 - c), device_id_type=MESH,
        )
        cp.start()
        cp.wait()

    hbm = pl.BlockSpec(memory_space=pl.ANY)
    return pl.pallas_call(
        body,
        in_specs=[hbm],
        out_specs=hbm,
        out_shape=jax.ShapeDtypeStruct(part.shape, part.dtype),
        scratch_shapes=[pltpu.SemaphoreType.DMA, pltpu.SemaphoreType.DMA],
        name=name,
    )(part)


def _gather_from_all(vec, *, name):
    R, D = vec.shape

    def body(v_ref, out_ref, send_sems, recv_sems):
        x, y, c = lax.axis_index("x"), lax.axis_index("y"), lax.axis_index("c")
        me = 4 * x + 2 * y + c
        out_ref[me] = v_ref[...]
        rel = [(j >> 2 & 1, j >> 1 & 1, j & 1) for j in range(1, 8)]
        peers = [((1 - x) if fx else x, (1 - y) if fy else y, (1 - c) if fc else c) for fx, fy, fc in rel]
        sends = [
            pltpu.make_async_remote_copy(
                src_ref=v_ref, dst_ref=out_ref.at[me], send_sem=send_sems.at[j], recv_sem=recv_sems.at[j],
                device_id=p, device_id_type=MESH,
            )
            for j, p in enumerate(peers)
        ]
        for cp in sends:
            cp.start()
        for j, p in enumerate(peers):
            theirs = out_ref.at[4 * p[0] + 2 * p[1] + p[2]]
            pltpu.make_async_remote_copy(
                src_ref=theirs, dst_ref=theirs, send_sem=send_sems.at[j], recv_sem=recv_sems.at[j],
                device_id=p, device_id_type=MESH,
            ).wait_recv()
        for cp in sends:
            cp.wait_send()

    vmem = pl.BlockSpec(memory_space=pltpu.VMEM)
    return pl.pallas_call(
        body,
        in_specs=[vmem],
        out_specs=vmem,
        out_shape=jax.ShapeDtypeStruct((8, R, D), vec.dtype),
        scratch_shapes=[pltpu.SemaphoreType.DMA((7,)), pltpu.SemaphoreType.DMA((7,))],
        name=name,
    )(vec)


def _sum4(a, rest, *, name):
    R, D = a.shape
    tm = _pick(R, (256, 128, 8))

    def body(a_ref, r_ref, o_ref):
        o_ref[...] = ((a_ref[...] + r_ref[0].astype(F32)) + r_ref[1].astype(F32)) + r_ref[2].astype(F32)

    row = pl.BlockSpec((tm, D), lambda i: (i, 0))
    return pl.pallas_call(
        body,
        grid=(R // tm,),
        in_specs=[row, pl.BlockSpec((3, tm, D), lambda i: (0, i, 0))],
        out_specs=row,
        out_shape=jax.ShapeDtypeStruct((R, D), F32),
        compiler_params=_params("parallel"),
        name=name,
    )(a, rest)


def _adamw(parts, w, m, v, *, name):
    P, R, D = parts.shape
    tm = _pick(R, (256, 128, 8))

    def body(p_ref, w_ref, m_ref, v_ref, g_ref, d_ref, nm_ref, nv_ref):
        g = p_ref[0]
        for k in range(1, P):
            g = g + p_ref[k]
        nm = ADAM_B1 * m_ref[...] + (1.0 - ADAM_B1) * g
        nv = ADAM_B2 * v_ref[...] + (1.0 - ADAM_B2) * jnp.square(g)
        m_hat = nm / (1.0 - ADAM_B1**ADAM_STEP)
        v_hat = nv / (1.0 - ADAM_B2**ADAM_STEP)
        g_ref[...] = g
        d_ref[...] = -ADAM_LR * (m_hat / (jnp.sqrt(v_hat) + ADAM_EPS) + ADAM_WD * w_ref[...])
        nm_ref[...] = nm
        nv_ref[...] = nv

    row = pl.BlockSpec((tm, D), lambda i: (i, 0))
    return pl.pallas_call(
        body,
        grid=(R // tm,),
        in_specs=[pl.BlockSpec((P, tm, D), lambda i: (0, i, 0)), row, row, row],
        out_specs=[row] * 4,
        out_shape=[jax.ShapeDtypeStruct((R, D), F32)] * 4,
        compiler_params=_params("parallel"),
        name=name,
    )(parts, w, m, v)


MATS = ("w_qkv", "w_o", "w_ff1", "w_ff2")
VECS = ("ln1_g", "ln1_b", "ln2_g", "ln2_b")
PARAM_ORDER = ("w_qkv", "w_o", "ln1_g", "ln1_b", "w_ff1", "w_ff2", "ln2_g", "ln2_b")


def _pack_rows(arrs, D):
    return jnp.concatenate([a.reshape(-1, D) for a in arrs], axis=0)


def kernel(x, w_qkv_0, w_o_0, ln1_g_0, ln1_b_0, w_ff1_0, w_ff2_0, ln2_g_0, ln2_b_0, w_qkv_1, w_o_1, ln1_g_1, ln1_b_1, w_ff1_1, w_ff2_1, ln2_g_1, ln2_b_1, loss_target, m_w_qkv_0, m_w_o_0, m_ln1_g_0, m_ln1_b_0, m_w_ff1_0, m_w_ff2_0, m_ln2_g_0, m_ln2_b_0, m_w_qkv_1, m_w_o_1, m_ln1_g_1, m_ln1_b_1, m_w_ff1_1, m_w_ff2_1, m_ln2_g_1, m_ln2_b_1, v_w_qkv_0, v_w_o_0, v_ln1_g_0, v_ln1_b_0, v_w_ff1_0, v_w_ff2_0, v_ln2_g_0, v_ln2_b_0, v_w_qkv_1, v_w_o_1, v_ln1_g_1, v_ln1_b_1, v_w_ff1_1, v_w_ff2_1, v_ln2_g_1, v_ln2_b_1):
    given = dict(locals())
    xs = x[0]
    target = loss_target[0]
    S, D = xs.shape
    C3 = 3 * D // N_CHIPS
    names = [f"{p}_{l}" for l in range(DEPTH) for p in PARAM_ORDER]
    mat_names = [f"{p}_{l}" for l in range(DEPTH) for p in MATS]
    vec_names = [f"{p}_{l}" for l in range(DEPTH) for p in VECS]
    rows = {"w_qkv": C3, "w_o": D // N_CHIPS, "w_ff1": D, "w_ff2": D}
    layer_rows = sum(rows.values())

    gathered = _allgather_weights(_pack_rows([given[n] for n in mat_names], D).astype(MXU), name="allgather_weights")
    W = []
    for l in range(DEPTH):
        r0 = l * layer_rows
        seg = {}
        for p in MATS:
            seg[p] = gathered[:, r0 : r0 + rows[p], :]
            r0 += rows[p]
        W.append(
            dict(
                qkv=seg["w_qkv"].reshape(N_CHIPS, D, C3),
                o=seg["w_o"].reshape(1, D, D),
                ff1=seg["w_ff1"],
                ff2=seg["w_ff2"].reshape(1, 4 * D, D),
            )
        )
    tables = _rope_tables(S)

    h = xs
    saved = []
    for l in range(DEPTH):
        g1, b1, g2, b2 = (given[f"{p}_{l}"] for p in VECS)
        qkv = _mm_nn(h, W[l]["qkv"], a_op=None, out_dtype=F32, name=f"qkv_{l}")
        qkv3 = _qkv_post(qkv, tables if l == 1 else None, name=f"qkv_post_{l}")
        st = dict(h=h, qkv3=qkv3)
        if l == 0:
            o, car = _sb_fwd(qkv3[0], qkv3[1], qkv3[2], name="stickbreak_fwd")
            st.update(car=car)
        else:
            q3, k3, v3 = _perm3(qkv3[0]), _perm3(qkv3[1]), _perm3(qkv3[2])
            ob, lb = _dil_fwd(q3, k3, v3, name="dilated_fwd")
            ob = jnp.stack([_unperm(ob[i], d) for i, d in enumerate(DILATIONS)])
            lb = jnp.stack([_unperm(lb[i], d) for i, d in enumerate(DILATIONS)])
            o, lse = _dil_combine(ob, lb, name="dilated_combine")
            st.update(q3=q3, k3=k3, v3=v3, lse=lse)
        mix = _mm_nn(o, W[l]["o"], a_op=None, out_dtype=F32, name=f"attn_out_{l}")
        y1, x1 = _ln_fwd(h, mix, g1, b1, name=f"ln1_{l}")
        hp = _mm_nn(x1, W[l]["ff1"], a_op=None, out_dtype=F32, name=f"ff1_{l}")
        mlp = _mm_nn(hp, W[l]["ff2"], a_op="relu2", out_dtype=F32, name=f"ff2_{l}")
        y2, x2 = _ln_fwd(x1, mlp, g2, b2, name=f"ln2_{l}")
        st.update(o=o, y1=y1, x1=x1, hp=hp, y2=y2)
        saved.append(st)
        h = x2

    grads = {}
    top = saved[-1]
    dy2, dg, db, loss_lanes = _ln_bwd_loss(
        top["y2"], given[f"ln2_g_{DEPTH - 1}"], given[f"ln2_b_{DEPTH - 1}"], target, name="loss_ln2_bwd"
    )
    loss = lax.psum(jnp.sum(loss_lanes) * (0.5 / D), ("x", "y", "c"))
    grads[f"ln2_g_{DEPTH - 1}"], grads[f"ln2_b_{DEPTH - 1}"] = dg, db
    grad_x = None
    for l in reversed(range(DEPTH)):
        st = saved[l]
        dhp = _mm_nt(dy2, W[l]["ff2"], epi="relu2grad", extra=st["hp"], out_dtype=MXU, name=f"d_ff2_in_{l}")
        grads[f"w_ff2_{l}"] = _mm_tn(st["hp"], dy2, a_op="relu2", groups=1, name=f"d_w_ff2_{l}")
        dx1 = _mm_nt(dhp, W[l]["ff1"], epi="residual", extra=dy2, out_dtype=F32, name=f"d_ff1_in_{l}")
        grads[f"w_ff1_{l}"] = _mm_tn(st["x1"], dhp, a_op=None, groups=N_CHIPS, name=f"d_w_ff1_{l}")
        dy1, grads[f"ln1_g_{l}"], grads[f"ln1_b_{l}"] = _ln_bwd(dx1, st["y1"], given[f"ln1_g_{l}"], name=f"ln1_bwd_{l}")
        grads[f"w_o_{l}"] = _mm_tn(st["o"], dy1, a_op=None, groups=1, name=f"d_w_o_{l}")
        q, k, v = st["qkv3"][0], st["qkv3"][1], st["qkv3"][2]
        if l == 0:
            do = _mm_nt(dy1, W[l]["o"], epi=None, extra=None, out_dtype=MXU, name=f"d_attn_out_{l}")
            dq, dk, dv = _sb_bwd(q, k, v, do, st["car"], name="stickbreak_bwd")
            dqkv = _qkv_bwd_post([dq], [dk], [dv], None, name=f"qkv_bwd_post_{l}")
        else:
            do = _mm_nt(dy1, W[l]["o"], epi=None, extra=None, out_dtype=F32, name=f"d_attn_out_{l}")
            do3, out3, lse3 = _perm3(do), _perm3(st["o"]), _perm3(st["lse"])
            dq3 = _dil_bwd_dq(st["q3"], st["k3"], st["v3"], do3, out3, lse3, name="dilated_bwd_dq")
            dk3, dv3 = _dil_bwd_dkv(st["q3"], st["k3"], st["v3"], do3, out3, lse3, name="dilated_bwd_dkv")
            un = lambda t: [_unperm(t[i], d) for i, d in enumerate(DILATIONS)]
            dqkv = _qkv_bwd_post(un(dq3), un(dk3), un(dv3), tables, name=f"qkv_bwd_post_{l}")
        dh = _mm_nt(dqkv, W[l]["qkv"], epi="residual", extra=dy1, out_dtype=F32, name=f"d_qkv_in_{l}")
        grads[f"w_qkv_{l}"] = _mm_tn(st["h"], dqkv, a_op=None, groups=N_CHIPS, name=f"d_w_qkv_{l}")
        if l > 0:
            prev = saved[l - 1]
            dy2, grads[f"ln2_g_{l - 1}"], grads[f"ln2_b_{l - 1}"] = _ln_bwd(
                dh, prev["y2"], given[f"ln2_g_{l - 1}"], name=f"ln2_bwd_{l - 1}"
            )
        else:
            grad_x = dh

    gflat = jnp.concatenate([grads[n].reshape(N_CHIPS, -1, D) for n in mat_names], axis=1)
    me = 2 * lax.axis_index("x") + lax.axis_index("y")
    own = lax.dynamic_index_in_dim(gflat, me, 0, keepdims=False)
    from_chips = _scatter_grads(gflat.astype(MXU), name="scatter_grads")
    part = _sum4(own, from_chips, name="sum_chips")
    sib = _swap_with_sibling(part, name="swap_sibling")
    pack = lambda prefix: _pack_rows([given[prefix + n] for n in mat_names], D)
    outs_m = _adamw(jnp.stack([part, sib]), pack(""), pack("m_"), pack("v_"), name="adamw_matrices")
    vflat = _pack_rows([grads[n] for n in vec_names], D)
    packv = lambda prefix: _pack_rows([given[prefix + n] for n in vec_names], D)
    outs_v = _adamw(_gather_from_all(vflat, name="gather_ln_grads"), packv(""), packv("m_"), packv("v_"), name="adamw_vectors")

    def unpack(flat_m, flat_v):
        out, r0 = {}, 0
        for n in mat_names:
            r = given[n].size // D
            out[n] = flat_m[r0 : r0 + r].reshape(given[n].shape)
            r0 += r
        for i, n in enumerate(vec_names):
            out[n] = flat_v[i]
        return [out[n] for n in names]

    result = [loss, grad_x[None]]
    for k in range(4):
        result += unpack(outs_m[k], outs_v[k])
    return tuple(result)
```

```python
import functools

import jax
import jax.numpy as jnp
from jax import lax
from jax.experimental import pallas as pl
from jax.experimental.pallas import tpu as pltpu

F32 = jnp.float32
MXU = jnp.bfloat16

HEAD_DIM = 64
LANES = 128
CHUNK = 128
ROPE_THETA = 500000.0
ROPE_HALF = 8
DILATIONS = (1, 4, 16)
DEPTH = 2
ALPHA = (2 * DEPTH) ** 0.25
LN_EPS = 1e-5
QK_SCALE = 0.125
ADAM_LR, ADAM_B1, ADAM_B2, ADAM_EPS, ADAM_WD, ADAM_STEP = 0.001, 0.9, 0.999, 1e-08, 0.01, 10
NEG = -1e30
UNDERFLOW = -104.0
SB_TQ = 2 * CHUNK
V7X_VMEM_LIMIT = 56 * 1024 * 1024
MESH = pl.DeviceIdType.MESH
NT = (((1,), (1,)), ((), ()))
TN = (((0,), (0,)), ((), ()))
N_CHIPS = 4
MM_ROWS = (1024, 512, 256, 128)


def _params(*sem):
    return pltpu.CompilerParams(dimension_semantics=sem, vmem_limit_bytes=V7X_VMEM_LIMIT)


def _pick(n, cands=(1024, 768, 512, 384, 256, 128)):
    for c in cands:
        if n % c == 0:
            return c
    raise ValueError(f"no tile for {n}")


def _mm_nn(a, b, *, a_op, out_dtype, name):
    M, K = a.shape
    G, Kb, Nc = b.shape
    assert K == Kb
    N = G * Nc
    tm, tn, tk = _pick(M, MM_ROWS), _pick(Nc), _pick(K)
    nbs, nk = Nc // tn, K // tk

    def body(a_ref, b_ref, o_ref, *acc):
        av = a_ref[...]
        if a_op == "relu2":
            av = jnp.square(jnp.maximum(av, 0.0))
        part = jnp.dot(av.astype(MXU), b_ref[...], preferred_element_type=F32)
        if nk == 1:
            o_ref[...] = part.astype(out_dtype)
            return
        (acc_ref,) = acc
        kk = pl.program_id(2)

        @pl.when(kk == 0)
        def _():
            acc_ref[...] = part

        @pl.when(kk > 0)
        def _():
            acc_ref[...] += part

        @pl.when(kk == nk - 1)
        def _():
            o_ref[...] = acc_ref[...].astype(out_dtype)

    return pl.pallas_call(
        body,
        grid=(M // tm, N // tn, nk),
        in_specs=[
            pl.BlockSpec((tm, tk), lambda i, j, k: (i, k)),
            pl.BlockSpec((None, tk, tn), lambda i, j, k: (j // nbs, k, j % nbs)),
        ],
        out_specs=pl.BlockSpec((tm, tn), lambda i, j, k: (i, j)),
        out_shape=jax.ShapeDtypeStruct((M, N), out_dtype),
        scratch_shapes=[pltpu.VMEM((tm, tn), F32)] if nk > 1 else [],
        compiler_params=_params("parallel", "parallel", "arbitrary"),
        name=name,
    )(a, b)


def _mm_nt(a, b, *, epi, extra, out_dtype, name):
    M, N = a.shape
    G, K, Nc = b.shape
    assert N == G * Nc
    tm, tko, tn = _pick(M, MM_ROWS), _pick(K), _pick(Nc)
    nbs, nn = Nc // tn, N // tn

    def body(*refs):
        if epi is None:
            a_ref, b_ref, o_ref, acc_ref = refs
        else:
            a_ref, b_ref, e_ref, o_ref, acc_ref = refs
        kk = pl.program_id(2)
        part = lax.dot_general(a_ref[...].astype(MXU), b_ref[...], NT, preferred_element_type=F32)

        @pl.when(kk == 0)
        def _():
            acc_ref[...] = part

        @pl.when(kk > 0)
        def _():
            acc_ref[...] += part

        @pl.when(kk == nn - 1)
        def _():
            r = acc_ref[...]
            if epi == "relu2grad":
                r = r * (2.0 * jnp.maximum(e_ref[...], 0.0))
            elif epi == "residual":
                r = ALPHA * e_ref[...] + r
            o_ref[...] = r.astype(out_dtype)

    in_specs = [
        pl.BlockSpec((tm, tn), lambda i, j, k: (i, k)),
        pl.BlockSpec((None, tko, tn), lambda i, j, k: (k // nbs, j, k % nbs)),
    ]
    ops = [a, b]
    if epi is not None:
        in_specs.append(pl.BlockSpec((tm, tko), lambda i, j, k: (i, j)))
        ops.append(extra)
    return pl.pallas_call(
        body,
        grid=(M // tm, K // tko, nn),
        in_specs=in_specs,
        out_specs=pl.BlockSpec((tm, tko), lambda i, j, k: (i, j)),
        out_shape=jax.ShapeDtypeStruct((M, K), out_dtype),
        scratch_shapes=[pltpu.VMEM((tm, tko), F32)],
        compiler_params=_params("parallel", "parallel", "arbitrary"),
        name=name,
    )(*ops)


def _mm_tn(a, b, *, a_op, groups, name):
    M, Ka = a.shape
    Mb, N = b.shape
    assert M == Mb
    Nc = N // groups
    tka, tn, tm = _pick(Ka), _pick(Nc), _pick(M, (512, 256, 128))
    nbs, nm = Nc // tn, M // tm

    def body(a_ref, b_ref, o_ref, acc_ref):
        kk = pl.program_id(2)
        av = a_ref[...]
        if a_op == "relu2":
            av = jnp.square(jnp.maximum(av, 0.0))
        part = lax.dot_general(av.astype(MXU), b_ref[...].astype(MXU), TN, preferred_element_type=F32)

        @pl.when(kk == 0)
        def _():
            acc_ref[...] = part

        @pl.when(kk > 0)
        def _():
            acc_ref[...] += part

        @pl.when(kk == nm - 1)
        def _():
            o_ref[...] = acc_ref[...]

    return pl.pallas_call(
        body,
        grid=(Ka // tka, N // tn, nm),
        in_specs=[
            pl.BlockSpec((tm, tka), lambda i, j, k: (k, i)),
            pl.BlockSpec((tm, tn), lambda i, j, k: (k, j)),
        ],
        out_specs=pl.BlockSpec((None, tka, tn), lambda i, j, k: (j // nbs, i, j % nbs)),
        out_shape=jax.ShapeDtypeStruct((groups, Ka, Nc), F32),
        scratch_shapes=[pltpu.VMEM((tka, tn), F32)],
        compiler_params=_params("parallel", "parallel", "arbitrary"),
        name=name,
    )(a, b)


def _ln_stats(y):
    mu = jnp.mean(y, axis=-1, keepdims=True)
    yc = y - mu
    var = jnp.mean(yc * yc, axis=-1, keepdims=True)
    rstd = lax.rsqrt(var + LN_EPS)
    return yc * rstd, rstd


def _ln_fwd(x, mix, g, b, *, name):
    S, D = x.shape
    tm = _pick(S, (256, 128))

    def body(x_ref, m_ref, g_ref, b_ref, y_ref, o_ref, ob_ref):
        y = ALPHA * x_ref[...] + m_ref[...]
        xhat, _ = _ln_stats(y)
        y_ref[...] = y
        out = xhat * g_ref[...] + b_ref[...]
        o_ref[...] = out
        ob_ref[...] = out.astype(MXU)

    row = pl.BlockSpec((tm, D), lambda i: (i, 0))
    vec = pl.BlockSpec((1, D), lambda i: (0, 0))
    return pl.pallas_call(
        body,
        grid=(S // tm,),
        in_specs=[row, row, vec, vec],
        out_specs=[row, row, row],
        out_shape=[jax.ShapeDtypeStruct((S, D), F32)] * 2 + [jax.ShapeDtypeStruct((S, D), MXU)],
        compiler_params=_params("parallel"),
        name=name,
    )(x, mix, g.reshape(1, D), b.reshape(1, D))


def _ln_bwd(dx, y, g, *, name):
    S, D = y.shape
    tm = _pick(S, (256, 128))

    def body(dx_ref, y_ref, g_ref, dy_ref, dyb_ref, dg_ref, db_ref):
        xhat, rstd = _ln_stats(y_ref[...])
        dx_v = dx_ref[...]
        dxh = dx_v * g_ref[...]
        m1 = jnp.mean(dxh, axis=-1, keepdims=True)
        m2 = jnp.mean(dxh * xhat, axis=-1, keepdims=True)
        dy = rstd * (dxh - m1 - xhat * m2)
        dy_ref[...] = dy
        dyb_ref[...] = dy.astype(MXU)

        @pl.when(pl.program_id(0) == 0)
        def _():
            dg_ref[...] = jnp.zeros_like(dg_ref)
            db_ref[...] = jnp.zeros_like(db_ref)

        dg_ref[...] += jnp.sum(dx_v * xhat, axis=0, keepdims=True)
        db_ref[...] += jnp.sum(dx_v, axis=0, keepdims=True)

    row = pl.BlockSpec((tm, D), lambda i: (i, 0))
    vec = pl.BlockSpec((1, D), lambda i: (0, 0))
    return pl.pallas_call(
        body,
        grid=(S // tm,),
        in_specs=[row, row, vec],
        out_specs=[row, row, vec, vec],
        out_shape=[jax.ShapeDtypeStruct((S, D), F32), jax.ShapeDtypeStruct((S, D), MXU)]
        + [jax.ShapeDtypeStruct((1, D), F32)] * 2,
        compiler_params=_params("arbitrary"),
        name=name,
    )(dx, y, g.reshape(1, D))


def _ln_bwd_loss(y, g, b, target, *, name):
    S, D = y.shape
    tm = _pick(S, (256, 128))

    def body(y_ref, g_ref, b_ref, t_ref, dy_ref, dyb_ref, dg_ref, db_ref, ls_ref):
        xhat, rstd = _ln_stats(y_ref[...])
        err = xhat * g_ref[...] + b_ref[...] - t_ref[...]
        dx_v = err * (1.0 / D)
        dxh = dx_v * g_ref[...]
        m1 = jnp.mean(dxh, axis=-1, keepdims=True)
        m2 = jnp.mean(dxh * xhat, axis=-1, keepdims=True)
        dy = rstd * (dxh - m1 - xhat * m2)
        dy_ref[...] = dy
        dyb_ref[...] = dy.astype(MXU)

        @pl.when(pl.program_id(0) == 0)
        def _():
            dg_ref[...] = jnp.zeros_like(dg_ref)
            db_ref[...] = jnp.zeros_like(db_ref)
            ls_ref[...] = jnp.zeros_like(ls_ref)

        dg_ref[...] += jnp.sum(dx_v * xhat, axis=0, keepdims=True)
        db_ref[...] += jnp.sum(dx_v, axis=0, keepdims=True)
        ls_ref[...] += jnp.sum(err * err, axis=0, keepdims=True)

    row = pl.BlockSpec((tm, D), lambda i: (i, 0))
    vec = pl.BlockSpec((1, D), lambda i: (0, 0))
    return pl.pallas_call(
        body,
        grid=(S // tm,),
        in_specs=[row, vec, vec, row],
        out_specs=[row, row, vec, vec, vec],
        out_shape=[jax.ShapeDtypeStruct((S, D), F32), jax.ShapeDtypeStruct((S, D), MXU)]
        + [jax.ShapeDtypeStruct((1, D), F32)] * 3,
        compiler_params=_params("arbitrary"),
        name=name,
    )(y, g.reshape(1, D), b.reshape(1, D), target)


def _rope_tables(S):
    inv_freq = ROPE_THETA ** (-jnp.arange(ROPE_HALF, dtype=F32) / ROPE_HALF)
    ang = jnp.arange(S, dtype=jnp.int32).astype(F32)[:, None] * inv_freq[None, :]
    cos, sin = jnp.cos(ang), jnp.sin(ang)
    ones = jnp.ones((S, HEAD_DIM - 2 * ROPE_HALF), F32)
    zeros8 = jnp.zeros((S, ROPE_HALF), F32)
    c = jnp.concatenate([cos, cos, ones], axis=1)
    s1 = jnp.concatenate([zeros8, sin, 0.0 * ones], axis=1)
    s2 = jnp.concatenate([-sin, zeros8, 0.0 * ones], axis=1)
    return tuple(jnp.concatenate([t, t], axis=1) for t in (c, s1, s2))


def _rope(xb, c, s1, s2):
    return xb * c + pltpu.roll(xb, ROPE_HALF, 1) * s1 + pltpu.roll(xb, LANES - ROPE_HALF, 1) * s2


def _rope_t(dyb, c, s1, s2):
    return dyb * c + pltpu.roll(dyb * s1, LANES - ROPE_HALF, 1) + pltpu.roll(dyb * s2, ROPE_HALF, 1)


def _qkv_post(qkv, tables, *, name):
    S, D3 = qkv.shape
    D = D3 // 3
    tm = _pick(S, (256, 128))
    rope = tables is not None

    def body(*refs):
        x_ref, outs = refs[0], refs[-3:]
        tabs = tuple(r[...] for r in refs[1:-3])
        for j, o_ref in enumerate(outs):
            for blk in range(D // LANES):
                cols = slice(blk * LANES, (blk + 1) * LANES)
                xb = x_ref[:, j * D + blk * LANES : j * D + (blk + 1) * LANES]
                if rope and j < 2:
                    xb = _rope(xb, *tabs)
                if j == 0:
                    xb = xb * QK_SCALE
                o_ref[:, cols] = xb.astype(MXU)

    tab = pl.BlockSpec((tm, LANES), lambda i: (i, 0))
    row = pl.BlockSpec((tm, D), lambda i: (i, 0))
    return pl.pallas_call(
        body,
        grid=(S // tm,),
        in_specs=[pl.BlockSpec((tm, D3), lambda i: (i, 0))] + ([tab] * 3 if rope else []),
        out_specs=[row] * 3,
        out_shape=[jax.ShapeDtypeStruct((S, D), MXU)] * 3,
        compiler_params=_params("parallel"),
        name=name,
    )(qkv, *(tables if rope else ()))


def _qkv_bwd_post(dqs, dks, dvs, tables, *, name):
    S, D = dqs[0].shape
    tm = _pick(S, (256, 128))
    nb = len(dqs)
    rope = tables is not None

    def body(*refs):
        ins, o_ref = refs[:-1], refs[-1]
        groups = [ins[0:nb], ins[nb : 2 * nb], ins[2 * nb : 3 * nb]]
        tabs = [r[...] for r in ins[3 * nb :]]
        for j, grp in enumerate(groups):
            for blk in range(D // LANES):
                cols = slice(blk * LANES, (blk + 1) * LANES)
                v = grp[0][:, cols]
                for r in grp[1:]:
                    v = v + r[:, cols]
                if rope and j < 2:
                    v = _rope_t(v, *tabs)
                if j == 0:
                    v = v * QK_SCALE
                o_ref[:, j * D + blk * LANES : j * D + (blk + 1) * LANES] = v.astype(MXU)

    row = pl.BlockSpec((tm, D), lambda i: (i, 0))
    tab = pl.BlockSpec((tm, LANES), lambda i: (i, 0))
    return pl.pallas_call(
        body,
        grid=(S // tm,),
        in_specs=[row] * (3 * nb) + ([tab] * 3 if rope else []),
        out_specs=pl.BlockSpec((tm, 3 * D), lambda i: (i, 0)),
        out_shape=jax.ShapeDtypeStruct((S, 3 * D), MXU),
        compiler_params=_params("parallel"),
        name=name,
    )(*dqs, *dks, *dvs, *(tables if rope else ()))


def _log_fail(z):
    return -(jnp.maximum(z, 0.0) + jnp.log1p(jnp.exp(-jnp.abs(z))))


def _sb_fwd(q, k, v, *, name):
    S, D = q.shape
    HP, H = D // LANES, D // HEAD_DIM
    tq = SB_TQ
    assert S % tq == 0
    cpb = tq // CHUNK
    assert S // CHUNK <= LANES

    def body(q_ref, k_ref, v_ref, o_ref, car_ref, acc_ref, carry0_ref, carry1_ref):
        i = pl.program_id(1)
        lane = lax.broadcasted_iota(jnp.int32, (tq, LANES), 1)
        row = lax.broadcasted_iota(jnp.int32, (tq, LANES), 0)
        kl = lax.broadcasted_iota(jnp.int32, (CHUNK, LANES), 1)
        kr = lax.broadcasted_iota(jnp.int32, (CHUNK, LANES), 0)
        tri = (kr >= kl).astype(MXU)
        qv = q_ref[...]
        qh = (jnp.where(lane < HEAD_DIM, qv, jnp.zeros_like(qv)), jnp.where(lane >= HEAD_DIM, qv, jnp.zeros_like(qv)))
        carry_refs = (carry0_ref, carry1_ref)
        acc_ref[...] = jnp.zeros_like(acc_ref)
        car_ref[...] = jnp.full((2, tq, LANES), NEG, F32)
        for r in carry_refs:
            r[...] = jnp.zeros_like(r)

        def run(chunks, masked):
            offs = [pl.multiple_of(c * CHUNK, CHUNK) for c in chunks]
            kcs = [k_ref[pl.ds(o, CHUNK), :] for o in offs]
            vcs = [v_ref[pl.ds(o, CHUNK), :] for o in offs]
            jobs = [(n, hh) for n in range(len(chunks)) for hh in range(2)]
            zs = {j: lax.dot_general(qh[j[1]], kcs[j[0]], NT, preferred_element_type=F32) for j in jobs}
            if masked:
                cms = [(o + lane) < (i * tq + row) for o in offs]
            lfs = {}
            for j in jobs:
                lf = _log_fail(zs[j])
                lfs[j] = jnp.where(cms[j[0]], lf, 0.0) if masked else lf
            rs = {j: jnp.dot(lfs[j].astype(MXU), tri, preferred_element_type=F32) for j in jobs}
            ws = {}
            for hh in range(2):
                carry = carry_refs[hh][...]
                tile = car_ref[hh]
                for n, c in enumerate(chunks):
                    w = jnp.exp(zs[n, hh] + rs[n, hh] + carry)
                    if masked:
                        w = jnp.where(cms[n], w, 0.0)
                    ws[n, hh] = w.astype(MXU)
                    tile = jnp.where(lane == c, carry, tile)
                    carry = carry + jnp.sum(lfs[n, hh], axis=-1, keepdims=True)
                car_ref[hh] = tile
                carry_refs[hh][...] = carry
            for hh in range(2):
                pv = jnp.dot(ws[0, hh], vcs[0], preferred_element_type=F32)
                for n in range(1, len(chunks)):
                    pv = pv + jnp.dot(ws[n, hh], vcs[n], preferred_element_type=F32)
                acc_ref[hh] += pv

        def max_carry():
            return jnp.maximum(jnp.max(carry0_ref[...]), jnp.max(carry1_ref[...]))

        run([i * cpb + u for u in reversed(range(cpb))], True)

        def cond(st):
            return jnp.logical_and(st[0] >= 0, st[1] >= UNDERFLOW)

        def pair(st):
            c = st[0]
            run([c, c - 1], False)
            return c - 2, max_carry()

        lax.while_loop(cond, pair, (i * cpb - 1, max_carry()))
        o_ref[...] = jnp.where(lane < HEAD_DIM, acc_ref[0], acc_ref[1]).astype(o_ref.dtype)

    blk = pl.BlockSpec((tq, LANES), lambda h, i: (i, h))
    full = pl.BlockSpec((S, LANES), lambda h, i: (0, h))
    return pl.pallas_call(
        body,
        grid=(HP, S // tq),
        in_specs=[blk, full, full],
        out_specs=[blk, pl.BlockSpec((2, tq, LANES), lambda h, i: (h, i, 0))],
        out_shape=[jax.ShapeDtypeStruct((S, D), MXU), jax.ShapeDtypeStruct((H, S, LANES), F32)],
        scratch_shapes=[pltpu.VMEM((2, tq, LANES), F32), pltpu.VMEM((tq, 1), F32), pltpu.VMEM((tq, 1), F32)],
        compiler_params=_params("parallel", "arbitrary"),
        name=name,
    )(q, k, v)


def _sb_bwd(q, k, v, do, car, *, name):
    S, D = q.shape
    HP = D // LANES
    tq = SB_TQ
    assert S % tq == 0
    cpb = tq // CHUNK
    nq = S // tq

    def body(q_ref, k_ref, v_ref, do_ref, car_ref, dq_ref, dk_hbm, dv_hbm, dq_acc, dk_acc, dv_acc, gcar0_ref, gcar1_ref, sem):
        hp = pl.program_id(0)
        i = pl.program_id(1)
        lane = lax.broadcasted_iota(jnp.int32, (tq, LANES), 1)
        row = lax.broadcasted_iota(jnp.int32, (tq, LANES), 0)
        kl = lax.broadcasted_iota(jnp.int32, (CHUNK, LANES), 1)
        kr = lax.broadcasted_iota(jnp.int32, (CHUNK, LANES), 0)
        tri = (kr >= kl).astype(MXU)
        tri_prefix = (kr <= kl).astype(MXU)

        @pl.when(i == 0)
        def _():
            dk_acc[...] = jnp.zeros_like(dk_acc)
            dv_acc[...] = jnp.zeros_like(dv_acc)

        qv = q_ref[...]
        dov = do_ref[...]
        qh = (jnp.where(lane < HEAD_DIM, qv, jnp.zeros_like(qv)), jnp.where(lane >= HEAD_DIM, qv, jnp.zeros_like(qv)))
        doh = (jnp.where(lane < HEAD_DIM, dov, jnp.zeros_like(dov)), jnp.where(lane >= HEAD_DIM, dov, jnp.zeros_like(dov)))
        gcar_refs = (gcar0_ref, gcar1_ref)
        dq_acc[...] = jnp.zeros_like(dq_acc)
        for r in gcar_refs:
            r[...] = jnp.zeros_like(r)

        def step(c, masked):
            off = pl.multiple_of(c * CHUNK, CHUNK)
            kc = k_ref[pl.ds(off, CHUNK), :]
            vc = v_ref[pl.ds(off, CHUNK), :]
            if masked:
                cm = (off + lane) < (i * tq + row)
            heads = range(2)
            zs = [lax.dot_general(qh[hh], kc, NT, preferred_element_type=F32) for hh in heads]
            dws = [lax.dot_general(doh[hh], vc, NT, preferred_element_type=F32) for hh in heads]
            lfs, sigs = [], []
            for hh in heads:
                lf = _log_fail(zs[hh])
                sigs.append(jnp.exp(zs[hh] + lf))
                lfs.append(jnp.where(cm, lf, 0.0) if masked else lf)
            rs = [jnp.dot(lfs[hh].astype(MXU), tri, preferred_element_type=F32) for hh in heads]
            ws, gs = [], []
            for hh in heads:
                carry = jnp.sum(jnp.where(lane == c, car_ref[hh], 0.0), axis=-1, keepdims=True)
                w = jnp.exp(zs[hh] + rs[hh] + carry)
                if masked:
                    w = jnp.where(cm, w, 0.0)
                ws.append(w.astype(MXU))
                gs.append(w * dws[hh])
            big_gs = [jnp.dot(gs[hh].astype(MXU), tri_prefix, preferred_element_type=F32) for hh in heads]
            dzs = []
            for hh in heads:
                dz = gs[hh] - sigs[hh] * (big_gs[hh] + gcar_refs[hh][...])
                if masked:
                    dz = jnp.where(cm, dz, 0.0)
                gcar_refs[hh][...] += jnp.sum(gs[hh], axis=-1, keepdims=True)
                dzs.append(dz.astype(MXU))
            for hh in heads:
                dq_acc[hh] += jnp.dot(dzs[hh], kc, preferred_element_type=F32)
            dk_acc[pl.ds(off, CHUNK), :] += lax.dot_general(dzs[0], qh[0], TN, preferred_element_type=F32) + lax.dot_general(
                dzs[1], qh[1], TN, preferred_element_type=F32
            )
            dv_acc[pl.ds(off, CHUNK), :] += lax.dot_general(ws[0], doh[0], TN, preferred_element_type=F32) + lax.dot_general(
                ws[1], doh[1], TN, preferred_element_type=F32
            )

        colmax = jnp.maximum(jnp.max(car_ref[0], axis=0, keepdims=True), jnp.max(car_ref[1], axis=0, keepdims=True))
        lane1 = lax.broadcasted_iota(jnp.int32, (1, LANES), 1)
        dead = jnp.logical_and(colmax < UNDERFLOW, lane1 < i * cpb)
        first = jnp.sum(jnp.where(dead, 1.0, 0.0)).astype(jnp.int32)

        def loop_body(c, carry):
            step(c, False)
            return carry

        lax.fori_loop(first, i * cpb, loop_body, 0)
        for u in range(cpb):
            step(i * cpb + u, True)
        dq_ref[...] = jnp.where(lane < HEAD_DIM, dq_acc[0], dq_acc[1])

        @pl.when(i == nq - 1)
        def _():
            cols = pl.ds(pl.multiple_of(hp * LANES, LANES), LANES)
            for src, dst in ((dk_acc, dk_hbm), (dv_acc, dv_hbm)):
                cp = pltpu.make_async_copy(src, dst.at[:, cols], sem)
                cp.start()
                cp.wait()

    blk = pl.BlockSpec((tq, LANES), lambda h, i: (i, h))
    full = pl.BlockSpec((S, LANES), lambda h, i: (0, h))
    hbm = pl.BlockSpec(memory_space=pl.ANY)
    return pl.pallas_call(
        body,
        grid=(HP, nq),
        in_specs=[blk, full, full, blk, pl.BlockSpec((2, tq, LANES), lambda h, i: (h, i, 0))],
        out_specs=[blk, hbm, hbm],
        out_shape=[jax.ShapeDtypeStruct((S, D), F32)] * 3,
        scratch_shapes=[
            pltpu.VMEM((2, tq, LANES), F32),
            pltpu.VMEM((S, LANES), F32),
            pltpu.VMEM((S, LANES), F32),
            pltpu.VMEM((tq, 1), F32),
            pltpu.VMEM((tq, 1), F32),
            pltpu.SemaphoreType.DMA,
        ],
        compiler_params=_params("arbitrary", "arbitrary"),
        name=name,
    )(q, k, v, do, car)


def _perm(x, d):
    if d == 1:
        return x
    S, D = x.shape
    return x.reshape(S // d, d, D).transpose(1, 0, 2).reshape(S, D)


def _unperm(x, d):
    if d == 1:
        return x
    S, D = x.shape
    return x.reshape(d, S // d, D).transpose(1, 0, 2).reshape(S, D)


def _perm3(x):
    return jnp.stack([_perm(x, d) for d in DILATIONS])


def _dil_tb(S):
    return min(512, S // DILATIONS[-1])


def _dil_masks():
    qi = lax.broadcasted_iota(jnp.int32, (CHUNK, CHUNK), 0)
    kj = lax.broadcasted_iota(jnp.int32, (CHUNK, CHUNK), 1)
    return qi, kj


def _dil_fwd(q, k, v, *, branch, name):
    S, D = q.shape
    HP = D // LANES
    tb = _dil_tb(S)
    nb, nsub = S // tb, tb // CHUNK
    bps = nb // DILATIONS[branch]

    def body(q_ref, k_ref, v_ref, kp_ref, vp_ref, o_ref, l_ref):
        n = pl.program_id(1)
        first = jnp.bitwise_and(n, bps - 1) == 0
        qi, kj = _dil_masks()
        hl = lax.broadcasted_iota(jnp.int32, (CHUNK, LANES), 1)
        m_cur = kj <= qi
        m_prev = (kj - qi) >= 0
        m_prev0 = (kj - qi) >= jnp.where(first, 2 * CHUNK, 0)
        hms = (hl < HEAD_DIM, hl >= HEAD_DIM)
        tiles, scores = [], {}
        for u in range(nsub):
            rows = slice(u * CHUNK, (u + 1) * CHUNK)
            qs, kc, vc = q_ref[rows, :], k_ref[rows, :], v_ref[rows, :]
            if u == 0:
                kp, vp, pm = kp_ref[...], vp_ref[...], m_prev0
            else:
                prev = slice((u - 1) * CHUNK, u * CHUNK)
                kp, vp, pm = k_ref[prev, :], v_ref[prev, :], m_prev
            tiles.append((rows, vp, vc, pm))
            for hh in range(2):
                qh = jnp.where(hms[hh], qs, jnp.zeros_like(qs))
                scores[u, hh] = (
                    lax.dot_general(qh, kp, NT, preferred_element_type=F32),
                    lax.dot_general(qh, kc, NT, preferred_element_type=F32),
                )
        probs = {}
        for (u, hh), (s_p, s_c) in scores.items():
            s_p = jnp.where(tiles[u][3], s_p, NEG)
            s_c = jnp.where(m_cur, s_c, NEG)
            m = jnp.max(jnp.maximum(s_p, s_c), axis=-1, keepdims=True)
            p_p, p_c = jnp.exp(s_p - m), jnp.exp(s_c - m)
            den = jnp.sum(p_p + p_c, axis=-1, keepdims=True)
            probs[u, hh] = (p_p.astype(MXU), p_c.astype(MXU), 1.0 / den, m + jnp.log(den))
        for u, (rows, vp, vc, _) in enumerate(tiles):
            o_t = l_t = None
            for hh in range(2):
                p_p, p_c, inv, lse = probs[u, hh]
                num = jnp.dot(p_p, vp, preferred_element_type=F32) + jnp.dot(p_c, vc, preferred_element_type=F32)
                o_h, l_h = num * inv, jnp.broadcast_to(lse, (CHUNK, LANES))
                o_t = o_h if hh == 0 else jnp.where(hms[1], o_h, o_t)
                l_t = l_h if hh == 0 else jnp.where(hms[1], l_h, l_t)
            o_ref[rows, :] = o_t
            l_ref[rows, :] = l_t

    blk = pl.BlockSpec((tb, LANES), lambda h, n: (n, h))
    tail = pl.BlockSpec((CHUNK, LANES), lambda h, n: (jnp.maximum(n * nsub - 1, 0), h))
    return pl.pallas_call(
        body,
        grid=(HP, nb),
        in_specs=[blk, blk, blk, tail, tail],
        out_specs=[blk, blk],
        out_shape=[jax.ShapeDtypeStruct((S, D), F32)] * 2,
        compiler_params=_params("parallel", "parallel"),
        name=name,
    )(q, k, v, k, v)


def _dil_combine(os, ls, *, name):
    S, D = os[0].shape
    tm = _pick(S, (256, 128))

    def body(o0_ref, o1_ref, o2_ref, l0_ref, l1_ref, l2_ref, out_ref, lse_ref):
        l0, l1, l2 = l0_ref[...], l1_ref[...], l2_ref[...]
        m = jnp.maximum(jnp.maximum(l0, l1), l2)
        a0, a1, a2 = jnp.exp(l0 - m), jnp.exp(l1 - m), jnp.exp(l2 - m)
        den = a0 + a1 + a2
        out_ref[...] = (a0 * o0_ref[...] + a1 * o1_ref[...] + a2 * o2_ref[...]) / den
        lse_ref[...] = m + jnp.log(den)

    row = pl.BlockSpec((tm, D), lambda i: (i, 0))
    return pl.pallas_call(
        body,
        grid=(S // tm,),
        in_specs=[row] * 6,
        out_specs=[row, row],
        out_shape=[jax.ShapeDtypeStruct((S, D), F32)] * 2,
        compiler_params=_params("parallel"),
        name=name,
    )(*os, *ls)


def _head_col(tile, hl, hh):
    return jnp.sum(jnp.where(hl == hh * HEAD_DIM, tile, 0.0), axis=-1, keepdims=True)


def _dil_bwd_dq(q, k, v, do, out, lse, *, branch, name):
    S, D = q.shape
    HP = D // LANES
    tb = _dil_tb(S)
    nb, nsub = S // tb, tb // CHUNK
    bps = nb // DILATIONS[branch]

    def body(q_ref, k_ref, v_ref, kp_ref, vp_ref, do_ref, out_ref, lse_ref, dq_ref):
        n = pl.program_id(1)
        first = jnp.bitwise_and(n, bps - 1) == 0
        qi, kj = _dil_masks()
        hl = lax.broadcasted_iota(jnp.int32, (CHUNK, LANES), 1)
        m_cur = kj <= qi
        m_prev = (kj - qi) >= 0
        m_prev0 = (kj - qi) >= jnp.where(first, 2 * CHUNK, 0)
        hms = (hl < HEAD_DIM, hl >= HEAD_DIM)
        jobs, cols = [], {}
        for u in range(nsub):
            rows = slice(u * CHUNK, (u + 1) * CHUNK)
            qs, kc, vc = q_ref[rows, :], k_ref[rows, :], v_ref[rows, :]
            if u == 0:
                kp, vp, pm = kp_ref[...], vp_ref[...], m_prev0
            else:
                prev = slice((u - 1) * CHUNK, u * CHUNK)
                kp, vp, pm = k_ref[prev, :], v_ref[prev, :], m_prev
            do_t, out_t, lse_t = do_ref[rows, :], out_ref[rows, :], lse_ref[rows, :]
            for hh in range(2):
                qh = jnp.where(hms[hh], qs, jnp.zeros_like(qs))
                doh = jnp.where(hms[hh], do_t, 0.0)
                cols[u, hh] = (jnp.sum(doh * out_t, axis=-1, keepdims=True), _head_col(lse_t, hl, hh))
                dob = doh.astype(MXU)
                for kk, vv, msk in ((kp, vp, pm), (kc, vc, m_cur)):
                    s = lax.dot_general(qh, kk, NT, preferred_element_type=F32)
                    dp = lax.dot_general(dob, vv, NT, preferred_element_type=F32)
                    jobs.append((u, hh, s, dp, kk, msk))
        dss = []
        for u, hh, s, dp, kk, msk in jobs:
            delta, lse = cols[u, hh]
            p = jnp.exp(jnp.where(msk, s, NEG) - lse)
            dss.append((p * (dp - delta)).astype(MXU))
        dq_h = {}
        for (u, hh, _, _, kk, _), ds in zip(jobs, dss):
            part = jnp.dot(ds, kk, preferred_element_type=F32)
            dq_h[u, hh] = part if (u, hh) not in dq_h else dq_h[u, hh] + part
        for u in range(nsub):
            dq_ref[u * CHUNK : (u + 1) * CHUNK, :] = jnp.where(hms[0], dq_h[u, 0], dq_h[u, 1])

    blk = pl.BlockSpec((tb, LANES), lambda h, n: (n, h))
    tail = pl.BlockSpec((CHUNK, LANES), lambda h, n: (jnp.maximum(n * nsub - 1, 0), h))
    return pl.pallas_call(
        body,
        grid=(HP, nb),
        in_specs=[blk, blk, blk, tail, tail, blk, blk, blk],
        out_specs=blk,
        out_shape=jax.ShapeDtypeStruct((S, D), F32),
        compiler_params=_params("parallel", "parallel"),
        name=name,
    )(q, k, v, k, v, do, out, lse)


def _dil_bwd_dkv(q, k, v, do, out, lse, *, branch, name):
    S, D = q.shape
    HP = D // LANES
    tb = _dil_tb(S)
    nb, nsub = S // tb, tb // CHUNK
    last_chunk = S // CHUNK - 1
    bps = nb // DILATIONS[branch]

    def body(q_ref, k_ref, v_ref, do_ref, out_ref, lse_ref, qn_ref, don_ref, outn_ref, lsen_ref, dk_ref, dv_ref):
        n = pl.program_id(1)
        last =jnp.bitwise_and(n + 1, bps - 1) == 0
        qi, kj = _dil_masks()
        hl = lax.broadcasted_iota(jnp.int32, (CHUNK, LANES), 1)
        m_cur = kj <= qi
        m_next = (kj - qi) >= 0
        m_next_last = (kj - qi) >= jnp.where(last, 2 * CHUNK, 0)
        hms = (hl < HEAD_DIM, hl >= HEAD_DIM)
        qtiles = {}
        for t in range(nsub + 1):
            if t == nsub:
                qs, do_t, out_t, lse_t = qn_ref[...], don_ref[...], outn_ref[...], lsen_ref[...]
            else:
                rows = slice(t * CHUNK, (t + 1) * CHUNK)
                qs, do_t, out_t, lse_t = q_ref[rows, :], do_ref[rows, :], out_ref[rows, :], lse_ref[rows, :]
            for hh in range(2):
                doh = jnp.where(hms[hh], do_t, 0.0)
                qtiles[t, hh] = (
                    jnp.where(hms[hh], qs, jnp.zeros_like(qs)),
                    doh.astype(MXU),
                    jnp.sum(doh * out_t, axis=-1, keepdims=True),
                    _head_col(lse_t, hl, hh),
                )
        jobs = []
        for u in range(nsub):
            rows = slice(u * CHUNK, (u + 1) * CHUNK)
            kc, vc = k_ref[rows, :], v_ref[rows, :]
            for t, msk in ((u, m_cur), (u + 1, m_next_last if u == nsub - 1 else m_next)):
                for hh in range(2):
                    qh, dob, _, _ = qtiles[t, hh]
                    s = lax.dot_general(qh, kc, NT, preferred_element_type=F32)
                    dp = lax.dot_general(dob, vc, NT, preferred_element_type=F32)
                    jobs.append((u, t, hh, s, dp, msk))
        pds = []
        for u, t, hh, s, dp, msk in jobs:
            _, _, delta, lse = qtiles[t, hh]
            p = jnp.exp(jnp.where(msk, s, NEG) - lse)
            pds.append((p.astype(MXU), (p * (dp - delta)).astype(MXU)))
        dks, dvs = {}, {}
        for (u, t, hh, _, _, _), (pb, dsb) in zip(jobs, pds):
            qh, dob, _, _ = qtiles[t, hh]
            dv_p = lax.dot_general(pb, dob, TN, preferred_element_type=F32)
            dk_p = lax.dot_general(dsb, qh, TN, preferred_element_type=F32)
            dvs[u] = dv_p if u not in dvs else dvs[u] + dv_p
            dks[u] = dk_p if u not in dks else dks[u] + dk_p
        for u in range(nsub):
            dk_ref[u * CHUNK : (u + 1) * CHUNK, :] = dks[u]
            dv_ref[u * CHUNK : (u + 1) * CHUNK, :] = dvs[u]

    blk = pl.BlockSpec((tb, LANES), lambda h, n: (n, h))
    head = pl.BlockSpec((CHUNK, LANES), lambda h, n: (jnp.minimum((n + 1) * nsub, last_chunk), h))
    return pl.pallas_call(
        body,
        grid=(HP, nb),
        in_specs=[blk] * 6 + [head] * 4,
        out_specs=[blk, blk],
        out_shape=[jax.ShapeDtypeStruct((S, D), F32)] * 2,
        compiler_params=_params("parallel", "parallel"),
        name=name,
    )(q, k, v, do, out, lse, q, do, out, lse)


def _chip_peers():
    x, y, c = lax.axis_index("x"), lax.axis_index("y"), lax.axis_index("c")
    peers = [(1 - x, y, c), (x, 1 - y, c), (1 - x, 1 - y, c)]
    return x, y, c, peers


def _allgather_weights(wsh, *, name):
    R, D = wsh.shape

    def body(w_ref, out_ref, send_sems, recv_sems, local_sem):
        x, y, c, peers = _chip_peers()
        me = 2 * x + y
        mine = pltpu.make_async_copy(w_ref, out_ref.at[me], local_sem)
        mine.start()
        sends = [
            pltpu.make_async_remote_copy(
                src_ref=w_ref, dst_ref=out_ref.at[me], send_sem=send_sems.at[j], recv_sem=recv_sems.at[j],
                device_id=p, device_id_type=MESH,
            )
            for j, p in enumerate(peers)
        ]
        for cp in sends:
            cp.start()
        for j, p in enumerate(peers):
            theirs = out_ref.at[2 * p[0] + p[1]]
            pltpu.make_async_remote_copy(
                src_ref=theirs, dst_ref=theirs, send_sem=send_sems.at[j], recv_sem=recv_sems.at[j],
                device_id=p, device_id_type=MESH,
            ).wait_recv()
        for cp in sends:
            cp.wait_send()
        mine.wait()

    hbm = pl.BlockSpec(memory_space=pl.ANY)
    return pl.pallas_call(
        body,
        in_specs=[hbm],
        out_specs=hbm,
        out_shape=jax.ShapeDtypeStruct((N_CHIPS, R, D), wsh.dtype),
        scratch_shapes=[pltpu.SemaphoreType.DMA((3,)), pltpu.SemaphoreType.DMA((3,)), pltpu.SemaphoreType.DMA],
        name=name,
    )(wsh)


def _scatter_grads(gflat, *, name):
    _, R, D = gflat.shape

    def body(g_ref, out_ref, send_sems, recv_sems):
        x, y, c, peers = _chip_peers()
        sends = [
            pltpu.make_async_remote_copy(
                src_ref=g_ref.at[2 * p[0] + p[1]], dst_ref=out_ref.at[j], send_sem=send_sems.at[j],
                recv_sem=recv_sems.at[j], device_id=p, device_id_type=MESH,
            )
            for j, p in enumerate(peers)
        ]
        for cp in sends:
            cp.start()
        for j, p in enumerate(peers):
            pltpu.make_async_remote_copy(
                src_ref=out_ref.at[j], dst_ref=out_ref.at[j], send_sem=send_sems.at[j], recv_sem=recv_sems.at[j],
                device_id=p, device_id_type=MESH,
            ).wait_recv()
        for cp in sends:
            cp.wait_send()

    hbm = pl.BlockSpec(memory_space=pl.ANY)
    return pl.pallas_call(
        body,
        in_specs=[hbm],
        out_specs=hbm,
        out_shape=jax.ShapeDtypeStruct((3, R, D), gflat.dtype),
        scratch_shapes=[pltpu.SemaphoreType.DMA((3,)), pltpu.SemaphoreType.DMA((3,))],
        name=name,
    )(gflat)


def _swap_with_sibling(part, *, name):
    def body(p_ref, out_ref, send_sem, recv_sem):
        x, y, c = lax.axis_index("x"), lax.axis_index("y"), lax.axis_index("c")
        cp = pltpu.make_async_remote_copy(
            src_ref=p_ref, dst_ref=out_ref, send_sem=send_sem, recv_sem=recv_sem,
            device_id=(x, y, 1 - c), device_id_type=MESH,
        )
        cp.start()
        cp.wait()

    hbm = pl.BlockSpec(memory_space=pl.ANY)
    return pl.pallas_call(
        body,
        in_specs=[hbm],
        out_specs=hbm,
        out_shape=jax.ShapeDtypeStruct(part.shape, part.dtype),
        scratch_shapes=[pltpu.SemaphoreType.DMA, pltpu.SemaphoreType.DMA],
        name=name,
    )(part)


def _gather_from_all(vec, *, name):
    R, D = vec.shape

    def body(v_ref, out_ref, send_sems, recv_sems):
        x, y, c = lax.axis_index("x"), lax.axis_index("y"), lax.axis_index("c")
        me = 4 * x + 2 * y + c
        out_ref[me] = v_ref[...]
        rel = [(j >> 2 & 1, j >> 1 & 1, j & 1) for j in range(1, 8)]
        peers = [((1 - x) if fx else x, (1 - y) if fy else y, (1 - c) if fc else c) for fx, fy, fc in rel]
        sends = [
            pltpu.make_async_remote_copy(
                src_ref=v_ref, dst_ref=out_ref.at[me], send_sem=send_sems.at[j], recv_sem=recv_sems.at[j],
                device_id=p, device_id_type=MESH,
            )
            for j, p in enumerate(peers)
        ]
        for cp in sends:
            cp.start()
        for j, p in enumerate(peers):
            theirs = out_ref.at[4 * p[0] + 2 * p[1] + p[2]]
            pltpu.make_async_remote_copy(
                src_ref=theirs, dst_ref=theirs, send_sem=send_sems.at[j], recv_sem=recv_sems.at[j],
                device_id=p, device_id_type=MESH,
            ).wait_recv()
        for cp in sends:
            cp.wait_send()

    vmem = pl.BlockSpec(memory_space=pltpu.VMEM)
    return pl.pallas_call(
        body,
        in_specs=[vmem],
        out_specs=vmem,
        out_shape=jax.ShapeDtypeStruct((8, R, D), vec.dtype),
        scratch_shapes=[pltpu.SemaphoreType.DMA((7,)), pltpu.SemaphoreType.DMA((7,))],
        name=name,
    )(vec)


def _sum4(a, rest, *, name):
    R, D = a.shape
    tm = _pick(R, (256, 128, 8))

    def body(a_ref, r_ref, o_ref):
        o_ref[...] = ((a_ref[...] + r_ref[0].astype(F32)) + r_ref[1].astype(F32)) + r_ref[2].astype(F32)

    row = pl.BlockSpec((tm, D), lambda i: (i, 0))
    return pl.pallas_call(
        body,
        grid=(R // tm,),
        in_specs=[row, pl.BlockSpec((3, tm, D), lambda i: (0, i, 0))],
        out_specs=row,
        out_shape=jax.ShapeDtypeStruct((R, D), F32),
        compiler_params=_params("parallel"),
        name=name,
    )(a, rest)


def _adamw(parts, w, m, v, *, name):
    P, R, D = parts.shape
    tm = _pick(R, (256, 128, 8))

    def body(p_ref, w_ref, m_ref, v_ref, g_ref, d_ref, nm_ref, nv_ref):
        g = p_ref[0]
        for k in range(1, P):
            g = g + p_ref[k]
        nm = ADAM_B1 * m_ref[...] + (1.0 - ADAM_B1) * g
        nv = ADAM_B2 * v_ref[...] + (1.0 - ADAM_B2) * jnp.square(g)
        m_hat = nm / (1.0 - ADAM_B1**ADAM_STEP)
        v_hat = nv / (1.0 - ADAM_B2**ADAM_STEP)
        g_ref[...] = g
        d_ref[...] = -ADAM_LR * (m_hat / (jnp.sqrt(v_hat) + ADAM_EPS) + ADAM_WD * w_ref[...])
        nm_ref[...] = nm
        nv_ref[...] = nv

    row = pl.BlockSpec((tm, D), lambda i: (i, 0))
    return pl.pallas_call(
        body,
        grid=(R // tm,),
        in_specs=[pl.BlockSpec((P, tm, D), lambda i: (0, i, 0)), row, row, row],
        out_specs=[row] * 4,
        out_shape=[jax.ShapeDtypeStruct((R, D), F32)] * 4,
        compiler_params=_params("parallel"),
        name=name,
    )(parts, w, m, v)


MATS = ("w_qkv", "w_o", "w_ff1", "w_ff2")
VECS = ("ln1_g", "ln1_b", "ln2_g", "ln2_b")
PARAM_ORDER = ("w_qkv", "w_o", "ln1_g", "ln1_b", "w_ff1", "w_ff2", "ln2_g", "ln2_b")


def _pack_rows(arrs, D):
    return jnp.concatenate([a.reshape(-1, D) for a in arrs], axis=0)


def kernel(x, w_qkv_0, w_o_0, ln1_g_0, ln1_b_0, w_ff1_0, w_ff2_0, ln2_g_0, ln2_b_0, w_qkv_1, w_o_1, ln1_g_1, ln1_b_1, w_ff1_1, w_ff2_1, ln2_g_1, ln2_b_1, loss_target, m_w_qkv_0, m_w_o_0, m_ln1_g_0, m_ln1_b_0, m_w_ff1_0, m_w_ff2_0, m_ln2_g_0, m_ln2_b_0, m_w_qkv_1, m_w_o_1, m_ln1_g_1, m_ln1_b_1, m_w_ff1_1, m_w_ff2_1, m_ln2_g_1, m_ln2_b_1, v_w_qkv_0, v_w_o_0, v_ln1_g_0, v_ln1_b_0, v_w_ff1_0, v_w_ff2_0, v_ln2_g_0, v_ln2_b_0, v_w_qkv_1, v_w_o_1, v_ln1_g_1, v_ln1_b_1, v_w_ff1_1, v_w_ff2_1, v_ln2_g_1, v_ln2_b_1):
    given = dict(locals())
    xs = x[0]
    target = loss_target[0]
    S, D = xs.shape
    C3 = 3 * D // N_CHIPS
    names = [f"{p}_{l}" for l in range(DEPTH) for p in PARAM_ORDER]
    mat_names = [f"{p}_{l}" for l in range(DEPTH) for p in MATS]
    vec_names = [f"{p}_{l}" for l in range(DEPTH) for p in VECS]
    rows = {"w_qkv": C3, "w_o": D // N_CHIPS, "w_ff1": D, "w_ff2": D}
    layer_rows = sum(rows.values())

    gathered = _allgather_weights(_pack_rows([given[n] for n in mat_names], D).astype(MXU), name="allgather_weights")
    W = []
    for l in range(DEPTH):
        r0 = l * layer_rows
        seg = {}
        for p in MATS:
            seg[p] = gathered[:, r0 : r0 + rows[p], :]
            r0 += rows[p]
        W.append(
            dict(
                qkv=seg["w_qkv"].reshape(N_CHIPS, D, C3),
                o=seg["w_o"].reshape(1, D, D),
                ff1=seg["w_ff1"],
                ff2=seg["w_ff2"].reshape(1, 4 * D, D),
            )
        )
    tables = _rope_tables(S)

    h, hb = xs, xs.astype(MXU)
    saved = []
    for l in range(DEPTH):
        g1, b1, g2, b2 = (given[f"{p}_{l}"] for p in VECS)
        qkv = _mm_nn(hb, W[l]["qkv"], a_op=None, out_dtype=F32, name=f"qkv_{l}")
        q, k, v = _qkv_post(qkv, tables if l == 1 else None, name=f"qkv_post_{l}")
        st = dict(hb=hb, q=q, k=k, v=v)
        if l == 0:
            o, car = _sb_fwd(q, k, v, name="stickbreak_fwd")
            st.update(car=car)
        else:
            qp, kp, vp = ([_perm(t, d) for d in DILATIONS] for t in (q, k, v))
            branches = [
                _dil_fwd(qp[i], kp[i], vp[i], branch=i, name=f"dilated_fwd_d{d}") for i, d in enumerate(DILATIONS)
            ]
            o, lse = _dil_combine(
                [_unperm(ob, d) for (ob, _), d in zip(branches, DILATIONS)],
                [_unperm(lb, d) for (_, lb), d in zip(branches, DILATIONS)],
                name="dilated_combine",
            )
            st.update(qp=qp, kp=kp, vp=vp, lse=lse)
        mix = _mm_nn(o, W[l]["o"], a_op=None, out_dtype=F32, name=f"attn_out_{l}")
        y1, x1, x1b = _ln_fwd(h, mix, g1, b1, name=f"ln1_{l}")
        hp = _mm_nn(x1b, W[l]["ff1"], a_op=None, out_dtype=F32, name=f"ff1_{l}")
        mlp = _mm_nn(hp, W[l]["ff2"], a_op="relu2", out_dtype=F32, name=f"ff2_{l}")
        y2, x2, x2b = _ln_fwd(x1, mlp, g2, b2, name=f"ln2_{l}")
        st.update(o=o, y1=y1, x1b=x1b, hp=hp, y2=y2)
        saved.append(st)
        h, hb = x2, x2b

    grads = {}
    top = saved[-1]
    dy2, dy2b, dg, db, loss_lanes = _ln_bwd_loss(
        top["y2"], given[f"ln2_g_{DEPTH - 1}"], given[f"ln2_b_{DEPTH - 1}"], target, name="loss_ln2_bwd"
    )
    loss = lax.psum(jnp.sum(loss_lanes) * (0.5 / D), ("x", "y", "c"))
    grads[f"ln2_g_{DEPTH - 1}"], grads[f"ln2_b_{DEPTH - 1}"] = dg, db
    grad_x = None
    for l in reversed(range(DEPTH)):
        st = saved[l]
        dhp = _mm_nt(dy2b, W[l]["ff2"], epi="relu2grad", extra=st["hp"], out_dtype=MXU, name=f"d_ff2_in_{l}")
        grads[f"w_ff2_{l}"] = _mm_tn(st["hp"], dy2b, a_op="relu2", groups=1, name=f"d_w_ff2_{l}")
        dx1 = _mm_nt(dhp, W[l]["ff1"], epi="residual", extra=dy2, out_dtype=F32, name=f"d_ff1_in_{l}")
        grads[f"w_ff1_{l}"] = _mm_tn(st["x1b"], dhp, a_op=None, groups=N_CHIPS, name=f"d_w_ff1_{l}")
        dy1, dy1b, grads[f"ln1_g_{l}"], grads[f"ln1_b_{l}"] = _ln_bwd(
            dx1, st["y1"], given[f"ln1_g_{l}"], name=f"ln1_bwd_{l}"
        )
        grads[f"w_o_{l}"] = _mm_tn(st["o"], dy1b, a_op=None, groups=1, name=f"d_w_o_{l}")
        if l == 0:
            do = _mm_nt(dy1b, W[l]["o"], epi=None, extra=None, out_dtype=MXU, name=f"d_attn_out_{l}")
            dq, dk, dv = _sb_bwd(st["q"], st["k"], st["v"], do, st["car"], name="stickbreak_bwd")
            dqkv = _qkv_bwd_post([dq], [dk], [dv], None, name=f"qkv_bwd_post_{l}")
        else:
            do = _mm_nt(dy1b, W[l]["o"], epi=None, extra=None, out_dtype=F32, name=f"d_attn_out_{l}")
            dqs, dks, dvs = [], [], []
            for i, d in enumerate(DILATIONS):
                ops = (st["qp"][i], st["kp"][i], st["vp"][i], _perm(do, d), _perm(st["o"], d), _perm(st["lse"], d))
                dqs.append(_unperm(_dil_bwd_dq(*ops, branch=i, name=f"dilated_bwd_dq_d{d}"), d))
                dk_d, dv_d = _dil_bwd_dkv(*ops, branch=i, name=f"dilated_bwd_dkv_d{d}")
                dks.append(_unperm(dk_d, d))
                dvs.append(_unperm(dv_d, d))
            dqkv = _qkv_bwd_post(dqs, dks, dvs, tables, name=f"qkv_bwd_post_{l}")
        dh = _mm_nt(dqkv, W[l]["qkv"], epi="residual", extra=dy1, out_dtype=F32, name=f"d_qkv_in_{l}")
        grads[f"w_qkv_{l}"] = _mm_tn(st["hb"], dqkv, a_op=None, groups=N_CHIPS, name=f"d_w_qkv_{l}")
        if l > 0:
            prev = saved[l - 1]
            dy2, dy2b, grads[f"ln2_g_{l - 1}"], grads[f"ln2_b_{l - 1}"] = _ln_bwd(
                dh, prev["y2"], given[f"ln2_g_{l - 1}"], name=f"ln2_bwd_{l - 1}"
            )
        else:
            grad_x = dh

    gflat = jnp.concatenate([grads[n].reshape(N_CHIPS, -1, D) for n in mat_names], axis=1)
    me = 2 * lax.axis_index("x") + lax.axis_index("y")
    own = lax.dynamic_index_in_dim(gflat, me, 0, keepdims=False)
    from_chips = _scatter_grads(gflat.astype(MXU), name="scatter_grads")
    part = _sum4(own, from_chips, name="sum_chips")
    sib = _swap_with_sibling(part, name="swap_sibling")
    pack = lambda prefix: _pack_rows([given[prefix + n] for n in mat_names], D)
    outs_m = _adamw(jnp.stack([part, sib]), pack(""), pack("m_"), pack("v_"), name="adamw_matrices")
    vflat = _pack_rows([grads[n] for n in vec_names], D)
    packv = lambda prefix: _pack_rows([given[prefix + n] for n in vec_names], D)
    outs_v = _adamw(_gather_from_all(vflat, name="gather_ln_grads"), packv(""), packv("m_"), packv("v_"), name="adamw_vectors")

    def unpack(flat_m, flat_v):
        out, r0 = {}, 0
        for n in mat_names:
            r = given[n].size // D
            out[n] = flat_m[r0 : r0 + r].reshape(given[n].shape)
            r0 += r
        for i, n in enumerate(vec_names):
            out[n] = flat_v[i]
        return [out[n] for n in names]

    result = [loss, grad_x[None]]
    for k in range(4):
        result += unpack(outs_m[k], outs_v[k])
    return tuple(result)
```

```python
import functools

import jax
import jax.numpy as jnp
from jax import lax
from jax.experimental import pallas as pl
from jax.experimental.pallas import tpu as pltpu

F32 = jnp.float32
MXU = jnp.bfloat16

HEAD_DIM = 64
LANES = 128
CHUNK = 128
ROPE_THETA = 500000.0
ROPE_HALF = 8
DILATIONS = (1, 4, 16)
DEPTH = 2
ALPHA = (2 * DEPTH) ** 0.25
LN_EPS = 1e-5
QK_SCALE = 0.125
ADAM_LR, ADAM_B1, ADAM_B2, ADAM_EPS, ADAM_WD, ADAM_STEP = 0.001, 0.9, 0.999, 1e-08, 0.01, 10
NEG = -1e30
UNDERFLOW = -104.0
SB_TQ = 2 * CHUNK
V7X_VMEM_LIMIT = 56 * 1024 * 1024
MESH = pl.DeviceIdType.MESH
NT = (((1,), (1,)), ((), ()))
TN = (((0,), (0,)), ((), ()))
N_CHIPS = 4
MM_ROWS = (1024, 512, 256, 128)


def _params(*sem):
    return pltpu.CompilerParams(dimension_semantics=sem, vmem_limit_bytes=V7X_VMEM_LIMIT)


def _pick(n, cands=(1024, 768, 512, 384, 256, 128)):
    for c in cands:
        if n % c == 0:
            return c
    raise ValueError(f"no tile for {n}")


def _mm_nn(a, b, *, a_op, out_dtype, name):
    M, K = a.shape
    G, Kb, Nc = b.shape
    assert K == Kb
    N = G * Nc
    tm, tn, tk = _pick(M, MM_ROWS), _pick(Nc), _pick(K)
    nbs, nk = Nc // tn, K // tk

    def body(a_ref, b_ref, o_ref, *acc):
        av = a_ref[...]
        if a_op == "relu2":
            av = jnp.square(jnp.maximum(av, 0.0))
        part = jnp.dot(av.astype(MXU), b_ref[...], preferred_element_type=F32)
        if nk == 1:
            o_ref[...] = part.astype(out_dtype)
            return
        (acc_ref,) = acc
        kk = pl.program_id(2)

        @pl.when(kk == 0)
        def _():
            acc_ref[...] = part

        @pl.when(kk > 0)
        def _():
            acc_ref[...] += part

        @pl.when(kk == nk - 1)
        def _():
            o_ref[...] = acc_ref[...].astype(out_dtype)

    return pl.pallas_call(
        body,
        grid=(M // tm, N // tn, nk),
        in_specs=[
            pl.BlockSpec((tm, tk), lambda i, j, k: (i, k)),
            pl.BlockSpec((None, tk, tn), lambda i, j, k: (j // nbs, k, j % nbs)),
        ],
        out_specs=pl.BlockSpec((tm, tn), lambda i, j, k: (i, j)),
        out_shape=jax.ShapeDtypeStruct((M, N), out_dtype),
        scratch_shapes=[pltpu.VMEM((tm, tn), F32)] if nk > 1 else [],
        compiler_params=_params("parallel", "parallel", "arbitrary"),
        name=name,
    )(a, b)


def _mm_nt(a, b, *, epi, extra, out_dtype, name):
    M, N = a.shape
    G, K, Nc = b.shape
    assert N == G * Nc
    tm, tko, tn = _pick(M, MM_ROWS), _pick(K), _pick(Nc)
    nbs, nn = Nc // tn, N // tn

    def body(*refs):
        if epi is None:
            a_ref, b_ref, o_ref, acc_ref = refs
        else:
            a_ref, b_ref, e_ref, o_ref, acc_ref = refs
        kk = pl.program_id(2)
        part = lax.dot_general(a_ref[...].astype(MXU), b_ref[...], NT, preferred_element_type=F32)

        @pl.when(kk == 0)
        def _():
            acc_ref[...] = part

        @pl.when(kk > 0)
        def _():
            acc_ref[...] += part

        @pl.when(kk == nn - 1)
        def _():
            r = acc_ref[...]
            if epi == "relu2grad":
                r = r * (2.0 * jnp.maximum(e_ref[...], 0.0))
            elif epi == "residual":
                r = ALPHA * e_ref[...] + r
            o_ref[...] = r.astype(out_dtype)

    in_specs = [
        pl.BlockSpec((tm, tn), lambda i, j, k: (i, k)),
        pl.BlockSpec((None, tko, tn), lambda i, j, k: (k // nbs, j, k % nbs)),
    ]
    ops = [a, b]
    if epi is not None:
        in_specs.append(pl.BlockSpec((tm, tko), lambda i, j, k: (i, j)))
        ops.append(extra)
    return pl.pallas_call(
        body,
        grid=(M // tm, K // tko, nn),
        in_specs=in_specs,
        out_specs=pl.BlockSpec((tm, tko), lambda i, j, k: (i, j)),
        out_shape=jax.ShapeDtypeStruct((M, K), out_dtype),
        scratch_shapes=[pltpu.VMEM((tm, tko), F32)],
        compiler_params=_params("parallel", "parallel", "arbitrary"),
        name=name,
    )(*ops)


def _mm_tn(a, b, *, a_op, groups, name):
    M, Ka = a.shape
    Mb, N = b.shape
    assert M == Mb
    Nc = N // groups
    tka, tn, tm = _pick(Ka), _pick(Nc), _pick(M, (2048, 1024, 512, 256, 128))
    nbs, nm = Nc // tn, M // tm

    def body(a_ref, b_ref, o_ref, acc_ref):
        kk = pl.program_id(2)
        av = a_ref[...]
        if a_op == "relu2":
            av = jnp.square(jnp.maximum(av, 0.0))
        part = lax.dot_general(av.astype(MXU), b_ref[...].astype(MXU), TN, preferred_element_type=F32)

        @pl.when(kk == 0)
        def _():
            acc_ref[...] = part

        @pl.when(kk > 0)
        def _():
            acc_ref[...] += part

        @pl.when(kk == nm - 1)
        def _():
            o_ref[...] = acc_ref[...]

    return pl.pallas_call(
        body,
        grid=(Ka // tka, N // tn, nm),
        in_specs=[
            pl.BlockSpec((tm, tka), lambda i, j, k: (k, i)),
            pl.BlockSpec((tm, tn), lambda i, j, k: (k, j)),
        ],
        out_specs=pl.BlockSpec((None, tka, tn), lambda i, j, k: (j // nbs, i, j % nbs)),
        out_shape=jax.ShapeDtypeStruct((groups, Ka, Nc), F32),
        scratch_shapes=[pltpu.VMEM((tka, tn), F32)],
        compiler_params=_params("parallel", "parallel", "arbitrary"),
        name=name,
    )(a, b)


def _ln_stats(y):
    mu = jnp.mean(y, axis=-1, keepdims=True)
    yc = y - mu
    var = jnp.mean(yc * yc, axis=-1, keepdims=True)
    rstd = lax.rsqrt(var + LN_EPS)
    return yc * rstd, rstd


def _ln_fwd(x, mix, g, b, *, name):
    S, D = x.shape
    tm = _pick(S, (256, 128))

    def body(x_ref, m_ref, g_ref, b_ref, y_ref, o_ref, ob_ref):
        y = ALPHA * x_ref[...] + m_ref[...]
        xhat, _ = _ln_stats(y)
        y_ref[...] = y
        out = xhat * g_ref[...] + b_ref[...]
        o_ref[...] = out
        ob_ref[...] = out.astype(MXU)

    row = pl.BlockSpec((tm, D), lambda i: (i, 0))
    vec = pl.BlockSpec((1, D), lambda i: (0, 0))
    return pl.pallas_call(
        body,
        grid=(S // tm,),
        in_specs=[row, row, vec, vec],
        out_specs=[row, row, row],
        out_shape=[jax.ShapeDtypeStruct((S, D), F32)] * 2 + [jax.ShapeDtypeStruct((S, D), MXU)],
        compiler_params=_params("parallel"),
        name=name,
    )(x, mix, g.reshape(1, D), b.reshape(1, D))


def _ln_bwd(dx, y, g, *, name):
    S, D = y.shape
    tm = _pick(S, (256, 128))

    def body(dx_ref, y_ref, g_ref, dy_ref, dyb_ref, dg_ref, db_ref):
        xhat, rstd = _ln_stats(y_ref[...])
        dx_v = dx_ref[...]
        dxh = dx_v * g_ref[...]
        m1 = jnp.mean(dxh, axis=-1, keepdims=True)
        m2 = jnp.mean(dxh * xhat, axis=-1, keepdims=True)
        dy = rstd * (dxh - m1 - xhat * m2)
        dy_ref[...] = dy
        dyb_ref[...] = dy.astype(MXU)

        @pl.when(pl.program_id(0) == 0)
        def _():
            dg_ref[...] = jnp.zeros_like(dg_ref)
            db_ref[...] = jnp.zeros_like(db_ref)

        dg_ref[...] += jnp.sum(dx_v * xhat, axis=0, keepdims=True)
        db_ref[...] += jnp.sum(dx_v, axis=0, keepdims=True)

    row = pl.BlockSpec((tm, D), lambda i: (i, 0))
    vec = pl.BlockSpec((1, D), lambda i: (0, 0))
    return pl.pallas_call(
        body,
        grid=(S // tm,),
        in_specs=[row, row, vec],
        out_specs=[row, row, vec, vec],
        out_shape=[jax.ShapeDtypeStruct((S, D), F32), jax.ShapeDtypeStruct((S, D), MXU)]
        + [jax.ShapeDtypeStruct((1, D), F32)] * 2,
        compiler_params=_params("arbitrary"),
        name=name,
    )(dx, y, g.reshape(1, D))


def _ln_bwd_loss(y, g, b, target, *, name):
    S, D = y.shape
    tm = _pick(S, (256, 128))

    def body(y_ref, g_ref, b_ref, t_ref, dy_ref, dyb_ref, dg_ref, db_ref, ls_ref):
        xhat, rstd = _ln_stats(y_ref[...])
        err = xhat * g_ref[...] + b_ref[...] - t_ref[...]
        dx_v = err * (1.0 / D)
        dxh = dx_v * g_ref[...]
        m1 = jnp.mean(dxh, axis=-1, keepdims=True)
        m2 = jnp.mean(dxh * xhat, axis=-1, keepdims=True)
        dy = rstd * (dxh - m1 - xhat * m2)
        dy_ref[...] = dy
        dyb_ref[...] = dy.astype(MXU)

        @pl.when(pl.program_id(0) == 0)
        def _():
            dg_ref[...] = jnp.zeros_like(dg_ref)
            db_ref[...] = jnp.zeros_like(db_ref)
            ls_ref[...] = jnp.zeros_like(ls_ref)

        dg_ref[...] += jnp.sum(dx_v * xhat, axis=0, keepdims=True)
        db_ref[...] += jnp.sum(dx_v, axis=0, keepdims=True)
        ls_ref[...] += jnp.sum(err * err, axis=0, keepdims=True)

    row = pl.BlockSpec((tm, D), lambda i: (i, 0))
    vec = pl.BlockSpec((1, D), lambda i: (0, 0))
    return pl.pallas_call(
        body,
        grid=(S // tm,),
        in_specs=[row, vec, vec, row],
        out_specs=[row, row, vec, vec, vec],
        out_shape=[jax.ShapeDtypeStruct((S, D), F32), jax.ShapeDtypeStruct((S, D), MXU)]
        + [jax.ShapeDtypeStruct((1, D), F32)] * 3,
        compiler_params=_params("arbitrary"),
        name=name,
    )(y, g.reshape(1, D), b.reshape(1, D), target)


def _rope_tables(S):
    inv_freq = ROPE_THETA ** (-jnp.arange(ROPE_HALF, dtype=F32) / ROPE_HALF)
    ang = jnp.arange(S, dtype=jnp.int32).astype(F32)[:, None] * inv_freq[None, :]
    cos, sin = jnp.cos(ang), jnp.sin(ang)
    ones = jnp.ones((S, HEAD_DIM - 2 * ROPE_HALF), F32)
    zeros8 = jnp.zeros((S, ROPE_HALF), F32)
    c = jnp.concatenate([cos, cos, ones], axis=1)
    s1 = jnp.concatenate([zeros8, sin, 0.0 * ones], axis=1)
    s2 = jnp.concatenate([-sin, zeros8, 0.0 * ones], axis=1)
    return tuple(jnp.concatenate([t, t], axis=1) for t in (c, s1, s2))


def _rope(xb, c, s1, s2):
    return xb * c + pltpu.roll(xb, ROPE_HALF, 1) * s1 + pltpu.roll(xb, LANES - ROPE_HALF, 1) * s2


def _rope_t(dyb, c, s1, s2):
    return dyb * c + pltpu.roll(dyb * s1, LANES - ROPE_HALF, 1) + pltpu.roll(dyb * s2, ROPE_HALF, 1)


def _qkv_post(qkv, tables, *, name):
    S, D3 = qkv.shape
    D = D3 // 3
    tm = _pick(S, (256, 128))
    rope = tables is not None

    def body(*refs):
        x_ref, outs = refs[0], refs[-3:]
        tabs = tuple(r[...] for r in refs[1:-3])
        for j, o_ref in enumerate(outs):
            for blk in range(D // LANES):
                cols = slice(blk * LANES, (blk + 1) * LANES)
                xb = x_ref[:, j * D + blk * LANES : j * D + (blk + 1) * LANES]
                if rope and j < 2:
                    xb = _rope(xb, *tabs)
                if j == 0:
                    xb = xb * QK_SCALE
                o_ref[:, cols] = xb.astype(MXU)

    tab = pl.BlockSpec((tm, LANES), lambda i: (i, 0))
    row = pl.BlockSpec((tm, D), lambda i: (i, 0))
    return pl.pallas_call(
        body,
        grid=(S // tm,),
        in_specs=[pl.BlockSpec((tm, D3), lambda i: (i, 0))] + ([tab] * 3 if rope else []),
        out_specs=[row] * 3,
        out_shape=[jax.ShapeDtypeStruct((S, D), MXU)] * 3,
        compiler_params=_params("parallel"),
        name=name,
    )(qkv, *(tables if rope else ()))


def _qkv_bwd_post(dqs, dks, dvs, tables, *, name):
    S, D = dqs[0].shape
    tm = _pick(S, (256, 128))
    nb = len(dqs)
    rope = tables is not None

    def body(*refs):
        ins, o_ref = refs[:-1], refs[-1]
        groups = [ins[0:nb], ins[nb : 2 * nb], ins[2 * nb : 3 * nb]]
        tabs = [r[...] for r in ins[3 * nb :]]
        for j, grp in enumerate(groups):
            for blk in range(D // LANES):
                cols = slice(blk * LANES, (blk + 1) * LANES)
                v = grp[0][:, cols]
                for r in grp[1:]:
                    v = v + r[:, cols]
                if rope and j < 2:
                    v = _rope_t(v, *tabs)
                if j == 0:
                    v = v * QK_SCALE
                o_ref[:, j * D + blk * LANES : j * D + (blk + 1) * LANES] = v.astype(MXU)

    row = pl.BlockSpec((tm, D), lambda i: (i, 0))
    tab = pl.BlockSpec((tm, LANES), lambda i: (i, 0))
    return pl.pallas_call(
        body,
        grid=(S // tm,),
        in_specs=[row] * (3 * nb) + ([tab] * 3 if rope else []),
        out_specs=pl.BlockSpec((tm, 3 * D), lambda i: (i, 0)),
        out_shape=jax.ShapeDtypeStruct((S, 3 * D), MXU),
        compiler_params=_params("parallel"),
        name=name,
    )(*dqs, *dks, *dvs, *(tables if rope else ()))


def _log_fail(z):
    return -(jnp.maximum(z, 0.0) + jnp.log1p(jnp.exp(-jnp.abs(z))))


def _sb_fwd(q, k, v, *, name):
    S, D = q.shape
    HP, H = D // LANES, D // HEAD_DIM
    tq = SB_TQ
    assert S % tq == 0
    cpb = tq // CHUNK
    assert S // CHUNK <= LANES

    def body(q_ref, k_ref, v_ref, o_ref, car_ref, acc_ref, carry0_ref, carry1_ref):
        i = pl.program_id(1)
        lane = lax.broadcasted_iota(jnp.int32, (tq, LANES), 1)
        row = lax.broadcasted_iota(jnp.int32, (tq, LANES), 0)
        kl = lax.broadcasted_iota(jnp.int32, (CHUNK, LANES), 1)
        kr = lax.broadcasted_iota(jnp.int32, (CHUNK, LANES), 0)
        tri = (kr >= kl).astype(MXU)
        qv = q_ref[...]
        qh = (jnp.where(lane < HEAD_DIM, qv, jnp.zeros_like(qv)), jnp.where(lane >= HEAD_DIM, qv, jnp.zeros_like(qv)))
        carry_refs = (carry0_ref, carry1_ref)
        acc_ref[...] = jnp.zeros_like(acc_ref)
        car_ref[...] = jnp.full((2, tq, LANES), NEG, F32)
        for r in carry_refs:
            r[...] = jnp.zeros_like(r)

        def run(chunks, masked):
            offs = [pl.multiple_of(c * CHUNK, CHUNK) for c in chunks]
            kcs = [k_ref[pl.ds(o, CHUNK), :] for o in offs]
            vcs = [v_ref[pl.ds(o, CHUNK), :] for o in offs]
            jobs = [(n, hh) for n in range(len(chunks)) for hh in range(2)]
            zs = {j: lax.dot_general(qh[j[1]], kcs[j[0]], NT, preferred_element_type=F32) for j in jobs}
            if masked:
                cms = [(o + lane) < (i * tq + row) for o in offs]
            lfs = {}
            for j in jobs:
                lf = _log_fail(zs[j])
                lfs[j] = jnp.where(cms[j[0]], lf, 0.0) if masked else lf
            rs = {j: jnp.dot(lfs[j].astype(MXU), tri, preferred_element_type=F32) for j in jobs}
            ws = {}
            for hh in range(2):
                carry = carry_refs[hh][...]
                tile = car_ref[hh]
                for n, c in enumerate(chunks):
                    w = jnp.exp(zs[n, hh] + rs[n, hh] + carry)
                    if masked:
                        w = jnp.where(cms[n], w, 0.0)
                    ws[n, hh] = w.astype(MXU)
                    tile = jnp.where(lane == c, carry, tile)
                    carry = carry + jnp.sum(lfs[n, hh], axis=-1, keepdims=True)
                car_ref[hh] = tile
                carry_refs[hh][...] = carry
            for hh in range(2):
                pv = jnp.dot(ws[0, hh], vcs[0], preferred_element_type=F32)
                for n in range(1, len(chunks)):
                    pv = pv + jnp.dot(ws[n, hh], vcs[n], preferred_element_type=F32)
                acc_ref[hh] += pv

        def max_carry():
            return jnp.maximum(jnp.max(carry0_ref[...]), jnp.max(carry1_ref[...]))

        run([i * cpb + u for u in reversed(range(cpb))], True)

        def cond(st):
            return jnp.logical_and(st[0] >= 0, st[1] >= UNDERFLOW)

        def pair(st):
            c = st[0]
            run([c, c - 1], False)
            return c - 2, max_carry()

        lax.while_loop(cond, pair, (i * cpb - 1, max_carry()))
        o_ref[...] = jnp.where(lane < HEAD_DIM, acc_ref[0], acc_ref[1]).astype(o_ref.dtype)

    blk = pl.BlockSpec((tq, LANES), lambda h, i: (i, h))
    full = pl.BlockSpec((S, LANES), lambda h, i: (0, h))
    return pl.pallas_call(
        body,
        grid=(HP, S // tq),
        in_specs=[blk, full, full],
        out_specs=[blk, pl.BlockSpec((2, tq, LANES), lambda h, i: (h, i, 0))],
        out_shape=[jax.ShapeDtypeStruct((S, D), MXU), jax.ShapeDtypeStruct((H, S, LANES), F32)],
        scratch_shapes=[pltpu.VMEM((2, tq, LANES), F32), pltpu.VMEM((tq, 1), F32), pltpu.VMEM((tq, 1), F32)],
        compiler_params=_params("parallel", "arbitrary"),
        name=name,
    )(q, k, v)


def _sb_bwd(q, k, v, do, car, *, name):
    S, D = q.shape
    HP = D // LANES
    tq = SB_TQ
    assert S % tq == 0
    cpb = tq // CHUNK
    nq = S // tq

    def body(q_ref, k_ref, v_ref, do_ref, car_ref, dq_ref, dk_hbm, dv_hbm, dq_acc, dk_acc, dv_acc, gcar0_ref, gcar1_ref, sem):
        hp = pl.program_id(0)
        i = pl.program_id(1)
        lane = lax.broadcasted_iota(jnp.int32, (tq, LANES), 1)
        row = lax.broadcasted_iota(jnp.int32, (tq, LANES), 0)
        kl = lax.broadcasted_iota(jnp.int32, (CHUNK, LANES), 1)
        kr = lax.broadcasted_iota(jnp.int32, (CHUNK, LANES), 0)
        tri = (kr >= kl).astype(MXU)
        tri_prefix = (kr <= kl).astype(MXU)

        @pl.when(i == 0)
        def _():
            dk_acc[...] = jnp.zeros_like(dk_acc)
            dv_acc[...] = jnp.zeros_like(dv_acc)

        qv = q_ref[...]
        dov = do_ref[...]
        qh = (jnp.where(lane < HEAD_DIM, qv, jnp.zeros_like(qv)), jnp.where(lane >= HEAD_DIM, qv, jnp.zeros_like(qv)))
        doh = (jnp.where(lane < HEAD_DIM, dov, jnp.zeros_like(dov)), jnp.where(lane >= HEAD_DIM, dov, jnp.zeros_like(dov)))
        gcar_refs = (gcar0_ref, gcar1_ref)
        dq_acc[...] = jnp.zeros_like(dq_acc)
        for r in gcar_refs:
            r[...] = jnp.zeros_like(r)

        def run(chunks, masked):
            offs = [pl.multiple_of(c * CHUNK, CHUNK) for c in chunks]
            kcs = [k_ref[pl.ds(o, CHUNK), :] for o in offs]
            vcs = [v_ref[pl.ds(o, CHUNK), :] for o in offs]
            jobs = [(n, hh) for n in range(len(chunks)) for hh in range(2)]
            if masked:
                cms = [(o + lane) < (i * tq + row) for o in offs]
            zs = {j: lax.dot_general(qh[j[1]], kcs[j[0]], NT, preferred_element_type=F32) for j in jobs}
            dws = {j: lax.dot_general(doh[j[1]], vcs[j[0]], NT, preferred_element_type=F32) for j in jobs}
            lfs, sigs = {}, {}
            for j in jobs:
                lf = _log_fail(zs[j])
                sigs[j] = jnp.exp(zs[j] + lf)
                lfs[j] = jnp.where(cms[j[0]], lf, 0.0) if masked else lf
            rs = {j: jnp.dot(lfs[j].astype(MXU), tri, preferred_element_type=F32) for j in jobs}
            ws, gs = {}, {}
            for hh in range(2):
                tile = car_ref[hh]
                for n, c in enumerate(chunks):
                    carry = jnp.sum(jnp.where(lane == c, tile, 0.0), axis=-1, keepdims=True)
                    w = jnp.exp(zs[n, hh] + rs[n, hh] + carry)
                    if masked:
                        w = jnp.where(cms[n], w, 0.0)
                    ws[n, hh] = w.astype(MXU)
                    gs[n, hh] = w * dws[n, hh]
            big_gs = {j: jnp.dot(gs[j].astype(MXU), tri_prefix, preferred_element_type=F32) for j in jobs}
            dzs = {}
            for hh in range(2):
                gcar = gcar_refs[hh][...]
                for n in range(len(chunks)):
                    dz = gs[n, hh] - sigs[n, hh] * (big_gs[n, hh] + gcar)
                    if masked:
                        dz = jnp.where(cms[n], dz, 0.0)
                    dzs[n, hh] = dz.astype(MXU)
                    gcar = gcar + jnp.sum(gs[n, hh], axis=-1, keepdims=True)
                gcar_refs[hh][...] = gcar
            for hh in range(2):
                part = jnp.dot(dzs[0, hh], kcs[0], preferred_element_type=F32)
                for n in range(1, len(chunks)):
                    part = part + jnp.dot(dzs[n, hh], kcs[n], preferred_element_type=F32)
                dq_acc[hh] += part
            for n, o in enumerate(offs):
                dk_acc[pl.ds(o, CHUNK), :] += lax.dot_general(
                    dzs[n, 0], qh[0], TN, preferred_element_type=F32
                ) + lax.dot_general(dzs[n, 1], qh[1], TN, preferred_element_type=F32)
                dv_acc[pl.ds(o, CHUNK), :] += lax.dot_general(
                    ws[n, 0], doh[0], TN, preferred_element_type=F32
                ) + lax.dot_general(ws[n, 1], doh[1], TN, preferred_element_type=F32)

        colmax = jnp.maximum(jnp.max(car_ref[0], axis=0, keepdims=True), jnp.max(car_ref[1], axis=0, keepdims=True))
        lane1 = lax.broadcasted_iota(jnp.int32, (1, LANES), 1)
        dead = jnp.logical_and(colmax < UNDERFLOW, lane1 < i * cpb)
        first_pair = jnp.sum(jnp.where(dead, 1.0, 0.0)).astype(jnp.int32) // 2

        def loop_body(p, carry):
            run([2 * p, 2 * p + 1], False)
            return carry

        lax.fori_loop(first_pair, i * cpb // 2, loop_body, 0)
        run([i * cpb + u for u in range(cpb)], True)
        dq_ref[...] = jnp.where(lane < HEAD_DIM, dq_acc[0], dq_acc[1])

        @pl.when(i == nq - 1)
        def _():
            cols = pl.ds(pl.multiple_of(hp * LANES, LANES), LANES)
            for src, dst in ((dk_acc, dk_hbm), (dv_acc, dv_hbm)):
                cp = pltpu.make_async_copy(src, dst.at[:, cols], sem)
                cp.start()
                cp.wait()

    blk = pl.BlockSpec((tq, LANES), lambda h, i: (i, h))
    full = pl.BlockSpec((S, LANES), lambda h, i: (0, h))
    hbm = pl.BlockSpec(memory_space=pl.ANY)
    return pl.pallas_call(
        body,
        grid=(HP, nq),
        in_specs=[blk, full, full, blk, pl.BlockSpec((2, tq, LANES), lambda h, i: (h, i, 0))],
        out_specs=[blk, hbm, hbm],
        out_shape=[jax.ShapeDtypeStruct((S, D), F32)] * 3,
        scratch_shapes=[
            pltpu.VMEM((2, tq, LANES), F32),
            pltpu.VMEM((S, LANES), F32),
            pltpu.VMEM((S, LANES), F32),
            pltpu.VMEM((tq, 1), F32),
            pltpu.VMEM((tq, 1), F32),
            pltpu.SemaphoreType.DMA,
        ],
        compiler_params=_params("arbitrary", "arbitrary"),
        name=name,
    )(q, k, v, do, car)


def _perm(x, d):
    if d == 1:
        return x
    S, D = x.shape
    return x.reshape(S // d, d, D).transpose(1, 0, 2).reshape(S, D)


def _unperm(x, d):
    if d == 1:
        return x
    S, D = x.shape
    return x.reshape(d, S // d, D).transpose(1, 0, 2).reshape(S, D)


def _perm3(x):
    return jnp.stack([_perm(x, d) for d in DILATIONS])


def _dil_tb(S):
    return min(1024, S // DILATIONS[-1])


def _dil_masks():
    qi = lax.broadcasted_iota(jnp.int32, (CHUNK, CHUNK), 0)
    kj = lax.broadcasted_iota(jnp.int32, (CHUNK, CHUNK), 1)
    return qi, kj


def _dil_fwd(q, k, v, *, branch, name):
    S, D = q.shape
    HP = D // LANES
    tb = _dil_tb(S)
    nb, nsub = S // tb, tb // CHUNK
    bps = nb // DILATIONS[branch]

    def body(q_ref, k_ref, v_ref, kp_ref, vp_ref, o_ref, l_ref):
        n = pl.program_id(1)
        first = jnp.bitwise_and(n, bps - 1) == 0
        qi, kj = _dil_masks()
        hl = lax.broadcasted_iota(jnp.int32, (CHUNK, LANES), 1)
        m_cur = kj <= qi
        m_prev = (kj - qi) >= 0
        m_prev0 = (kj - qi) >= jnp.where(first, 2 * CHUNK, 0)
        hms = (hl < HEAD_DIM, hl >= HEAD_DIM)
        tiles, scores = [], {}
        for u in range(nsub):
            rows = slice(u * CHUNK, (u + 1) * CHUNK)
            qs, kc, vc = q_ref[rows, :], k_ref[rows, :], v_ref[rows, :]
            if u == 0:
                kp, vp, pm = kp_ref[...], vp_ref[...], m_prev0
            else:
                prev = slice((u - 1) * CHUNK, u * CHUNK)
                kp, vp, pm = k_ref[prev, :], v_ref[prev, :], m_prev
            tiles.append((rows, vp, vc, pm))
            for hh in range(2):
                qh = jnp.where(hms[hh], qs, jnp.zeros_like(qs))
                scores[u, hh] = (
                    lax.dot_general(qh, kp, NT, preferred_element_type=F32),
                    lax.dot_general(qh, kc, NT, preferred_element_type=F32),
                )
        probs = {}
        for (u, hh), (s_p, s_c) in scores.items():
            s_p = jnp.where(tiles[u][3], s_p, NEG)
            s_c = jnp.where(m_cur, s_c, NEG)
            m = jnp.max(jnp.maximum(s_p, s_c), axis=-1, keepdims=True)
            p_p, p_c = jnp.exp(s_p - m), jnp.exp(s_c - m)
            den = jnp.sum(p_p + p_c, axis=-1, keepdims=True)
            probs[u, hh] = (p_p.astype(MXU), p_c.astype(MXU), 1.0 / den, m + jnp.log(den))
        for u, (rows, vp, vc, _) in enumerate(tiles):
            o_t = l_t = None
            for hh in range(2):
                p_p, p_c, inv, lse = probs[u, hh]
                num = jnp.dot(p_p, vp, preferred_element_type=F32) + jnp.dot(p_c, vc, preferred_element_type=F32)
                o_h, l_h = num * inv, jnp.broadcast_to(lse, (CHUNK, LANES))
                o_t = o_h if hh == 0 else jnp.where(hms[1], o_h, o_t)
                l_t = l_h if hh == 0 else jnp.where(hms[1], l_h, l_t)
            o_ref[rows, :] = o_t
            l_ref[rows, :] = l_t

    blk = pl.BlockSpec((tb, LANES), lambda h, n: (n, h))
    tail = pl.BlockSpec((CHUNK, LANES), lambda h, n: (jnp.maximum(n * nsub - 1, 0), h))
    return pl.pallas_call(
        body,
        grid=(HP, nb),
        in_specs=[blk, blk, blk, tail, tail],
        out_specs=[blk, blk],
        out_shape=[jax.ShapeDtypeStruct((S, D), F32)] * 2,
        compiler_params=_params("parallel", "parallel"),
        name=name,
    )(q, k, v, k, v)


def _dil_combine(os, ls, *, name):
    S, D = os[0].shape
    tm = _pick(S, (256, 128))

    def body(o0_ref, o1_ref, o2_ref, l0_ref, l1_ref, l2_ref, out_ref, lse_ref):
        l0, l1, l2 = l0_ref[...], l1_ref[...], l2_ref[...]
        m = jnp.maximum(jnp.maximum(l0, l1), l2)
        a0, a1, a2 = jnp.exp(l0 - m), jnp.exp(l1 - m), jnp.exp(l2 - m)
        den = a0 + a1 + a2
        out_ref[...] = (a0 * o0_ref[...] + a1 * o1_ref[...] + a2 * o2_ref[...]) / den
        lse_ref[...] = m + jnp.log(den)

    row = pl.BlockSpec((tm, D), lambda i: (i, 0))
    return pl.pallas_call(
        body,
        grid=(S // tm,),
        in_specs=[row] * 6,
        out_specs=[row, row],
        out_shape=[jax.ShapeDtypeStruct((S, D), F32)] * 2,
        compiler_params=_params("parallel"),
        name=name,
    )(*os, *ls)


def _head_col(tile, hl, hh):
    return jnp.sum(jnp.where(hl == hh * HEAD_DIM, tile, 0.0), axis=-1, keepdims=True)


def _dil_bwd_dq(q, k, v, do, out, lse, *, branch, name):
    S, D = q.shape
    HP = D // LANES
    tb = _dil_tb(S)
    nb, nsub = S // tb, tb // CHUNK
    bps = nb // DILATIONS[branch]

    def body(q_ref, k_ref, v_ref, kp_ref, vp_ref, do_ref, out_ref, lse_ref, dq_ref):
        n = pl.program_id(1)
        first = jnp.bitwise_and(n, bps - 1) == 0
        qi, kj = _dil_masks()
        hl = lax.broadcasted_iota(jnp.int32, (CHUNK, LANES), 1)
        m_cur = kj <= qi
        m_prev = (kj - qi) >= 0
        m_prev0 = (kj - qi) >= jnp.where(first, 2 * CHUNK, 0)
        hms = (hl < HEAD_DIM, hl >= HEAD_DIM)
        jobs, cols = [], {}
        for u in range(nsub):
            rows = slice(u * CHUNK, (u + 1) * CHUNK)
            qs, kc, vc = q_ref[rows, :], k_ref[rows, :], v_ref[rows, :]
            if u == 0:
                kp, vp, pm = kp_ref[...], vp_ref[...], m_prev0
            else:
                prev = slice((u - 1) * CHUNK, u * CHUNK)
                kp, vp, pm = k_ref[prev, :], v_ref[prev, :], m_prev
            do_t, out_t, lse_t = do_ref[rows, :], out_ref[rows, :], lse_ref[rows, :]
            for hh in range(2):
                qh = jnp.where(hms[hh], qs, jnp.zeros_like(qs))
                doh = jnp.where(hms[hh], do_t, 0.0)
                cols[u, hh] = (jnp.sum(doh * out_t, axis=-1, keepdims=True), _head_col(lse_t, hl, hh))
                dob = doh.astype(MXU)
                for kk, vv, msk in ((kp, vp, pm), (kc, vc, m_cur)):
                    s = lax.dot_general(qh, kk, NT, preferred_element_type=F32)
                    dp = lax.dot_general(dob, vv, NT, preferred_element_type=F32)
                    jobs.append((u, hh, s, dp, kk, msk))
        dss = []
        for u, hh, s, dp, kk, msk in jobs:
            delta, lse = cols[u, hh]
            p = jnp.exp(jnp.where(msk, s, NEG) - lse)
            dss.append((p * (dp - delta)).astype(MXU))
        dq_h = {}
        for (u, hh, _, _, kk, _), ds in zip(jobs, dss):
            part = jnp.dot(ds, kk, preferred_element_type=F32)
            dq_h[u, hh] = part if (u, hh) not in dq_h else dq_h[u, hh] + part
        for u in range(nsub):
            dq_ref[u * CHUNK : (u + 1) * CHUNK, :] = jnp.where(hms[0], dq_h[u, 0], dq_h[u, 1])

    blk = pl.BlockSpec((tb, LANES), lambda h, n: (n, h))
    tail = pl.BlockSpec((CHUNK, LANES), lambda h, n: (jnp.maximum(n * nsub - 1, 0), h))
    return pl.pallas_call(
        body,
        grid=(HP, nb),
        in_specs=[blk, blk, blk, tail, tail, blk, blk, blk],
        out_specs=blk,
        out_shape=jax.ShapeDtypeStruct((S, D), F32),
        compiler_params=_params("parallel", "parallel"),
        name=name,
    )(q, k, v, k, v, do, out, lse)


def _dil_bwd_dkv(q, k, v, do, out, lse, *, branch, name):
    S, D = q.shape
    HP = D // LANES
    tb = _dil_tb(S)
    nb, nsub = S // tb, tb // CHUNK
    last_chunk = S // CHUNK - 1
    bps = nb // DILATIONS[branch]

    def body(q_ref, k_ref, v_ref, do_ref, out_ref, lse_ref, qn_ref, don_ref, outn_ref, lsen_ref, dk_ref, dv_ref):
        n = pl.program_id(1)
        last =jnp.bitwise_and(n + 1, bps - 1) == 0
        qi, kj = _dil_masks()
        hl = lax.broadcasted_iota(jnp.int32, (CHUNK, LANES), 1)
        m_cur = kj <= qi
        m_next = (kj - qi) >= 0
        m_next_last = (kj - qi) >= jnp.where(last, 2 * CHUNK, 0)
        hms = (hl < HEAD_DIM, hl >= HEAD_DIM)
        qtiles = {}
        for t in range(nsub + 1):
            if t == nsub:
                qs, do_t, out_t, lse_t = qn_ref[...], don_ref[...], outn_ref[...], lsen_ref[...]
            else:
                rows = slice(t * CHUNK, (t + 1) * CHUNK)
                qs, do_t, out_t, lse_t = q_ref[rows, :], do_ref[rows, :], out_ref[rows, :], lse_ref[rows, :]
            for hh in range(2):
                doh = jnp.where(hms[hh], do_t, 0.0)
                qtiles[t, hh] = (
                    jnp.where(hms[hh], qs, jnp.zeros_like(qs)),
                    doh.astype(MXU),
                    jnp.sum(doh * out_t, axis=-1, keepdims=True),
                    _head_col(lse_t, hl, hh),
                )
        jobs = []
        for u in range(nsub):
            rows = slice(u * CHUNK, (u + 1) * CHUNK)
            kc, vc = k_ref[rows, :], v_ref[rows, :]
            for t, msk in ((u, m_cur), (u + 1, m_next_last if u == nsub - 1 else m_next)):
                for hh in range(2):
                    qh, dob, _, _ = qtiles[t, hh]
                    s = lax.dot_general(qh, kc, NT, preferred_element_type=F32)
                    dp = lax.dot_general(dob, vc, NT, preferred_element_type=F32)
                    jobs.append((u, t, hh, s, dp, msk))
        pds = []
        for u, t, hh, s, dp, msk in jobs:
            _, _, delta, lse = qtiles[t, hh]
            p = jnp.exp(jnp.where(msk, s, NEG) - lse)
            pds.append((p.astype(MXU), (p * (dp - delta)).astype(MXU)))
        dks, dvs = {}, {}
        for (u, t, hh, _, _, _), (pb, dsb) in zip(jobs, pds):
            qh, dob, _, _ = qtiles[t, hh]
            dv_p = lax.dot_general(pb, dob, TN, preferred_element_type=F32)
            dk_p = lax.dot_general(dsb, qh, TN, preferred_element_type=F32)
            dvs[u] = dv_p if u not in dvs else dvs[u] + dv_p
            dks[u] = dk_p if u not in dks else dks[u] + dk_p
        for u in range(nsub):
            dk_ref[u * CHUNK : (u + 1) * CHUNK, :] = dks[u]
            dv_ref[u * CHUNK : (u + 1) * CHUNK, :] = dvs[u]

    blk = pl.BlockSpec((tb, LANES), lambda h, n: (n, h))
    head = pl.BlockSpec((CHUNK, LANES), lambda h, n: (jnp.minimum((n + 1) * nsub, last_chunk), h))
    return pl.pallas_call(
        body,
        grid=(HP, nb),
        in_specs=[blk] * 6 + [head] * 4,
        out_specs=[blk, blk],
        out_shape=[jax.ShapeDtypeStruct((S, D), F32)] * 2,
        compiler_params=_params("parallel", "parallel"),
        name=name,
    )(q, k, v, do, out, lse, q, do, out, lse)


def _chip_peers():
    x, y, c = lax.axis_index("x"), lax.axis_index("y"), lax.axis_index("c")
    peers = [(1 - x, y, c), (x, 1 - y, c), (1 - x, 1 - y, c)]
    return x, y, c, peers


def _allgather_weights(wsh, *, name):
    R, D = wsh.shape

    def body(w_ref, out_ref, send_sems, recv_sems, local_sem):
        x, y, c, peers = _chip_peers()
        me = 2 * x + y
        mine = pltpu.make_async_copy(w_ref, out_ref.at[me], local_sem)
        mine.start()
        sends = [
            pltpu.make_async_remote_copy(
                src_ref=w_ref, dst_ref=out_ref.at[me], send_sem=send_sems.at[j], recv_sem=recv_sems.at[j],
                device_id=p, device_id_type=MESH,
            )
            for j, p in enumerate(peers)
        ]
        for cp in sends:
            cp.start()
        for j, p in enumerate(peers):
            theirs = out_ref.at[2 * p[0] + p[1]]
            pltpu.make_async_remote_copy(
                src_ref=theirs, dst_ref=theirs, send_sem=send_sems.at[j], recv_sem=recv_sems.at[j],
                device_id=p, device_id_type=MESH,
            ).wait_recv()
        for cp in sends:
            cp.wait_send()
        mine.wait()

    hbm = pl.BlockSpec(memory_space=pl.ANY)
    return pl.pallas_call(
        body,
        in_specs=[hbm],
        out_specs=hbm,
        out_shape=jax.ShapeDtypeStruct((N_CHIPS, R, D), wsh.dtype),
        scratch_shapes=[pltpu.SemaphoreType.DMA((3,)), pltpu.SemaphoreType.DMA((3,)), pltpu.SemaphoreType.DMA],
        name=name,
    )(wsh)


def _scatter_grads(gflat, *, name):
    _, R, D = gflat.shape

    def body(g_ref, out_ref, send_sems, recv_sems):
        x, y, c, peers = _chip_peers()
        sends = [
            pltpu.make_async_remote_copy(
                src_ref=g_ref.at[2 * p[0] + p[1]], dst_ref=out_ref.at[j], send_sem=send_sems.at[j],
                recv_sem=recv_sems.at[j], device_id=p, device_id_type=MESH,
            )
            for j, p in enumerate(peers)
        ]
        for cp in sends:
            cp.start()
        for j, p in enumerate(peers):
            pltpu.make_async_remote_copy(
                src_ref=out_ref.at[j], dst_ref=out_ref.at[j], send_sem=send_sems.at[j], recv_sem=recv_sems.at[j],
                device_id=p, device_id_type=MESH,
            ).wait_recv()
        for cp in sends:
            cp.wait_send()

    hbm = pl.BlockSpec(memory_space=pl.ANY)
    return pl.pallas_call(
        body,
        in_specs=[hbm],
        out_specs=hbm,
        out_shape=jax.ShapeDtypeStruct((3, R, D), gflat.dtype),
        scratch_shapes=[pltpu.SemaphoreType.DMA((3,)), pltpu.SemaphoreType.DMA((3,))],
        name=name,
    )(gflat)


def _swap_with_sibling(part, *, name):
    def body(p_ref, out_ref, send_sem, recv_sem):
        x, y, c = lax.axis_index("x"), lax.axis_index("y"), lax.axis_index("c")
        cp = pltpu.make_async_remote_copy(
            src_ref=p_ref, dst_ref=out_ref, send_sem=send_sem, recv_sem=recv_sem,
            device_id=(x, y, 1 - c), device_id_type=MESH,
        )
        cp.start()
        cp.wait()

    hbm = pl.BlockSpec(memory_space=pl.ANY)
    return pl.pallas_call(
        body,
        in_specs=[hbm],
        out_specs=hbm,
        out_shape=jax.ShapeDtypeStruct(part.shape, part.dtype),
        scratch_shapes=[pltpu.SemaphoreType.DMA, pltpu.SemaphoreType.DMA],
        name=name,
    )(part)


def _gather_from_all(vec, *, name):
    R, D = vec.shape

    def body(v_ref, out_ref, send_sems, recv_sems):
        x, y, c = lax.axis_index("x"), lax.axis_index("y"), lax.axis_index("c")
        me = 4 * x + 2 * y + c
        out_ref[me] = v_ref[...]
        rel = [(j >> 2 & 1, j >> 1 & 1, j & 1) for j in range(1, 8)]
        peers = [((1 - x) if fx else x, (1 - y) if fy else y, (1 - c) if fc else c) for fx, fy, fc in rel]
        sends = [
            pltpu.make_async_remote_copy(
                src_ref=v_ref, dst_ref=out_ref.at[me], send_sem=send_sems.at[j], recv_sem=recv_sems.at[j],
                device_id=p, device_id_type=MESH,
            )
            for j, p in enumerate(peers)
        ]
        for cp in sends:
            cp.start()
        for j, p in enumerate(peers):
            theirs = out_ref.at[4 * p[0] + 2 * p[1] + p[2]]
            pltpu.make_async_remote_copy(
                src_ref=theirs, dst_ref=theirs, send_sem=send_sems.at[j], recv_sem=recv_sems.at[j],
                device_id=p, device_id_type=MESH,
            ).wait_recv()
        for cp in sends:
            cp.wait_send()

    vmem = pl.BlockSpec(memory_space=pltpu.VMEM)
    return pl.pallas_call(
        body,
        in_specs=[vmem],
        out_specs=vmem,
        out_shape=jax.ShapeDtypeStruct((8, R, D), vec.dtype),
        scratch_shapes=[pltpu.SemaphoreType.DMA((7,)), pltpu.SemaphoreType.DMA((7,))],
        name=name,
    )(vec)


def _sum4(a, rest, *, name):
    R, D = a.shape
    tm = _pick(R, (256, 128, 8))

    def body(a_ref, r_ref, o_ref):
        o_ref[...] = ((a_ref[...] + r_ref[0].astype(F32)) + r_ref[1].astype(F32)) + r_ref[2].astype(F32)

    row = pl.BlockSpec((tm, D), lambda i: (i, 0))
    return pl.pallas_call(
        body,
        grid=(R // tm,),
        in_specs=[row, pl.BlockSpec((3, tm, D), lambda i: (0, i, 0))],
        out_specs=row,
        out_shape=jax.ShapeDtypeStruct((R, D), F32),
        compiler_params=_params("parallel"),
        name=name,
    )(a, rest)


def _adamw(parts, w, m, v, *, name):
    P, R, D = parts.shape
    tm = _pick(R, (256, 128, 8))

    def body(p_ref, w_ref, m_ref, v_ref, g_ref, d_ref, nm_ref, nv_ref):
        g = p_ref[0]
        for k in range(1, P):
            g = g + p_ref[k]
        nm = ADAM_B1 * m_ref[...] + (1.0 - ADAM_B1) * g
        nv = ADAM_B2 * v_ref[...] + (1.0 - ADAM_B2) * jnp.square(g)
        m_hat = nm / (1.0 - ADAM_B1**ADAM_STEP)
        v_hat = nv / (1.0 - ADAM_B2**ADAM_STEP)
        g_ref[...] = g
        d_ref[...] = -ADAM_LR * (m_hat / (jnp.sqrt(v_hat) + ADAM_EPS) + ADAM_WD * w_ref[...])
        nm_ref[...] = nm
        nv_ref[...] = nv

    row = pl.BlockSpec((tm, D), lambda i: (i, 0))
    return pl.pallas_call(
        body,
        grid=(R // tm,),
        in_specs=[pl.BlockSpec((P, tm, D), lambda i: (0, i, 0)), row, row, row],
        out_specs=[row] * 4,
        out_shape=[jax.ShapeDtypeStruct((R, D), F32)] * 4,
        compiler_params=_params("parallel"),
        name=name,
    )(parts, w, m, v)


MATS = ("w_qkv", "w_o", "w_ff1", "w_ff2")
VECS = ("ln1_g", "ln1_b", "ln2_g", "ln2_b")
PARAM_ORDER = ("w_qkv", "w_o", "ln1_g", "ln1_b", "w_ff1", "w_ff2", "ln2_g", "ln2_b")


def _pack_rows(arrs, D):
    return jnp.concatenate([a.reshape(-1, D) for a in arrs], axis=0)


def kernel(x, w_qkv_0, w_o_0, ln1_g_0, ln1_b_0, w_ff1_0, w_ff2_0, ln2_g_0, ln2_b_0, w_qkv_1, w_o_1, ln1_g_1, ln1_b_1, w_ff1_1, w_ff2_1, ln2_g_1, ln2_b_1, loss_target, m_w_qkv_0, m_w_o_0, m_ln1_g_0, m_ln1_b_0, m_w_ff1_0, m_w_ff2_0, m_ln2_g_0, m_ln2_b_0, m_w_qkv_1, m_w_o_1, m_ln1_g_1, m_ln1_b_1, m_w_ff1_1, m_w_ff2_1, m_ln2_g_1, m_ln2_b_1, v_w_qkv_0, v_w_o_0, v_ln1_g_0, v_ln1_b_0, v_w_ff1_0, v_w_ff2_0, v_ln2_g_0, v_ln2_b_0, v_w_qkv_1, v_w_o_1, v_ln1_g_1, v_ln1_b_1, v_w_ff1_1, v_w_ff2_1, v_ln2_g_1, v_ln2_b_1):
    given = dict(locals())
    xs = x[0]
    target = loss_target[0]
    S, D = xs.shape
    C3 = 3 * D // N_CHIPS
    names = [f"{p}_{l}" for l in range(DEPTH) for p in PARAM_ORDER]
    mat_names = [f"{p}_{l}" for l in range(DEPTH) for p in MATS]
    vec_names = [f"{p}_{l}" for l in range(DEPTH) for p in VECS]
    rows = {"w_qkv": C3, "w_o": D // N_CHIPS, "w_ff1": D, "w_ff2": D}
    layer_rows = sum(rows.values())

    gathered = _allgather_weights(_pack_rows([given[n] for n in mat_names], D).astype(MXU), name="allgather_weights")
    W = []
    for l in range(DEPTH):
        r0 = l * layer_rows
        seg = {}
        for p in MATS:
            seg[p] = gathered[:, r0 : r0 + rows[p], :]
            r0 += rows[p]
        W.append(
            dict(
                qkv=seg["w_qkv"].reshape(N_CHIPS, D, C3),
                o=seg["w_o"].reshape(1, D, D),
                ff1=seg["w_ff1"],
                ff2=seg["w_ff2"].reshape(1, 4 * D, D),
            )
        )
    tables = _rope_tables(S)

    h, hb = xs, xs.astype(MXU)
    saved = []
    for l in range(DEPTH):
        g1, b1, g2, b2 = (given[f"{p}_{l}"] for p in VECS)
        qkv = _mm_nn(hb, W[l]["qkv"], a_op=None, out_dtype=F32, name=f"qkv_{l}")
        q, k, v = _qkv_post(qkv, tables if l == 1 else None, name=f"qkv_post_{l}")
        st = dict(hb=hb, q=q, k=k, v=v)
        if l == 0:
            o, car = _sb_fwd(q, k, v, name="stickbreak_fwd")
            st.update(car=car)
        else:
            qp, kp, vp = ([_perm(t, d) for d in DILATIONS] for t in (q, k, v))
            branches = [
                _dil_fwd(qp[i], kp[i], vp[i], branch=i, name=f"dilated_fwd_d{d}") for i, d in enumerate(DILATIONS)
            ]
            o, lse = _dil_combine(
                [_unperm(ob, d) for (ob, _), d in zip(branches, DILATIONS)],
                [_unperm(lb, d) for (_, lb), d in zip(branches, DILATIONS)],
                name="dilated_combine",
            )
            st.update(qp=qp, kp=kp, vp=vp, lse=lse)
        mix = _mm_nn(o, W[l]["o"], a_op=None, out_dtype=F32, name=f"attn_out_{l}")
        y1, x1, x1b = _ln_fwd(h, mix, g1, b1, name=f"ln1_{l}")
        hp = _mm_nn(x1b, W[l]["ff1"], a_op=None, out_dtype=F32, name=f"ff1_{l}")
        mlp = _mm_nn(hp, W[l]["ff2"], a_op="relu2", out_dtype=F32, name=f"ff2_{l}")
        y2, x2, x2b = _ln_fwd(x1, mlp, g2, b2, name=f"ln2_{l}")
        st.update(o=o, y1=y1, x1b=x1b, hp=hp, y2=y2)
        saved.append(st)
        h, hb = x2, x2b

    grads = {}
    top = saved[-1]
    dy2, dy2b, dg, db, loss_lanes = _ln_bwd_loss(
        top["y2"], given[f"ln2_g_{DEPTH - 1}"], given[f"ln2_b_{DEPTH - 1}"], target, name="loss_ln2_bwd"
    )
    loss = lax.psum(jnp.sum(loss_lanes) * (0.5 / D), ("x", "y", "c"))
    grads[f"ln2_g_{DEPTH - 1}"], grads[f"ln2_b_{DEPTH - 1}"] = dg, db
    grad_x = None
    for l in reversed(range(DEPTH)):
        st = saved[l]
        dhp = _mm_nt(dy2b, W[l]["ff2"], epi="relu2grad", extra=st["hp"], out_dtype=MXU, name=f"d_ff2_in_{l}")
        grads[f"w_ff2_{l}"] = _mm_tn(st["hp"], dy2b, a_op="relu2", groups=1, name=f"d_w_ff2_{l}")
        dx1 = _mm_nt(dhp, W[l]["ff1"], epi="residual", extra=dy2, out_dtype=F32, name=f"d_ff1_in_{l}")
        grads[f"w_ff1_{l}"] = _mm_tn(st["x1b"], dhp, a_op=None, groups=N_CHIPS, name=f"d_w_ff1_{l}")
        dy1, dy1b, grads[f"ln1_g_{l}"], grads[f"ln1_b_{l}"] = _ln_bwd(
            dx1, st["y1"], given[f"ln1_g_{l}"], name=f"ln1_bwd_{l}"
        )
        grads[f"w_o_{l}"] = _mm_tn(st["o"], dy1b, a_op=None, groups=1, name=f"d_w_o_{l}")
        if l == 0:
            do = _mm_nt(dy1b, W[l]["o"], epi=None, extra=None, out_dtype=MXU, name=f"d_attn_out_{l}")
            dq, dk, dv = _sb_bwd(st["q"], st["k"], st["v"], do, st["car"], name="stickbreak_bwd")
            dqkv = _qkv_bwd_post([dq], [dk], [dv], None, name=f"qkv_bwd_post_{l}")
        else:
            do = _mm_nt(dy1b, W[l]["o"], epi=None, extra=None, out_dtype=F32, name=f"d_attn_out_{l}")
            dqs, dks, dvs = [], [], []
            for i, d in enumerate(DILATIONS):
                ops = (st["qp"][i], st["kp"][i], st["vp"][i], _perm(do, d), _perm(st["o"], d), _perm(st["lse"], d))
                dqs.append(_unperm(_dil_bwd_dq(*ops, branch=i, name=f"dilated_bwd_dq_d{d}"), d))
                dk_d, dv_d = _dil_bwd_dkv(*ops, branch=i, name=f"dilated_bwd_dkv_d{d}")
                dks.append(_unperm(dk_d, d))
                dvs.append(_unperm(dv_d, d))
            dqkv = _qkv_bwd_post(dqs, dks, dvs, tables, name=f"qkv_bwd_post_{l}")
        dh = _mm_nt(dqkv, W[l]["qkv"], epi="residual", extra=dy1, out_dtype=F32, name=f"d_qkv_in_{l}")
        grads[f"w_qkv_{l}"] = _mm_tn(st["hb"], dqkv, a_op=None, groups=N_CHIPS, name=f"d_w_qkv_{l}")
        if l > 0:
            prev = saved[l - 1]
            dy2, dy2b, grads[f"ln2_g_{l - 1}"], grads[f"ln2_b_{l - 1}"] = _ln_bwd(
                dh, prev["y2"], given[f"ln2_g_{l - 1}"], name=f"ln2_bwd_{l - 1}"
            )
        else:
            grad_x = dh

    gflat = jnp.concatenate([grads[n].reshape(N_CHIPS, -1, D) for n in mat_names], axis=1)
    me = 2 * lax.axis_index("x") + lax.axis_index("y")
    own = lax.dynamic_index_in_dim(gflat, me, 0, keepdims=False)
    from_chips = _scatter_grads(gflat.astype(MXU), name="scatter_grads")
    part = _sum4(own, from_chips, name="sum_chips")
    sib = _swap_with_sibling(part, name="swap_sibling")
    pack = lambda prefix: _pack_rows([given[prefix + n] for n in mat_names], D)
    outs_m = _adamw(jnp.stack([part, sib]), pack(""), pack("m_"), pack("v_"), name="adamw_matrices")
    vflat = _pack_rows([grads[n] for n in vec_names], D)
    packv = lambda prefix: _pack_rows([given[prefix + n] for n in vec_names], D)
    outs_v = _adamw(_gather_from_all(vflat, name="gather_ln_grads"), packv(""), packv("m_"), packv("v_"), name="adamw_vectors")

    def unpack(flat_m, flat_v):
        out, r0 = {}, 0
        for n in mat_names:
            r = given[n].size // D
            out[n] = flat_m[r0 : r0 + r].reshape(given[n].shape)
            r0 += r
        for i, n in enumerate(vec_names):
            out[n] = flat_v[i]
        return [out[n] for n in names]

    result = [loss, grad_x[None]]
    for k in range(4):
        result += unpack(outs_m[k], outs_v[k])
    return tuple(result)
```

```python
import functools

import jax
import jax.numpy as jnp
from jax import lax
from jax.experimental import pallas as pl
from jax.experimental.pallas import tpu as pltpu

F32 = jnp.float32
MXU = jnp.bfloat16

HEAD_DIM = 64
LANES = 128
CHUNK = 128
ROPE_THETA = 500000.0
ROPE_HALF = 8
DILATIONS = (1, 4, 16)
DEPTH = 2
ALPHA = (2 * DEPTH) ** 0.25
LN_EPS = 1e-5
QK_SCALE = 0.125
ADAM_LR, ADAM_B1, ADAM_B2, ADAM_EPS, ADAM_WD, ADAM_STEP = 0.001, 0.9, 0.999, 1e-08, 0.01, 10
NEG = -1e30
UNDERFLOW = -104.0
SB_TQ = 2 * CHUNK
V7X_VMEM_LIMIT = 56 * 1024 * 1024
MESH = pl.DeviceIdType.MESH
NT = (((1,), (1,)), ((), ()))
TN = (((0,), (0,)), ((), ()))
N_CHIPS = 4
MM_ROWS = (1024, 512, 256, 128)


def _params(*sem):
    return pltpu.CompilerParams(dimension_semantics=sem, vmem_limit_bytes=V7X_VMEM_LIMIT)


def _pick(n, cands=(1024, 768, 512, 384, 256, 128)):
    for c in cands:
        if n % c == 0:
            return c
    raise ValueError(f"no tile for {n}")


def _mm_nn(a, b, *, a_op, out_dtype, name):
    M, K = a.shape
    G, Kb, Nc = b.shape
    assert K == Kb
    N = G * Nc
    tm, tn, tk = _pick(M, MM_ROWS), _pick(Nc), _pick(K)
    nbs, nk = Nc // tn, K // tk

    def body(a_ref, b_ref, o_ref, *acc):
        av = a_ref[...]
        if a_op == "relu2":
            av = jnp.square(jnp.maximum(av.astype(F32), 0.0))
        part = jnp.dot(av.astype(MXU), b_ref[...], preferred_element_type=F32)
        if nk == 1:
            o_ref[...] = part.astype(out_dtype)
            return
        (acc_ref,) = acc
        kk = pl.program_id(2)

        @pl.when(kk == 0)
        def _():
            acc_ref[...] = part

        @pl.when(kk > 0)
        def _():
            acc_ref[...] += part

        @pl.when(kk == nk - 1)
        def _():
            o_ref[...] = acc_ref[...].astype(out_dtype)

    return pl.pallas_call(
        body,
        grid=(M // tm, N // tn, nk),
        in_specs=[
            pl.BlockSpec((tm, tk), lambda i, j, k: (i, k)),
            pl.BlockSpec((None, tk, tn), lambda i, j, k: (j // nbs, k, j % nbs)),
        ],
        out_specs=pl.BlockSpec((tm, tn), lambda i, j, k: (i, j)),
        out_shape=jax.ShapeDtypeStruct((M, N), out_dtype),
        scratch_shapes=[pltpu.VMEM((tm, tn), F32)] if nk > 1 else [],
        compiler_params=_params("parallel", "parallel", "arbitrary"),
        name=name,
    )(a, b)


def _mm_nt(a, b, *, epi, extra, out_dtype, name):
    M, N = a.shape
    G, K, Nc = b.shape
    assert N == G * Nc
    tm, tko, tn = _pick(M, MM_ROWS), _pick(K), _pick(Nc)
    nbs, nn = Nc // tn, N // tn

    def body(*refs):
        if epi is None:
            a_ref, b_ref, o_ref, acc_ref = refs
        else:
            a_ref, b_ref, e_ref, o_ref, acc_ref = refs
        kk = pl.program_id(2)
        part = lax.dot_general(a_ref[...].astype(MXU), b_ref[...], NT, preferred_element_type=F32)

        @pl.when(kk == 0)
        def _():
            acc_ref[...] = part

        @pl.when(kk > 0)
        def _():
            acc_ref[...] += part

        @pl.when(kk == nn - 1)
        def _():
            r = acc_ref[...]
            if epi == "relu2grad":
                r = r * (2.0 * jnp.maximum(e_ref[...].astype(F32), 0.0))
            elif epi == "residual":
                r = ALPHA * e_ref[...] + r
            o_ref[...] = r.astype(out_dtype)

    in_specs = [
        pl.BlockSpec((tm, tn), lambda i, j, k: (i, k)),
        pl.BlockSpec((None, tko, tn), lambda i, j, k: (k // nbs, j, k % nbs)),
    ]
    ops = [a, b]
    if epi is not None:
        in_specs.append(pl.BlockSpec((tm, tko), lambda i, j, k: (i, j)))
        ops.append(extra)
    return pl.pallas_call(
        body,
        grid=(M // tm, K // tko, nn),
        in_specs=in_specs,
        out_specs=pl.BlockSpec((tm, tko), lambda i, j, k: (i, j)),
        out_shape=jax.ShapeDtypeStruct((M, K), out_dtype),
        scratch_shapes=[pltpu.VMEM((tm, tko), F32)],
        compiler_params=_params("parallel", "parallel", "arbitrary"),
        name=name,
    )(*ops)


def _mm_tn(a, b, *, a_op, groups, name):
    M, Ka = a.shape
    Mb, N = b.shape
    assert M == Mb
    Nc = N // groups
    tka, tn, tm = _pick(Ka), _pick(Nc), _pick(M, (2048, 1024, 512, 256, 128))
    nbs, nm = Nc // tn, M // tm

    def body(a_ref, b_ref, o_ref, acc_ref):
        kk = pl.program_id(2)
        av = a_ref[...]
        if a_op == "relu2":
            av = jnp.square(jnp.maximum(av.astype(F32), 0.0))
        part = lax.dot_general(av.astype(MXU), b_ref[...].astype(MXU), TN, preferred_element_type=F32)

        @pl.when(kk == 0)
        def _():
            acc_ref[...] = part

        @pl.when(kk > 0)
        def _():
            acc_ref[...] += part

        @pl.when(kk == nm - 1)
        def _():
            o_ref[...] = acc_ref[...]

    return pl.pallas_call(
        body,
        grid=(Ka // tka, N // tn, nm),
        in_specs=[
            pl.BlockSpec((tm, tka), lambda i, j, k: (k, i)),
            pl.BlockSpec((tm, tn), lambda i, j, k: (k, j)),
        ],
        out_specs=pl.BlockSpec((None, tka, tn), lambda i, j, k: (j // nbs, i, j % nbs)),
        out_shape=jax.ShapeDtypeStruct((groups, Ka, Nc), F32),
        scratch_shapes=[pltpu.VMEM((tka, tn), F32)],
        compiler_params=_params("parallel", "parallel", "arbitrary"),
        name=name,
    )(a, b)


def _ln_stats(y):
    mu = jnp.mean(y, axis=-1, keepdims=True)
    yc = y - mu
    var = jnp.mean(yc * yc, axis=-1, keepdims=True)
    rstd = lax.rsqrt(var + LN_EPS)
    return yc * rstd, rstd


def _ln_fwd(x, mix, g, b, *, name):
    S, D = x.shape
    tm = _pick(S, (256, 128))

    def body(x_ref, m_ref, g_ref, b_ref, y_ref, o_ref, ob_ref):
        y = ALPHA * x_ref[...] + m_ref[...]
        xhat, _ = _ln_stats(y)
        y_ref[...] = y
        out = xhat * g_ref[...] + b_ref[...]
        o_ref[...] = out
        ob_ref[...] = out.astype(MXU)

    row = pl.BlockSpec((tm, D), lambda i: (i, 0))
    vec = pl.BlockSpec((1, D), lambda i: (0, 0))
    return pl.pallas_call(
        body,
        grid=(S // tm,),
        in_specs=[row, row, vec, vec],
        out_specs=[row, row, row],
        out_shape=[jax.ShapeDtypeStruct((S, D), F32)] * 2 + [jax.ShapeDtypeStruct((S, D), MXU)],
        compiler_params=_params("parallel"),
        name=name,
    )(x, mix, g.reshape(1, D), b.reshape(1, D))


def _ln_bwd(dx, y, g, *, name):
    S, D = y.shape
    tm = _pick(S, (256, 128))

    def body(dx_ref, y_ref, g_ref, dy_ref, dyb_ref, dg_ref, db_ref):
        xhat, rstd = _ln_stats(y_ref[...])
        dx_v = dx_ref[...]
        dxh = dx_v * g_ref[...]
        m1 = jnp.mean(dxh, axis=-1, keepdims=True)
        m2 = jnp.mean(dxh * xhat, axis=-1, keepdims=True)
        dy = rstd * (dxh - m1 - xhat * m2)
        dy_ref[...] = dy
        dyb_ref[...] = dy.astype(MXU)

        @pl.when(pl.program_id(0) == 0)
        def _():
            dg_ref[...] = jnp.zeros_like(dg_ref)
            db_ref[...] = jnp.zeros_like(db_ref)

        dg_ref[...] += jnp.sum(dx_v * xhat, axis=0, keepdims=True)
        db_ref[...] += jnp.sum(dx_v, axis=0, keepdims=True)

    row = pl.BlockSpec((tm, D), lambda i: (i, 0))
    vec = pl.BlockSpec((1, D), lambda i: (0, 0))
    return pl.pallas_call(
        body,
        grid=(S // tm,),
        in_specs=[row, row, vec],
        out_specs=[row, row, vec, vec],
        out_shape=[jax.ShapeDtypeStruct((S, D), F32), jax.ShapeDtypeStruct((S, D), MXU)]
        + [jax.ShapeDtypeStruct((1, D), F32)] * 2,
        compiler_params=_params("arbitrary"),
        name=name,
    )(dx, y, g.reshape(1, D))


def _ln_bwd_loss(y, g, b, target, *, name):
    S, D = y.shape
    tm = _pick(S, (256, 128))

    def body(y_ref, g_ref, b_ref, t_ref, dy_ref, dyb_ref, dg_ref, db_ref, ls_ref):
        xhat, rstd = _ln_stats(y_ref[...])
        err = xhat * g_ref[...] + b_ref[...] - t_ref[...]
        dx_v = err * (1.0 / D)
        dxh = dx_v * g_ref[...]
        m1 = jnp.mean(dxh, axis=-1, keepdims=True)
        m2 = jnp.mean(dxh * xhat, axis=-1, keepdims=True)
        dy = rstd * (dxh - m1 - xhat * m2)
        dy_ref[...] = dy
        dyb_ref[...] = dy.astype(MXU)

        @pl.when(pl.program_id(0) == 0)
        def _():
            dg_ref[...] = jnp.zeros_like(dg_ref)
            db_ref[...] = jnp.zeros_like(db_ref)
            ls_ref[...] = jnp.zeros_like(ls_ref)

        dg_ref[...] += jnp.sum(dx_v * xhat, axis=0, keepdims=True)
        db_ref[...] += jnp.sum(dx_v, axis=0, keepdims=True)
        ls_ref[...] += jnp.sum(err * err, axis=0, keepdims=True)

    row = pl.BlockSpec((tm, D), lambda i: (i, 0))
    vec = pl.BlockSpec((1, D), lambda i: (0, 0))
    return pl.pallas_call(
        body,
        grid=(S // tm,),
        in_specs=[row, vec, vec, row],
        out_specs=[row, row, vec, vec, vec],
        out_shape=[jax.ShapeDtypeStruct((S, D), F32), jax.ShapeDtypeStruct((S, D), MXU)]
        + [jax.ShapeDtypeStruct((1, D), F32)] * 3,
        compiler_params=_params("arbitrary"),
        name=name,
    )(y, g.reshape(1, D), b.reshape(1, D), target)


def _rope_tables(S):
    inv_freq = ROPE_THETA ** (-jnp.arange(ROPE_HALF, dtype=F32) / ROPE_HALF)
    ang = jnp.arange(S, dtype=jnp.int32).astype(F32)[:, None] * inv_freq[None, :]
    cos, sin = jnp.cos(ang), jnp.sin(ang)
    ones = jnp.ones((S, HEAD_DIM - 2 * ROPE_HALF), F32)
    zeros8 = jnp.zeros((S, ROPE_HALF), F32)
    c = jnp.concatenate([cos, cos, ones], axis=1)
    s1 = jnp.concatenate([zeros8, sin, 0.0 * ones], axis=1)
    s2 = jnp.concatenate([-sin, zeros8, 0.0 * ones], axis=1)
    return tuple(jnp.concatenate([t, t], axis=1) for t in (c, s1, s2))


def _rope(xb, c, s1, s2):
    return xb * c + pltpu.roll(xb, ROPE_HALF, 1) * s1 + pltpu.roll(xb, LANES - ROPE_HALF, 1) * s2


def _rope_t(dyb, c, s1, s2):
    return dyb * c + pltpu.roll(dyb * s1, LANES - ROPE_HALF, 1) + pltpu.roll(dyb * s2, ROPE_HALF, 1)


def _qkv_post(qkv, tables, *, name):
    S, D3 = qkv.shape
    D = D3 // 3
    tm = _pick(S, (256, 128))
    rope = tables is not None

    def body(*refs):
        x_ref, outs = refs[0], refs[-3:]
        tabs = tuple(r[...] for r in refs[1:-3])
        for j, o_ref in enumerate(outs):
            for blk in range(D // LANES):
                cols = slice(blk * LANES, (blk + 1) * LANES)
                xb = x_ref[:, j * D + blk * LANES : j * D + (blk + 1) * LANES]
                if rope and j < 2:
                    xb = _rope(xb, *tabs)
                if j == 0:
                    xb = xb * QK_SCALE
                o_ref[:, cols] = xb.astype(MXU)

    tab = pl.BlockSpec((tm, LANES), lambda i: (i, 0))
    row = pl.BlockSpec((tm, D), lambda i: (i, 0))
    return pl.pallas_call(
        body,
        grid=(S // tm,),
        in_specs=[pl.BlockSpec((tm, D3), lambda i: (i, 0))] + ([tab] * 3 if rope else []),
        out_specs=[row] * 3,
        out_shape=[jax.ShapeDtypeStruct((S, D), MXU)] * 3,
        compiler_params=_params("parallel"),
        name=name,
    )(qkv, *(tables if rope else ()))


def _qkv_bwd_post(dqs, dks, dvs, tables, *, name):
    S, D = dqs[0].shape
    tm = _pick(S, (256, 128))
    nb = len(dqs)
    rope = tables is not None

    def body(*refs):
        ins, o_ref = refs[:-1], refs[-1]
        groups = [ins[0:nb], ins[nb : 2 * nb], ins[2 * nb : 3 * nb]]
        tabs = [r[...] for r in ins[3 * nb :]]
        for j, grp in enumerate(groups):
            for blk in range(D // LANES):
                cols = slice(blk * LANES, (blk + 1) * LANES)
                v = grp[0][:, cols].astype(F32)
                for r in grp[1:]:
                    v = v + r[:, cols].astype(F32)
                if rope and j < 2:
                    v = _rope_t(v, *tabs)
                if j == 0:
                    v = v * QK_SCALE
                o_ref[:, j * D + blk * LANES : j * D + (blk + 1) * LANES] = v.astype(MXU)

    row = pl.BlockSpec((tm, D), lambda i: (i, 0))
    tab = pl.BlockSpec((tm, LANES), lambda i: (i, 0))
    return pl.pallas_call(
        body,
        grid=(S // tm,),
        in_specs=[row] * (3 * nb) + ([tab] * 3 if rope else []),
        out_specs=pl.BlockSpec((tm, 3 * D), lambda i: (i, 0)),
        out_shape=jax.ShapeDtypeStruct((S, 3 * D), MXU),
        compiler_params=_params("parallel"),
        name=name,
    )(*dqs, *dks, *dvs, *(tables if rope else ()))


def _log_fail(z):
    return -(jnp.maximum(z, 0.0) + jnp.log1p(jnp.exp(-jnp.abs(z))))


def _sb_fwd(q, k, v, *, name):
    S, D = q.shape
    HP, H = D // LANES, D // HEAD_DIM
    tq = SB_TQ
    assert S % tq == 0
    cpb = tq // CHUNK
    assert S // CHUNK <= LANES

    def body(q_ref, k_ref, v_ref, o_ref, car_ref, acc_ref, carry0_ref, carry1_ref):
        i = pl.program_id(1)
        lane = lax.broadcasted_iota(jnp.int32, (tq, LANES), 1)
        row = lax.broadcasted_iota(jnp.int32, (tq, LANES), 0)
        kl = lax.broadcasted_iota(jnp.int32, (CHUNK, LANES), 1)
        kr = lax.broadcasted_iota(jnp.int32, (CHUNK, LANES), 0)
        tri = (kr >= kl).astype(MXU)
        qv = q_ref[...]
        qh = (jnp.where(lane < HEAD_DIM, qv, jnp.zeros_like(qv)), jnp.where(lane >= HEAD_DIM, qv, jnp.zeros_like(qv)))
        carry_refs = (carry0_ref, carry1_ref)
        acc_ref[...] = jnp.zeros_like(acc_ref)
        car_ref[...] = jnp.full((2, tq, LANES), NEG, F32)
        for r in carry_refs:
            r[...] = jnp.zeros_like(r)

        def run(chunks, masked):
            offs = [pl.multiple_of(c * CHUNK, CHUNK) for c in chunks]
            kcs = [k_ref[pl.ds(o, CHUNK), :] for o in offs]
            vcs = [v_ref[pl.ds(o, CHUNK), :] for o in offs]
            jobs = [(n, hh) for n in range(len(chunks)) for hh in range(2)]
            zs = {j: lax.dot_general(qh[j[1]], kcs[j[0]], NT, preferred_element_type=F32) for j in jobs}
            if masked:
                cms = [(o + lane) < (i * tq + row) for o in offs]
            lfs = {}
            for j in jobs:
                lf = _log_fail(zs[j])
                lfs[j] = jnp.where(cms[j[0]], lf, 0.0) if masked else lf
            rs = {j: jnp.dot(lfs[j].astype(MXU), tri, preferred_element_type=F32) for j in jobs}
            ws = {}
            for hh in range(2):
                carry = carry_refs[hh][...]
                tile = car_ref[hh]
                for n, c in enumerate(chunks):
                    w = jnp.exp(zs[n, hh] + rs[n, hh] + carry)
                    if masked:
                        w = jnp.where(cms[n], w, 0.0)
                    ws[n, hh] = w.astype(MXU)
                    tile = jnp.where(lane == c, carry, tile)
                    carry = carry + jnp.sum(lfs[n, hh], axis=-1, keepdims=True)
                car_ref[hh] = tile
                carry_refs[hh][...] = carry
            for hh in range(2):
                pv = jnp.dot(ws[0, hh], vcs[0], preferred_element_type=F32)
                for n in range(1, len(chunks)):
                    pv = pv + jnp.dot(ws[n, hh], vcs[n], preferred_element_type=F32)
                acc_ref[hh] += pv

        def max_carry():
            return jnp.maximum(jnp.max(carry0_ref[...]), jnp.max(carry1_ref[...]))

        run([i * cpb + u for u in reversed(range(cpb))], True)

        def cond(st):
            return jnp.logical_and(st[0] >= 0, st[1] >= UNDERFLOW)

        def pair(st):
            c = st[0]
            run([c, c - 1], False)
            return c - 2, max_carry()

        lax.while_loop(cond, pair, (i * cpb - 1, max_carry()))
        o_ref[...] = jnp.where(lane < HEAD_DIM, acc_ref[0], acc_ref[1]).astype(o_ref.dtype)

    blk = pl.BlockSpec((tq, LANES), lambda h, i: (i, h))
    full = pl.BlockSpec((S, LANES), lambda h, i: (0, h))
    return pl.pallas_call(
        body,
        grid=(HP, S // tq),
        in_specs=[blk, full, full],
        out_specs=[blk, pl.BlockSpec((2, tq, LANES), lambda h, i: (h, i, 0))],
        out_shape=[jax.ShapeDtypeStruct((S, D), MXU), jax.ShapeDtypeStruct((H, S, LANES), F32)],
        scratch_shapes=[pltpu.VMEM((2, tq, LANES), F32), pltpu.VMEM((tq, 1), F32), pltpu.VMEM((tq, 1), F32)],
        compiler_params=_params("parallel", "arbitrary"),
        name=name,
    )(q, k, v)


def _sb_bwd(q, k, v, do, car, *, name):
    S, D = q.shape
    HP = D // LANES
    tq = SB_TQ
    assert S % tq == 0
    cpb = tq // CHUNK
    nq = S // tq

    def body(q_ref, k_ref, v_ref, do_ref, car_ref, dq_ref, dk_hbm, dv_hbm, dq_acc, dk_acc, dv_acc, gcar0_ref, gcar1_ref, sem):
        hp = pl.program_id(0)
        i = pl.program_id(1)
        lane = lax.broadcasted_iota(jnp.int32, (tq, LANES), 1)
        row = lax.broadcasted_iota(jnp.int32, (tq, LANES), 0)
        kl = lax.broadcasted_iota(jnp.int32, (CHUNK, LANES), 1)
        kr = lax.broadcasted_iota(jnp.int32, (CHUNK, LANES), 0)
        tri = (kr >= kl).astype(MXU)
        tri_prefix = (kr <= kl).astype(MXU)

        @pl.when(i == 0)
        def _():
            dk_acc[...] = jnp.zeros_like(dk_acc)
            dv_acc[...] = jnp.zeros_like(dv_acc)

        qv = q_ref[...]
        dov = do_ref[...]
        qh = (jnp.where(lane < HEAD_DIM, qv, jnp.zeros_like(qv)), jnp.where(lane >= HEAD_DIM, qv, jnp.zeros_like(qv)))
        doh = (jnp.where(lane < HEAD_DIM, dov, jnp.zeros_like(dov)), jnp.where(lane >= HEAD_DIM, dov, jnp.zeros_like(dov)))
        gcar_refs = (gcar0_ref, gcar1_ref)
        dq_acc[...] = jnp.zeros_like(dq_acc)
        for r in gcar_refs:
            r[...] = jnp.zeros_like(r)

        def run(chunks, masked):
            offs = [pl.multiple_of(c * CHUNK, CHUNK) for c in chunks]
            kcs = [k_ref[pl.ds(o, CHUNK), :] for o in offs]
            vcs = [v_ref[pl.ds(o, CHUNK), :] for o in offs]
            jobs = [(n, hh) for n in range(len(chunks)) for hh in range(2)]
            if masked:
                cms = [(o + lane) < (i * tq + row) for o in offs]
            zs = {j: lax.dot_general(qh[j[1]], kcs[j[0]], NT, preferred_element_type=F32) for j in jobs}
            dws = {j: lax.dot_general(doh[j[1]], vcs[j[0]], NT, preferred_element_type=F32) for j in jobs}
            lfs, sigs = {}, {}
            for j in jobs:
                lf = _log_fail(zs[j])
                sigs[j] = jnp.exp(zs[j] + lf)
                lfs[j] = jnp.where(cms[j[0]], lf, 0.0) if masked else lf
            rs = {j: jnp.dot(lfs[j].astype(MXU), tri, preferred_element_type=F32) for j in jobs}
            ws, gs = {}, {}
            for hh in range(2):
                tile = car_ref[hh]
                for n, c in enumerate(chunks):
                    carry = jnp.sum(jnp.where(lane == c, tile, 0.0), axis=-1, keepdims=True)
                    w = jnp.exp(zs[n, hh] + rs[n, hh] + carry)
                    if masked:
                        w = jnp.where(cms[n], w, 0.0)
                    ws[n, hh] = w.astype(MXU)
                    gs[n, hh] = w * dws[n, hh]
            big_gs = {j: jnp.dot(gs[j].astype(MXU), tri_prefix, preferred_element_type=F32) for j in jobs}
            dzs = {}
            for hh in range(2):
                gcar = gcar_refs[hh][...]
                for n in range(len(chunks)):
                    dz = gs[n, hh] - sigs[n, hh] * (big_gs[n, hh] + gcar)
                    if masked:
                        dz = jnp.where(cms[n], dz, 0.0)
                    dzs[n, hh] = dz.astype(MXU)
                    gcar = gcar + jnp.sum(gs[n, hh], axis=-1, keepdims=True)
                gcar_refs[hh][...] = gcar
            for hh in range(2):
                part = jnp.dot(dzs[0, hh], kcs[0], preferred_element_type=F32)
                for n in range(1, len(chunks)):
                    part = part + jnp.dot(dzs[n, hh], kcs[n], preferred_element_type=F32)
                dq_acc[hh] += part
            for n, o in enumerate(offs):
                dk_acc[pl.ds(o, CHUNK), :] += lax.dot_general(
                    dzs[n, 0], qh[0], TN, preferred_element_type=F32
                ) + lax.dot_general(dzs[n, 1], qh[1], TN, preferred_element_type=F32)
                dv_acc[pl.ds(o, CHUNK), :] += lax.dot_general(
                    ws[n, 0], doh[0], TN, preferred_element_type=F32
                ) + lax.dot_general(ws[n, 1], doh[1], TN, preferred_element_type=F32)

        colmax = jnp.maximum(jnp.max(car_ref[0], axis=0, keepdims=True), jnp.max(car_ref[1], axis=0, keepdims=True))
        lane1 = lax.broadcasted_iota(jnp.int32, (1, LANES), 1)
        dead = jnp.logical_and(colmax < UNDERFLOW, lane1 < i * cpb)
        first_pair = jnp.sum(jnp.where(dead, 1.0, 0.0)).astype(jnp.int32) // 2

        def loop_body(p, carry):
            run([2 * p, 2 * p + 1], False)
            return carry

        lax.fori_loop(first_pair, i * cpb // 2, loop_body, 0)
        run([i * cpb + u for u in range(cpb)], True)
        dq_ref[...] = jnp.where(lane < HEAD_DIM, dq_acc[0], dq_acc[1])

        @pl.when(i == nq - 1)
        def _():
            cols = pl.ds(pl.multiple_of(hp * LANES, LANES), LANES)
            for src, dst in ((dk_acc, dk_hbm), (dv_acc, dv_hbm)):
                cp = pltpu.make_async_copy(src, dst.at[:, cols], sem)
                cp.start()
                cp.wait()

    blk = pl.BlockSpec((tq, LANES), lambda h, i: (i, h))
    full = pl.BlockSpec((S, LANES), lambda h, i: (0, h))
    hbm = pl.BlockSpec(memory_space=pl.ANY)
    return pl.pallas_call(
        body,
        grid=(HP, nq),
        in_specs=[blk, full, full, blk, pl.BlockSpec((2, tq, LANES), lambda h, i: (h, i, 0))],
        out_specs=[blk, hbm, hbm],
        out_shape=[jax.ShapeDtypeStruct((S, D), F32)] * 3,
        scratch_shapes=[
            pltpu.VMEM((2, tq, LANES), F32),
            pltpu.VMEM((S, LANES), F32),
            pltpu.VMEM((S, LANES), F32),
            pltpu.VMEM((tq, 1), F32),
            pltpu.VMEM((tq, 1), F32),
            pltpu.SemaphoreType.DMA,
        ],
        compiler_params=_params("arbitrary", "arbitrary"),
        name=name,
    )(q, k, v, do, car)


def _perm(x, d):
    if d == 1:
        return x
    S, D = x.shape
    return x.reshape(S // d, d, D).transpose(1, 0, 2).reshape(S, D)


def _unperm(x, d):
    if d == 1:
        return x
    S, D = x.shape
    return x.reshape(d, S // d, D).transpose(1, 0, 2).reshape(S, D)


def _perm3(x):
    return jnp.stack([_perm(x, d) for d in DILATIONS])


def _dil_tb(S):
    return min(1024, S // DILATIONS[-1])


def _dil_masks():
    qi = lax.broadcasted_iota(jnp.int32, (CHUNK, CHUNK), 0)
    kj = lax.broadcasted_iota(jnp.int32, (CHUNK, CHUNK), 1)
    return qi, kj


def _dil_fwd(q, k, v, *, branch, name):
    S, D = q.shape
    HP = D // LANES
    tb = _dil_tb(S)
    nb, nsub = S // tb, tb // CHUNK
    bps = nb // DILATIONS[branch]

    def body(q_ref, k_ref, v_ref, kp_ref, vp_ref, o_ref, l_ref):
        n = pl.program_id(1)
        first = jnp.bitwise_and(n, bps - 1) == 0
        qi, kj = _dil_masks()
        hl = lax.broadcasted_iota(jnp.int32, (CHUNK, LANES), 1)
        m_cur = kj <= qi
        m_prev = (kj - qi) >= 0
        m_prev0 = (kj - qi) >= jnp.where(first, 2 * CHUNK, 0)
        hms = (hl < HEAD_DIM, hl >= HEAD_DIM)
        tiles, scores = [], {}
        for u in range(nsub):
            rows = slice(u * CHUNK, (u + 1) * CHUNK)
            qs, kc, vc = q_ref[rows, :], k_ref[rows, :], v_ref[rows, :]
            if u == 0:
                kp, vp, pm = kp_ref[...], vp_ref[...], m_prev0
            else:
                prev = slice((u - 1) * CHUNK, u * CHUNK)
                kp, vp, pm = k_ref[prev, :], v_ref[prev, :], m_prev
            tiles.append((rows, vp, vc, pm))
            for hh in range(2):
                qh = jnp.where(hms[hh], qs, jnp.zeros_like(qs))
                scores[u, hh] = (
                    lax.dot_general(qh, kp, NT, preferred_element_type=F32),
                    lax.dot_general(qh, kc, NT, preferred_element_type=F32),
                )
        probs = {}
        for (u, hh), (s_p, s_c) in scores.items():
            s_p = jnp.where(tiles[u][3], s_p, NEG)
            s_c = jnp.where(m_cur, s_c, NEG)
            m = jnp.max(jnp.maximum(s_p, s_c), axis=-1, keepdims=True)
            p_p, p_c = jnp.exp(s_p - m), jnp.exp(s_c - m)
            den = jnp.sum(p_p + p_c, axis=-1, keepdims=True)
            probs[u, hh] = (p_p.astype(MXU), p_c.astype(MXU), 1.0 / den, m + jnp.log(den))
        for u, (rows, vp, vc, _) in enumerate(tiles):
            o_t = l_t = None
            for hh in range(2):
                p_p, p_c, inv, lse = probs[u, hh]
                num = jnp.dot(p_p, vp, preferred_element_type=F32) + jnp.dot(p_c, vc, preferred_element_type=F32)
                o_h, l_h = num * inv, jnp.broadcast_to(lse, (CHUNK, LANES))
                o_t = o_h if hh == 0 else jnp.where(hms[1], o_h, o_t)
                l_t = l_h if hh == 0 else jnp.where(hms[1], l_h, l_t)
            o_ref[rows, :] = o_t
            l_ref[rows, :] = l_t

    blk = pl.BlockSpec((tb, LANES), lambda h, n: (n, h))
    tail = pl.BlockSpec((CHUNK, LANES), lambda h, n: (jnp.maximum(n * nsub - 1, 0), h))
    return pl.pallas_call(
        body,
        grid=(HP, nb),
        in_specs=[blk, blk, blk, tail, tail],
        out_specs=[blk, blk],
        out_shape=[jax.ShapeDtypeStruct((S, D), F32)] * 2,
        compiler_params=_params("parallel", "parallel"),
        name=name,
    )(q, k, v, k, v)


def _dil_combine(os, ls, *, name):
    S, D = os[0].shape
    tm = _pick(S, (256, 128))

    def body(o0_ref, o1_ref, o2_ref, l0_ref, l1_ref, l2_ref, out_ref, lse_ref):
        l0, l1, l2 = l0_ref[...], l1_ref[...], l2_ref[...]
        m = jnp.maximum(jnp.maximum(l0, l1), l2)
        a0, a1, a2 = jnp.exp(l0 - m), jnp.exp(l1 - m), jnp.exp(l2 - m)
        den = a0 + a1 + a2
        out_ref[...] = (a0 * o0_ref[...] + a1 * o1_ref[...] + a2 * o2_ref[...]) / den
        lse_ref[...] = m + jnp.log(den)

    row = pl.BlockSpec((tm, D), lambda i: (i, 0))
    return pl.pallas_call(
        body,
        grid=(S // tm,),
        in_specs=[row] * 6,
        out_specs=[row, row],
        out_shape=[jax.ShapeDtypeStruct((S, D), F32)] * 2,
        compiler_params=_params("parallel"),
        name=name,
    )(*os, *ls)


def _head_col(tile, hl, hh):
    return jnp.sum(jnp.where(hl == hh * HEAD_DIM, tile, 0.0), axis=-1, keepdims=True)


def _dil_bwd_dq(q, k, v, do, out, lse, *, branch, name):
    S, D = q.shape
    HP = D // LANES
    tb = _dil_tb(S)
    nb, nsub = S // tb, tb // CHUNK
    bps = nb // DILATIONS[branch]

    def body(q_ref, k_ref, v_ref, kp_ref, vp_ref, do_ref, out_ref, lse_ref, dq_ref):
        n = pl.program_id(1)
        first = jnp.bitwise_and(n, bps - 1) == 0
        qi, kj = _dil_masks()
        hl = lax.broadcasted_iota(jnp.int32, (CHUNK, LANES), 1)
        m_cur = kj <= qi
        m_prev = (kj - qi) >= 0
        m_prev0 = (kj - qi) >= jnp.where(first, 2 * CHUNK, 0)
        hms = (hl < HEAD_DIM, hl >= HEAD_DIM)
        jobs, cols = [], {}
        for u in range(nsub):
            rows = slice(u * CHUNK, (u + 1) * CHUNK)
            qs, kc, vc = q_ref[rows, :], k_ref[rows, :], v_ref[rows, :]
            if u == 0:
                kp, vp, pm = kp_ref[...], vp_ref[...], m_prev0
            else:
                prev = slice((u - 1) * CHUNK, u * CHUNK)
                kp, vp, pm = k_ref[prev, :], v_ref[prev, :], m_prev
            do_t, out_t, lse_t = do_ref[rows, :], out_ref[rows, :], lse_ref[rows, :]
            for hh in range(2):
                qh = jnp.where(hms[hh], qs, jnp.zeros_like(qs))
                doh = jnp.where(hms[hh], do_t, jnp.zeros_like(do_t))
                cols[u, hh] = (jnp.sum(doh.astype(F32) * out_t, axis=-1, keepdims=True), _head_col(lse_t, hl, hh))
                dob = doh.astype(MXU)
                for kk, vv, msk in ((kp, vp, pm), (kc, vc, m_cur)):
                    s = lax.dot_general(qh, kk, NT, preferred_element_type=F32)
                    dp = lax.dot_general(dob, vv, NT, preferred_element_type=F32)
                    jobs.append((u, hh, s, dp, kk, msk))
        dss = []
        for u, hh, s, dp, kk, msk in jobs:
            delta, lse = cols[u, hh]
            p = jnp.exp(jnp.where(msk, s, NEG) - lse)
            dss.append((p * (dp - delta)).astype(MXU))
        dq_h = {}
        for (u, hh, _, _, kk, _), ds in zip(jobs, dss):
            part = jnp.dot(ds, kk, preferred_element_type=F32)
            dq_h[u, hh] = part if (u, hh) not in dq_h else dq_h[u, hh] + part
        for u in range(nsub):
            dq_ref[u * CHUNK : (u + 1) * CHUNK, :] = jnp.where(hms[0], dq_h[u, 0], dq_h[u, 1]).astype(MXU)

    blk = pl.BlockSpec((tb, LANES), lambda h, n: (n, h))
    tail = pl.BlockSpec((CHUNK, LANES), lambda h, n: (jnp.maximum(n * nsub - 1, 0), h))
    return pl.pallas_call(
        body,
        grid=(HP, nb),
        in_specs=[blk, blk, blk, tail, tail, blk, blk, blk],
        out_specs=blk,
        out_shape=jax.ShapeDtypeStruct((S, D), MXU),
        compiler_params=_params("parallel", "parallel"),
        name=name,
    )(q, k, v, k, v, do, out, lse)


def _dil_bwd_dkv(q, k, v, do, out, lse, *, branch, name):
    S, D = q.shape
    HP = D // LANES
    tb = _dil_tb(S)
    nb, nsub = S // tb, tb // CHUNK
    last_chunk = S // CHUNK - 1
    bps = nb // DILATIONS[branch]

    def body(q_ref, k_ref, v_ref, do_ref, out_ref, lse_ref, qn_ref, don_ref, outn_ref, lsen_ref, dk_ref, dv_ref):
        n = pl.program_id(1)
        last =jnp.bitwise_and(n + 1, bps - 1) == 0
        qi, kj = _dil_masks()
        hl = lax.broadcasted_iota(jnp.int32, (CHUNK, LANES), 1)
        m_cur = kj <= qi
        m_next = (kj - qi) >= 0
        m_next_last = (kj - qi) >= jnp.where(last, 2 * CHUNK, 0)
        hms = (hl < HEAD_DIM, hl >= HEAD_DIM)
        qtiles = {}
        for t in range(nsub + 1):
            if t == nsub:
                qs, do_t, out_t, lse_t = qn_ref[...], don_ref[...], outn_ref[...], lsen_ref[...]
            else:
                rows = slice(t * CHUNK, (t + 1) * CHUNK)
                qs, do_t, out_t, lse_t = q_ref[rows, :], do_ref[rows, :], out_ref[rows, :], lse_ref[rows, :]
            for hh in range(2):
                doh = jnp.where(hms[hh], do_t, jnp.zeros_like(do_t))
                qtiles[t, hh] = (
                    jnp.where(hms[hh], qs, jnp.zeros_like(qs)),
                    doh.astype(MXU),
                    jnp.sum(doh.astype(F32) * out_t, axis=-1, keepdims=True),
                    _head_col(lse_t, hl, hh),
                )
        jobs = []
        for u in range(nsub):
            rows = slice(u * CHUNK, (u + 1) * CHUNK)
            kc, vc = k_ref[rows, :], v_ref[rows, :]
            for t, msk in ((u, m_cur), (u + 1, m_next_last if u == nsub - 1 else m_next)):
                for hh in range(2):
                    qh, dob, _, _ = qtiles[t, hh]
                    s = lax.dot_general(qh, kc, NT, preferred_element_type=F32)
                    dp = lax.dot_general(dob, vc, NT, preferred_element_type=F32)
                    jobs.append((u, t, hh, s, dp, msk))
        pds = []
        for u, t, hh, s, dp, msk in jobs:
            _, _, delta, lse = qtiles[t, hh]
            p = jnp.exp(jnp.where(msk, s, NEG) - lse)
            pds.append((p.astype(MXU), (p * (dp - delta)).astype(MXU)))
        dks, dvs = {}, {}
        for (u, t, hh, _, _, _), (pb, dsb) in zip(jobs, pds):
            qh, dob, _, _ = qtiles[t, hh]
            dv_p = lax.dot_general(pb, dob, TN, preferred_element_type=F32)
            dk_p = lax.dot_general(dsb, qh, TN, preferred_element_type=F32)
            dvs[u] = dv_p if u not in dvs else dvs[u] + dv_p
            dks[u] = dk_p if u not in dks else dks[u] + dk_p
        for u in range(nsub):
            dk_ref[u * CHUNK : (u + 1) * CHUNK, :] = dks[u].astype(MXU)
            dv_ref[u * CHUNK : (u + 1) * CHUNK, :] = dvs[u].astype(MXU)

    blk = pl.BlockSpec((tb, LANES), lambda h, n: (n, h))
    head = pl.BlockSpec((CHUNK, LANES), lambda h, n: (jnp.minimum((n + 1) * nsub, last_chunk), h))
    return pl.pallas_call(
        body,
        grid=(HP, nb),
        in_specs=[blk] * 6 + [head] * 4,
        out_specs=[blk, blk],
        out_shape=[jax.ShapeDtypeStruct((S, D), MXU)] * 2,
        compiler_params=_params("parallel", "parallel"),
        name=name,
    )(q, k, v, do, out, lse, q, do, out, lse)


def _chip_peers():
    x, y, c = lax.axis_index("x"), lax.axis_index("y"), lax.axis_index("c")
    peers = [(1 - x, y, c), (x, 1 - y, c), (1 - x, 1 - y, c)]
    return x, y, c, peers


def _allgather_weights(wsh, *, name):
    R, D = wsh.shape
    half = R // 2

    def body(w_ref, out_ref, send_sems, recv_sems, local_sem):
        x, y, c, peers = _chip_peers()
        me, sibling = 2 * x + y, (x, y, 1 - c)

        def rows(chip, hc):
            return out_ref.at[chip, pl.ds(hc * half, half), :]

        def copy(k, chip, hc, to, src=None):
            return pltpu.make_async_remote_copy(
                src_ref=rows(chip, hc) if src is None else src, dst_ref=rows(chip, hc),
                send_sem=send_sems.at[k], recv_sem=recv_sems.at[k], device_id=to, device_id_type=MESH,
            )

        mine = pltpu.make_async_copy(w_ref, out_ref.at[me], local_sem)
        mine.start()
        first = [copy(j, me, c, p, src=w_ref.at[pl.ds(c * half, half), :]) for j, p in enumerate(peers)]
        for cp in first:
            cp.start()
        chips = [2 * p[0] + p[1] for p in peers]
        passed = [copy(3 + j, chip, c, sibling) for j, chip in enumerate(chips)]
        for j, chip in enumerate(chips):
            copy(j, chip, c, peers[j]).wait_recv()
            passed[j].start()
        for j, chip in enumerate(chips):
            copy(3 + j, chip, 1 - c, sibling).wait_recv()
        for cp in first + passed:
            cp.wait_send()
        mine.wait()

    hbm = pl.BlockSpec(memory_space=pl.ANY)
    return pl.pallas_call(
        body,
        in_specs=[hbm],
        out_specs=hbm,
        out_shape=jax.ShapeDtypeStruct((N_CHIPS, R, D), wsh.dtype),
        scratch_shapes=[pltpu.SemaphoreType.DMA((6,)), pltpu.SemaphoreType.DMA((6,)), pltpu.SemaphoreType.DMA],
        name=name,
    )(wsh)


def _scatter_grads(gflat, *, name):
    _, R, D = gflat.shape

    def body(g_ref, out_ref, send_sems, recv_sems):
        x, y, c, peers = _chip_peers()
        sends = [
            pltpu.make_async_remote_copy(
                src_ref=g_ref.at[2 * p[0] + p[1]], dst_ref=out_ref.at[j], send_sem=send_sems.at[j],
                recv_sem=recv_sems.at[j], device_id=p, device_id_type=MESH,
            )
            for j, p in enumerate(peers)
        ]
        for cp in sends:
            cp.start()
        for j, p in enumerate(peers):
            pltpu.make_async_remote_copy(
                src_ref=out_ref.at[j], dst_ref=out_ref.at[j], send_sem=send_sems.at[j], recv_sem=recv_sems.at[j],
                device_id=p, device_id_type=MESH,
            ).wait_recv()
        for cp in sends:
            cp.wait_send()

    hbm = pl.BlockSpec(memory_space=pl.ANY)
    return pl.pallas_call(
        body,
        in_specs=[hbm],
        out_specs=hbm,
        out_shape=jax.ShapeDtypeStruct((3, R, D), gflat.dtype),
        scratch_shapes=[pltpu.SemaphoreType.DMA((3,)), pltpu.SemaphoreType.DMA((3,))],
        name=name,
    )(gflat)


def _swap_with_sibling(part, *, name):
    def body(p_ref, out_ref, send_sem, recv_sem):
        x, y, c = lax.axis_index("x"), lax.axis_index("y"), lax.axis_index("c")
        cp = pltpu.make_async_remote_copy(
            src_ref=p_ref, dst_ref=out_ref, send_sem=send_sem, recv_sem=recv_sem,
            device_id=(x, y, 1 - c), device_id_type=MESH,
        )
        cp.start()
        cp.wait()

    hbm = pl.BlockSpec(memory_space=pl.ANY)
    return pl.pallas_call(
        body,
        in_specs=[hbm],
        out_specs=hbm,
        out_shape=jax.ShapeDtypeStruct(part.shape, part.dtype),
        scratch_shapes=[pltpu.SemaphoreType.DMA, pltpu.SemaphoreType.DMA],
        name=name,
    )(part)


def _gather_from_all(vec, *, name):
    R, D = vec.shape

    def body(v_ref, out_ref, send_sems, recv_sems):
        x, y, c = lax.axis_index("x"), lax.axis_index("y"), lax.axis_index("c")
        me = 4 * x + 2 * y + c
        out_ref[me] = v_ref[...]
        rel = [(j >> 2 & 1, j >> 1 & 1, j & 1) for j in range(1, 8)]
        peers = [((1 - x) if fx else x, (1 - y) if fy else y, (1 - c) if fc else c) for fx, fy, fc in rel]
        sends = [
            pltpu.make_async_remote_copy(
                src_ref=v_ref, dst_ref=out_ref.at[me], send_sem=send_sems.at[j], recv_sem=recv_sems.at[j],
                device_id=p, device_id_type=MESH,
            )
            for j, p in enumerate(peers)
        ]
        for cp in sends:
            cp.start()
        for j, p in enumerate(peers):
            theirs = out_ref.at[4 * p[0] + 2 * p[1] + p[2]]
            pltpu.make_async_remote_copy(
                src_ref=theirs, dst_ref=theirs, send_sem=send_sems.at[j], recv_sem=recv_sems.at[j],
                device_id=p, device_id_type=MESH,
            ).wait_recv()
        for cp in sends:
            cp.wait_send()

    vmem = pl.BlockSpec(memory_space=pltpu.VMEM)
    return pl.pallas_call(
        body,
        in_specs=[vmem],
        out_specs=vmem,
        out_shape=jax.ShapeDtypeStruct((8, R, D), vec.dtype),
        scratch_shapes=[pltpu.SemaphoreType.DMA((7,)), pltpu.SemaphoreType.DMA((7,))],
        name=name,
    )(vec)


def _sum4(a, rest, *, name):
    R, D = a.shape
    tm = _pick(R, (256, 128, 8))

    def body(a_ref, r_ref, o_ref):
        o_ref[...] = ((a_ref[...] + r_ref[0].astype(F32)) + r_ref[1].astype(F32)) + r_ref[2].astype(F32)

    row = pl.BlockSpec((tm, D), lambda i: (i, 0))
    return pl.pallas_call(
        body,
        grid=(R // tm,),
        in_specs=[row, pl.BlockSpec((3, tm, D), lambda i: (0, i, 0))],
        out_specs=row,
        out_shape=jax.ShapeDtypeStruct((R, D), F32),
        compiler_params=_params("parallel"),
        name=name,
    )(a, rest)


def _adamw(parts, w, m, v, *, name):
    P, R, D = parts.shape
    tm = _pick(R, (256, 128, 8))

    def body(p_ref, w_ref, m_ref, v_ref, g_ref, d_ref, nm_ref, nv_ref):
        g = p_ref[0]
        for k in range(1, P):
            g = g + p_ref[k]
        nm = ADAM_B1 * m_ref[...] + (1.0 - ADAM_B1) * g
        nv = ADAM_B2 * v_ref[...] + (1.0 - ADAM_B2) * jnp.square(g)
        m_hat = nm / (1.0 - ADAM_B1**ADAM_STEP)
        v_hat = nv / (1.0 - ADAM_B2**ADAM_STEP)
        g_ref[...] = g
        d_ref[...] = -ADAM_LR * (m_hat / (jnp.sqrt(v_hat) + ADAM_EPS) + ADAM_WD * w_ref[...])
        nm_ref[...] = nm
        nv_ref[...] = nv

    row = pl.BlockSpec((tm, D), lambda i: (i, 0))
    return pl.pallas_call(
        body,
        grid=(R // tm,),
        in_specs=[pl.BlockSpec((P, tm, D), lambda i: (0, i, 0)), row, row, row],
        out_specs=[row] * 4,
        out_shape=[jax.ShapeDtypeStruct((R, D), F32)] * 4,
        compiler_params=_params("parallel"),
        name=name,
    )(parts, w, m, v)


MATS = ("w_qkv", "w_o", "w_ff1", "w_ff2")
VECS = ("ln1_g", "ln1_b", "ln2_g", "ln2_b")
PARAM_ORDER = ("w_qkv", "w_o", "ln1_g", "ln1_b", "w_ff1", "w_ff2", "ln2_g", "ln2_b")


def _pack_rows(arrs, D):
    return jnp.concatenate([a.reshape(-1, D) for a in arrs], axis=0)


def kernel(x, w_qkv_0, w_o_0, ln1_g_0, ln1_b_0, w_ff1_0, w_ff2_0, ln2_g_0, ln2_b_0, w_qkv_1, w_o_1, ln1_g_1, ln1_b_1, w_ff1_1, w_ff2_1, ln2_g_1, ln2_b_1, loss_target, m_w_qkv_0, m_w_o_0, m_ln1_g_0, m_ln1_b_0, m_w_ff1_0, m_w_ff2_0, m_ln2_g_0, m_ln2_b_0, m_w_qkv_1, m_w_o_1, m_ln1_g_1, m_ln1_b_1, m_w_ff1_1, m_w_ff2_1, m_ln2_g_1, m_ln2_b_1, v_w_qkv_0, v_w_o_0, v_ln1_g_0, v_ln1_b_0, v_w_ff1_0, v_w_ff2_0, v_ln2_g_0, v_ln2_b_0, v_w_qkv_1, v_w_o_1, v_ln1_g_1, v_ln1_b_1, v_w_ff1_1, v_w_ff2_1, v_ln2_g_1, v_ln2_b_1):
    given = dict(locals())
    xs = x[0]
    target = loss_target[0]
    S, D = xs.shape
    C3 = 3 * D // N_CHIPS
    names = [f"{p}_{l}" for l in range(DEPTH) for p in PARAM_ORDER]
    mat_names = [f"{p}_{l}" for l in range(DEPTH) for p in MATS]
    vec_names = [f"{p}_{l}" for l in range(DEPTH) for p in VECS]
    rows = {"w_qkv": C3, "w_o": D // N_CHIPS, "w_ff1": D, "w_ff2": D}
    layer_rows = sum(rows.values())

    gathered = _allgather_weights(_pack_rows([given[n] for n in mat_names], D).astype(MXU), name="allgather_weights")
    W = []
    for l in range(DEPTH):
        r0 = l * layer_rows
        seg = {}
        for p in MATS:
            seg[p] = gathered[:, r0 : r0 + rows[p], :]
            r0 += rows[p]
        W.append(
            dict(
                qkv=seg["w_qkv"].reshape(N_CHIPS, D, C3),
                o=seg["w_o"].reshape(1, D, D),
                ff1=seg["w_ff1"],
                ff2=seg["w_ff2"].reshape(1, 4 * D, D),
            )
        )
    tables = _rope_tables(S)

    h, hb = xs, xs.astype(MXU)
    saved = []
    for l in range(DEPTH):
        g1, b1, g2, b2 = (given[f"{p}_{l}"] for p in VECS)
        qkv = _mm_nn(hb, W[l]["qkv"], a_op=None, out_dtype=F32, name=f"qkv_{l}")
        q, k, v = _qkv_post(qkv, tables if l == 1 else None, name=f"qkv_post_{l}")
        st = dict(hb=hb, q=q, k=k, v=v)
        if l == 0:
            o, car = _sb_fwd(q, k, v, name="stickbreak_fwd")
            st.update(car=car)
        else:
            qp, kp, vp = ([_perm(t, d) for d in DILATIONS] for t in (q, k, v))
            branches = [
                _dil_fwd(qp[i], kp[i], vp[i], branch=i, name=f"dilated_fwd_d{d}") for i, d in enumerate(DILATIONS)
            ]
            o, lse = _dil_combine(
                [_unperm(ob, d) for (ob, _), d in zip(branches, DILATIONS)],
                [_unperm(lb, d) for (_, lb), d in zip(branches, DILATIONS)],
                name="dilated_combine",
            )
            st.update(qp=qp, kp=kp, vp=vp, lse=lse)
        mix = _mm_nn(o, W[l]["o"], a_op=None, out_dtype=F32, name=f"attn_out_{l}")
        y1, x1, x1b = _ln_fwd(h, mix, g1, b1, name=f"ln1_{l}")
        hp = _mm_nn(x1b, W[l]["ff1"], a_op=None, out_dtype=MXU, name=f"ff1_{l}")
        mlp = _mm_nn(hp, W[l]["ff2"], a_op="relu2", out_dtype=F32, name=f"ff2_{l}")
        y2, x2, x2b = _ln_fwd(x1, mlp, g2, b2, name=f"ln2_{l}")
        st.update(o=o, y1=y1, x1b=x1b, hp=hp, y2=y2)
        saved.append(st)
        h, hb = x2, x2b

    grads = {}
    top = saved[-1]
    dy2, dy2b, dg, db, loss_lanes = _ln_bwd_loss(
        top["y2"], given[f"ln2_g_{DEPTH - 1}"], given[f"ln2_b_{DEPTH - 1}"], target, name="loss_ln2_bwd"
    )
    loss = lax.psum(jnp.sum(loss_lanes) * (0.5 / D), ("x", "y", "c"))
    grads[f"ln2_g_{DEPTH - 1}"], grads[f"ln2_b_{DEPTH - 1}"] = dg, db
    grad_x = None
    for l in reversed(range(DEPTH)):
        st = saved[l]
        dhp = _mm_nt(dy2b, W[l]["ff2"], epi="relu2grad", extra=st["hp"], out_dtype=MXU, name=f"d_ff2_in_{l}")
        grads[f"w_ff2_{l}"] = _mm_tn(st["hp"], dy2b, a_op="relu2", groups=1, name=f"d_w_ff2_{l}")
        dx1 = _mm_nt(dhp, W[l]["ff1"], epi="residual", extra=dy2, out_dtype=F32, name=f"d_ff1_in_{l}")
        grads[f"w_ff1_{l}"] = _mm_tn(st["x1b"], dhp, a_op=None, groups=N_CHIPS, name=f"d_w_ff1_{l}")
        dy1, dy1b, grads[f"ln1_g_{l}"], grads[f"ln1_b_{l}"] = _ln_bwd(
            dx1, st["y1"], given[f"ln1_g_{l}"], name=f"ln1_bwd_{l}"
        )
        grads[f"w_o_{l}"] = _mm_tn(st["o"], dy1b, a_op=None, groups=1, name=f"d_w_o_{l}")
        if l == 0:
            do = _mm_nt(dy1b, W[l]["o"], epi=None, extra=None, out_dtype=MXU, name=f"d_attn_out_{l}")
            dq, dk, dv = _sb_bwd(st["q"], st["k"], st["v"], do, st["car"], name="stickbreak_bwd")
            dqkv = _qkv_bwd_post([dq], [dk], [dv], None, name=f"qkv_bwd_post_{l}")
        else:
            do = _mm_nt(dy1b, W[l]["o"], epi=None, extra=None, out_dtype=MXU, name=f"d_attn_out_{l}")
            dqs, dks, dvs = [], [], []
            for i, d in enumerate(DILATIONS):
                ops = (st["qp"][i], st["kp"][i], st["vp"][i], _perm(do, d), _perm(st["o"], d), _perm(st["lse"], d))
                dqs.append(_unperm(_dil_bwd_dq(*ops, branch=i, name=f"dilated_bwd_dq_d{d}"), d))
                dk_d, dv_d = _dil_bwd_dkv(*ops, branch=i, name=f"dilated_bwd_dkv_d{d}")
                dks.append(_unperm(dk_d, d))
                dvs.append(_unperm(dv_d, d))
            dqkv = _qkv_bwd_post(dqs, dks, dvs, tables, name=f"qkv_bwd_post_{l}")
        dh = _mm_nt(dqkv, W[l]["qkv"], epi="residual", extra=dy1, out_dtype=F32, name=f"d_qkv_in_{l}")
        grads[f"w_qkv_{l}"] = _mm_tn(st["hb"], dqkv, a_op=None, groups=N_CHIPS, name=f"d_w_qkv_{l}")
        if l > 0:
            prev = saved[l - 1]
            dy2, dy2b, grads[f"ln2_g_{l - 1}"], grads[f"ln2_b_{l - 1}"] = _ln_bwd(
                dh, prev["y2"], given[f"ln2_g_{l - 1}"], name=f"ln2_bwd_{l - 1}"
            )
        else:
            grad_x = dh

    gflat = jnp.concatenate([grads[n].reshape(N_CHIPS, -1, D) for n in mat_names], axis=1)
    me = 2 * lax.axis_index("x") + lax.axis_index("y")
    own = lax.dynamic_index_in_dim(gflat, me, 0, keepdims=False)
    from_chips = _scatter_grads(gflat.astype(MXU), name="scatter_grads")
    part = _sum4(own, from_chips, name="sum_chips")
    sib = _swap_with_sibling(part, name="swap_sibling")
    pack = lambda prefix: _pack_rows([given[prefix + n] for n in mat_names], D)
    outs_m = _adamw(jnp.stack([part, sib]), pack(""), pack("m_"), pack("v_"), name="adamw_matrices")
    vflat = _pack_rows([grads[n] for n in vec_names], D)
    packv = lambda prefix: _pack_rows([given[prefix + n] for n in vec_names], D)
    outs_v = _adamw(_gather_from_all(vflat, name="gather_ln_grads"), packv(""), packv("m_"), packv("v_"), name="adamw_vectors")

    def unpack(flat_m, flat_v):
        out, r0 = {}, 0
        for n in mat_names:
            r = given[n].size // D
            out[n] = flat_m[r0 : r0 + r].reshape(given[n].shape)
            r0 += r
        for i, n in enumerate(vec_names):
            out[n] = flat_v[i]
        return [out[n] for n in names]

    result = [loss, grad_x[None]]
    for k in range(4):
        result += unpack(outs_m[k], outs_v[k])
    return tuple(result)
```

```python
import functools

import jax
import jax.numpy as jnp
from jax import lax
from jax.experimental import pallas as pl
from jax.experimental.pallas import tpu as pltpu

F32 = jnp.float32
MXU = jnp.bfloat16

HEAD_DIM = 64
LANES = 128
CHUNK = 128
ROPE_THETA = 500000.0
ROPE_HALF = 8
DILATIONS = (1, 4, 16)
DEPTH = 2
ALPHA = (2 * DEPTH) ** 0.25
LN_EPS = 1e-5
QK_SCALE = 0.125
ADAM_LR, ADAM_B1, ADAM_B2, ADAM_EPS, ADAM_WD, ADAM_STEP = 0.001, 0.9, 0.999, 1e-08, 0.01, 10
NEG = -1e30
UNDERFLOW = -104.0
SB_TQ = 2 * CHUNK
V7X_VMEM_LIMIT = 56 * 1024 * 1024
MESH = pl.DeviceIdType.MESH
NT = (((1,), (1,)), ((), ()))
TN = (((0,), (0,)), ((), ()))
N_CHIPS = 4
MM_ROWS = (1024, 512, 256, 128)


def _params(*sem):
    return pltpu.CompilerParams(dimension_semantics=sem, vmem_limit_bytes=V7X_VMEM_LIMIT)


def _pick(n, cands=(1024, 768, 512, 384, 256, 128)):
    for c in cands:
        if n % c == 0:
            return c
    raise ValueError(f"no tile for {n}")


def _mm_nn(a, b, *, a_op=None, epi=None, extras=(), out_dtype=F32, name):
    M, K = a.shape
    Kb, N = b.shape
    assert K == Kb
    tn = N // 3 if epi == "qkv" else (N if epi == "ln" else _pick(N))
    tm = _pick(M, MM_ROWS) if (K <= 1024 and epi != "ln") else _pick(M, MM_ROWS[1:])
    rope = epi == "qkv" and len(extras) == 3

    def body(a_ref, b_ref, *refs):
        av = a_ref[...]
        if a_op == "relu2":
            av = jnp.square(jnp.maximum(av.astype(F32), 0.0))
        part = jnp.dot(av.astype(MXU), b_ref[...], preferred_element_type=F32)
        if epi is None:
            refs[0][...] = part.astype(out_dtype)
        elif epi == "qkv":
            o_ref = refs[-1]
            tabs = tuple(r[...] for r in refs[:-1])
            j = pl.program_id(1)

            def write(rotate, scale):
                for blk in range(tn // LANES):
                    cols = slice(blk * LANES, (blk + 1) * LANES)
                    xb = part[:, cols]
                    if rotate:
                        xb = _rope(xb, *tabs)
                    if scale:
                        xb = xb * QK_SCALE
                    o_ref[:, cols] = xb.astype(MXU)

            pl.when(j == 0)(lambda: write(rope, True))
            pl.when(j == 1)(lambda: write(rope, False))
            pl.when(j == 2)(lambda: write(False, False))
        else:
            x_ref, g_ref, be_ref, y_ref, o_ref, ob_ref = refs
            y = ALPHA * x_ref[...] + part
            xhat, _ = _ln_stats(y)
            out = xhat * g_ref[...] + be_ref[...]
            y_ref[...] = y
            o_ref[...] = out
            ob_ref[...] = out.astype(MXU)

    in_specs = [pl.BlockSpec((tm, K), lambda i, j: (i, 0)), pl.BlockSpec((K, tn), lambda i, j: (0, j))]
    tile = pl.BlockSpec((tm, tn), lambda i, j: (i, j))
    ops = [a, b]
    if epi == "qkv":
        in_specs += [pl.BlockSpec((tm, LANES), lambda i, j: (i, 0))] * len(extras)
        ops += list(extras)
        out_specs, out_shape = tile, jax.ShapeDtypeStruct((M, N), MXU)
    elif epi == "ln":
        x, g, be = extras
        in_specs += [tile, pl.BlockSpec((1, N), lambda i, j: (0, 0)), pl.BlockSpec((1, N), lambda i, j: (0, 0))]
        ops += [x, g.reshape(1, N), be.reshape(1, N)]
        out_specs = [tile] * 3
        out_shape = [jax.ShapeDtypeStruct((M, N), F32)] * 2 + [jax.ShapeDtypeStruct((M, N), MXU)]
    else:
        out_specs, out_shape = tile, jax.ShapeDtypeStruct((M, N), out_dtype)
    return pl.pallas_call(
        body,
        grid=(M // tm, N // tn),
        in_specs=in_specs,
        out_specs=out_specs,
        out_shape=out_shape,
        compiler_params=_params("parallel", "parallel"),
        name=name,
    )(*ops)


def _mm_nt(a, b, *, epi, extra, out_dtype, name):
    M, N = a.shape
    K, Nb = b.shape
    assert N == Nb
    tm = _pick(M, MM_ROWS) if N <= 1024 else _pick(M, MM_ROWS[1:])
    tko = _pick(K)

    def body(*refs):
        if epi is None:
            a_ref, b_ref, o_ref = refs
        else:
            a_ref, b_ref, e_ref, o_ref = refs
        r = lax.dot_general(a_ref[...].astype(MXU), b_ref[...], NT, preferred_element_type=F32)
        if epi == "relu2grad":
            r = r * (2.0 * jnp.maximum(e_ref[...].astype(F32), 0.0))
        elif epi == "residual":
            r = ALPHA * e_ref[...] + r
        o_ref[...] = r.astype(out_dtype)

    in_specs = [pl.BlockSpec((tm, N), lambda i, j: (i, 0)), pl.BlockSpec((tko, N), lambda i, j: (j, 0))]
    ops = [a, b]
    if epi is not None:
        in_specs.append(pl.BlockSpec((tm, tko), lambda i, j: (i, j)))
        ops.append(extra)
    return pl.pallas_call(
        body,
        grid=(M // tm, K // tko),
        in_specs=in_specs,
        out_specs=pl.BlockSpec((tm, tko), lambda i, j: (i, j)),
        out_shape=jax.ShapeDtypeStruct((M, K), out_dtype),
        compiler_params=_params("parallel", "parallel"),
        name=name,
    )(*ops)


def _mm_tn(a, b, *, a_op, groups, name):
    M, Ka = a.shape
    Mb, N = b.shape
    assert M == Mb
    Nc = N // groups
    tka, tn, tm = _pick(Ka), _pick(Nc), _pick(M, (2048, 1024, 512, 256, 128))
    nbs, nm = Nc // tn, M // tm

    def body(a_ref, b_ref, o_ref, acc_ref):
        kk = pl.program_id(2)
        av = a_ref[...]
        if a_op == "relu2":
            av = jnp.square(jnp.maximum(av.astype(F32), 0.0))
        part = lax.dot_general(av.astype(MXU), b_ref[...].astype(MXU), TN, preferred_element_type=F32)

        @pl.when(kk == 0)
        def _():
            acc_ref[...] = part

        @pl.when(kk > 0)
        def _():
            acc_ref[...] += part

        @pl.when(kk == nm - 1)
        def _():
            o_ref[...] = acc_ref[...]

    return pl.pallas_call(
        body,
        grid=(Ka // tka, N // tn, nm),
        in_specs=[
            pl.BlockSpec((tm, tka), lambda i, j, k: (k, i)),
            pl.BlockSpec((tm, tn), lambda i, j, k: (k, j)),
        ],
        out_specs=pl.BlockSpec((None, tka, tn), lambda i, j, k: (j // nbs, i, j % nbs)),
        out_shape=jax.ShapeDtypeStruct((groups, Ka, Nc), F32),
        scratch_shapes=[pltpu.VMEM((tka, tn), F32)],
        compiler_params=_params("parallel", "parallel", "arbitrary"),
        name=name,
    )(a, b)


def _ln_stats(y):
    mu = jnp.mean(y, axis=-1, keepdims=True)
    yc = y - mu
    var = jnp.mean(yc * yc, axis=-1, keepdims=True)
    rstd = lax.rsqrt(var + LN_EPS)
    return yc * rstd, rstd


def _ln_bwd(dx, y, g, *, name):
    S, D = y.shape
    tm = _pick(S, (256, 128))

    def body(dx_ref, y_ref, g_ref, dy_ref, dyb_ref, dg_ref, db_ref):
        xhat, rstd = _ln_stats(y_ref[...])
        dx_v = dx_ref[...]
        dxh = dx_v * g_ref[...]
        m1 = jnp.mean(dxh, axis=-1, keepdims=True)
        m2 = jnp.mean(dxh * xhat, axis=-1, keepdims=True)
        dy = rstd * (dxh - m1 - xhat * m2)
        dy_ref[...] = dy
        dyb_ref[...] = dy.astype(MXU)

        @pl.when(pl.program_id(0) == 0)
        def _():
            dg_ref[...] = jnp.zeros_like(dg_ref)
            db_ref[...] = jnp.zeros_like(db_ref)

        dg_ref[...] += jnp.sum(dx_v * xhat, axis=0, keepdims=True)
        db_ref[...] += jnp.sum(dx_v, axis=0, keepdims=True)

    row = pl.BlockSpec((tm, D), lambda i: (i, 0))
    vec = pl.BlockSpec((1, D), lambda i: (0, 0))
    return pl.pallas_call(
        body,
        grid=(S // tm,),
        in_specs=[row, row, vec],
        out_specs=[row, row, vec, vec],
        out_shape=[jax.ShapeDtypeStruct((S, D), F32), jax.ShapeDtypeStruct((S, D), MXU)]
        + [jax.ShapeDtypeStruct((1, D), F32)] * 2,
        compiler_params=_params("arbitrary"),
        name=name,
    )(dx, y, g.reshape(1, D))


def _ln_bwd_loss(y, g, b, target, *, name):
    S, D = y.shape
    tm = _pick(S, (256, 128))

    def body(y_ref, g_ref, b_ref, t_ref, dy_ref, dyb_ref, dg_ref, db_ref, ls_ref):
        xhat, rstd = _ln_stats(y_ref[...])
        err = xhat * g_ref[...] + b_ref[...] - t_ref[...]
        dx_v = err * (1.0 / D)
        dxh = dx_v * g_ref[...]
        m1 = jnp.mean(dxh, axis=-1, keepdims=True)
        m2 = jnp.mean(dxh * xhat, axis=-1, keepdims=True)
        dy = rstd * (dxh - m1 - xhat * m2)
        dy_ref[...] = dy
        dyb_ref[...] = dy.astype(MXU)

        @pl.when(pl.program_id(0) == 0)
        def _():
            dg_ref[...] = jnp.zeros_like(dg_ref)
            db_ref[...] = jnp.zeros_like(db_ref)
            ls_ref[...] = jnp.zeros_like(ls_ref)

        dg_ref[...] += jnp.sum(dx_v * xhat, axis=0, keepdims=True)
        db_ref[...] += jnp.sum(dx_v, axis=0, keepdims=True)
        ls_ref[...] += jnp.sum(err * err, axis=0, keepdims=True)

    row = pl.BlockSpec((tm, D), lambda i: (i, 0))
    vec = pl.BlockSpec((1, D), lambda i: (0, 0))
    return pl.pallas_call(
        body,
        grid=(S // tm,),
        in_specs=[row, vec, vec, row],
        out_specs=[row, row, vec, vec, vec],
        out_shape=[jax.ShapeDtypeStruct((S, D), F32), jax.ShapeDtypeStruct((S, D), MXU)]
        + [jax.ShapeDtypeStruct((1, D), F32)] * 3,
        compiler_params=_params("arbitrary"),
        name=name,
    )(y, g.reshape(1, D), b.reshape(1, D), target)


def _rope_tables(S):
    inv_freq = ROPE_THETA ** (-jnp.arange(ROPE_HALF, dtype=F32) / ROPE_HALF)
    ang = jnp.arange(S, dtype=jnp.int32).astype(F32)[:, None] * inv_freq[None, :]
    cos, sin = jnp.cos(ang), jnp.sin(ang)
    ones = jnp.ones((S, HEAD_DIM - 2 * ROPE_HALF), F32)
    zeros8 = jnp.zeros((S, ROPE_HALF), F32)
    c = jnp.concatenate([cos, cos, ones], axis=1)
    s1 = jnp.concatenate([zeros8, sin, 0.0 * ones], axis=1)
    s2 = jnp.concatenate([-sin, zeros8, 0.0 * ones], axis=1)
    return tuple(jnp.concatenate([t, t], axis=1) for t in (c, s1, s2))


def _rope(xb, c, s1, s2):
    return xb * c + pltpu.roll(xb, ROPE_HALF, 1) * s1 + pltpu.roll(xb, LANES - ROPE_HALF, 1) * s2


def _rope_t(dyb, c, s1, s2):
    return dyb * c + pltpu.roll(dyb * s1, LANES - ROPE_HALF, 1) + pltpu.roll(dyb * s2, ROPE_HALF, 1)


def _qkv_bwd_post(dqs, dks, dvs, tables, *, name):
    S, D = dqs[0].shape
    tm = _pick(S, (256, 128))
    nb = len(dqs)
    rope = tables is not None

    def body(*refs):
        ins, o_ref = refs[:-1], refs[-1]
        groups = [ins[0:nb], ins[nb : 2 * nb], ins[2 * nb : 3 * nb]]
        tabs = [r[...] for r in ins[3 * nb :]]
        for j, grp in enumerate(groups):
            for blk in range(D // LANES):
                cols = slice(blk * LANES, (blk + 1) * LANES)
                v = grp[0][:, cols].astype(F32)
                for r in grp[1:]:
                    v = v + r[:, cols].astype(F32)
                if rope and j < 2:
                    v = _rope_t(v, *tabs)
                if j == 0:
                    v = v * QK_SCALE
                o_ref[:, j * D + blk * LANES : j * D + (blk + 1) * LANES] = v.astype(MXU)

    row = pl.BlockSpec((tm, D), lambda i: (i, 0))
    tab = pl.BlockSpec((tm, LANES), lambda i: (i, 0))
    return pl.pallas_call(
        body,
        grid=(S // tm,),
        in_specs=[row] * (3 * nb) + ([tab] * 3 if rope else []),
        out_specs=pl.BlockSpec((tm, 3 * D), lambda i: (i, 0)),
        out_shape=jax.ShapeDtypeStruct((S, 3 * D), MXU),
        compiler_params=_params("parallel"),
        name=name,
    )(*dqs, *dks, *dvs, *(tables if rope else ()))


def _log_fail(z):
    return -(jnp.maximum(z, 0.0) + jnp.log(1.0 + jnp.exp(-jnp.abs(z))))


def _sb_fwd(qkv, *, name):
    S, D = qkv.shape[0], qkv.shape[1] // 3
    HP, H = D // LANES, D // HEAD_DIM
    tq = SB_TQ
    assert S % tq == 0
    cpb = tq // CHUNK
    assert S // CHUNK <= LANES

    def body(q_ref, k_ref, v_ref, o_ref, car_ref, acc_ref, carry0_ref, carry1_ref):
        i = pl.program_id(1)
        lane = lax.broadcasted_iota(jnp.int32, (tq, LANES), 1)
        row = lax.broadcasted_iota(jnp.int32, (tq, LANES), 0)
        kl = lax.broadcasted_iota(jnp.int32, (CHUNK, LANES), 1)
        kr = lax.broadcasted_iota(jnp.int32, (CHUNK, LANES), 0)
        tri = (kr >= kl).astype(MXU)
        qv = q_ref[...]
        qh = (jnp.where(lane < HEAD_DIM, qv, jnp.zeros_like(qv)), jnp.where(lane >= HEAD_DIM, qv, jnp.zeros_like(qv)))
        carry_refs = (carry0_ref, carry1_ref)
        acc_ref[...] = jnp.zeros_like(acc_ref)
        car_ref[...] = jnp.full((2, tq, LANES), NEG, F32)
        for r in carry_refs:
            r[...] = jnp.zeros_like(r)

        def run(chunks, masked):
            offs = [pl.multiple_of(c * CHUNK, CHUNK) for c in chunks]
            kcs = [k_ref[pl.ds(o, CHUNK), :] for o in offs]
            vcs = [v_ref[pl.ds(o, CHUNK), :] for o in offs]
            jobs = [(n, hh) for n in range(len(chunks)) for hh in range(2)]
            zs = {j: lax.dot_general(qh[j[1]], kcs[j[0]], NT, preferred_element_type=F32) for j in jobs}
            if masked:
                cms = [(o + lane) < (i * tq + row) for o in offs]
            lfs = {}
            for j in jobs:
                lf = _log_fail(zs[j])
                lfs[j] = jnp.where(cms[j[0]], lf, 0.0) if masked else lf
            rs = {j: jnp.dot(lfs[j].astype(MXU), tri, preferred_element_type=F32) for j in jobs}
            ws = {}
            for hh in range(2):
                carry = carry_refs[hh][...]
                tile = car_ref[hh]
                for n, c in enumerate(chunks):
                    w = jnp.exp(zs[n, hh] + rs[n, hh] + carry)
                    if masked:
                        w = jnp.where(cms[n], w, 0.0)
                    ws[n, hh] = w.astype(MXU)
                    tile = jnp.where(lane == c, carry, tile)
                    carry = carry + jnp.sum(lfs[n, hh], axis=-1, keepdims=True)
                car_ref[hh] = tile
                carry_refs[hh][...] = carry
            for hh in range(2):
                pv = jnp.dot(ws[0, hh], vcs[0], preferred_element_type=F32)
                for n in range(1, len(chunks)):
                    pv = pv + jnp.dot(ws[n, hh], vcs[n], preferred_element_type=F32)
                acc_ref[hh] += pv

        def max_carry():
            return jnp.maximum(jnp.max(carry0_ref[...]), jnp.max(carry1_ref[...]))

        run([i * cpb + u for u in reversed(range(cpb))], True)

        def cond(st):
            return jnp.logical_and(st[0] >= 0, st[1] >= UNDERFLOW)

        def pair(st):
            c = st[0]
            run([c, c - 1], False)
            return c - 2, max_carry()

        lax.while_loop(cond, pair, (i * cpb - 1, max_carry()))
        o_ref[...] = jnp.where(lane < HEAD_DIM, acc_ref[0], acc_ref[1]).astype(o_ref.dtype)

    blk = pl.BlockSpec((tq, LANES), lambda h, i: (i, h))
    k_full = pl.BlockSpec((S, LANES), lambda h, i: (0, HP + h))
    v_full = pl.BlockSpec((S, LANES), lambda h, i: (0, 2 * HP + h))
    return pl.pallas_call(
        body,
        grid=(HP, S // tq),
        in_specs=[blk, k_full, v_full],
        out_specs=[blk, pl.BlockSpec((2, tq, LANES), lambda h, i: (h, i, 0))],
        out_shape=[jax.ShapeDtypeStruct((S, D), MXU), jax.ShapeDtypeStruct((H, S, LANES), F32)],
        scratch_shapes=[pltpu.VMEM((2, tq, LANES), F32), pltpu.VMEM((tq, 1), F32), pltpu.VMEM((tq, 1), F32)],
        compiler_params=_params("parallel", "arbitrary"),
        name=name,
    )(qkv, qkv, qkv)


def _sb_bwd(qkv, do, car, *, name):
    S, D = qkv.shape[0], qkv.shape[1] // 3
    HP = D // LANES
    tq = SB_TQ
    assert S % tq == 0
    cpb = tq // CHUNK
    nq = S // tq

    def body(q_ref, k_ref, v_ref, do_ref, car_ref, dq_ref, dk_hbm, dv_hbm, dq_acc, dk_acc, dv_acc, gcar0_ref, gcar1_ref, sem):
        hp = pl.program_id(0)
        i = pl.program_id(1)
        lane = lax.broadcasted_iota(jnp.int32, (tq, LANES), 1)
        row = lax.broadcasted_iota(jnp.int32, (tq, LANES), 0)
        kl = lax.broadcasted_iota(jnp.int32, (CHUNK, LANES), 1)
        kr = lax.broadcasted_iota(jnp.int32, (CHUNK, LANES), 0)
        tri = (kr >= kl).astype(MXU)
        tri_prefix = (kr <= kl).astype(MXU)

        @pl.when(i == 0)
        def _():
            dk_acc[...] = jnp.zeros_like(dk_acc)
            dv_acc[...] = jnp.zeros_like(dv_acc)

        qv = q_ref[...]
        dov = do_ref[...]
        qh = (jnp.where(lane < HEAD_DIM, qv, jnp.zeros_like(qv)), jnp.where(lane >= HEAD_DIM, qv, jnp.zeros_like(qv)))
        doh = (jnp.where(lane < HEAD_DIM, dov, jnp.zeros_like(dov)), jnp.where(lane >= HEAD_DIM, dov, jnp.zeros_like(dov)))
        gcar_refs = (gcar0_ref, gcar1_ref)
        dq_acc[...] = jnp.zeros_like(dq_acc)
        for r in gcar_refs:
            r[...] = jnp.zeros_like(r)

        def run(chunks, masked):
            offs = [pl.multiple_of(c * CHUNK, CHUNK) for c in chunks]
            kcs = [k_ref[pl.ds(o, CHUNK), :] for o in offs]
            vcs = [v_ref[pl.ds(o, CHUNK), :] for o in offs]
            jobs = [(n, hh) for n in range(len(chunks)) for hh in range(2)]
            if masked:
                cms = [(o + lane) < (i * tq + row) for o in offs]
            zs = {j: lax.dot_general(qh[j[1]], kcs[j[0]], NT, preferred_element_type=F32) for j in jobs}
            dws = {j: lax.dot_general(doh[j[1]], vcs[j[0]], NT, preferred_element_type=F32) for j in jobs}
            lfs, sigs = {}, {}
            for j in jobs:
                lf = _log_fail(zs[j])
                sigs[j] = jnp.exp(zs[j] + lf)
                lfs[j] = jnp.where(cms[j[0]], lf, 0.0) if masked else lf
            rs = {j: jnp.dot(lfs[j].astype(MXU), tri, preferred_element_type=F32) for j in jobs}
            ws, gs = {}, {}
            for hh in range(2):
                tile = car_ref[hh]
                for n, c in enumerate(chunks):
                    carry = jnp.sum(jnp.where(lane == c, tile, 0.0), axis=-1, keepdims=True)
                    w = jnp.exp(zs[n, hh] + rs[n, hh] + carry)
                    if masked:
                        w = jnp.where(cms[n], w, 0.0)
                    ws[n, hh] = w.astype(MXU)
                    gs[n, hh] = w * dws[n, hh]
            big_gs = {j: jnp.dot(gs[j].astype(MXU), tri_prefix, preferred_element_type=F32) for j in jobs}
            dzs = {}
            for hh in range(2):
                gcar = gcar_refs[hh][...]
                for n in range(len(chunks)):
                    dz = gs[n, hh] - sigs[n, hh] * (big_gs[n, hh] + gcar)
                    if masked:
                        dz = jnp.where(cms[n], dz, 0.0)
                    dzs[n, hh] = dz.astype(MXU)
                    gcar = gcar + jnp.sum(gs[n, hh], axis=-1, keepdims=True)
                gcar_refs[hh][...] = gcar
            for hh in range(2):
                part = jnp.dot(dzs[0, hh], kcs[0], preferred_element_type=F32)
                for n in range(1, len(chunks)):
                    part = part + jnp.dot(dzs[n, hh], kcs[n], preferred_element_type=F32)
                dq_acc[hh] += part
            for n, o in enumerate(offs):
                dk_acc[pl.ds(o, CHUNK), :] += lax.dot_general(
                    dzs[n, 0], qh[0], TN, preferred_element_type=F32
                ) + lax.dot_general(dzs[n, 1], qh[1], TN, preferred_element_type=F32)
                dv_acc[pl.ds(o, CHUNK), :] += lax.dot_general(
                    ws[n, 0], doh[0], TN, preferred_element_type=F32
                ) + lax.dot_general(ws[n, 1], doh[1], TN, preferred_element_type=F32)

        colmax = jnp.maximum(jnp.max(car_ref[0], axis=0, keepdims=True), jnp.max(car_ref[1], axis=0, keepdims=True))
        lane1 = lax.broadcasted_iota(jnp.int32, (1, LANES), 1)
        dead = jnp.logical_and(colmax < UNDERFLOW, lane1 < i * cpb)
        first_pair = jnp.sum(jnp.where(dead, 1.0, 0.0)).astype(jnp.int32) // 2

        def loop_body(p, carry):
            run([2 * p, 2 * p + 1], False)
            return carry

        lax.fori_loop(first_pair, i * cpb // 2, loop_body, 0)
        run([i * cpb + u for u in range(cpb)], True)
        dq_ref[...] = jnp.where(lane < HEAD_DIM, dq_acc[0], dq_acc[1])

        @pl.when(i == nq - 1)
        def _():
            cols = pl.ds(pl.multiple_of(hp * LANES, LANES), LANES)
            for src, dst in ((dk_acc, dk_hbm), (dv_acc, dv_hbm)):
                cp = pltpu.make_async_copy(src, dst.at[:, cols], sem)
                cp.start()
                cp.wait()

    blk = pl.BlockSpec((tq, LANES), lambda h, i: (i, h))
    k_full = pl.BlockSpec((S, LANES), lambda h, i: (0, HP + h))
    v_full = pl.BlockSpec((S, LANES), lambda h, i: (0, 2 * HP + h))
    hbm = pl.BlockSpec(memory_space=pl.ANY)
    return pl.pallas_call(
        body,
        grid=(HP, nq),
        in_specs=[blk, k_full, v_full, blk, pl.BlockSpec((2, tq, LANES), lambda h, i: (h, i, 0))],
        out_specs=[blk, hbm, hbm],
        out_shape=[jax.ShapeDtypeStruct((S, D), F32)] * 3,
        scratch_shapes=[
            pltpu.VMEM((2, tq, LANES), F32),
            pltpu.VMEM((S, LANES), F32),
            pltpu.VMEM((S, LANES), F32),
            pltpu.VMEM((tq, 1), F32),
            pltpu.VMEM((tq, 1), F32),
            pltpu.SemaphoreType.DMA,
        ],
        compiler_params=_params("arbitrary", "arbitrary"),
        name=name,
    )(qkv, qkv, qkv, do, car)


def _perm(x, d):
    if d == 1:
        return x
    S, D = x.shape
    return x.reshape(S // d, d, D).transpose(1, 0, 2).reshape(S, D)


def _unperm(x, d):
    if d == 1:
        return x
    S, D = x.shape
    return x.reshape(d, S // d, D).transpose(1, 0, 2).reshape(S, D)


def _dil_tb(S):
    return min(1024, S // DILATIONS[-1])


def _dil_masks():
    qi = lax.broadcasted_iota(jnp.int32, (CHUNK, CHUNK), 0)
    kj = lax.broadcasted_iota(jnp.int32, (CHUNK, CHUNK), 1)
    return qi, kj


def _dil_fwd(q, k, v, *, branch, name):
    S, D = q.shape
    HP = D // LANES
    tb = _dil_tb(S)
    nb, nsub = S // tb, tb // CHUNK
    bps = nb // DILATIONS[branch]

    def body(q_ref, k_ref, v_ref, kp_ref, vp_ref, o_ref, l_ref):
        n = pl.program_id(1)
        first = jnp.bitwise_and(n, bps - 1) == 0
        qi, kj = _dil_masks()
        hl = lax.broadcasted_iota(jnp.int32, (CHUNK, LANES), 1)
        m_cur = kj <= qi
        m_prev = (kj - qi) >= 0
        m_prev0 = (kj - qi) >= jnp.where(first, 2 * CHUNK, 0)
        hms = (hl < HEAD_DIM, hl >= HEAD_DIM)
        tiles, scores = [], {}
        for u in range(nsub):
            rows = slice(u * CHUNK, (u + 1) * CHUNK)
            qs, kc, vc = q_ref[rows, :], k_ref[rows, :], v_ref[rows, :]
            if u == 0:
                kp, vp, pm = kp_ref[...], vp_ref[...], m_prev0
            else:
                prev = slice((u - 1) * CHUNK, u * CHUNK)
                kp, vp, pm = k_ref[prev, :], v_ref[prev, :], m_prev
            tiles.append((rows, vp, vc, pm))
            for hh in range(2):
                qh = jnp.where(hms[hh], qs, jnp.zeros_like(qs))
                scores[u, hh] = (
                    lax.dot_general(qh, kp, NT, preferred_element_type=F32),
                    lax.dot_general(qh, kc, NT, preferred_element_type=F32),
                )
        probs = {}
        for (u, hh), (s_p, s_c) in scores.items():
            s_p = jnp.where(tiles[u][3], s_p, NEG)
            s_c = jnp.where(m_cur, s_c, NEG)
            m = jnp.max(jnp.maximum(s_p, s_c), axis=-1, keepdims=True)
            p_p, p_c = jnp.exp(s_p - m), jnp.exp(s_c - m)
            den = jnp.sum(p_p + p_c, axis=-1, keepdims=True)
            probs[u, hh] = (p_p.astype(MXU), p_c.astype(MXU), 1.0 / den, m + jnp.log(den))
        for u, (rows, vp, vc, _) in enumerate(tiles):
            o_t = l_t = None
            for hh in range(2):
                p_p, p_c, inv, lse = probs[u, hh]
                num = jnp.dot(p_p, vp, preferred_element_type=F32) + jnp.dot(p_c, vc, preferred_element_type=F32)
                o_h, l_h = num * inv, jnp.broadcast_to(lse, (CHUNK, LANES))
                o_t = o_h if hh == 0 else jnp.where(hms[1], o_h, o_t)
                l_t = l_h if hh == 0 else jnp.where(hms[1], l_h, l_t)
            o_ref[rows, :] = o_t
            l_ref[rows, :] = l_t

    blk = pl.BlockSpec((tb, LANES), lambda h, n: (n, h))
    tail = pl.BlockSpec((CHUNK, LANES), lambda h, n: (jnp.maximum(n * nsub - 1, 0), h))
    return pl.pallas_call(
        body,
        grid=(HP, nb),
        in_specs=[blk, blk, blk, tail, tail],
        out_specs=[blk, blk],
        out_shape=[jax.ShapeDtypeStruct((S, D), F32)] * 2,
        compiler_params=_params("parallel", "parallel"),
        name=name,
    )(q, k, v, k, v)


def _dil_combine(os, ls, *, name):
    S, D = os[0].shape
    tm = _pick(S, (256, 128))

    def body(o0_ref, o1_ref, o2_ref, l0_ref, l1_ref, l2_ref, out_ref, lse_ref):
        l0, l1, l2 = l0_ref[...], l1_ref[...], l2_ref[...]
        m = jnp.maximum(jnp.maximum(l0, l1), l2)
        a0, a1, a2 = jnp.exp(l0 - m), jnp.exp(l1 - m), jnp.exp(l2 - m)
        den = a0 + a1 + a2
        out_ref[...] = (a0 * o0_ref[...] + a1 * o1_ref[...] + a2 * o2_ref[...]) / den
        lse_ref[...] = m + jnp.log(den)

    row = pl.BlockSpec((tm, D), lambda i: (i, 0))
    return pl.pallas_call(
        body,
        grid=(S // tm,),
        in_specs=[row] * 6,
        out_specs=[row, row],
        out_shape=[jax.ShapeDtypeStruct((S, D), F32)] * 2,
        compiler_params=_params("parallel"),
        name=name,
    )(*os, *ls)


def _head_col(tile, hl, hh):
    return jnp.sum(jnp.where(hl == hh * HEAD_DIM, tile, 0.0), axis=-1, keepdims=True)


def _dil_bwd_dq(q, k, v, do, out, lse, *, branch, name):
    S, D = q.shape
    HP = D // LANES
    tb = _dil_tb(S)
    nb, nsub = S // tb, tb // CHUNK
    bps = nb // DILATIONS[branch]

    def body(q_ref, k_ref, v_ref, kp_ref, vp_ref, do_ref, out_ref, lse_ref, dq_ref):
        n = pl.program_id(1)
        first = jnp.bitwise_and(n, bps - 1) == 0
        qi, kj = _dil_masks()
        hl = lax.broadcasted_iota(jnp.int32, (CHUNK, LANES), 1)
        m_cur = kj <= qi
        m_prev = (kj - qi) >= 0
        m_prev0 = (kj - qi) >= jnp.where(first, 2 * CHUNK, 0)
        hms = (hl < HEAD_DIM, hl >= HEAD_DIM)
        jobs, cols = [], {}
        for u in range(nsub):
            rows = slice(u * CHUNK, (u + 1) * CHUNK)
            qs, kc, vc = q_ref[rows, :], k_ref[rows, :], v_ref[rows, :]
            if u == 0:
                kp, vp, pm = kp_ref[...], vp_ref[...], m_prev0
            else:
                prev = slice((u - 1) * CHUNK, u * CHUNK)
                kp, vp, pm = k_ref[prev, :], v_ref[prev, :], m_prev
            do_t, out_t, lse_t = do_ref[rows, :], out_ref[rows, :], lse_ref[rows, :]
            for hh in range(2):
                qh = jnp.where(hms[hh], qs, jnp.zeros_like(qs))
                doh = jnp.where(hms[hh], do_t, jnp.zeros_like(do_t))
                cols[u, hh] = (jnp.sum(doh.astype(F32) * out_t, axis=-1, keepdims=True), _head_col(lse_t, hl, hh))
                dob = doh.astype(MXU)
                for kk, vv, msk in ((kp, vp, pm), (kc, vc, m_cur)):
                    s = lax.dot_general(qh, kk, NT, preferred_element_type=F32)
                    dp = lax.dot_general(dob, vv, NT, preferred_element_type=F32)
                    jobs.append((u, hh, s, dp, kk, msk))
        dss = []
        for u, hh, s, dp, kk, msk in jobs:
            delta, lse = cols[u, hh]
            p = jnp.exp(jnp.where(msk, s, NEG) - lse)
            dss.append((p * (dp - delta)).astype(MXU))
        dq_h = {}
        for (u, hh, _, _, kk, _), ds in zip(jobs, dss):
            part = jnp.dot(ds, kk, preferred_element_type=F32)
            dq_h[u, hh] = part if (u, hh) not in dq_h else dq_h[u, hh] + part
        for u in range(nsub):
            dq_ref[u * CHUNK : (u + 1) * CHUNK, :] = jnp.where(hms[0], dq_h[u, 0], dq_h[u, 1]).astype(MXU)

    blk = pl.BlockSpec((tb, LANES), lambda h, n: (n, h))
    tail = pl.BlockSpec((CHUNK, LANES), lambda h, n: (jnp.maximum(n * nsub - 1, 0), h))
    return pl.pallas_call(
        body,
        grid=(HP, nb),
        in_specs=[blk, blk, blk, tail, tail, blk, blk, blk],
        out_specs=blk,
        out_shape=jax.ShapeDtypeStruct((S, D), MXU),
        compiler_params=_params("parallel", "parallel"),
        name=name,
    )(q, k, v, k, v, do, out, lse)


def _dil_bwd_dkv(q, k, v, do, out, lse, *, branch, name):
    S, D = q.shape
    HP = D // LANES
    tb = _dil_tb(S)
    nb, nsub = S // tb, tb // CHUNK
    last_chunk = S // CHUNK - 1
    bps = nb // DILATIONS[branch]

    def body(q_ref, k_ref, v_ref, do_ref, out_ref, lse_ref, qn_ref, don_ref, outn_ref, lsen_ref, dk_ref, dv_ref):
        n = pl.program_id(1)
        last =jnp.bitwise_and(n + 1, bps - 1) == 0
        qi, kj = _dil_masks()
        hl = lax.broadcasted_iota(jnp.int32, (CHUNK, LANES), 1)
        m_cur = kj <= qi
        m_next = (kj - qi) >= 0
        m_next_last = (kj - qi) >= jnp.where(last, 2 * CHUNK, 0)
        hms = (hl < HEAD_DIM, hl >= HEAD_DIM)
        qtiles = {}
        for t in range(nsub + 1):
            if t == nsub:
                qs, do_t, out_t, lse_t = qn_ref[...], don_ref[...], outn_ref[...], lsen_ref[...]
            else:
                rows = slice(t * CHUNK, (t + 1) * CHUNK)
                qs, do_t, out_t, lse_t = q_ref[rows, :], do_ref[rows, :], out_ref[rows, :], lse_ref[rows, :]
            for hh in range(2):
                doh = jnp.where(hms[hh], do_t, jnp.zeros_like(do_t))
                qtiles[t, hh] = (
                    jnp.where(hms[hh], qs, jnp.zeros_like(qs)),
                    doh.astype(MXU),
                    jnp.sum(doh.astype(F32) * out_t, axis=-1, keepdims=True),
                    _head_col(lse_t, hl, hh),
                )
        jobs = []
        for u in range(nsub):
            rows = slice(u * CHUNK, (u + 1) * CHUNK)
            kc, vc = k_ref[rows, :], v_ref[rows, :]
            for t, msk in ((u, m_cur), (u + 1, m_next_last if u == nsub - 1 else m_next)):
                for hh in range(2):
                    qh, dob, _, _ = qtiles[t, hh]
                    s = lax.dot_general(qh, kc, NT, preferred_element_type=F32)
                    dp = lax.dot_general(dob, vc, NT, preferred_element_type=F32)
                    jobs.append((u, t, hh, s, dp, msk))
        pds = []
        for u, t, hh, s, dp, msk in jobs:
            _, _, delta, lse = qtiles[t, hh]
            p = jnp.exp(jnp.where(msk, s, NEG) - lse)
            pds.append((p.astype(MXU), (p * (dp - delta)).astype(MXU)))
        dks, dvs = {}, {}
        for (u, t, hh, _, _, _), (pb, dsb) in zip(jobs, pds):
            qh, dob, _, _ = qtiles[t, hh]
            dv_p = lax.dot_general(pb, dob, TN, preferred_element_type=F32)
            dk_p = lax.dot_general(dsb, qh, TN, preferred_element_type=F32)
            dvs[u] = dv_p if u not in dvs else dvs[u] + dv_p
            dks[u] = dk_p if u not in dks else dks[u] + dk_p
        for u in range(nsub):
            dk_ref[u * CHUNK : (u + 1) * CHUNK, :] = dks[u].astype(MXU)
            dv_ref[u * CHUNK : (u + 1) * CHUNK, :] = dvs[u].astype(MXU)

    blk = pl.BlockSpec((tb, LANES), lambda h, n: (n, h))
    head = pl.BlockSpec((CHUNK, LANES), lambda h, n: (jnp.minimum((n + 1) * nsub, last_chunk), h))
    return pl.pallas_call(
        body,
        grid=(HP, nb),
        in_specs=[blk] * 6 + [head] * 4,
        out_specs=[blk, blk],
        out_shape=[jax.ShapeDtypeStruct((S, D), MXU)] * 2,
        compiler_params=_params("parallel", "parallel"),
        name=name,
    )(q, k, v, do, out, lse, q, do, out, lse)


def _chip_peers():
    x, y, c = lax.axis_index("x"), lax.axis_index("y"), lax.axis_index("c")
    peers = [(1 - x, y, c), (x, 1 - y, c), (1 - x, 1 - y, c)]
    return x, y, c, peers


def _allgather_weights(wsh, *, name):
    R, D = wsh.shape
    half = R // 2

    def body(w_ref, out_ref, send_sems, recv_sems, local_sem):
        x, y, c, peers = _chip_peers()
        me, sibling = 2 * x + y, (x, y, 1 - c)

        def rows(chip, hc):
            return out_ref.at[chip, pl.ds(hc * half, half), :]

        def copy(k, chip, hc, to, src=None):
            return pltpu.make_async_remote_copy(
                src_ref=rows(chip, hc) if src is None else src, dst_ref=rows(chip, hc),
                send_sem=send_sems.at[k], recv_sem=recv_sems.at[k], device_id=to, device_id_type=MESH,
            )

        mine = pltpu.make_async_copy(w_ref, out_ref.at[me], local_sem)
        mine.start()
        first = [copy(j, me, c, p, src=w_ref.at[pl.ds(c * half, half), :]) for j, p in enumerate(peers)]
        for cp in first:
            cp.start()
        chips = [2 * p[0] + p[1] for p in peers]
        passed = [copy(3 + j, chip, c, sibling) for j, chip in enumerate(chips)]
        for j, chip in enumerate(chips):
            copy(j, chip, c, peers[j]).wait_recv()
            passed[j].start()
        for j, chip in enumerate(chips):
            copy(3 + j, chip, 1 - c, sibling).wait_recv()
        for cp in first + passed:
            cp.wait_send()
        mine.wait()

    hbm = pl.BlockSpec(memory_space=pl.ANY)
    return pl.pallas_call(
        body,
        in_specs=[hbm],
        out_specs=hbm,
        out_shape=jax.ShapeDtypeStruct((N_CHIPS, R, D), wsh.dtype),
        scratch_shapes=[pltpu.SemaphoreType.DMA((6,)), pltpu.SemaphoreType.DMA((6,)), pltpu.SemaphoreType.DMA],
        name=name,
    )(wsh)


def _scatter_grads(gflat, *, name):
    _, R, D = gflat.shape

    def body(g_ref, out_ref, send_sems, recv_sems):
        x, y, c, peers = _chip_peers()
        sends = [
            pltpu.make_async_remote_copy(
                src_ref=g_ref.at[2 * p[0] + p[1]], dst_ref=out_ref.at[j], send_sem=send_sems.at[j],
                recv_sem=recv_sems.at[j], device_id=p, device_id_type=MESH,
            )
            for j, p in enumerate(peers)
        ]
        for cp in sends:
            cp.start()
        for j, p in enumerate(peers):
            pltpu.make_async_remote_copy(
                src_ref=out_ref.at[j], dst_ref=out_ref.at[j], send_sem=send_sems.at[j], recv_sem=recv_sems.at[j],
                device_id=p, device_id_type=MESH,
            ).wait_recv()
        for cp in sends:
            cp.wait_send()

    hbm = pl.BlockSpec(memory_space=pl.ANY)
    return pl.pallas_call(
        body,
        in_specs=[hbm],
        out_specs=hbm,
        out_shape=jax.ShapeDtypeStruct((3, R, D), gflat.dtype),
        scratch_shapes=[pltpu.SemaphoreType.DMA((3,)), pltpu.SemaphoreType.DMA((3,))],
        name=name,
    )(gflat)


def _swap_with_sibling(part, *, name):
    def body(p_ref, out_ref, send_sem, recv_sem):
        x, y, c = lax.axis_index("x"), lax.axis_index("y"), lax.axis_index("c")
        cp = pltpu.make_async_remote_copy(
            src_ref=p_ref, dst_ref=out_ref, send_sem=send_sem, recv_sem=recv_sem,
            device_id=(x, y, 1 - c), device_id_type=MESH,
        )
        cp.start()
        cp.wait()

    hbm = pl.BlockSpec(memory_space=pl.ANY)
    return pl.pallas_call(
        body,
        in_specs=[hbm],
        out_specs=hbm,
        out_shape=jax.ShapeDtypeStruct(part.shape, part.dtype),
        scratch_shapes=[pltpu.SemaphoreType.DMA, pltpu.SemaphoreType.DMA],
        name=name,
    )(part)


def _gather_from_all(vec, *, name):
    R, D = vec.shape

    def body(v_ref, out_ref, send_sems, recv_sems):
        x, y, c = lax.axis_index("x"), lax.axis_index("y"), lax.axis_index("c")
        me = 4 * x + 2 * y + c
        out_ref[me] = v_ref[...]
        rel = [(j >> 2 & 1, j >> 1 & 1, j & 1) for j in range(1, 8)]
        peers = [((1 - x) if fx else x, (1 - y) if fy else y, (1 - c) if fc else c) for fx, fy, fc in rel]
        sends = [
            pltpu.make_async_remote_copy(
                src_ref=v_ref, dst_ref=out_ref.at[me], send_sem=send_sems.at[j], recv_sem=recv_sems.at[j],
                device_id=p, device_id_type=MESH,
            )
            for j, p in enumerate(peers)
        ]
        for cp in sends:
            cp.start()
        for j, p in enumerate(peers):
            theirs = out_ref.at[4 * p[0] + 2 * p[1] + p[2]]
            pltpu.make_async_remote_copy(
                src_ref=theirs, dst_ref=theirs, send_sem=send_sems.at[j], recv_sem=recv_sems.at[j],
                device_id=p, device_id_type=MESH,
            ).wait_recv()
        for cp in sends:
            cp.wait_send()

    vmem = pl.BlockSpec(memory_space=pltpu.VMEM)
    return pl.pallas_call(
        body,
        in_specs=[vmem],
        out_specs=vmem,
        out_shape=jax.ShapeDtypeStruct((8, R, D), vec.dtype),
        scratch_shapes=[pltpu.SemaphoreType.DMA((7,)), pltpu.SemaphoreType.DMA((7,))],
        name=name,
    )(vec)


def _sum4(a, rest, *, name):
    R, D = a.shape
    tm = _pick(R, (256, 128, 8))

    def body(a_ref, r_ref, o_ref):
        o_ref[...] = ((a_ref[...] + r_ref[0].astype(F32)) + r_ref[1].astype(F32)) + r_ref[2].astype(F32)

    row = pl.BlockSpec((tm, D), lambda i: (i, 0))
    return pl.pallas_call(
        body,
        grid=(R // tm,),
        in_specs=[row, pl.BlockSpec((3, tm, D), lambda i: (0, i, 0))],
        out_specs=row,
        out_shape=jax.ShapeDtypeStruct((R, D), F32),
        compiler_params=_params("parallel"),
        name=name,
    )(a, rest)


def _adamw(parts, w, m, v, *, name):
    P, R, D = parts.shape
    tm = _pick(R, (256, 128, 8))

    def body(p_ref, w_ref, m_ref, v_ref, g_ref, d_ref, nm_ref, nv_ref):
        g = p_ref[0]
        for k in range(1, P):
            g = g + p_ref[k]
        nm = ADAM_B1 * m_ref[...] + (1.0 - ADAM_B1) * g
        nv = ADAM_B2 * v_ref[...] + (1.0 - ADAM_B2) * jnp.square(g)
        m_hat = nm / (1.0 - ADAM_B1**ADAM_STEP)
        v_hat = nv / (1.0 - ADAM_B2**ADAM_STEP)
        g_ref[...] = g
        d_ref[...] = -ADAM_LR * (m_hat / (jnp.sqrt(v_hat) + ADAM_EPS) + ADAM_WD * w_ref[...])
        nm_ref[...] = nm
        nv_ref[...] = nv

    row = pl.BlockSpec((tm, D), lambda i: (i, 0))
    return pl.pallas_call(
        body,
        grid=(R // tm,),
        in_specs=[pl.BlockSpec((P, tm, D), lambda i: (0, i, 0)), row, row, row],
        out_specs=[row] * 4,
        out_shape=[jax.ShapeDtypeStruct((R, D), F32)] * 4,
        compiler_params=_params("parallel"),
        name=name,
    )(parts, w, m, v)


MATS = ("w_qkv", "w_o", "w_ff1", "w_ff2")
VECS = ("ln1_g", "ln1_b", "ln2_g", "ln2_b")
PARAM_ORDER = ("w_qkv", "w_o", "ln1_g", "ln1_b", "w_ff1", "w_ff2", "ln2_g", "ln2_b")


def _pack_rows(arrs, D):
    return jnp.concatenate([a.reshape(-1, D) for a in arrs], axis=0)


def kernel(x, w_qkv_0, w_o_0, ln1_g_0, ln1_b_0, w_ff1_0, w_ff2_0, ln2_g_0, ln2_b_0, w_qkv_1, w_o_1, ln1_g_1, ln1_b_1, w_ff1_1, w_ff2_1, ln2_g_1, ln2_b_1, loss_target, m_w_qkv_0, m_w_o_0, m_ln1_g_0, m_ln1_b_0, m_w_ff1_0, m_w_ff2_0, m_ln2_g_0, m_ln2_b_0, m_w_qkv_1, m_w_o_1, m_ln1_g_1, m_ln1_b_1, m_w_ff1_1, m_w_ff2_1, m_ln2_g_1, m_ln2_b_1, v_w_qkv_0, v_w_o_0, v_ln1_g_0, v_ln1_b_0, v_w_ff1_0, v_w_ff2_0, v_ln2_g_0, v_ln2_b_0, v_w_qkv_1, v_w_o_1, v_ln1_g_1, v_ln1_b_1, v_w_ff1_1, v_w_ff2_1, v_ln2_g_1, v_ln2_b_1):
    given = dict(locals())
    xs = x[0]
    target = loss_target[0]
    S, D = xs.shape
    C3 = 3 * D // N_CHIPS
    names = [f"{p}_{l}" for l in range(DEPTH) for p in PARAM_ORDER]
    mat_names = [f"{p}_{l}" for l in range(DEPTH) for p in MATS]
    vec_names = [f"{p}_{l}" for l in range(DEPTH) for p in VECS]
    rows = {"w_qkv": C3, "w_o": D // N_CHIPS, "w_ff1": D, "w_ff2": D}
    layer_rows = sum(rows.values())

    gathered = _allgather_weights(_pack_rows([given[n] for n in mat_names], D).astype(MXU), name="allgather_weights")
    W = []
    for l in range(DEPTH):
        r0 = l * layer_rows
        seg = {}
        for p in MATS:
            seg[p] = gathered[:, r0 : r0 + rows[p], :]
            r0 += rows[p]
        cat = lambda t, cols: t.reshape(N_CHIPS, D, cols).transpose(1, 0, 2).reshape(D, N_CHIPS * cols)
        W.append(
            dict(
                qkv=cat(seg["w_qkv"], C3),
                o=seg["w_o"].reshape(D, D),
                ff1=cat(seg["w_ff1"], D),
                ff2=seg["w_ff2"].reshape(4 * D, D),
            )
        )
    tables = _rope_tables(S)

    h, hb = xs, xs.astype(MXU)
    saved = []
    for l in range(DEPTH):
        g1, b1, g2, b2 = (given[f"{p}_{l}"] for p in VECS)
        qkv = _mm_nn(hb, W[l]["qkv"], epi="qkv", extras=tables if l == 1 else (), name=f"qkv_{l}")
        st = dict(hb=hb, qkv=qkv)
        if l == 0:
            o, car = _sb_fwd(qkv, name="stickbreak_fwd")
            st.update(car=car)
        else:
            q, k, v = qkv[:, :D], qkv[:, D : 2 * D], qkv[:, 2 * D :]
            qp, kp, vp = ([_perm(t, d) for d in DILATIONS] for t in (q, k, v))
            branches = [
                _dil_fwd(qp[i], kp[i], vp[i], branch=i, name=f"dilated_fwd_d{d}") for i, d in enumerate(DILATIONS)
            ]
            o, lse = _dil_combine(
                [_unperm(ob, d) for (ob, _), d in zip(branches, DILATIONS)],
                [_unperm(lb, d) for (_, lb), d in zip(branches, DILATIONS)],
                name="dilated_combine",
            )
            st.update(qp=qp, kp=kp, vp=vp, lse=lse)
        y1, x1, x1b = _mm_nn(o, W[l]["o"], epi="ln", extras=(h, g1, b1), name=f"attn_out_ln1_{l}")
        hp = _mm_nn(x1b, W[l]["ff1"], out_dtype=MXU, name=f"ff1_{l}")
        y2, x2, x2b = _mm_nn(hp, W[l]["ff2"], a_op="relu2", epi="ln", extras=(x1, g2, b2), name=f"ff2_ln2_{l}")
        st.update(o=o, y1=y1, x1b=x1b, hp=hp, y2=y2)
        saved.append(st)
        h, hb = x2, x2b

    grads = {}
    top = saved[-1]
    dy2, dy2b, dg, db, loss_lanes = _ln_bwd_loss(
        top["y2"], given[f"ln2_g_{DEPTH - 1}"], given[f"ln2_b_{DEPTH - 1}"], target, name="loss_ln2_bwd"
    )
    loss = lax.psum(jnp.sum(loss_lanes) * (0.5 / D), ("x", "y", "c"))
    grads[f"ln2_g_{DEPTH - 1}"], grads[f"ln2_b_{DEPTH - 1}"] = dg, db
    grad_x = None
    for l in reversed(range(DEPTH)):
        st = saved[l]
        dhp = _mm_nt(dy2b, W[l]["ff2"], epi="relu2grad", extra=st["hp"], out_dtype=MXU, name=f"d_ff2_in_{l}")
        grads[f"w_ff2_{l}"] = _mm_tn(st["hp"], dy2b, a_op="relu2", groups=1, name=f"d_w_ff2_{l}")
        dx1 = _mm_nt(dhp, W[l]["ff1"], epi="residual", extra=dy2, out_dtype=F32, name=f"d_ff1_in_{l}")
        grads[f"w_ff1_{l}"] = _mm_tn(st["x1b"], dhp, a_op=None, groups=N_CHIPS, name=f"d_w_ff1_{l}")
        dy1, dy1b, grads[f"ln1_g_{l}"], grads[f"ln1_b_{l}"] = _ln_bwd(
            dx1, st["y1"], given[f"ln1_g_{l}"], name=f"ln1_bwd_{l}"
        )
        grads[f"w_o_{l}"] = _mm_tn(st["o"], dy1b, a_op=None, groups=1, name=f"d_w_o_{l}")
        if l == 0:
            do = _mm_nt(dy1b, W[l]["o"], epi=None, extra=None, out_dtype=MXU, name=f"d_attn_out_{l}")
            dq, dk, dv = _sb_bwd(st["qkv"], do, st["car"], name="stickbreak_bwd")
            dqkv = _qkv_bwd_post([dq], [dk], [dv], None, name=f"qkv_bwd_post_{l}")
        else:
            do = _mm_nt(dy1b, W[l]["o"], epi=None, extra=None, out_dtype=MXU, name=f"d_attn_out_{l}")
            dqs, dks, dvs = [], [], []
            for i, d in enumerate(DILATIONS):
                ops = (st["qp"][i], st["kp"][i], st["vp"][i], _perm(do, d), _perm(st["o"], d), _perm(st["lse"], d))
                dqs.append(_unperm(_dil_bwd_dq(*ops, branch=i, name=f"dilated_bwd_dq_d{d}"), d))
                dk_d, dv_d = _dil_bwd_dkv(*ops, branch=i, name=f"dilated_bwd_dkv_d{d}")
                dks.append(_unperm(dk_d, d))
                dvs.append(_unperm(dv_d, d))
            dqkv = _qkv_bwd_post(dqs, dks, dvs, tables, name=f"qkv_bwd_post_{l}")
        dh = _mm_nt(dqkv, W[l]["qkv"], epi="residual", extra=dy1, out_dtype=F32, name=f"d_qkv_in_{l}")
        grads[f"w_qkv_{l}"] = _mm_tn(st["hb"], dqkv, a_op=None, groups=N_CHIPS, name=f"d_w_qkv_{l}")
        if l > 0:
            prev = saved[l - 1]
            dy2, dy2b, grads[f"ln2_g_{l - 1}"], grads[f"ln2_b_{l - 1}"] = _ln_bwd(
                dh, prev["y2"], given[f"ln2_g_{l - 1}"], name=f"ln2_bwd_{l - 1}"
            )
        else:
            grad_x = dh

    gflat = jnp.concatenate([grads[n].reshape(N_CHIPS, -1, D) for n in mat_names], axis=1)
    me = 2 * lax.axis_index("x") + lax.axis_index("y")
    own = lax.dynamic_index_in_dim(gflat, me, 0, keepdims=False)
    from_chips = _scatter_grads(gflat.astype(MXU), name="scatter_grads")
    part = _sum4(own, from_chips, name="sum_chips")
    sib = _swap_with_sibling(part, name="swap_sibling")
    pack = lambda prefix: _pack_rows([given[prefix + n] for n in mat_names], D)
    outs_m = _adamw(jnp.stack([part, sib]), pack(""), pack("m_"), pack("v_"), name="adamw_matrices")
    vflat = _pack_rows([grads[n] for n in vec_names], D)
    packv = lambda prefix: _pack_rows([given[prefix + n] for n in vec_names], D)
    outs_v = _adamw(_gather_from_all(vflat, name="gather_ln_grads"), packv(""), packv("m_"), packv("v_"), name="adamw_vectors")

    def unpack(flat_m, flat_v):
        out, r0 = {}, 0
        for n in mat_names:
            r = given[n].size // D
            out[n] = flat_m[r0 : r0 + r].reshape(given[n].shape)
            r0 += r
        for i, n in enumerate(vec_names):
            out[n] = flat_v[i]
        return [out[n] for n in names]

    result = [loss, grad_x[None]]
    for k in range(4):
        result += unpack(outs_m[k], outs_v[k])
    return tuple(result)
```

```python
import functools

import jax
import jax.numpy as jnp
from jax import lax
from jax.experimental import pallas as pl
from jax.experimental.pallas import tpu as pltpu

F32 = jnp.float32
MXU = jnp.bfloat16

HEAD_DIM = 64
LANES = 128
CHUNK = 128
ROPE_THETA = 500000.0
ROPE_HALF = 8
DILATIONS = (1, 4, 16)
DEPTH = 2
ALPHA = (2 * DEPTH) ** 0.25
LN_EPS = 1e-5
QK_SCALE = 0.125
ADAM_LR, ADAM_B1, ADAM_B2, ADAM_EPS, ADAM_WD, ADAM_STEP = 0.001, 0.9, 0.999, 1e-08, 0.01, 10
NEG = -1e30
UNDERFLOW = -104.0
SB_TQ = 2 * CHUNK
V7X_VMEM_LIMIT = 56 * 1024 * 1024
MESH = pl.DeviceIdType.MESH
NT = (((1,), (1,)), ((), ()))
TN = (((0,), (0,)), ((), ()))
N_CHIPS = 4
MM_ROWS = (1024, 512, 256, 128)


def _params(*sem):
    return pltpu.CompilerParams(dimension_semantics=sem, vmem_limit_bytes=V7X_VMEM_LIMIT)


def _pick(n, cands=(1024, 768, 512, 384, 256, 128)):
    for c in cands:
        if n % c == 0:
            return c
    raise ValueError(f"no tile for {n}")


def _mm_nn(a, b, *, a_op=None, epi=None, extras=(), out_dtype=F32, name):
    M, K = a.shape
    Kb, N = b.shape
    assert K == Kb
    tn = N // 3 if epi == "qkv" else (N if epi == "ln" else _pick(N))
    tm = _pick(M, MM_ROWS) if (K <= 1024 and epi != "ln") else _pick(M, MM_ROWS[1:])
    rope = epi == "qkv" and len(extras) == 3

    def body(a_ref, b_ref, *refs):
        av = a_ref[...]
        if a_op == "relu2":
            av = jnp.square(jnp.maximum(av.astype(F32), 0.0))
        part = jnp.dot(av.astype(MXU), b_ref[...], preferred_element_type=F32)
        if epi is None:
            refs[0][...] = part.astype(out_dtype)
        elif epi == "qkv":
            o_ref = refs[-1]
            tabs = tuple(r[...] for r in refs[:-1])
            j = pl.program_id(1)

            def write(rotate, scale):
                for blk in range(tn // LANES):
                    cols = slice(blk * LANES, (blk + 1) * LANES)
                    xb = part[:, cols]
                    if rotate:
                        xb = _rope(xb, *tabs)
                    if scale:
                        xb = xb * QK_SCALE
                    o_ref[:, cols] = xb.astype(MXU)

            pl.when(j == 0)(lambda: write(rope, True))
            pl.when(j == 1)(lambda: write(rope, False))
            pl.when(j == 2)(lambda: write(False, False))
        else:
            x_ref, g_ref, be_ref, y_ref, o_ref, ob_ref = refs
            y = ALPHA * x_ref[...] + part
            xhat, _ = _ln_stats(y)
            out = xhat * g_ref[...] + be_ref[...]
            y_ref[...] = y
            o_ref[...] = out
            ob_ref[...] = out.astype(MXU)

    in_specs = [pl.BlockSpec((tm, K), lambda i, j: (i, 0)), pl.BlockSpec((K, tn), lambda i, j: (0, j))]
    tile = pl.BlockSpec((tm, tn), lambda i, j: (i, j))
    ops = [a, b]
    if epi == "qkv":
        in_specs += [pl.BlockSpec((tm, LANES), lambda i, j: (i, 0))] * len(extras)
        ops += list(extras)
        out_specs, out_shape = tile, jax.ShapeDtypeStruct((M, N), MXU)
    elif epi == "ln":
        x, g, be = extras
        in_specs += [tile, pl.BlockSpec((1, N), lambda i, j: (0, 0)), pl.BlockSpec((1, N), lambda i, j: (0, 0))]
        ops += [x, g.reshape(1, N), be.reshape(1, N)]
        out_specs = [tile] * 3
        out_shape = [jax.ShapeDtypeStruct((M, N), F32)] * 2 + [jax.ShapeDtypeStruct((M, N), MXU)]
    else:
        out_specs, out_shape = tile, jax.ShapeDtypeStruct((M, N), out_dtype)
    return pl.pallas_call(
        body,
        grid=(M // tm, N // tn),
        in_specs=in_specs,
        out_specs=out_specs,
        out_shape=out_shape,
        compiler_params=_params("parallel", "parallel"),
        name=name,
    )(*ops)


def _mm_nt(a, b, *, epi, extra, out_dtype, name):
    M, N = a.shape
    K, Nb = b.shape
    assert N == Nb
    tm = _pick(M, MM_ROWS) if N <= 1024 else _pick(M, MM_ROWS[1:])
    tko = _pick(K)

    def body(*refs):
        if epi is None:
            a_ref, b_ref, o_ref = refs
        else:
            a_ref, b_ref, e_ref, o_ref = refs
        r = lax.dot_general(a_ref[...].astype(MXU), b_ref[...], NT, preferred_element_type=F32)
        if epi == "relu2grad":
            r = r * (2.0 * jnp.maximum(e_ref[...].astype(F32), 0.0))
        elif epi == "residual":
            r = ALPHA * e_ref[...] + r
        o_ref[...] = r.astype(out_dtype)

    in_specs = [pl.BlockSpec((tm, N), lambda i, j: (i, 0)), pl.BlockSpec((tko, N), lambda i, j: (j, 0))]
    ops = [a, b]
    if epi is not None:
        in_specs.append(pl.BlockSpec((tm, tko), lambda i, j: (i, j)))
        ops.append(extra)
    return pl.pallas_call(
        body,
        grid=(M // tm, K // tko),
        in_specs=in_specs,
        out_specs=pl.BlockSpec((tm, tko), lambda i, j: (i, j)),
        out_shape=jax.ShapeDtypeStruct((M, K), out_dtype),
        compiler_params=_params("parallel", "parallel"),
        name=name,
    )(*ops)


def _mm_tn(a, b, *, a_op, groups, name):
    M, Ka = a.shape
    Mb, N = b.shape
    assert M == Mb
    Nc = N // groups
    tka, tn, tm = _pick(Ka), _pick(Nc), _pick(M, (2048, 1024, 512, 256, 128))
    nbs, nm = Nc // tn, M // tm

    def body(a_ref, b_ref, o_ref, acc_ref):
        kk = pl.program_id(2)
        av = a_ref[...]
        if a_op == "relu2":
            av = jnp.square(jnp.maximum(av.astype(F32), 0.0))
        part = lax.dot_general(av.astype(MXU), b_ref[...].astype(MXU), TN, preferred_element_type=F32)

        @pl.when(kk == 0)
        def _():
            acc_ref[...] = part

        @pl.when(kk > 0)
        def _():
            acc_ref[...] += part

        @pl.when(kk == nm - 1)
        def _():
            o_ref[...] = acc_ref[...]

    return pl.pallas_call(
        body,
        grid=(Ka // tka, N // tn, nm),
        in_specs=[
            pl.BlockSpec((tm, tka), lambda i, j, k: (k, i)),
            pl.BlockSpec((tm, tn), lambda i, j, k: (k, j)),
        ],
        out_specs=pl.BlockSpec((None, tka, tn), lambda i, j, k: (j // nbs, i, j % nbs)),
        out_shape=jax.ShapeDtypeStruct((groups, Ka, Nc), F32),
        scratch_shapes=[pltpu.VMEM((tka, tn), F32)],
        compiler_params=_params("parallel", "parallel", "arbitrary"),
        name=name,
    )(a, b)


def _ln_stats(y):
    mu = jnp.mean(y, axis=-1, keepdims=True)
    yc = y - mu
    var = jnp.mean(yc * yc, axis=-1, keepdims=True)
    rstd = lax.rsqrt(var + LN_EPS)
    return yc * rstd, rstd


def _ln_bwd(dx, y, g, *, name):
    S, D = y.shape
    tm = _pick(S, (256, 128))

    def body(dx_ref, y_ref, g_ref, dy_ref, dyb_ref, dg_ref, db_ref):
        xhat, rstd = _ln_stats(y_ref[...])
        dx_v = dx_ref[...]
        dxh = dx_v * g_ref[...]
        m1 = jnp.mean(dxh, axis=-1, keepdims=True)
        m2 = jnp.mean(dxh * xhat, axis=-1, keepdims=True)
        dy = rstd * (dxh - m1 - xhat * m2)
        dy_ref[...] = dy
        dyb_ref[...] = dy.astype(MXU)

        @pl.when(pl.program_id(0) == 0)
        def _():
            dg_ref[...] = jnp.zeros_like(dg_ref)
            db_ref[...] = jnp.zeros_like(db_ref)

        dg_ref[...] += jnp.sum(dx_v * xhat, axis=0, keepdims=True)
        db_ref[...] += jnp.sum(dx_v, axis=0, keepdims=True)

    row = pl.BlockSpec((tm, D), lambda i: (i, 0))
    vec = pl.BlockSpec((1, D), lambda i: (0, 0))
    return pl.pallas_call(
        body,
        grid=(S // tm,),
        in_specs=[row, row, vec],
        out_specs=[row, row, vec, vec],
        out_shape=[jax.ShapeDtypeStruct((S, D), F32), jax.ShapeDtypeStruct((S, D), MXU)]
        + [jax.ShapeDtypeStruct((1, D), F32)] * 2,
        compiler_params=_params("arbitrary"),
        name=name,
    )(dx, y, g.reshape(1, D))


def _ln_bwd_loss(y, g, b, target, *, name):
    S, D = y.shape
    tm = _pick(S, (256, 128))

    def body(y_ref, g_ref, b_ref, t_ref, dy_ref, dyb_ref, dg_ref, db_ref, ls_ref):
        xhat, rstd = _ln_stats(y_ref[...])
        err = xhat * g_ref[...] + b_ref[...] - t_ref[...]
        dx_v = err * (1.0 / D)
        dxh = dx_v * g_ref[...]
        m1 = jnp.mean(dxh, axis=-1, keepdims=True)
        m2 = jnp.mean(dxh * xhat, axis=-1, keepdims=True)
        dy = rstd * (dxh - m1 - xhat * m2)
        dy_ref[...] = dy
        dyb_ref[...] = dy.astype(MXU)

        @pl.when(pl.program_id(0) == 0)
        def _():
            dg_ref[...] = jnp.zeros_like(dg_ref)
            db_ref[...] = jnp.zeros_like(db_ref)
            ls_ref[...] = jnp.zeros_like(ls_ref)

        dg_ref[...] += jnp.sum(dx_v * xhat, axis=0, keepdims=True)
        db_ref[...] += jnp.sum(dx_v, axis=0, keepdims=True)
        ls_ref[...] += jnp.sum(err * err, axis=0, keepdims=True)

    row = pl.BlockSpec((tm, D), lambda i: (i, 0))
    vec = pl.BlockSpec((1, D), lambda i: (0, 0))
    return pl.pallas_call(
        body,
        grid=(S // tm,),
        in_specs=[row, vec, vec, row],
        out_specs=[row, row, vec, vec, vec],
        out_shape=[jax.ShapeDtypeStruct((S, D), F32), jax.ShapeDtypeStruct((S, D), MXU)]
        + [jax.ShapeDtypeStruct((1, D), F32)] * 3,
        compiler_params=_params("arbitrary"),
        name=name,
    )(y, g.reshape(1, D), b.reshape(1, D), target)


def _rope_tables(S):
    inv_freq = ROPE_THETA ** (-jnp.arange(ROPE_HALF, dtype=F32) / ROPE_HALF)
    ang = jnp.arange(S, dtype=jnp.int32).astype(F32)[:, None] * inv_freq[None, :]
    cos, sin = jnp.cos(ang), jnp.sin(ang)
    ones = jnp.ones((S, HEAD_DIM - 2 * ROPE_HALF), F32)
    zeros8 = jnp.zeros((S, ROPE_HALF), F32)
    c = jnp.concatenate([cos, cos, ones], axis=1)
    s1 = jnp.concatenate([zeros8, sin, 0.0 * ones], axis=1)
    s2 = jnp.concatenate([-sin, zeros8, 0.0 * ones], axis=1)
    return tuple(jnp.concatenate([t, t], axis=1) for t in (c, s1, s2))


def _rope(xb, c, s1, s2):
    return xb * c + pltpu.roll(xb, ROPE_HALF, 1) * s1 + pltpu.roll(xb, LANES - ROPE_HALF, 1) * s2


def _rope_t(dyb, c, s1, s2):
    return dyb * c + pltpu.roll(dyb * s1, LANES - ROPE_HALF, 1) + pltpu.roll(dyb * s2, ROPE_HALF, 1)


def _qkv_bwd_post(dqs, dks, dvs, tables, *, name):
    S, D = dqs[0].shape
    tm = _pick(S, (256, 128))
    nb = len(dqs)
    rope = tables is not None

    def body(*refs):
        ins, o_ref = refs[:-1], refs[-1]
        groups = [ins[0:nb], ins[nb : 2 * nb], ins[2 * nb : 3 * nb]]
        tabs = [r[...] for r in ins[3 * nb :]]
        for j, grp in enumerate(groups):
            for blk in range(D // LANES):
                cols = slice(blk * LANES, (blk + 1) * LANES)
                v = grp[0][:, cols].astype(F32)
                for r in grp[1:]:
                    v = v + r[:, cols].astype(F32)
                if rope and j < 2:
                    v = _rope_t(v, *tabs)
                if j == 0:
                    v = v * QK_SCALE
                o_ref[:, j * D + blk * LANES : j * D + (blk + 1) * LANES] = v.astype(MXU)

    row = pl.BlockSpec((tm, D), lambda i: (i, 0))
    tab = pl.BlockSpec((tm, LANES), lambda i: (i, 0))
    return pl.pallas_call(
        body,
        grid=(S // tm,),
        in_specs=[row] * (3 * nb) + ([tab] * 3 if rope else []),
        out_specs=pl.BlockSpec((tm, 3 * D), lambda i: (i, 0)),
        out_shape=jax.ShapeDtypeStruct((S, 3 * D), MXU),
        compiler_params=_params("parallel"),
        name=name,
    )(*dqs, *dks, *dvs, *(tables if rope else ()))


def _log_fail(z):
    return -(jnp.maximum(z, 0.0) + jnp.log(1.0 + jnp.exp(-jnp.abs(z))))


def _sb_fwd(qkv, *, name):
    S, D = qkv.shape[0], qkv.shape[1] // 3
    HP, H = D // LANES, D // HEAD_DIM
    tq = SB_TQ
    assert S % tq == 0
    cpb = tq // CHUNK
    assert S // CHUNK <= LANES

    def body(q_ref, k_ref, v_ref, o_ref, car_ref, acc_ref, carry0_ref, carry1_ref):
        i = pl.program_id(1)
        lane = lax.broadcasted_iota(jnp.int32, (tq, LANES), 1)
        row = lax.broadcasted_iota(jnp.int32, (tq, LANES), 0)
        kl = lax.broadcasted_iota(jnp.int32, (CHUNK, LANES), 1)
        kr = lax.broadcasted_iota(jnp.int32, (CHUNK, LANES), 0)
        tri = (kr >= kl).astype(MXU)
        qv = q_ref[...]
        qh = (jnp.where(lane < HEAD_DIM, qv, jnp.zeros_like(qv)), jnp.where(lane >= HEAD_DIM, qv, jnp.zeros_like(qv)))
        carry_refs = (carry0_ref, carry1_ref)
        acc_ref[...] = jnp.zeros_like(acc_ref)
        car_ref[...] = jnp.full((2, tq, LANES), NEG, F32)
        for r in carry_refs:
            r[...] = jnp.zeros_like(r)

        def run(chunks, masked):
            offs = [pl.multiple_of(c * CHUNK, CHUNK) for c in chunks]
            kcs = [k_ref[pl.ds(o, CHUNK), :] for o in offs]
            vcs = [v_ref[pl.ds(o, CHUNK), :] for o in offs]
            jobs = [(n, hh) for n in range(len(chunks)) for hh in range(2)]
            zs = {j: lax.dot_general(qh[j[1]], kcs[j[0]], NT, preferred_element_type=F32) for j in jobs}
            if masked:
                cms = [(o + lane) < (i * tq + row) for o in offs]
            lfs = {}
            for j in jobs:
                lf = _log_fail(zs[j])
                lfs[j] = jnp.where(cms[j[0]], lf, 0.0) if masked else lf
            rs = {j: jnp.dot(lfs[j].astype(MXU), tri, preferred_element_type=F32) for j in jobs}
            ws = {}
            for hh in range(2):
                carry = carry_refs[hh][...]
                tile = car_ref[hh]
                for n, c in enumerate(chunks):
                    w = jnp.exp(zs[n, hh] + rs[n, hh] + carry)
                    if masked:
                        w = jnp.where(cms[n], w, 0.0)
                    ws[n, hh] = w.astype(MXU)
                    tile = jnp.where(lane == c, carry, tile)
                    carry = carry + jnp.sum(lfs[n, hh], axis=-1, keepdims=True)
                car_ref[hh] = tile
                carry_refs[hh][...] = carry
            for hh in range(2):
                pv = jnp.dot(ws[0, hh], vcs[0], preferred_element_type=F32)
                for n in range(1, len(chunks)):
                    pv = pv + jnp.dot(ws[n, hh], vcs[n], preferred_element_type=F32)
                acc_ref[hh] += pv

        def max_carry():
            return jnp.maximum(jnp.max(carry0_ref[...]), jnp.max(carry1_ref[...]))

        run([i * cpb + u for u in reversed(range(cpb))], True)

        def cond(st):
            return jnp.logical_and(st[0] >= 0, st[1] >= UNDERFLOW)

        def pair(st):
            c = st[0]
            run([c, c - 1], False)
            return c - 2, max_carry()

        lax.while_loop(cond, pair, (i * cpb - 1, max_carry()))
        o_ref[...] = jnp.where(lane < HEAD_DIM, acc_ref[0], acc_ref[1]).astype(o_ref.dtype)

    blk = pl.BlockSpec((tq, LANES), lambda h, i: (i, h))
    k_full = pl.BlockSpec((S, LANES), lambda h, i: (0, HP + h))
    v_full = pl.BlockSpec((S, LANES), lambda h, i: (0, 2 * HP + h))
    return pl.pallas_call(
        body,
        grid=(HP, S // tq),
        in_specs=[blk, k_full, v_full],
        out_specs=[blk, pl.BlockSpec((2, tq, LANES), lambda h, i: (h, i, 0))],
        out_shape=[jax.ShapeDtypeStruct((S, D), MXU), jax.ShapeDtypeStruct((H, S, LANES), F32)],
        scratch_shapes=[pltpu.VMEM((2, tq, LANES), F32), pltpu.VMEM((tq, 1), F32), pltpu.VMEM((tq, 1), F32)],
        compiler_params=_params("parallel", "arbitrary"),
        name=name,
    )(qkv, qkv, qkv)


def _sb_bwd(qkv, do, car, *, name):
    S, D = qkv.shape[0], qkv.shape[1] // 3
    HP = D // LANES
    tq = SB_TQ
    assert S % tq == 0
    cpb = tq // CHUNK
    nq = S // tq

    def body(q_ref, k_ref, v_ref, do_ref, car_ref, dq_ref, dk_hbm, dv_hbm, dq_acc, dk_acc, dv_acc, gcar0_ref, gcar1_ref, sem):
        hp = pl.program_id(0)
        i = pl.program_id(1)
        lane = lax.broadcasted_iota(jnp.int32, (tq, LANES), 1)
        row = lax.broadcasted_iota(jnp.int32, (tq, LANES), 0)
        kl = lax.broadcasted_iota(jnp.int32, (CHUNK, LANES), 1)
        kr = lax.broadcasted_iota(jnp.int32, (CHUNK, LANES), 0)
        tri = (kr >= kl).astype(MXU)
        tri_prefix = (kr <= kl).astype(MXU)

        @pl.when(i == 0)
        def _():
            dk_acc[...] = jnp.zeros_like(dk_acc)
            dv_acc[...] = jnp.zeros_like(dv_acc)

        qv = q_ref[...]
        dov = do_ref[...]
        qh = (jnp.where(lane < HEAD_DIM, qv, jnp.zeros_like(qv)), jnp.where(lane >= HEAD_DIM, qv, jnp.zeros_like(qv)))
        doh = (jnp.where(lane < HEAD_DIM, dov, jnp.zeros_like(dov)), jnp.where(lane >= HEAD_DIM, dov, jnp.zeros_like(dov)))
        gcar_refs = (gcar0_ref, gcar1_ref)
        dq_acc[...] = jnp.zeros_like(dq_acc)
        for r in gcar_refs:
            r[...] = jnp.zeros_like(r)

        def run(chunks, masked):
            offs = [pl.multiple_of(c * CHUNK, CHUNK) for c in chunks]
            kcs = [k_ref[pl.ds(o, CHUNK), :] for o in offs]
            vcs = [v_ref[pl.ds(o, CHUNK), :] for o in offs]
            jobs = [(n, hh) for n in range(len(chunks)) for hh in range(2)]
            if masked:
                cms = [(o + lane) < (i * tq + row) for o in offs]
            zs = {j: lax.dot_general(qh[j[1]], kcs[j[0]], NT, preferred_element_type=F32) for j in jobs}
            dws = {j: lax.dot_general(doh[j[1]], vcs[j[0]], NT, preferred_element_type=F32) for j in jobs}
            lfs, sigs = {}, {}
            for j in jobs:
                lf = _log_fail(zs[j])
                sigs[j] = jnp.exp(zs[j] + lf)
                lfs[j] = jnp.where(cms[j[0]], lf, 0.0) if masked else lf
            rs = {j: jnp.dot(lfs[j].astype(MXU), tri, preferred_element_type=F32) for j in jobs}
            ws, gs = {}, {}
            for hh in range(2):
                tile = car_ref[hh]
                for n, c in enumerate(chunks):
                    carry = jnp.sum(jnp.where(lane == c, tile, 0.0), axis=-1, keepdims=True)
                    w = jnp.exp(zs[n, hh] + rs[n, hh] + carry)
                    if masked:
                        w = jnp.where(cms[n], w, 0.0)
                    ws[n, hh] = w.astype(MXU)
                    gs[n, hh] = w * dws[n, hh]
            big_gs = {j: jnp.dot(gs[j].astype(MXU), tri_prefix, preferred_element_type=F32) for j in jobs}
            dzs = {}
            for hh in range(2):
                gcar = gcar_refs[hh][...]
                for n in range(len(chunks)):
                    dz = gs[n, hh] - sigs[n, hh] * (big_gs[n, hh] + gcar)
                    if masked:
                        dz = jnp.where(cms[n], dz, 0.0)
                    dzs[n, hh] = dz.astype(MXU)
                    gcar = gcar + jnp.sum(gs[n, hh], axis=-1, keepdims=True)
                gcar_refs[hh][...] = gcar
            for hh in range(2):
                part = jnp.dot(dzs[0, hh], kcs[0], preferred_element_type=F32)
                for n in range(1, len(chunks)):
                    part = part + jnp.dot(dzs[n, hh], kcs[n], preferred_element_type=F32)
                dq_acc[hh] += part
            for n, o in enumerate(offs):
                dk_acc[pl.ds(o, CHUNK), :] += lax.dot_general(
                    dzs[n, 0], qh[0], TN, preferred_element_type=F32
                ) + lax.dot_general(dzs[n, 1], qh[1], TN, preferred_element_type=F32)
                dv_acc[pl.ds(o, CHUNK), :] += lax.dot_general(
                    ws[n, 0], doh[0], TN, preferred_element_type=F32
                ) + lax.dot_general(ws[n, 1], doh[1], TN, preferred_element_type=F32)

        colmax = jnp.maximum(jnp.max(car_ref[0], axis=0, keepdims=True), jnp.max(car_ref[1], axis=0, keepdims=True))
        lane1 = lax.broadcasted_iota(jnp.int32, (1, LANES), 1)
        dead = jnp.logical_and(colmax < UNDERFLOW, lane1 < i * cpb)
        first_pair = jnp.sum(jnp.where(dead, 1.0, 0.0)).astype(jnp.int32) // 2

        def loop_body(p, carry):
            run([2 * p, 2 * p + 1], False)
            return carry

        lax.fori_loop(first_pair, i * cpb // 2, loop_body, 0)
        run([i * cpb + u for u in range(cpb)], True)
        dq_ref[...] = jnp.where(lane < HEAD_DIM, dq_acc[0], dq_acc[1])

        @pl.when(i == nq - 1)
        def _():
            cols = pl.ds(pl.multiple_of(hp * LANES, LANES), LANES)
            for src, dst in ((dk_acc, dk_hbm), (dv_acc, dv_hbm)):
                cp = pltpu.make_async_copy(src, dst.at[:, cols], sem)
                cp.start()
                cp.wait()

    blk = pl.BlockSpec((tq, LANES), lambda h, i: (i, h))
    k_full = pl.BlockSpec((S, LANES), lambda h, i: (0, HP + h))
    v_full = pl.BlockSpec((S, LANES), lambda h, i: (0, 2 * HP + h))
    hbm = pl.BlockSpec(memory_space=pl.ANY)
    return pl.pallas_call(
        body,
        grid=(HP, nq),
        in_specs=[blk, k_full, v_full, blk, pl.BlockSpec((2, tq, LANES), lambda h, i: (h, i, 0))],
        out_specs=[blk, hbm, hbm],
        out_shape=[jax.ShapeDtypeStruct((S, D), F32)] * 3,
        scratch_shapes=[
            pltpu.VMEM((2, tq, LANES), F32),
            pltpu.VMEM((S, LANES), F32),
            pltpu.VMEM((S, LANES), F32),
            pltpu.VMEM((tq, 1), F32),
            pltpu.VMEM((tq, 1), F32),
            pltpu.SemaphoreType.DMA,
        ],
        compiler_params=_params("arbitrary", "arbitrary"),
        name=name,
    )(qkv, qkv, qkv, do, car)


def _perm(x, d):
    if d == 1:
        return x
    S, D = x.shape
    return x.reshape(S // d, d, D).transpose(1, 0, 2).reshape(S, D)


def _unperm(x, d):
    if d == 1:
        return x
    S, D = x.shape
    return x.reshape(d, S // d, D).transpose(1, 0, 2).reshape(S, D)


def _dil_tb(S):
    return min(1024, S // DILATIONS[-1])


def _dil_masks():
    qi = lax.broadcasted_iota(jnp.int32, (CHUNK, CHUNK), 0)
    kj = lax.broadcasted_iota(jnp.int32, (CHUNK, CHUNK), 1)
    return qi, kj


def _dil_fwd(q, k, v, *, branch, name):
    S, D = q.shape
    HP = D // LANES
    tb = _dil_tb(S)
    nb, nsub = S // tb, tb // CHUNK
    bps = nb // DILATIONS[branch]

    def body(q_ref, k_ref, v_ref, kp_ref, vp_ref, o_ref, l_ref):
        n = pl.program_id(1)
        first = jnp.bitwise_and(n, bps - 1) == 0
        qi, kj = _dil_masks()
        hl = lax.broadcasted_iota(jnp.int32, (CHUNK, LANES), 1)
        m_cur = kj <= qi
        m_prev = (kj - qi) >= 0
        m_prev0 = (kj - qi) >= jnp.where(first, 2 * CHUNK, 0)
        hms = (hl < HEAD_DIM, hl >= HEAD_DIM)
        tiles, scores = [], {}
        for u in range(nsub):
            rows = slice(u * CHUNK, (u + 1) * CHUNK)
            qs, kc, vc = q_ref[rows, :], k_ref[rows, :], v_ref[rows, :]
            if u == 0:
                kp, vp, pm = kp_ref[...], vp_ref[...], m_prev0
            else:
                prev = slice((u - 1) * CHUNK, u * CHUNK)
                kp, vp, pm = k_ref[prev, :], v_ref[prev, :], m_prev
            tiles.append((rows, vp, vc, pm))
            for hh in range(2):
                qh = jnp.where(hms[hh], qs, jnp.zeros_like(qs))
                scores[u, hh] = (
                    lax.dot_general(qh, kp, NT, preferred_element_type=F32),
                    lax.dot_general(qh, kc, NT, preferred_element_type=F32),
                )
        probs = {}
        for (u, hh), (s_p, s_c) in scores.items():
            s_p = jnp.where(tiles[u][3], s_p, NEG)
            s_c = jnp.where(m_cur, s_c, NEG)
            m = jnp.max(jnp.maximum(s_p, s_c), axis=-1, keepdims=True)
            p_p, p_c = jnp.exp(s_p - m), jnp.exp(s_c - m)
            den = jnp.sum(p_p + p_c, axis=-1, keepdims=True)
            probs[u, hh] = (p_p.astype(MXU), p_c.astype(MXU), 1.0 / den, m + jnp.log(den))
        for u, (rows, vp, vc, _) in enumerate(tiles):
            o_t = l_t = None
            for hh in range(2):
                p_p, p_c, inv, lse = probs[u, hh]
                num = jnp.dot(p_p, vp, preferred_element_type=F32) + jnp.dot(p_c, vc, preferred_element_type=F32)
                o_h, l_h = num * inv, jnp.broadcast_to(lse, (CHUNK, LANES))
                o_t = o_h if hh == 0 else jnp.where(hms[1], o_h, o_t)
                l_t = l_h if hh == 0 else jnp.where(hms[1], l_h, l_t)
            o_ref[rows, :] = o_t
            l_ref[rows, :] = l_t

    blk = pl.BlockSpec((tb, LANES), lambda h, n: (n, h))
    tail = pl.BlockSpec((CHUNK, LANES), lambda h, n: (jnp.maximum(n * nsub - 1, 0), h))
    return pl.pallas_call(
        body,
        grid=(HP, nb),
        in_specs=[blk, blk, blk, tail, tail],
        out_specs=[blk, blk],
        out_shape=[jax.ShapeDtypeStruct((S, D), F32)] * 2,
        compiler_params=_params("parallel", "parallel"),
        name=name,
    )(q, k, v, k, v)


def _dil_combine(os, ls, *, name):
    S, D = os[0].shape
    tm = _pick(S, (256, 128))

    def body(o0_ref, o1_ref, o2_ref, l0_ref, l1_ref, l2_ref, out_ref, lse_ref):
        l0, l1, l2 = l0_ref[...], l1_ref[...], l2_ref[...]
        m = jnp.maximum(jnp.maximum(l0, l1), l2)
        a0, a1, a2 = jnp.exp(l0 - m), jnp.exp(l1 - m), jnp.exp(l2 - m)
        den = a0 + a1 + a2
        out_ref[...] = (a0 * o0_ref[...] + a1 * o1_ref[...] + a2 * o2_ref[...]) / den
        lse = m + jnp.log(den)
        lane = lax.broadcasted_iota(jnp.int32, (tm, LANES), 1)
        packed = jnp.zeros((tm, LANES), F32)
        for hp in range(D // LANES):
            blk = lse[:, hp * LANES : (hp + 1) * LANES]
            for hh in range(2):
                packed = jnp.where(lane == 2 * hp + hh, _lane_col(blk, lane, hh * HEAD_DIM), packed)
        lse_ref[...] = packed

    row = pl.BlockSpec((tm, D), lambda i: (i, 0))
    return pl.pallas_call(
        body,
        grid=(S // tm,),
        in_specs=[row] * 6,
        out_specs=[row, pl.BlockSpec((tm, LANES), lambda i: (i, 0))],
        out_shape=[jax.ShapeDtypeStruct((S, D), F32), jax.ShapeDtypeStruct((S, LANES), F32)],
        compiler_params=_params("parallel"),
        name=name,
    )(*os, *ls)


def _lane_col(tile, lane, at):
    return jnp.sum(jnp.where(lane == at, tile, 0.0), axis=-1, keepdims=True)


def _head_stats(do, out, lse_packed, *, name):
    S, D = out.shape
    H = D // HEAD_DIM
    assert 2 * H <= LANES
    tm = _pick(S, (256, 128))

    def body(do_ref, out_ref, l_ref, s_ref):
        lane = lax.broadcasted_iota(jnp.int32, (tm, LANES), 1)
        packed = l_ref[...]
        for hp in range(D // LANES):
            cols = slice(hp * LANES, (hp + 1) * LANES)
            prod = do_ref[:, cols].astype(F32) * out_ref[:, cols]
            for hh in range(2):
                head = (lane >= HEAD_DIM) if hh else (lane < HEAD_DIM)
                delta = jnp.sum(jnp.where(head, prod, 0.0), axis=-1, keepdims=True)
                packed = jnp.where(lane == H + 2 * hp + hh, delta, packed)
        s_ref[...] = packed

    row = pl.BlockSpec((tm, D), lambda i: (i, 0))
    vec = pl.BlockSpec((tm, LANES), lambda i: (i, 0))
    return pl.pallas_call(
        body,
        grid=(S // tm,),
        in_specs=[row, row, vec],
        out_specs=vec,
        out_shape=jax.ShapeDtypeStruct((S, LANES), F32),
        compiler_params=_params("parallel"),
        name=name,
    )(do, out, lse_packed)


def _dil_bwd_dq(q, k, v, do, stats, *, branch, name):
    S, D = q.shape
    HP, H = D // LANES, D // HEAD_DIM
    tb = _dil_tb(S)
    nb, nsub = S // tb, tb // CHUNK
    bps = nb // DILATIONS[branch]

    def body(q_ref, k_ref, v_ref, kp_ref, vp_ref, do_ref, st_ref, dq_ref):
        hp = pl.program_id(0)
        n = pl.program_id(1)
        first = jnp.bitwise_and(n, bps - 1) == 0
        qi, kj = _dil_masks()
        hl = lax.broadcasted_iota(jnp.int32, (CHUNK, LANES), 1)
        m_cur = kj <= qi
        m_prev = (kj - qi) >= 0
        m_prev0 = (kj - qi) >= jnp.where(first, 2 * CHUNK, 0)
        hms = (hl < HEAD_DIM, hl >= HEAD_DIM)
        jobs, cols = [], {}
        for u in range(nsub):
            rows = slice(u * CHUNK, (u + 1) * CHUNK)
            qs, kc, vc = q_ref[rows, :], k_ref[rows, :], v_ref[rows, :]
            if u == 0:
                kp, vp, pm = kp_ref[...], vp_ref[...], m_prev0
            else:
                prev = slice((u - 1) * CHUNK, u * CHUNK)
                kp, vp, pm = k_ref[prev, :], v_ref[prev, :], m_prev
            do_t, st_t = do_ref[rows, :], st_ref[rows, :]
            for hh in range(2):
                qh = jnp.where(hms[hh], qs, jnp.zeros_like(qs))
                dob = jnp.where(hms[hh], do_t, jnp.zeros_like(do_t))
                cols[u, hh] = (_lane_col(st_t, hl, H + 2 * hp + hh), _lane_col(st_t, hl, 2 * hp + hh))
                for kk, vv, msk in ((kp, vp, pm), (kc, vc, m_cur)):
                    s = lax.dot_general(qh, kk, NT, preferred_element_type=F32)
                    dp = lax.dot_general(dob, vv, NT, preferred_element_type=F32)
                    jobs.append((u, hh, s, dp, kk, msk))
        dss = []
        for u, hh, s, dp, kk, msk in jobs:
            delta, lse = cols[u, hh]
            p = jnp.exp(jnp.where(msk, s, NEG) - lse)
            dss.append((p * (dp - delta)).astype(MXU))
        dq_h = {}
        for (u, hh, _, _, kk, _), ds in zip(jobs, dss):
            part = jnp.dot(ds, kk, preferred_element_type=F32)
            dq_h[u, hh] = part if (u, hh) not in dq_h else dq_h[u, hh] + part
        for u in range(nsub):
            dq_ref[u * CHUNK : (u + 1) * CHUNK, :] = jnp.where(hms[0], dq_h[u, 0], dq_h[u, 1]).astype(MXU)

    blk = pl.BlockSpec((tb, LANES), lambda h, n: (n, h))
    tail = pl.BlockSpec((CHUNK, LANES), lambda h, n: (jnp.maximum(n * nsub - 1, 0), h))
    return pl.pallas_call(
        body,
        grid=(HP, nb),
        in_specs=[blk, blk, blk, tail, tail, blk, pl.BlockSpec((tb, LANES), lambda h, n: (n, 0))],
        out_specs=blk,
        out_shape=jax.ShapeDtypeStruct((S, D), MXU),
        compiler_params=_params("parallel", "parallel"),
        name=name,
    )(q, k, v, k, v, do, stats)


def _dil_bwd_dkv(q, k, v, do, stats, *, branch, name):
    S, D = q.shape
    HP, H = D // LANES, D // HEAD_DIM
    tb = _dil_tb(S)
    nb, nsub = S // tb, tb // CHUNK
    last_chunk = S // CHUNK - 1
    bps = nb // DILATIONS[branch]

    def body(q_ref, k_ref, v_ref, do_ref, st_ref, qn_ref, don_ref, stn_ref, dk_ref, dv_ref):
        hp = pl.program_id(0)
        n = pl.program_id(1)
        last =jnp.bitwise_and(n + 1, bps - 1) == 0
        qi, kj = _dil_masks()
        hl = lax.broadcasted_iota(jnp.int32, (CHUNK, LANES), 1)
        m_cur = kj <= qi
        m_next = (kj - qi) >= 0
        m_next_last = (kj - qi) >= jnp.where(last, 2 * CHUNK, 0)
        hms = (hl < HEAD_DIM, hl >= HEAD_DIM)
        qtiles = {}
        for t in range(nsub + 1):
            if t == nsub:
                qs, do_t, st_t = qn_ref[...], don_ref[...], stn_ref[...]
            else:
                rows = slice(t * CHUNK, (t + 1) * CHUNK)
                qs, do_t, st_t = q_ref[rows, :], do_ref[rows, :], st_ref[rows, :]
            for hh in range(2):
                qtiles[t, hh] = (
                    jnp.where(hms[hh], qs, jnp.zeros_like(qs)),
                    jnp.where(hms[hh], do_t, jnp.zeros_like(do_t)),
                    _lane_col(st_t, hl, H + 2 * hp + hh),
                    _lane_col(st_t, hl, 2 * hp + hh),
                )
        jobs = []
        for u in range(nsub):
            rows = slice(u * CHUNK, (u + 1) * CHUNK)
            kc, vc = k_ref[rows, :], v_ref[rows, :]
            for t, msk in ((u, m_cur), (u + 1, m_next_last if u == nsub - 1 else m_next)):
                for hh in range(2):
                    qh, dob, _, _ = qtiles[t, hh]
                    s = lax.dot_general(qh, kc, NT, preferred_element_type=F32)
                    dp = lax.dot_general(dob, vc, NT, preferred_element_type=F32)
                    jobs.append((u, t, hh, s, dp, msk))
        pds = []
        for u, t, hh, s, dp, msk in jobs:
            _, _, delta, lse = qtiles[t, hh]
            p = jnp.exp(jnp.where(msk, s, NEG) - lse)
            pds.append((p.astype(MXU), (p * (dp - delta)).astype(MXU)))
        dks, dvs = {}, {}
        for (u, t, hh, _, _, _), (pb, dsb) in zip(jobs, pds):
            qh, dob, _, _ = qtiles[t, hh]
            dv_p = lax.dot_general(pb, dob, TN, preferred_element_type=F32)
            dk_p = lax.dot_general(dsb, qh, TN, preferred_element_type=F32)
            dvs[u] = dv_p if u not in dvs else dvs[u] + dv_p
            dks[u] = dk_p if u not in dks else dks[u] + dk_p
        for u in range(nsub):
            dk_ref[u * CHUNK : (u + 1) * CHUNK, :] = dks[u].astype(MXU)
            dv_ref[u * CHUNK : (u + 1) * CHUNK, :] = dvs[u].astype(MXU)

    blk = pl.BlockSpec((tb, LANES), lambda h, n: (n, h))
    head = pl.BlockSpec((CHUNK, LANES), lambda h, n: (jnp.minimum((n + 1) * nsub, last_chunk), h))
    st_blk = pl.BlockSpec((tb, LANES), lambda h, n: (n, 0))
    st_head = pl.BlockSpec((CHUNK, LANES), lambda h, n: (jnp.minimum((n + 1) * nsub, last_chunk), 0))
    return pl.pallas_call(
        body,
        grid=(HP, nb),
        in_specs=[blk] * 4 + [st_blk, head, head, st_head],
        out_specs=[blk, blk],
        out_shape=[jax.ShapeDtypeStruct((S, D), MXU)] * 2,
        compiler_params=_params("parallel", "parallel"),
        name=name,
    )(q, k, v, do, stats, q, do, stats)


def _chip_peers():
    x, y, c = lax.axis_index("x"), lax.axis_index("y"), lax.axis_index("c")
    peers = [(1 - x, y, c), (x, 1 - y, c), (1 - x, 1 - y, c)]
    return x, y, c, peers


def _allgather_weights(wsh, *, name):
    R, D = wsh.shape
    half = R // 2

    def body(w_ref, out_ref, send_sems, recv_sems, local_sem):
        x, y, c, peers = _chip_peers()
        me, sibling = 2 * x + y, (x, y, 1 - c)

        def rows(chip, hc):
            return out_ref.at[chip, pl.ds(hc * half, half), :]

        def copy(k, chip, hc, to, src=None):
            return pltpu.make_async_remote_copy(
                src_ref=rows(chip, hc) if src is None else src, dst_ref=rows(chip, hc),
                send_sem=send_sems.at[k], recv_sem=recv_sems.at[k], device_id=to, device_id_type=MESH,
            )

        mine = pltpu.make_async_copy(w_ref, out_ref.at[me], local_sem)
        mine.start()
        first = [copy(j, me, c, p, src=w_ref.at[pl.ds(c * half, half), :]) for j, p in enumerate(peers)]
        for cp in first:
            cp.start()
        chips = [2 * p[0] + p[1] for p in peers]
        passed = [copy(3 + j, chip, c, sibling) for j, chip in enumerate(chips)]
        for j, chip in enumerate(chips):
            copy(j, chip, c, peers[j]).wait_recv()
            passed[j].start()
        for j, chip in enumerate(chips):
            copy(3 + j, chip, 1 - c, sibling).wait_recv()
        for cp in first + passed:
            cp.wait_send()
        mine.wait()

    hbm = pl.BlockSpec(memory_space=pl.ANY)
    return pl.pallas_call(
        body,
        in_specs=[hbm],
        out_specs=hbm,
        out_shape=jax.ShapeDtypeStruct((N_CHIPS, R, D), wsh.dtype),
        scratch_shapes=[pltpu.SemaphoreType.DMA((6,)), pltpu.SemaphoreType.DMA((6,)), pltpu.SemaphoreType.DMA],
        name=name,
    )(wsh)


def _scatter_grads(gflat, *, name):
    _, R, D = gflat.shape

    def body(g_ref, out_ref, send_sems, recv_sems):
        x, y, c, peers = _chip_peers()
        sends = [
            pltpu.make_async_remote_copy(
                src_ref=g_ref.at[2 * p[0] + p[1]], dst_ref=out_ref.at[j], send_sem=send_sems.at[j],
                recv_sem=recv_sems.at[j], device_id=p, device_id_type=MESH,
            )
            for j, p in enumerate(peers)
        ]
        for cp in sends:
            cp.start()
        for j, p in enumerate(peers):
            pltpu.make_async_remote_copy(
                src_ref=out_ref.at[j], dst_ref=out_ref.at[j], send_sem=send_sems.at[j], recv_sem=recv_sems.at[j],
                device_id=p, device_id_type=MESH,
            ).wait_recv()
        for cp in sends:
            cp.wait_send()

    hbm = pl.BlockSpec(memory_space=pl.ANY)
    return pl.pallas_call(
        body,
        in_specs=[hbm],
        out_specs=hbm,
        out_shape=jax.ShapeDtypeStruct((3, R, D), gflat.dtype),
        scratch_shapes=[pltpu.SemaphoreType.DMA((3,)), pltpu.SemaphoreType.DMA((3,))],
        name=name,
    )(gflat)


def _swap_with_sibling(part, *, name):
    def body(p_ref, out_ref, send_sem, recv_sem):
        x, y, c = lax.axis_index("x"), lax.axis_index("y"), lax.axis_index("c")
        cp = pltpu.make_async_remote_copy(
            src_ref=p_ref, dst_ref=out_ref, send_sem=send_sem, recv_sem=recv_sem,
            device_id=(x, y, 1 - c), device_id_type=MESH,
        )
        cp.start()
        cp.wait()

    hbm = pl.BlockSpec(memory_space=pl.ANY)
    return pl.pallas_call(
        body,
        in_specs=[hbm],
        out_specs=hbm,
        out_shape=jax.ShapeDtypeStruct(part.shape, part.dtype),
        scratch_shapes=[pltpu.SemaphoreType.DMA, pltpu.SemaphoreType.DMA],
        name=name,
    )(part)


def _gather_from_all(vec, *, name):
    R, D = vec.shape

    def body(v_ref, out_ref, send_sems, recv_sems):
        x, y, c = lax.axis_index("x"), lax.axis_index("y"), lax.axis_index("c")
        me = 4 * x + 2 * y + c
        out_ref[me] = v_ref[...]
        rel = [(j >> 2 & 1, j >> 1 & 1, j & 1) for j in range(1, 8)]
        peers = [((1 - x) if fx else x, (1 - y) if fy else y, (1 - c) if fc else c) for fx, fy, fc in rel]
        sends = [
            pltpu.make_async_remote_copy(
                src_ref=v_ref, dst_ref=out_ref.at[me], send_sem=send_sems.at[j], recv_sem=recv_sems.at[j],
                device_id=p, device_id_type=MESH,
            )
            for j, p in enumerate(peers)
        ]
        for cp in sends:
            cp.start()
        for j, p in enumerate(peers):
            theirs = out_ref.at[4 * p[0] + 2 * p[1] + p[2]]
            pltpu.make_async_remote_copy(
                src_ref=theirs, dst_ref=theirs, send_sem=send_sems.at[j], recv_sem=recv_sems.at[j],
                device_id=p, device_id_type=MESH,
            ).wait_recv()
        for cp in sends:
            cp.wait_send()

    vmem = pl.BlockSpec(memory_space=pltpu.VMEM)
    return pl.pallas_call(
        body,
        in_specs=[vmem],
        out_specs=vmem,
        out_shape=jax.ShapeDtypeStruct((8, R, D), vec.dtype),
        scratch_shapes=[pltpu.SemaphoreType.DMA((7,)), pltpu.SemaphoreType.DMA((7,))],
        name=name,
    )(vec)


def _sum4(a, rest, *, name):
    R, D = a.shape
    tm = _pick(R, (256, 128, 8))

    def body(a_ref, r_ref, o_ref):
        o_ref[...] = ((a_ref[...] + r_ref[0].astype(F32)) + r_ref[1].astype(F32)) + r_ref[2].astype(F32)

    row = pl.BlockSpec((tm, D), lambda i: (i, 0))
    return pl.pallas_call(
        body,
        grid=(R // tm,),
        in_specs=[row, pl.BlockSpec((3, tm, D), lambda i: (0, i, 0))],
        out_specs=row,
        out_shape=jax.ShapeDtypeStruct((R, D), F32),
        compiler_params=_params("parallel"),
        name=name,
    )(a, rest)


def _adamw(parts, w, m, v, *, name):
    P, R, D = parts.shape
    tm = _pick(R, (256, 128, 8))

    def body(p_ref, w_ref, m_ref, v_ref, g_ref, d_ref, nm_ref, nv_ref):
        g = p_ref[0]
        for k in range(1, P):
            g = g + p_ref[k]
        nm = ADAM_B1 * m_ref[...] + (1.0 - ADAM_B1) * g
        nv = ADAM_B2 * v_ref[...] + (1.0 - ADAM_B2) * jnp.square(g)
        m_hat = nm / (1.0 - ADAM_B1**ADAM_STEP)
        v_hat = nv / (1.0 - ADAM_B2**ADAM_STEP)
        g_ref[...] = g
        d_ref[...] = -ADAM_LR * (m_hat / (jnp.sqrt(v_hat) + ADAM_EPS) + ADAM_WD * w_ref[...])
        nm_ref[...] = nm
        nv_ref[...] = nv

    row = pl.BlockSpec((tm, D), lambda i: (i, 0))
    return pl.pallas_call(
        body,
        grid=(R // tm,),
        in_specs=[pl.BlockSpec((P, tm, D), lambda i: (0, i, 0)), row, row, row],
        out_specs=[row] * 4,
        out_shape=[jax.ShapeDtypeStruct((R, D), F32)] * 4,
        compiler_params=_params("parallel"),
        name=name,
    )(parts, w, m, v)


MATS = ("w_qkv", "w_o", "w_ff1", "w_ff2")
VECS = ("ln1_g", "ln1_b", "ln2_g", "ln2_b")
PARAM_ORDER = ("w_qkv", "w_o", "ln1_g", "ln1_b", "w_ff1", "w_ff2", "ln2_g", "ln2_b")


def _pack_rows(arrs, D):
    return jnp.concatenate([a.reshape(-1, D) for a in arrs], axis=0)


def kernel(x, w_qkv_0, w_o_0, ln1_g_0, ln1_b_0, w_ff1_0, w_ff2_0, ln2_g_0, ln2_b_0, w_qkv_1, w_o_1, ln1_g_1, ln1_b_1, w_ff1_1, w_ff2_1, ln2_g_1, ln2_b_1, loss_target, m_w_qkv_0, m_w_o_0, m_ln1_g_0, m_ln1_b_0, m_w_ff1_0, m_w_ff2_0, m_ln2_g_0, m_ln2_b_0, m_w_qkv_1, m_w_o_1, m_ln1_g_1, m_ln1_b_1, m_w_ff1_1, m_w_ff2_1, m_ln2_g_1, m_ln2_b_1, v_w_qkv_0, v_w_o_0, v_ln1_g_0, v_ln1_b_0, v_w_ff1_0, v_w_ff2_0, v_ln2_g_0, v_ln2_b_0, v_w_qkv_1, v_w_o_1, v_ln1_g_1, v_ln1_b_1, v_w_ff1_1, v_w_ff2_1, v_ln2_g_1, v_ln2_b_1):
    given = dict(locals())
    xs = x[0]
    target = loss_target[0]
    S, D = xs.shape
    C3 = 3 * D // N_CHIPS
    names = [f"{p}_{l}" for l in range(DEPTH) for p in PARAM_ORDER]
    mat_names = [f"{p}_{l}" for l in range(DEPTH) for p in MATS]
    vec_names = [f"{p}_{l}" for l in range(DEPTH) for p in VECS]
    rows = {"w_qkv": C3, "w_o": D // N_CHIPS, "w_ff1": D, "w_ff2": D}
    layer_rows = sum(rows.values())

    gathered = _allgather_weights(_pack_rows([given[n] for n in mat_names], D).astype(MXU), name="allgather_weights")
    W = []
    for l in range(DEPTH):
        r0 = l * layer_rows
        seg = {}
        for p in MATS:
            seg[p] = gathered[:, r0 : r0 + rows[p], :]
            r0 += rows[p]
        cat = lambda t, cols: t.reshape(N_CHIPS, D, cols).transpose(1, 0, 2).reshape(D, N_CHIPS * cols)
        W.append(
            dict(
                qkv=cat(seg["w_qkv"], C3),
                o=seg["w_o"].reshape(D, D),
                ff1=cat(seg["w_ff1"], D),
                ff2=seg["w_ff2"].reshape(4 * D, D),
            )
        )
    tables = _rope_tables(S)

    h, hb = xs, xs.astype(MXU)
    saved = []
    for l in range(DEPTH):
        g1, b1, g2, b2 = (given[f"{p}_{l}"] for p in VECS)
        qkv = _mm_nn(hb, W[l]["qkv"], epi="qkv", extras=tables if l == 1 else (), name=f"qkv_{l}")
        st = dict(hb=hb, qkv=qkv)
        if l == 0:
            o, car = _sb_fwd(qkv, name="stickbreak_fwd")
            st.update(car=car)
        else:
            q, k, v = qkv[:, :D], qkv[:, D : 2 * D], qkv[:, 2 * D :]
            qp, kp, vp = ([_perm(t, d) for d in DILATIONS] for t in (q, k, v))
            branches = [
                _dil_fwd(qp[i], kp[i], vp[i], branch=i, name=f"dilated_fwd_d{d}") for i, d in enumerate(DILATIONS)
            ]
            o, lse = _dil_combine(
                [_unperm(ob, d) for (ob, _), d in zip(branches, DILATIONS)],
                [_unperm(lb, d) for (_, lb), d in zip(branches, DILATIONS)],
                name="dilated_combine",
            )
            st.update(qp=qp, kp=kp, vp=vp, lse=lse)
        y1, x1, x1b = _mm_nn(o, W[l]["o"], epi="ln", extras=(h, g1, b1), name=f"attn_out_ln1_{l}")
        hp = _mm_nn(x1b, W[l]["ff1"], out_dtype=MXU, name=f"ff1_{l}")
        y2, x2, x2b = _mm_nn(hp, W[l]["ff2"], a_op="relu2", epi="ln", extras=(x1, g2, b2), name=f"ff2_ln2_{l}")
        st.update(o=o, y1=y1, x1b=x1b, hp=hp, y2=y2)
        saved.append(st)
        h, hb = x2, x2b

    grads = {}
    top = saved[-1]
    dy2, dy2b, dg, db, loss_lanes = _ln_bwd_loss(
        top["y2"], given[f"ln2_g_{DEPTH - 1}"], given[f"ln2_b_{DEPTH - 1}"], target, name="loss_ln2_bwd"
    )
    loss = lax.psum(jnp.sum(loss_lanes) * (0.5 / D), ("x", "y", "c"))
    grads[f"ln2_g_{DEPTH - 1}"], grads[f"ln2_b_{DEPTH - 1}"] = dg, db
    grad_x = None
    for l in reversed(range(DEPTH)):
        st = saved[l]
        dhp = _mm_nt(dy2b, W[l]["ff2"], epi="relu2grad", extra=st["hp"], out_dtype=MXU, name=f"d_ff2_in_{l}")
        grads[f"w_ff2_{l}"] = _mm_tn(st["hp"], dy2b, a_op="relu2", groups=1, name=f"d_w_ff2_{l}")
        dx1 = _mm_nt(dhp, W[l]["ff1"], epi="residual", extra=dy2, out_dtype=F32, name=f"d_ff1_in_{l}")
        grads[f"w_ff1_{l}"] = _mm_tn(st["x1b"], dhp, a_op=None, groups=N_CHIPS, name=f"d_w_ff1_{l}")
        dy1, dy1b, grads[f"ln1_g_{l}"], grads[f"ln1_b_{l}"] = _ln_bwd(
            dx1, st["y1"], given[f"ln1_g_{l}"], name=f"ln1_bwd_{l}"
        )
        grads[f"w_o_{l}"] = _mm_tn(st["o"], dy1b, a_op=None, groups=1, name=f"d_w_o_{l}")
        if l == 0:
            do = _mm_nt(dy1b, W[l]["o"], epi=None, extra=None, out_dtype=MXU, name=f"d_attn_out_{l}")
            dq, dk, dv = _sb_bwd(st["qkv"], do, st["car"], name="stickbreak_bwd")
            dqkv = _qkv_bwd_post([dq], [dk], [dv], None, name=f"qkv_bwd_post_{l}")
        else:
            do = _mm_nt(dy1b, W[l]["o"], epi=None, extra=None, out_dtype=MXU, name=f"d_attn_out_{l}")
            stats = _head_stats(do, st["o"], st["lse"], name="dilated_head_stats")
            dqs, dks, dvs = [], [], []
            for i, d in enumerate(DILATIONS):
                ops = (st["qp"][i], st["kp"][i], st["vp"][i], _perm(do, d), _perm(stats, d))
                dqs.append(_unperm(_dil_bwd_dq(*ops, branch=i, name=f"dilated_bwd_dq_d{d}"), d))
                dk_d, dv_d = _dil_bwd_dkv(*ops, branch=i, name=f"dilated_bwd_dkv_d{d}")
                dks.append(_unperm(dk_d, d))
                dvs.append(_unperm(dv_d, d))
            dqkv = _qkv_bwd_post(dqs, dks, dvs, tables, name=f"qkv_bwd_post_{l}")
        dh = _mm_nt(dqkv, W[l]["qkv"], epi="residual", extra=dy1, out_dtype=F32, name=f"d_qkv_in_{l}")
        grads[f"w_qkv_{l}"] = _mm_tn(st["hb"], dqkv, a_op=None, groups=N_CHIPS, name=f"d_w_qkv_{l}")
        if l > 0:
            prev = saved[l - 1]
            dy2, dy2b, grads[f"ln2_g_{l - 1}"], grads[f"ln2_b_{l - 1}"] = _ln_bwd(
                dh, prev["y2"], given[f"ln2_g_{l - 1}"], name=f"ln2_bwd_{l - 1}"
            )
        else:
            grad_x = dh

    gflat = jnp.concatenate([grads[n].reshape(N_CHIPS, -1, D) for n in mat_names], axis=1)
    me = 2 * lax.axis_index("x") + lax.axis_index("y")
    own = lax.dynamic_index_in_dim(gflat, me, 0, keepdims=False)
    from_chips = _scatter_grads(gflat.astype(MXU), name="scatter_grads")
    part = _sum4(own, from_chips, name="sum_chips")
    sib = _swap_with_sibling(part, name="swap_sibling")
    pack = lambda prefix: _pack_rows([given[prefix + n] for n in mat_names], D)
    outs_m = _adamw(jnp.stack([part, sib]), pack(""), pack("m_"), pack("v_"), name="adamw_matrices")
    vflat = _pack_rows([grads[n] for n in vec_names], D)
    packv = lambda prefix: _pack_rows([given[prefix + n] for n in vec_names], D)
    outs_v = _adamw(_gather_from_all(vflat, name="gather_ln_grads"), packv(""), packv("m_"), packv("v_"), name="adamw_vectors")

    def unpack(flat_m, flat_v):
        out, r0 = {}, 0
        for n in mat_names:
            r = given[n].size // D
            out[n] = flat_m[r0 : r0 + r].reshape(given[n].shape)
            r0 += r
        for i, n in enumerate(vec_names):
            out[n] = flat_v[i]
        return [out[n] for n in names]

    result = [loss, grad_x[None]]
    for k in range(4):
        result += unpack(outs_m[k], outs_v[k])
    return tuple(result)
```

```python
import functools

import jax
import jax.numpy as jnp
from jax import lax
from jax.experimental import pallas as pl
from jax.experimental.pallas import tpu as pltpu

F32 = jnp.float32
MXU = jnp.bfloat16

HEAD_DIM = 64
LANES = 128
CHUNK = 128
ROPE_THETA = 500000.0
ROPE_HALF = 8
DILATIONS = (1, 4, 16)
DEPTH = 2
ALPHA = (2 * DEPTH) ** 0.25
LN_EPS = 1e-5
QK_SCALE = 0.125
ADAM_LR, ADAM_B1, ADAM_B2, ADAM_EPS, ADAM_WD, ADAM_STEP = 0.001, 0.9, 0.999, 1e-08, 0.01, 10
NEG = -1e30
UNDERFLOW = -104.0
SB_TQ = 2 * CHUNK
V7X_VMEM_LIMIT = 56 * 1024 * 1024
MESH = pl.DeviceIdType.MESH
NT = (((1,), (1,)), ((), ()))
TN = (((0,), (0,)), ((), ()))
N_CHIPS = 4
MM_ROWS = (1024, 512, 256, 128)


def _params(*sem):
    return pltpu.CompilerParams(dimension_semantics=sem, vmem_limit_bytes=V7X_VMEM_LIMIT)


def _pick(n, cands=(1024, 768, 512, 384, 256, 128)):
    for c in cands:
        if n % c == 0:
            return c
    raise ValueError(f"no tile for {n}")


def _mm_nn(a, b, *, a_op=None, epi=None, extras=(), out_dtype=F32, name):
    M, K = a.shape
    Kb, N = b.shape
    assert K == Kb
    tn = N // 3 if epi == "qkv" else (N if epi == "ln" else _pick(N))
    tm = _pick(M, MM_ROWS) if (K <= 1024 and epi != "ln") else _pick(M, MM_ROWS[1:])
    rope = epi == "qkv" and len(extras) == 3

    def body(a_ref, b_ref, *refs):
        av = a_ref[...]
        if a_op == "relu2":
            av = jnp.square(jnp.maximum(av.astype(F32), 0.0))
        part = jnp.dot(av.astype(MXU), b_ref[...], preferred_element_type=F32)
        if epi is None:
            refs[0][...] = part.astype(out_dtype)
        elif epi == "qkv":
            o_ref = refs[-1]
            tabs = tuple(r[...] for r in refs[:-1])
            j = pl.program_id(1)

            def write(rotate, scale):
                for blk in range(tn // LANES):
                    cols = slice(blk * LANES, (blk + 1) * LANES)
                    xb = part[:, cols]
                    if rotate:
                        xb = _rope(xb, *tabs)
                    if scale:
                        xb = xb * QK_SCALE
                    o_ref[:, cols] = xb.astype(MXU)

            pl.when(j == 0)(lambda: write(rope, True))
            pl.when(j == 1)(lambda: write(rope, False))
            pl.when(j == 2)(lambda: write(False, False))
        else:
            x_ref, g_ref, be_ref, y_ref, o_ref, ob_ref = refs
            y = ALPHA * x_ref[...] + part
            xhat, _ = _ln_stats(y)
            out = xhat * g_ref[...] + be_ref[...]
            y_ref[...] = y
            o_ref[...] = out
            ob_ref[...] = out.astype(MXU)

    in_specs = [pl.BlockSpec((tm, K), lambda i, j: (i, 0)), pl.BlockSpec((K, tn), lambda i, j: (0, j))]
    tile = pl.BlockSpec((tm, tn), lambda i, j: (i, j))
    ops = [a, b]
    if epi == "qkv":
        in_specs += [pl.BlockSpec((tm, LANES), lambda i, j: (i, 0))] * len(extras)
        ops += list(extras)
        out_specs, out_shape = tile, jax.ShapeDtypeStruct((M, N), MXU)
    elif epi == "ln":
        x, g, be = extras
        in_specs += [tile, pl.BlockSpec((1, N), lambda i, j: (0, 0)), pl.BlockSpec((1, N), lambda i, j: (0, 0))]
        ops += [x, g.reshape(1, N), be.reshape(1, N)]
        out_specs = [tile] * 3
        out_shape = [jax.ShapeDtypeStruct((M, N), F32)] * 2 + [jax.ShapeDtypeStruct((M, N), MXU)]
    else:
        out_specs, out_shape = tile, jax.ShapeDtypeStruct((M, N), out_dtype)
    return pl.pallas_call(
        body,
        grid=(M // tm, N // tn),
        in_specs=in_specs,
        out_specs=out_specs,
        out_shape=out_shape,
        compiler_params=_params("parallel", "parallel"),
        name=name,
    )(*ops)


def _mm_nt(a, b, *, epi, extra, out_dtype, name):
    M, N = a.shape
    K, Nb = b.shape
    assert N == Nb
    tm = _pick(M, MM_ROWS) if N <= 1024 else _pick(M, MM_ROWS[1:])
    tko = _pick(K)

    def body(*refs):
        if epi is None:
            a_ref, b_ref, o_ref = refs
        else:
            a_ref, b_ref, e_ref, o_ref = refs
        r = lax.dot_general(a_ref[...].astype(MXU), b_ref[...], NT, preferred_element_type=F32)
        if epi == "relu2grad":
            r = r * (2.0 * jnp.maximum(e_ref[...].astype(F32), 0.0))
        elif epi == "residual":
            r = ALPHA * e_ref[...] + r
        o_ref[...] = r.astype(out_dtype)

    in_specs = [pl.BlockSpec((tm, N), lambda i, j: (i, 0)), pl.BlockSpec((tko, N), lambda i, j: (j, 0))]
    ops = [a, b]
    if epi is not None:
        in_specs.append(pl.BlockSpec((tm, tko), lambda i, j: (i, j)))
        ops.append(extra)
    return pl.pallas_call(
        body,
        grid=(M // tm, K // tko),
        in_specs=in_specs,
        out_specs=pl.BlockSpec((tm, tko), lambda i, j: (i, j)),
        out_shape=jax.ShapeDtypeStruct((M, K), out_dtype),
        compiler_params=_params("parallel", "parallel"),
        name=name,
    )(*ops)


def _mm_tn(a, b, *, a_op, groups, name):
    M, Ka = a.shape
    Mb, N = b.shape
    assert M == Mb
    Nc = N // groups
    tka, tn, tm = _pick(Ka), _pick(Nc), _pick(M, (2048, 1024, 512, 256, 128))
    nbs, nm = Nc // tn, M // tm

    def body(a_ref, b_ref, o_ref, acc_ref):
        kk = pl.program_id(2)
        av = a_ref[...]
        if a_op == "relu2":
            av = jnp.square(jnp.maximum(av.astype(F32), 0.0))
        part = lax.dot_general(av.astype(MXU), b_ref[...].astype(MXU), TN, preferred_element_type=F32)

        @pl.when(kk == 0)
        def _():
            acc_ref[...] = part

        @pl.when(kk > 0)
        def _():
            acc_ref[...] += part

        @pl.when(kk == nm - 1)
        def _():
            o_ref[...] = acc_ref[...]

    return pl.pallas_call(
        body,
        grid=(Ka // tka, N // tn, nm),
        in_specs=[
            pl.BlockSpec((tm, tka), lambda i, j, k: (k, i)),
            pl.BlockSpec((tm, tn), lambda i, j, k: (k, j)),
        ],
        out_specs=pl.BlockSpec((None, tka, tn), lambda i, j, k: (j // nbs, i, j % nbs)),
        out_shape=jax.ShapeDtypeStruct((groups, Ka, Nc), F32),
        scratch_shapes=[pltpu.VMEM((tka, tn), F32)],
        compiler_params=_params("parallel", "parallel", "arbitrary"),
        name=name,
    )(a, b)


def _ln_stats(y):
    mu = jnp.mean(y, axis=-1, keepdims=True)
    yc = y - mu
    var = jnp.mean(yc * yc, axis=-1, keepdims=True)
    rstd = lax.rsqrt(var + LN_EPS)
    return yc * rstd, rstd


def _ln_bwd(dx, y, g, *, name):
    S, D = y.shape
    tm = _pick(S, (256, 128))

    def body(dx_ref, y_ref, g_ref, dy_ref, dyb_ref, dg_ref, db_ref):
        xhat, rstd = _ln_stats(y_ref[...])
        dx_v = dx_ref[...]
        dxh = dx_v * g_ref[...]
        m1 = jnp.mean(dxh, axis=-1, keepdims=True)
        m2 = jnp.mean(dxh * xhat, axis=-1, keepdims=True)
        dy = rstd * (dxh - m1 - xhat * m2)
        dy_ref[...] = dy
        dyb_ref[...] = dy.astype(MXU)

        @pl.when(pl.program_id(0) == 0)
        def _():
            dg_ref[...] = jnp.zeros_like(dg_ref)
            db_ref[...] = jnp.zeros_like(db_ref)

        dg_ref[...] += jnp.sum(dx_v * xhat, axis=0, keepdims=True)
        db_ref[...] += jnp.sum(dx_v, axis=0, keepdims=True)

    row = pl.BlockSpec((tm, D), lambda i: (i, 0))
    vec = pl.BlockSpec((1, D), lambda i: (0, 0))
    return pl.pallas_call(
        body,
        grid=(S // tm,),
        in_specs=[row, row, vec],
        out_specs=[row, row, vec, vec],
        out_shape=[jax.ShapeDtypeStruct((S, D), F32), jax.ShapeDtypeStruct((S, D), MXU)]
        + [jax.ShapeDtypeStruct((1, D), F32)] * 2,
        compiler_params=_params("arbitrary"),
        name=name,
    )(dx, y, g.reshape(1, D))


def _ln_bwd_loss(y, g, b, target, *, name):
    S, D = y.shape
    tm = _pick(S, (256, 128))

    def body(y_ref, g_ref, b_ref, t_ref, dy_ref, dyb_ref, dg_ref, db_ref, ls_ref):
        xhat, rstd = _ln_stats(y_ref[...])
        err = xhat * g_ref[...] + b_ref[...] - t_ref[...]
        dx_v = err * (1.0 / D)
        dxh = dx_v * g_ref[...]
        m1 = jnp.mean(dxh, axis=-1, keepdims=True)
        m2 = jnp.mean(dxh * xhat, axis=-1, keepdims=True)
        dy = rstd * (dxh - m1 - xhat * m2)
        dy_ref[...] = dy
        dyb_ref[...] = dy.astype(MXU)

        @pl.when(pl.program_id(0) == 0)
        def _():
            dg_ref[...] = jnp.zeros_like(dg_ref)
            db_ref[...] = jnp.zeros_like(db_ref)
            ls_ref[...] = jnp.zeros_like(ls_ref)

        dg_ref[...] += jnp.sum(dx_v * xhat, axis=0, keepdims=True)
        db_ref[...] += jnp.sum(dx_v, axis=0, keepdims=True)
        ls_ref[...] += jnp.sum(err * err, axis=0, keepdims=True)

    row = pl.BlockSpec((tm, D), lambda i: (i, 0))
    vec = pl.BlockSpec((1, D), lambda i: (0, 0))
    return pl.pallas_call(
        body,
        grid=(S // tm,),
        in_specs=[row, vec, vec, row],
        out_specs=[row, row, vec, vec, vec],
        out_shape=[jax.ShapeDtypeStruct((S, D), F32), jax.ShapeDtypeStruct((S, D), MXU)]
        + [jax.ShapeDtypeStruct((1, D), F32)] * 3,
        compiler_params=_params("arbitrary"),
        name=name,
    )(y, g.reshape(1, D), b.reshape(1, D), target)


def _rope_tables(S):
    inv_freq = ROPE_THETA ** (-jnp.arange(ROPE_HALF, dtype=F32) / ROPE_HALF)
    ang = jnp.arange(S, dtype=jnp.int32).astype(F32)[:, None] * inv_freq[None, :]
    cos, sin = jnp.cos(ang), jnp.sin(ang)
    ones = jnp.ones((S, HEAD_DIM - 2 * ROPE_HALF), F32)
    zeros8 = jnp.zeros((S, ROPE_HALF), F32)
    c = jnp.concatenate([cos, cos, ones], axis=1)
    s1 = jnp.concatenate([zeros8, sin, 0.0 * ones], axis=1)
    s2 = jnp.concatenate([-sin, zeros8, 0.0 * ones], axis=1)
    return tuple(jnp.concatenate([t, t], axis=1) for t in (c, s1, s2))


def _rope(xb, c, s1, s2):
    return xb * c + pltpu.roll(xb, ROPE_HALF, 1) * s1 + pltpu.roll(xb, LANES - ROPE_HALF, 1) * s2


def _rope_t(dyb, c, s1, s2):
    return dyb * c + pltpu.roll(dyb * s1, LANES - ROPE_HALF, 1) + pltpu.roll(dyb * s2, ROPE_HALF, 1)


def _qkv_bwd_post(dqs, dks, dvs, tables, *, name):
    S, D = dqs[0].shape
    tm = _pick(S, (256, 128))
    nb = len(dqs)
    rope = tables is not None

    def body(*refs):
        ins, o_ref = refs[:-1], refs[-1]
        groups = [ins[0:nb], ins[nb : 2 * nb], ins[2 * nb : 3 * nb]]
        tabs = [r[...] for r in ins[3 * nb :]]
        for j, grp in enumerate(groups):
            for blk in range(D // LANES):
                cols = slice(blk * LANES, (blk + 1) * LANES)
                v = grp[0][:, cols].astype(F32)
                for r in grp[1:]:
                    v = v + r[:, cols].astype(F32)
                if rope and j < 2:
                    v = _rope_t(v, *tabs)
                if j == 0:
                    v = v * QK_SCALE
                o_ref[:, j * D + blk * LANES : j * D + (blk + 1) * LANES] = v.astype(MXU)

    row = pl.BlockSpec((tm, D), lambda i: (i, 0))
    tab = pl.BlockSpec((tm, LANES), lambda i: (i, 0))
    return pl.pallas_call(
        body,
        grid=(S // tm,),
        in_specs=[row] * (3 * nb) + ([tab] * 3 if rope else []),
        out_specs=pl.BlockSpec((tm, 3 * D), lambda i: (i, 0)),
        out_shape=jax.ShapeDtypeStruct((S, 3 * D), MXU),
        compiler_params=_params("parallel"),
        name=name,
    )(*dqs, *dks, *dvs, *(tables if rope else ()))


def _log_fail(z):
    return -(jnp.maximum(z, 0.0) + jnp.log(1.0 + jnp.exp(-jnp.abs(z))))


def _sb_fwd(qkv, *, gather=None, name):
    S, D = qkv.shape[0], qkv.shape[1] // 3
    HP, H = D // LANES, D // HEAD_DIM
    tq = SB_TQ
    assert S % tq == 0
    cpb = tq // CHUNK
    assert S // CHUNK <= LANES
    steps = HP * (S // tq)

    def body(*refs):
        if gather is None:
            q_ref, k_ref, v_ref, o_ref, car_ref, acc_ref, carry0_ref, carry1_ref = refs
        else:
            q_ref, k_ref, v_ref, w_ref, o_ref, car_ref, g_ref, acc_ref, carry0_ref, carry1_ref = refs[:10]
            start, hand_on, finish = _gather_stages(w_ref, g_ref, *refs[10:])
            t = pl.program_id(0) * (S // tq) + pl.program_id(1)
            pl.when(t == 0)(start)
            pl.when(t == steps // 2)(hand_on)
        i = pl.program_id(1)
        lane = lax.broadcasted_iota(jnp.int32, (tq, LANES), 1)
        row = lax.broadcasted_iota(jnp.int32, (tq, LANES), 0)
        kl = lax.broadcasted_iota(jnp.int32, (CHUNK, LANES), 1)
        kr = lax.broadcasted_iota(jnp.int32, (CHUNK, LANES), 0)
        tri = (kr >= kl).astype(MXU)
        qv = q_ref[...]
        qh = (jnp.where(lane < HEAD_DIM, qv, jnp.zeros_like(qv)), jnp.where(lane >= HEAD_DIM, qv, jnp.zeros_like(qv)))
        carry_refs = (carry0_ref, carry1_ref)
        acc_ref[...] = jnp.zeros_like(acc_ref)
        car_ref[...] = jnp.full((2, tq, LANES), NEG, F32)
        for r in carry_refs:
            r[...] = jnp.zeros_like(r)

        def run(chunks, masked):
            offs = [pl.multiple_of(c * CHUNK, CHUNK) for c in chunks]
            kcs = [k_ref[pl.ds(o, CHUNK), :] for o in offs]
            vcs = [v_ref[pl.ds(o, CHUNK), :] for o in offs]
            jobs = [(n, hh) for n in range(len(chunks)) for hh in range(2)]
            zs = {j: lax.dot_general(qh[j[1]], kcs[j[0]], NT, preferred_element_type=F32) for j in jobs}
            if masked:
                cms = [(o + lane) < (i * tq + row) for o in offs]
            lfs = {}
            for j in jobs:
                lf = _log_fail(zs[j])
                lfs[j] = jnp.where(cms[j[0]], lf, 0.0) if masked else lf
            rs = {j: jnp.dot(lfs[j].astype(MXU), tri, preferred_element_type=F32) for j in jobs}
            ws = {}
            for hh in range(2):
                carry = carry_refs[hh][...]
                tile = car_ref[hh]
                for n, c in enumerate(chunks):
                    w = jnp.exp(zs[n, hh] + rs[n, hh] + carry)
                    if masked:
                        w = jnp.where(cms[n], w, 0.0)
                    ws[n, hh] = w.astype(MXU)
                    tile = jnp.where(lane == c, carry, tile)
                    carry = carry + jnp.sum(lfs[n, hh], axis=-1, keepdims=True)
                car_ref[hh] = tile
                carry_refs[hh][...] = carry
            for hh in range(2):
                pv = jnp.dot(ws[0, hh], vcs[0], preferred_element_type=F32)
                for n in range(1, len(chunks)):
                    pv = pv + jnp.dot(ws[n, hh], vcs[n], preferred_element_type=F32)
                acc_ref[hh] += pv

        def max_carry():
            return jnp.maximum(jnp.max(carry0_ref[...]), jnp.max(carry1_ref[...]))

        run([i * cpb + u for u in reversed(range(cpb))], True)

        def cond(st):
            return jnp.logical_and(st[0] >= 0, st[1] >= UNDERFLOW)

        def pair(st):
            c = st[0]
            run([c, c - 1], False)
            return c - 2, max_carry()

        lax.while_loop(cond, pair, (i * cpb - 1, max_carry()))
        o_ref[...] = jnp.where(lane < HEAD_DIM, acc_ref[0], acc_ref[1]).astype(o_ref.dtype)
        if gather is not None:
            pl.when(t == steps - 1)(finish)

    blk = pl.BlockSpec((tq, LANES), lambda h, i: (i, h))
    k_full = pl.BlockSpec((S, LANES), lambda h, i: (0, HP + h))
    v_full = pl.BlockSpec((S, LANES), lambda h, i: (0, 2 * HP + h))
    hbm = pl.BlockSpec(memory_space=pl.ANY)
    in_specs, ops = [blk, k_full, v_full], [qkv, qkv, qkv]
    out_specs = [blk, pl.BlockSpec((2, tq, LANES), lambda h, i: (h, i, 0))]
    out_shape = [jax.ShapeDtypeStruct((S, D), MXU), jax.ShapeDtypeStruct((H, S, LANES), F32)]
    scratch = [pltpu.VMEM((2, tq, LANES), F32), pltpu.VMEM((tq, 1), F32), pltpu.VMEM((tq, 1), F32)]
    if gather is not None:
        in_specs, ops = in_specs + [hbm], ops + [gather]
        out_specs = out_specs + [hbm]
        out_shape = out_shape + [jax.ShapeDtypeStruct((N_CHIPS,) + gather.shape, gather.dtype)]
        scratch = scratch + GATHER_SEMS
    return pl.pallas_call(
        body,
        grid=(HP, S // tq),
        in_specs=in_specs,
        out_specs=out_specs,
        out_shape=out_shape,
        scratch_shapes=scratch,
        compiler_params=_params("arbitrary", "arbitrary"),
        name=name,
    )(*ops)


def _sb_bwd(qkv, do, car, *, scatter=None, name):
    S, D = qkv.shape[0], qkv.shape[1] // 3
    HP = D // LANES
    tq = SB_TQ
    assert S % tq == 0
    cpb = tq // CHUNK
    nq = S // tq

    def body(*refs):
        if scatter is None:
            q_ref, k_ref, v_ref, do_ref, car_ref, dq_ref, dk_hbm, dv_hbm = refs[:8]
            dq_acc, dk_acc, dv_acc, gcar0_ref, gcar1_ref, sem = refs[8:]
        else:
            q_ref, k_ref, v_ref, do_ref, car_ref, g_ref, dq_ref, dk_hbm, dv_hbm, r_ref = refs[:10]
            dq_acc, dk_acc, dv_acc, gcar0_ref, gcar1_ref, sem = refs[10:16]
            start, finish = _scatter_stages(g_ref, r_ref, *refs[16:])
            t = pl.program_id(0) * nq + pl.program_id(1)
            pl.when(t == 0)(start)
        hp = pl.program_id(0)
        i = pl.program_id(1)
        lane = lax.broadcasted_iota(jnp.int32, (tq, LANES), 1)
        row = lax.broadcasted_iota(jnp.int32, (tq, LANES), 0)
        kl = lax.broadcasted_iota(jnp.int32, (CHUNK, LANES), 1)
        kr = lax.broadcasted_iota(jnp.int32, (CHUNK, LANES), 0)
        tri = (kr >= kl).astype(MXU)
        tri_prefix = (kr <= kl).astype(MXU)

        @pl.when(i == 0)
        def _():
            dk_acc[...] = jnp.zeros_like(dk_acc)
            dv_acc[...] = jnp.zeros_like(dv_acc)

        qv = q_ref[...]
        dov = do_ref[...]
        qh = (jnp.where(lane < HEAD_DIM, qv, jnp.zeros_like(qv)), jnp.where(lane >= HEAD_DIM, qv, jnp.zeros_like(qv)))
        doh = (jnp.where(lane < HEAD_DIM, dov, jnp.zeros_like(dov)), jnp.where(lane >= HEAD_DIM, dov, jnp.zeros_like(dov)))
        gcar_refs = (gcar0_ref, gcar1_ref)
        dq_acc[...] = jnp.zeros_like(dq_acc)
        for r in gcar_refs:
            r[...] = jnp.zeros_like(r)

        def run(chunks, masked):
            offs = [pl.multiple_of(c * CHUNK, CHUNK) for c in chunks]
            kcs = [k_ref[pl.ds(o, CHUNK), :] for o in offs]
            vcs = [v_ref[pl.ds(o, CHUNK), :] for o in offs]
            jobs = [(n, hh) for n in range(len(chunks)) for hh in range(2)]
            if masked:
                cms = [(o + lane) < (i * tq + row) for o in offs]
            zs = {j: lax.dot_general(qh[j[1]], kcs[j[0]], NT, preferred_element_type=F32) for j in jobs}
            dws = {j: lax.dot_general(doh[j[1]], vcs[j[0]], NT, preferred_element_type=F32) for j in jobs}
            lfs, sigs = {}, {}
            for j in jobs:
                lf = _log_fail(zs[j])
                sigs[j] = jnp.exp(zs[j] + lf)
                lfs[j] = jnp.where(cms[j[0]], lf, 0.0) if masked else lf
            rs = {j: jnp.dot(lfs[j].astype(MXU), tri, preferred_element_type=F32) for j in jobs}
            ws, gs = {}, {}
            for hh in range(2):
                tile = car_ref[hh]
                for n, c in enumerate(chunks):
                    carry = jnp.sum(jnp.where(lane == c, tile, 0.0), axis=-1, keepdims=True)
                    w = jnp.exp(zs[n, hh] + rs[n, hh] + carry)
                    if masked:
                        w = jnp.where(cms[n], w, 0.0)
                    ws[n, hh] = w.astype(MXU)
                    gs[n, hh] = w * dws[n, hh]
            big_gs = {j: jnp.dot(gs[j].astype(MXU), tri_prefix, preferred_element_type=F32) for j in jobs}
            dzs = {}
            for hh in range(2):
                gcar = gcar_refs[hh][...]
                for n in range(len(chunks)):
                    dz = gs[n, hh] - sigs[n, hh] * (big_gs[n, hh] + gcar)
                    if masked:
                        dz = jnp.where(cms[n], dz, 0.0)
                    dzs[n, hh] = dz.astype(MXU)
                    gcar = gcar + jnp.sum(gs[n, hh], axis=-1, keepdims=True)
                gcar_refs[hh][...] = gcar
            for hh in range(2):
                part = jnp.dot(dzs[0, hh], kcs[0], preferred_element_type=F32)
                for n in range(1, len(chunks)):
                    part = part + jnp.dot(dzs[n, hh], kcs[n], preferred_element_type=F32)
                dq_acc[hh] += part
            for n, o in enumerate(offs):
                dk_acc[pl.ds(o, CHUNK), :] += lax.dot_general(
                    dzs[n, 0], qh[0], TN, preferred_element_type=F32
                ) + lax.dot_general(dzs[n, 1], qh[1], TN, preferred_element_type=F32)
                dv_acc[pl.ds(o, CHUNK), :] += lax.dot_general(
                    ws[n, 0], doh[0], TN, preferred_element_type=F32
                ) + lax.dot_general(ws[n, 1], doh[1], TN, preferred_element_type=F32)

        colmax = jnp.maximum(jnp.max(car_ref[0], axis=0, keepdims=True), jnp.max(car_ref[1], axis=0, keepdims=True))
        lane1 = lax.broadcasted_iota(jnp.int32, (1, LANES), 1)
        dead = jnp.logical_and(colmax < UNDERFLOW, lane1 < i * cpb)
        first_pair = jnp.sum(jnp.where(dead, 1.0, 0.0)).astype(jnp.int32) // 2

        def loop_body(p, carry):
            run([2 * p, 2 * p + 1], False)
            return carry

        lax.fori_loop(first_pair, i * cpb // 2, loop_body, 0)
        run([i * cpb + u for u in range(cpb)], True)
        dq_ref[...] = jnp.where(lane < HEAD_DIM, dq_acc[0], dq_acc[1])

        @pl.when(i == nq - 1)
        def _():
            cols = pl.ds(pl.multiple_of(hp * LANES, LANES), LANES)
            for src, dst in ((dk_acc, dk_hbm), (dv_acc, dv_hbm)):
                cp = pltpu.make_async_copy(src, dst.at[:, cols], sem)
                cp.start()
                cp.wait()

        if scatter is not None:
            pl.when(t == HP * nq - 1)(finish)

    blk = pl.BlockSpec((tq, LANES), lambda h, i: (i, h))
    k_full = pl.BlockSpec((S, LANES), lambda h, i: (0, HP + h))
    v_full = pl.BlockSpec((S, LANES), lambda h, i: (0, 2 * HP + h))
    hbm = pl.BlockSpec(memory_space=pl.ANY)
    in_specs = [blk, k_full, v_full, blk, pl.BlockSpec((2, tq, LANES), lambda h, i: (h, i, 0))]
    ops = [qkv, qkv, qkv, do, car]
    out_specs, out_shape = [blk, hbm, hbm], [jax.ShapeDtypeStruct((S, D), F32)] * 3
    scratch = [
        pltpu.VMEM((2, tq, LANES), F32),
        pltpu.VMEM((S, LANES), F32),
        pltpu.VMEM((S, LANES), F32),
        pltpu.VMEM((tq, 1), F32),
        pltpu.VMEM((tq, 1), F32),
        pltpu.SemaphoreType.DMA,
    ]
    if scatter is not None:
        in_specs, ops = in_specs + [hbm], ops + [scatter]
        out_specs = out_specs + [hbm]
        out_shape = out_shape + [jax.ShapeDtypeStruct((3,) + scatter.shape[1:], scatter.dtype)]
        scratch = scratch + SCATTER_SEMS
    return pl.pallas_call(
        body,
        grid=(HP, nq),
        in_specs=in_specs,
        out_specs=out_specs,
        out_shape=out_shape,
        scratch_shapes=scratch,
        compiler_params=_params("arbitrary", "arbitrary"),
        name=name,
    )(*ops)


def _perm(x, d):
    if d == 1:
        return x
    S, D = x.shape
    return x.reshape(S // d, d, D).transpose(1, 0, 2).reshape(S, D)


def _unperm(x, d):
    if d == 1:
        return x
    S, D = x.shape
    return x.reshape(d, S // d, D).transpose(1, 0, 2).reshape(S, D)


def _dil_tb(S):
    return min(1024, S // DILATIONS[-1])


def _dil_masks():
    qi = lax.broadcasted_iota(jnp.int32, (CHUNK, CHUNK), 0)
    kj = lax.broadcasted_iota(jnp.int32, (CHUNK, CHUNK), 1)
    return qi, kj


def _dil_fwd(q, k, v, *, branch, name):
    S, D = q.shape
    HP = D // LANES
    tb = _dil_tb(S)
    nb, nsub = S // tb, tb // CHUNK
    bps = nb // DILATIONS[branch]

    def body(q_ref, k_ref, v_ref, kp_ref, vp_ref, o_ref, l_ref):
        n = pl.program_id(1)
        first = jnp.bitwise_and(n, bps - 1) == 0
        qi, kj = _dil_masks()
        hl = lax.broadcasted_iota(jnp.int32, (CHUNK, LANES), 1)
        m_cur = kj <= qi
        m_prev = (kj - qi) >= 0
        m_prev0 = (kj - qi) >= jnp.where(first, 2 * CHUNK, 0)
        hms = (hl < HEAD_DIM, hl >= HEAD_DIM)
        tiles, scores = [], {}
        for u in range(nsub):
            rows = slice(u * CHUNK, (u + 1) * CHUNK)
            qs, kc, vc = q_ref[rows, :], k_ref[rows, :], v_ref[rows, :]
            if u == 0:
                kp, vp, pm = kp_ref[...], vp_ref[...], m_prev0
            else:
                prev = slice((u - 1) * CHUNK, u * CHUNK)
                kp, vp, pm = k_ref[prev, :], v_ref[prev, :], m_prev
            tiles.append((rows, vp, vc, pm))
            for hh in range(2):
                qh = jnp.where(hms[hh], qs, jnp.zeros_like(qs))
                scores[u, hh] = (
                    lax.dot_general(qh, kp, NT, preferred_element_type=F32),
                    lax.dot_general(qh, kc, NT, preferred_element_type=F32),
                )
        probs = {}
        for (u, hh), (s_p, s_c) in scores.items():
            s_p = jnp.where(tiles[u][3], s_p, NEG)
            s_c = jnp.where(m_cur, s_c, NEG)
            m = jnp.max(jnp.maximum(s_p, s_c), axis=-1, keepdims=True)
            p_p, p_c = jnp.exp(s_p - m), jnp.exp(s_c - m)
            den = jnp.sum(p_p + p_c, axis=-1, keepdims=True)
            probs[u, hh] = (p_p.astype(MXU), p_c.astype(MXU), 1.0 / den, m + jnp.log(den))
        for u, (rows, vp, vc, _) in enumerate(tiles):
            o_t = l_t = None
            for hh in range(2):
                p_p, p_c, inv, lse = probs[u, hh]
                num = jnp.dot(p_p, vp, preferred_element_type=F32) + jnp.dot(p_c, vc, preferred_element_type=F32)
                o_h, l_h = num * inv, jnp.broadcast_to(lse, (CHUNK, LANES))
                o_t = o_h if hh == 0 else jnp.where(hms[1], o_h, o_t)
                l_t = l_h if hh == 0 else jnp.where(hms[1], l_h, l_t)
            o_ref[rows, :] = o_t
            l_ref[rows, :] = l_t

    blk = pl.BlockSpec((tb, LANES), lambda h, n: (n, h))
    tail = pl.BlockSpec((CHUNK, LANES), lambda h, n: (jnp.maximum(n * nsub - 1, 0), h))
    return pl.pallas_call(
        body,
        grid=(HP, nb),
        in_specs=[blk, blk, blk, tail, tail],
        out_specs=[blk, blk],
        out_shape=[jax.ShapeDtypeStruct((S, D), F32)] * 2,
        compiler_params=_params("parallel", "parallel"),
        name=name,
    )(q, k, v, k, v)


def _dil_combine(os, ls, *, name):
    S, D = os[0].shape
    tm = _pick(S, (256, 128))

    def body(o0_ref, o1_ref, o2_ref, l0_ref, l1_ref, l2_ref, out_ref, lse_ref):
        l0, l1, l2 = l0_ref[...], l1_ref[...], l2_ref[...]
        m = jnp.maximum(jnp.maximum(l0, l1), l2)
        a0, a1, a2 = jnp.exp(l0 - m), jnp.exp(l1 - m), jnp.exp(l2 - m)
        den = a0 + a1 + a2
        out_ref[...] = (a0 * o0_ref[...] + a1 * o1_ref[...] + a2 * o2_ref[...]) / den
        lse = m + jnp.log(den)
        lane = lax.broadcasted_iota(jnp.int32, (tm, LANES), 1)
        packed = jnp.zeros((tm, LANES), F32)
        for hp in range(D // LANES):
            blk = lse[:, hp * LANES : (hp + 1) * LANES]
            for hh in range(2):
                packed = jnp.where(lane == 2 * hp + hh, _lane_col(blk, lane, hh * HEAD_DIM), packed)
        lse_ref[...] = packed

    row = pl.BlockSpec((tm, D), lambda i: (i, 0))
    return pl.pallas_call(
        body,
        grid=(S // tm,),
        in_specs=[row] * 6,
        out_specs=[row, pl.BlockSpec((tm, LANES), lambda i: (i, 0))],
        out_shape=[jax.ShapeDtypeStruct((S, D), F32), jax.ShapeDtypeStruct((S, LANES), F32)],
        compiler_params=_params("parallel"),
        name=name,
    )(*os, *ls)


def _lane_col(tile, lane, at):
    return jnp.sum(jnp.where(lane == at, tile, 0.0), axis=-1, keepdims=True)


def _head_stats(do, out, lse_packed, *, name):
    S, D = out.shape
    H = D // HEAD_DIM
    assert 2 * H <= LANES
    tm = _pick(S, (256, 128))

    def body(do_ref, out_ref, l_ref, s_ref):
        lane = lax.broadcasted_iota(jnp.int32, (tm, LANES), 1)
        packed = l_ref[...]
        for hp in range(D // LANES):
            cols = slice(hp * LANES, (hp + 1) * LANES)
            prod = do_ref[:, cols].astype(F32) * out_ref[:, cols]
            for hh in range(2):
                head = (lane >= HEAD_DIM) if hh else (lane < HEAD_DIM)
                delta = jnp.sum(jnp.where(head, prod, 0.0), axis=-1, keepdims=True)
                packed = jnp.where(lane == H + 2 * hp + hh, delta, packed)
        s_ref[...] = packed

    row = pl.BlockSpec((tm, D), lambda i: (i, 0))
    vec = pl.BlockSpec((tm, LANES), lambda i: (i, 0))
    return pl.pallas_call(
        body,
        grid=(S // tm,),
        in_specs=[row, row, vec],
        out_specs=vec,
        out_shape=jax.ShapeDtypeStruct((S, LANES), F32),
        compiler_params=_params("parallel"),
        name=name,
    )(do, out, lse_packed)


def _dil_bwd_dq(q, k, v, do, stats, *, branch, name):
    S, D = q.shape
    HP, H = D // LANES, D // HEAD_DIM
    tb = _dil_tb(S)
    nb, nsub = S // tb, tb // CHUNK
    bps = nb // DILATIONS[branch]

    def body(q_ref, k_ref, v_ref, kp_ref, vp_ref, do_ref, st_ref, dq_ref):
        hp = pl.program_id(0)
        n = pl.program_id(1)
        first = jnp.bitwise_and(n, bps - 1) == 0
        qi, kj = _dil_masks()
        hl = lax.broadcasted_iota(jnp.int32, (CHUNK, LANES), 1)
        m_cur = kj <= qi
        m_prev = (kj - qi) >= 0
        m_prev0 = (kj - qi) >= jnp.where(first, 2 * CHUNK, 0)
        hms = (hl < HEAD_DIM, hl >= HEAD_DIM)
        jobs, cols = [], {}
        for u in range(nsub):
            rows = slice(u * CHUNK, (u + 1) * CHUNK)
            qs, kc, vc = q_ref[rows, :], k_ref[rows, :], v_ref[rows, :]
            if u == 0:
                kp, vp, pm = kp_ref[...], vp_ref[...], m_prev0
            else:
                prev = slice((u - 1) * CHUNK, u * CHUNK)
                kp, vp, pm = k_ref[prev, :], v_ref[prev, :], m_prev
            do_t, st_t = do_ref[rows, :], st_ref[rows, :]
            for hh in range(2):
                qh = jnp.where(hms[hh], qs, jnp.zeros_like(qs))
                dob = jnp.where(hms[hh], do_t, jnp.zeros_like(do_t))
                cols[u, hh] = (_lane_col(st_t, hl, H + 2 * hp + hh), _lane_col(st_t, hl, 2 * hp + hh))
                for kk, vv, msk in ((kp, vp, pm), (kc, vc, m_cur)):
                    s = lax.dot_general(qh, kk, NT, preferred_element_type=F32)
                    dp = lax.dot_general(dob, vv, NT, preferred_element_type=F32)
                    jobs.append((u, hh, s, dp, kk, msk))
        dss = []
        for u, hh, s, dp, kk, msk in jobs:
            delta, lse = cols[u, hh]
            p = jnp.exp(jnp.where(msk, s, NEG) - lse)
            dss.append((p * (dp - delta)).astype(MXU))
        dq_h = {}
        for (u, hh, _, _, kk, _), ds in zip(jobs, dss):
            part = jnp.dot(ds, kk, preferred_element_type=F32)
            dq_h[u, hh] = part if (u, hh) not in dq_h else dq_h[u, hh] + part
        for u in range(nsub):
            dq_ref[u * CHUNK : (u + 1) * CHUNK, :] = jnp.where(hms[0], dq_h[u, 0], dq_h[u, 1]).astype(MXU)

    blk = pl.BlockSpec((tb, LANES), lambda h, n: (n, h))
    tail = pl.BlockSpec((CHUNK, LANES), lambda h, n: (jnp.maximum(n * nsub - 1, 0), h))
    return pl.pallas_call(
        body,
        grid=(HP, nb),
        in_specs=[blk, blk, blk, tail, tail, blk, pl.BlockSpec((tb, LANES), lambda h, n: (n, 0))],
        out_specs=blk,
        out_shape=jax.ShapeDtypeStruct((S, D), MXU),
        compiler_params=_params("parallel", "parallel"),
        name=name,
    )(q, k, v, k, v, do, stats)


def _dil_bwd_dkv(q, k, v, do, stats, *, branch, name):
    S, D = q.shape
    HP, H = D // LANES, D // HEAD_DIM
    tb = _dil_tb(S)
    nb, nsub = S // tb, tb // CHUNK
    last_chunk = S // CHUNK - 1
    bps = nb // DILATIONS[branch]

    def body(q_ref, k_ref, v_ref, do_ref, st_ref, qn_ref, don_ref, stn_ref, dk_ref, dv_ref):
        hp = pl.program_id(0)
        n = pl.program_id(1)
        last =jnp.bitwise_and(n + 1, bps - 1) == 0
        qi, kj = _dil_masks()
        hl = lax.broadcasted_iota(jnp.int32, (CHUNK, LANES), 1)
        m_cur = kj <= qi
        m_next = (kj - qi) >= 0
        m_next_last = (kj - qi) >= jnp.where(last, 2 * CHUNK, 0)
        hms = (hl < HEAD_DIM, hl >= HEAD_DIM)
        qtiles = {}
        for t in range(nsub + 1):
            if t == nsub:
                qs, do_t, st_t = qn_ref[...], don_ref[...], stn_ref[...]
            else:
                rows = slice(t * CHUNK, (t + 1) * CHUNK)
                qs, do_t, st_t = q_ref[rows, :], do_ref[rows, :], st_ref[rows, :]
            for hh in range(2):
                qtiles[t, hh] = (
                    jnp.where(hms[hh], qs, jnp.zeros_like(qs)),
                    jnp.where(hms[hh], do_t, jnp.zeros_like(do_t)),
                    _lane_col(st_t, hl, H + 2 * hp + hh),
                    _lane_col(st_t, hl, 2 * hp + hh),
                )
        jobs = []
        for u in range(nsub):
            rows = slice(u * CHUNK, (u + 1) * CHUNK)
            kc, vc = k_ref[rows, :], v_ref[rows, :]
            for t, msk in ((u, m_cur), (u + 1, m_next_last if u == nsub - 1 else m_next)):
                for hh in range(2):
                    qh, dob, _, _ = qtiles[t, hh]
                    s = lax.dot_general(qh, kc, NT, preferred_element_type=F32)
                    dp = lax.dot_general(dob, vc, NT, preferred_element_type=F32)
                    jobs.append((u, t, hh, s, dp, msk))
        pds = []
        for u, t, hh, s, dp, msk in jobs:
            _, _, delta, lse = qtiles[t, hh]
            p = jnp.exp(jnp.where(msk, s, NEG) - lse)
            pds.append((p.astype(MXU), (p * (dp - delta)).astype(MXU)))
        dks, dvs = {}, {}
        for (u, t, hh, _, _, _), (pb, dsb) in zip(jobs, pds):
            qh, dob, _, _ = qtiles[t, hh]
            dv_p = lax.dot_general(pb, dob, TN, preferred_element_type=F32)
            dk_p = lax.dot_general(dsb, qh, TN, preferred_element_type=F32)
            dvs[u] = dv_p if u not in dvs else dvs[u] + dv_p
            dks[u] = dk_p if u not in dks else dks[u] + dk_p
        for u in range(nsub):
            dk_ref[u * CHUNK : (u + 1) * CHUNK, :] = dks[u].astype(MXU)
            dv_ref[u * CHUNK : (u + 1) * CHUNK, :] = dvs[u].astype(MXU)

    blk = pl.BlockSpec((tb, LANES), lambda h, n: (n, h))
    head = pl.BlockSpec((CHUNK, LANES), lambda h, n: (jnp.minimum((n + 1) * nsub, last_chunk), h))
    st_blk = pl.BlockSpec((tb, LANES), lambda h, n: (n, 0))
    st_head = pl.BlockSpec((CHUNK, LANES), lambda h, n: (jnp.minimum((n + 1) * nsub, last_chunk), 0))
    return pl.pallas_call(
        body,
        grid=(HP, nb),
        in_specs=[blk] * 4 + [st_blk, head, head, st_head],
        out_specs=[blk, blk],
        out_shape=[jax.ShapeDtypeStruct((S, D), MXU)] * 2,
        compiler_params=_params("parallel", "parallel"),
        name=name,
    )(q, k, v, do, stats, q, do, stats)


def _chip_peers():
    x, y, c = lax.axis_index("x"), lax.axis_index("y"), lax.axis_index("c")
    peers = [(1 - x, y, c), (x, 1 - y, c), (1 - x, 1 - y, c)]
    return x, y, c, peers


def _allgather_weights(wsh, *, name):
    R, D = wsh.shape

    def body(w_ref, out_ref, send_sems, recv_sems, local_sem):
        for stage in _gather_stages(w_ref, out_ref, send_sems, recv_sems, local_sem):
            stage()

    hbm = pl.BlockSpec(memory_space=pl.ANY)
    return pl.pallas_call(
        body,
        in_specs=[hbm],
        out_specs=hbm,
        out_shape=jax.ShapeDtypeStruct((N_CHIPS, R, D), wsh.dtype),
        scratch_shapes=GATHER_SEMS,
        name=name,
    )(wsh)


GATHER_SEMS = [pltpu.SemaphoreType.DMA((6,)), pltpu.SemaphoreType.DMA((6,)), pltpu.SemaphoreType.DMA]
SCATTER_SEMS = [pltpu.SemaphoreType.DMA((3,)), pltpu.SemaphoreType.DMA((3,))]


def _gather_stages(w_ref, out_ref, send_sems, recv_sems, local_sem):
    half = w_ref.shape[0] // 2
    x, y, c, peers = _chip_peers()
    me, sibling = 2 * x + y, (x, y, 1 - c)
    chips = [2 * p[0] + p[1] for p in peers]

    def rows(chip, hc):
        return out_ref.at[chip, pl.ds(hc * half, half), :]

    def copy(k, chip, hc, to, src=None):
        return pltpu.make_async_remote_copy(
            src_ref=rows(chip, hc) if src is None else src, dst_ref=rows(chip, hc),
            send_sem=send_sems.at[k], recv_sem=recv_sems.at[k], device_id=to, device_id_type=MESH,
        )

    mine = pltpu.make_async_copy(w_ref, out_ref.at[me], local_sem)
    first = [copy(j, me, c, p, src=w_ref.at[pl.ds(c * half, half), :]) for j, p in enumerate(peers)]
    passed = [copy(3 + j, chip, c, sibling) for j, chip in enumerate(chips)]

    def start():
        mine.start()
        for cp in first:
            cp.start()

    def hand_on():
        for j, chip in enumerate(chips):
            copy(j, chip, c, peers[j]).wait_recv()
            passed[j].start()

    def finish():
        for j, chip in enumerate(chips):
            copy(3 + j, chip, 1 - c, sibling).wait_recv()
        for cp in first + passed:
            cp.wait_send()
        mine.wait()

    return start, hand_on, finish


def _scatter_stages(g_ref, out_ref, send_sems, recv_sems):
    x, y, c, peers = _chip_peers()
    sends = [
        pltpu.make_async_remote_copy(
            src_ref=g_ref.at[2 * p[0] + p[1]], dst_ref=out_ref.at[j], send_sem=send_sems.at[j],
            recv_sem=recv_sems.at[j], device_id=p, device_id_type=MESH,
        )
        for j, p in enumerate(peers)
    ]

    def start():
        for cp in sends:
            cp.start()

    def finish():
        for j, p in enumerate(peers):
            pltpu.make_async_remote_copy(
                src_ref=out_ref.at[j], dst_ref=out_ref.at[j], send_sem=send_sems.at[j], recv_sem=recv_sems.at[j],
                device_id=p, device_id_type=MESH,
            ).wait_recv()
        for cp in sends:
            cp.wait_send()

    return start, finish


def _scatter_grads(gflat, *, name):
    _, R, D = gflat.shape

    def body(g_ref, out_ref, send_sems, recv_sems):
        for stage in _scatter_stages(g_ref, out_ref, send_sems, recv_sems):
            stage()

    hbm = pl.BlockSpec(memory_space=pl.ANY)
    return pl.pallas_call(
        body,
        in_specs=[hbm],
        out_specs=hbm,
        out_shape=jax.ShapeDtypeStruct((3, R, D), gflat.dtype),
        scratch_shapes=SCATTER_SEMS,
        name=name,
    )(gflat)


def _swap_with_sibling(part, *, name):
    def body(p_ref, out_ref, send_sem, recv_sem):
        x, y, c = lax.axis_index("x"), lax.axis_index("y"), lax.axis_index("c")
        cp = pltpu.make_async_remote_copy(
            src_ref=p_ref, dst_ref=out_ref, send_sem=send_sem, recv_sem=recv_sem,
            device_id=(x, y, 1 - c), device_id_type=MESH,
        )
        cp.start()
        cp.wait()

    hbm = pl.BlockSpec(memory_space=pl.ANY)
    return pl.pallas_call(
        body,
        in_specs=[hbm],
        out_specs=hbm,
        out_shape=jax.ShapeDtypeStruct(part.shape, part.dtype),
        scratch_shapes=[pltpu.SemaphoreType.DMA, pltpu.SemaphoreType.DMA],
        name=name,
    )(part)


def _gather_from_all(vec, *, name):
    R, D = vec.shape

    def body(v_ref, out_ref, send_sems, recv_sems):
        x, y, c = lax.axis_index("x"), lax.axis_index("y"), lax.axis_index("c")
        me = 4 * x + 2 * y + c
        out_ref[me] = v_ref[...]
        rel = [(j >> 2 & 1, j >> 1 & 1, j & 1) for j in range(1, 8)]
        peers = [((1 - x) if fx else x, (1 - y) if fy else y, (1 - c) if fc else c) for fx, fy, fc in rel]
        sends = [
            pltpu.make_async_remote_copy(
                src_ref=v_ref, dst_ref=out_ref.at[me], send_sem=send_sems.at[j], recv_sem=recv_sems.at[j],
                device_id=p, device_id_type=MESH,
            )
            for j, p in enumerate(peers)
        ]
        for cp in sends:
            cp.start()
        for j, p in enumerate(peers):
            theirs = out_ref.at[4 * p[0] + 2 * p[1] + p[2]]
            pltpu.make_async_remote_copy(
                src_ref=theirs, dst_ref=theirs, send_sem=send_sems.at[j], recv_sem=recv_sems.at[j],
                device_id=p, device_id_type=MESH,
            ).wait_recv()
        for cp in sends:
            cp.wait_send()

    vmem = pl.BlockSpec(memory_space=pltpu.VMEM)
    return pl.pallas_call(
        body,
        in_specs=[vmem],
        out_specs=vmem,
        out_shape=jax.ShapeDtypeStruct((8, R, D), vec.dtype),
        scratch_shapes=[pltpu.SemaphoreType.DMA((7,)), pltpu.SemaphoreType.DMA((7,))],
        name=name,
    )(vec)


def _sum4(a, rest, *, name):
    R, D = a.shape
    tm = _pick(R, (256, 128, 8))

    def body(a_ref, r_ref, o_ref):
        o_ref[...] = ((a_ref[...] + r_ref[0].astype(F32)) + r_ref[1].astype(F32)) + r_ref[2].astype(F32)

    row = pl.BlockSpec((tm, D), lambda i: (i, 0))
    return pl.pallas_call(
        body,
        grid=(R // tm,),
        in_specs=[row, pl.BlockSpec((3, tm, D), lambda i: (0, i, 0))],
        out_specs=row,
        out_shape=jax.ShapeDtypeStruct((R, D), F32),
        compiler_params=_params("parallel"),
        name=name,
    )(a, rest)


def _adamw(parts, w, m, v, *, name):
    P, R, D = parts.shape
    tm = _pick(R, (256, 128, 8))

    def body(p_ref, w_ref, m_ref, v_ref, g_ref, d_ref, nm_ref, nv_ref):
        g = p_ref[0]
        for k in range(1, P):
            g = g + p_ref[k]
        nm = ADAM_B1 * m_ref[...] + (1.0 - ADAM_B1) * g
        nv = ADAM_B2 * v_ref[...] + (1.0 - ADAM_B2) * jnp.square(g)
        m_hat = nm / (1.0 - ADAM_B1**ADAM_STEP)
        v_hat = nv / (1.0 - ADAM_B2**ADAM_STEP)
        g_ref[...] = g
        d_ref[...] = -ADAM_LR * (m_hat / (jnp.sqrt(v_hat) + ADAM_EPS) + ADAM_WD * w_ref[...])
        nm_ref[...] = nm
        nv_ref[...] = nv

    row = pl.BlockSpec((tm, D), lambda i: (i, 0))
    return pl.pallas_call(
        body,
        grid=(R // tm,),
        in_specs=[pl.BlockSpec((P, tm, D), lambda i: (0, i, 0)), row, row, row],
        out_specs=[row] * 4,
        out_shape=[jax.ShapeDtypeStruct((R, D), F32)] * 4,
        compiler_params=_params("parallel"),
        name=name,
    )(parts, w, m, v)


MATS = ("w_qkv", "w_o", "w_ff1", "w_ff2")
VECS = ("ln1_g", "ln1_b", "ln2_g", "ln2_b")
PARAM_ORDER = ("w_qkv", "w_o", "ln1_g", "ln1_b", "w_ff1", "w_ff2", "ln2_g", "ln2_b")


def _pack_rows(arrs, D):
    return jnp.concatenate([a.reshape(-1, D) for a in arrs], axis=0)


def kernel(x, w_qkv_0, w_o_0, ln1_g_0, ln1_b_0, w_ff1_0, w_ff2_0, ln2_g_0, ln2_b_0, w_qkv_1, w_o_1, ln1_g_1, ln1_b_1, w_ff1_1, w_ff2_1, ln2_g_1, ln2_b_1, loss_target, m_w_qkv_0, m_w_o_0, m_ln1_g_0, m_ln1_b_0, m_w_ff1_0, m_w_ff2_0, m_ln2_g_0, m_ln2_b_0, m_w_qkv_1, m_w_o_1, m_ln1_g_1, m_ln1_b_1, m_w_ff1_1, m_w_ff2_1, m_ln2_g_1, m_ln2_b_1, v_w_qkv_0, v_w_o_0, v_ln1_g_0, v_ln1_b_0, v_w_ff1_0, v_w_ff2_0, v_ln2_g_0, v_ln2_b_0, v_w_qkv_1, v_w_o_1, v_ln1_g_1, v_ln1_b_1, v_w_ff1_1, v_w_ff2_1, v_ln2_g_1, v_ln2_b_1):
    given = dict(locals())
    xs = x[0]
    target = loss_target[0]
    S, D = xs.shape
    C3 = 3 * D // N_CHIPS
    names = [f"{p}_{l}" for l in range(DEPTH) for p in PARAM_ORDER]
    mat_names = [f"{p}_{l}" for l in range(DEPTH) for p in MATS]
    vec_names = [f"{p}_{l}" for l in range(DEPTH) for p in VECS]
    rows = {"w_qkv": C3, "w_o": D // N_CHIPS, "w_ff1": D, "w_ff2": D}
    layer_rows = sum(rows.values())

    shard = lambda l: _pack_rows([given[f"{p}_{l}"] for p in MATS], D).astype(MXU)

    def unpack_weights(gathered):
        r0, seg = 0, {}
        for p in MATS:
            seg[p] = gathered[:, r0 : r0 + rows[p], :]
            r0 += rows[p]
        cat = lambda t, cols: t.reshape(N_CHIPS, D, cols).transpose(1, 0, 2).reshape(D, N_CHIPS * cols)
        return dict(
            qkv=cat(seg["w_qkv"], C3),
            o=seg["w_o"].reshape(D, D),
            ff1=cat(seg["w_ff1"], D),
            ff2=seg["w_ff2"].reshape(4 * D, D),
        )

    W = [unpack_weights(_allgather_weights(shard(0), name="allgather_weights_0"))]
    tables = _rope_tables(S)

    h, hb = xs, xs.astype(MXU)
    saved = []
    for l in range(DEPTH):
        g1, b1, g2, b2 = (given[f"{p}_{l}"] for p in VECS)
        qkv = _mm_nn(hb, W[l]["qkv"], epi="qkv", extras=tables if l == 1 else (), name=f"qkv_{l}")
        st = dict(hb=hb, qkv=qkv)
        if l == 0:
            o, car, gathered_1 = _sb_fwd(qkv, gather=shard(1), name="stickbreak_fwd")
            W.append(unpack_weights(gathered_1))
            st.update(car=car)
        else:
            q, k, v = qkv[:, :D], qkv[:, D : 2 * D], qkv[:, 2 * D :]
            qp, kp, vp = ([_perm(t, d) for d in DILATIONS] for t in (q, k, v))
            branches = [
                _dil_fwd(qp[i], kp[i], vp[i], branch=i, name=f"dilated_fwd_d{d}") for i, d in enumerate(DILATIONS)
            ]
            o, lse = _dil_combine(
                [_unperm(ob, d) for (ob, _), d in zip(branches, DILATIONS)],
                [_unperm(lb, d) for (_, lb), d in zip(branches, DILATIONS)],
                name="dilated_combine",
            )
            st.update(qp=qp, kp=kp, vp=vp, lse=lse)
        y1, x1, x1b = _mm_nn(o, W[l]["o"], epi="ln", extras=(h, g1, b1), name=f"attn_out_ln1_{l}")
        hp = _mm_nn(x1b, W[l]["ff1"], out_dtype=MXU, name=f"ff1_{l}")
        y2, x2, x2b = _mm_nn(hp, W[l]["ff2"], a_op="relu2", epi="ln", extras=(x1, g2, b2), name=f"ff2_ln2_{l}")
        st.update(o=o, y1=y1, x1b=x1b, hp=hp, y2=y2)
        saved.append(st)
        h, hb = x2, x2b

    grads = {}
    layer_grads = lambda l: jnp.concatenate([grads[f"{p}_{l}"].reshape(N_CHIPS, -1, D) for p in MATS], axis=1)
    top = saved[-1]
    dy2, dy2b, dg, db, loss_lanes = _ln_bwd_loss(
        top["y2"], given[f"ln2_g_{DEPTH - 1}"], given[f"ln2_b_{DEPTH - 1}"], target, name="loss_ln2_bwd"
    )
    loss = lax.psum(jnp.sum(loss_lanes) * (0.5 / D), ("x", "y", "c"))
    grads[f"ln2_g_{DEPTH - 1}"], grads[f"ln2_b_{DEPTH - 1}"] = dg, db
    grad_x = None
    for l in reversed(range(DEPTH)):
        st = saved[l]
        dhp = _mm_nt(dy2b, W[l]["ff2"], epi="relu2grad", extra=st["hp"], out_dtype=MXU, name=f"d_ff2_in_{l}")
        grads[f"w_ff2_{l}"] = _mm_tn(st["hp"], dy2b, a_op="relu2", groups=1, name=f"d_w_ff2_{l}")
        dx1 = _mm_nt(dhp, W[l]["ff1"], epi="residual", extra=dy2, out_dtype=F32, name=f"d_ff1_in_{l}")
        grads[f"w_ff1_{l}"] = _mm_tn(st["x1b"], dhp, a_op=None, groups=N_CHIPS, name=f"d_w_ff1_{l}")
        dy1, dy1b, grads[f"ln1_g_{l}"], grads[f"ln1_b_{l}"] = _ln_bwd(
            dx1, st["y1"], given[f"ln1_g_{l}"], name=f"ln1_bwd_{l}"
        )
        grads[f"w_o_{l}"] = _mm_tn(st["o"], dy1b, a_op=None, groups=1, name=f"d_w_o_{l}")
        if l == 0:
            do = _mm_nt(dy1b, W[l]["o"], epi=None, extra=None, out_dtype=MXU, name=f"d_attn_out_{l}")
            dq, dk, dv, from_chips_1 = _sb_bwd(
                st["qkv"], do, st["car"], scatter=layer_grads(1).astype(MXU), name="stickbreak_bwd"
            )
            dqkv = _qkv_bwd_post([dq], [dk], [dv], None, name=f"qkv_bwd_post_{l}")
        else:
            do = _mm_nt(dy1b, W[l]["o"], epi=None, extra=None, out_dtype=MXU, name=f"d_attn_out_{l}")
            stats = _head_stats(do, st["o"], st["lse"], name="dilated_head_stats")
            dqs, dks, dvs = [], [], []
            for i, d in enumerate(DILATIONS):
                ops = (st["qp"][i], st["kp"][i], st["vp"][i], _perm(do, d), _perm(stats, d))
                dqs.append(_unperm(_dil_bwd_dq(*ops, branch=i, name=f"dilated_bwd_dq_d{d}"), d))
                dk_d, dv_d = _dil_bwd_dkv(*ops, branch=i, name=f"dilated_bwd_dkv_d{d}")
                dks.append(_unperm(dk_d, d))
                dvs.append(_unperm(dv_d, d))
            dqkv = _qkv_bwd_post(dqs, dks, dvs, tables, name=f"qkv_bwd_post_{l}")
        dh = _mm_nt(dqkv, W[l]["qkv"], epi="residual", extra=dy1, out_dtype=F32, name=f"d_qkv_in_{l}")
        grads[f"w_qkv_{l}"] = _mm_tn(st["hb"], dqkv, a_op=None, groups=N_CHIPS, name=f"d_w_qkv_{l}")
        if l > 0:
            prev = saved[l - 1]
            dy2, dy2b, grads[f"ln2_g_{l - 1}"], grads[f"ln2_b_{l - 1}"] = _ln_bwd(
                dh, prev["y2"], given[f"ln2_g_{l - 1}"], name=f"ln2_bwd_{l - 1}"
            )
        else:
            grad_x = dh

    from_chips_0 = _scatter_grads(layer_grads(0).astype(MXU), name="scatter_grads_0")
    me = 2 * lax.axis_index("x") + lax.axis_index("y")
    own = jnp.concatenate(
        [lax.dynamic_index_in_dim(layer_grads(l), me, 0, keepdims=False) for l in range(DEPTH)], axis=0
    )
    from_chips = jnp.concatenate([from_chips_0, from_chips_1], axis=1)
    part = _sum4(own, from_chips, name="sum_chips")
    sib = _swap_with_sibling(part, name="swap_sibling")
    pack = lambda prefix: _pack_rows([given[prefix + n] for n in mat_names], D)
    outs_m = _adamw(jnp.stack([part, sib]), pack(""), pack("m_"), pack("v_"), name="adamw_matrices")
    vflat = _pack_rows([grads[n] for n in vec_names], D)
    packv = lambda prefix: _pack_rows([given[prefix + n] for n in vec_names], D)
    outs_v = _adamw(_gather_from_all(vflat, name="gather_ln_grads"), packv(""), packv("m_"), packv("v_"), name="adamw_vectors")

    def unpack(flat_m, flat_v):
        out, r0 = {}, 0
        for n in mat_names:
            r = given[n].size // D
            out[n] = flat_m[r0 : r0 + r].reshape(given[n].shape)
            r0 += r
        for i, n in enumerate(vec_names):
            out[n] = flat_v[i]
        return [out[n] for n in names]

    result = [loss, grad_x[None]]
    for k in range(4):
        result += unpack(outs_m[k], outs_v[k])
    return tuple(result)
```

```python
import functools

import jax
import jax.numpy as jnp
from jax import lax
from jax.experimental import pallas as pl
from jax.experimental.pallas import tpu as pltpu

F32 = jnp.float32
MXU = jnp.bfloat16

HEAD_DIM = 64
LANES = 128
CHUNK = 128
ROPE_THETA = 500000.0
ROPE_HALF = 8
DILATIONS = (1, 4, 16)
DEPTH = 2
ALPHA = (2 * DEPTH) ** 0.25
LN_EPS = 1e-5
QK_SCALE = 0.125
ADAM_LR, ADAM_B1, ADAM_B2, ADAM_EPS, ADAM_WD, ADAM_STEP = 0.001, 0.9, 0.999, 1e-08, 0.01, 10
NEG = -1e30
UNDERFLOW = -104.0
SB_TQ = 2 * CHUNK
V7X_VMEM_LIMIT = 56 * 1024 * 1024
MESH = pl.DeviceIdType.MESH
NT = (((1,), (1,)), ((), ()))
TN = (((0,), (0,)), ((), ()))
N_CHIPS = 4
MM_ROWS = (1024, 512, 256, 128)


def _params(*sem):
    return pltpu.CompilerParams(dimension_semantics=sem, vmem_limit_bytes=V7X_VMEM_LIMIT)


def _pick(n, cands=(1024, 768, 512, 384, 256, 128)):
    for c in cands:
        if n % c == 0:
            return c
    raise ValueError(f"no tile for {n}")


def _mm_nn(a, b, *, a_op=None, epi=None, extras=(), out_dtype=F32, name):
    M, K = a.shape
    Kb, N = b.shape
    assert K == Kb
    tn = N // 3 if epi == "qkv" else (N if epi == "ln" else _pick(N))
    tm = _pick(M, MM_ROWS) if (K <= 1024 and epi != "ln") else _pick(M, MM_ROWS[1:])
    rope = epi == "qkv" and len(extras) == 3

    def body(a_ref, b_ref, *refs):
        av = a_ref[...]
        if a_op == "relu2":
            av = jnp.square(jnp.maximum(av.astype(F32), 0.0))
        part = jnp.dot(av.astype(MXU), b_ref[...], preferred_element_type=F32)
        if epi is None:
            refs[0][...] = part.astype(out_dtype)
        elif epi == "qkv":
            o_ref = refs[-1]
            tabs = tuple(r[...] for r in refs[:-1])
            j = pl.program_id(1)

            def write(rotate, scale):
                for blk in range(tn // LANES):
                    cols = slice(blk * LANES, (blk + 1) * LANES)
                    xb = part[:, cols]
                    if rotate:
                        xb = _rope(xb, *tabs)
                    if scale:
                        xb = xb * QK_SCALE
                    o_ref[:, cols] = xb.astype(MXU)

            pl.when(j == 0)(lambda: write(rope, True))
            pl.when(j == 1)(lambda: write(rope, False))
            pl.when(j == 2)(lambda: write(False, False))
        else:
            x_ref, g_ref, be_ref, y_ref, o_ref, ob_ref = refs
            y = ALPHA * x_ref[...] + part
            xhat, _ = _ln_stats(y)
            out = xhat * g_ref[...] + be_ref[...]
            y_ref[...] = y
            o_ref[...] = out
            ob_ref[...] = out.astype(MXU)

    in_specs = [pl.BlockSpec((tm, K), lambda i, j: (i, 0)), pl.BlockSpec((K, tn), lambda i, j: (0, j))]
    tile = pl.BlockSpec((tm, tn), lambda i, j: (i, j))
    ops = [a, b]
    if epi == "qkv":
        in_specs += [pl.BlockSpec((tm, LANES), lambda i, j: (i, 0))] * len(extras)
        ops += list(extras)
        out_specs, out_shape = tile, jax.ShapeDtypeStruct((M, N), MXU)
    elif epi == "ln":
        x, g, be = extras
        in_specs += [tile, pl.BlockSpec((1, N), lambda i, j: (0, 0)), pl.BlockSpec((1, N), lambda i, j: (0, 0))]
        ops += [x, g.reshape(1, N), be.reshape(1, N)]
        out_specs = [tile] * 3
        out_shape = [jax.ShapeDtypeStruct((M, N), F32)] * 2 + [jax.ShapeDtypeStruct((M, N), MXU)]
    else:
        out_specs, out_shape = tile, jax.ShapeDtypeStruct((M, N), out_dtype)
    return pl.pallas_call(
        body,
        grid=(M // tm, N // tn),
        in_specs=in_specs,
        out_specs=out_specs,
        out_shape=out_shape,
        compiler_params=_params("parallel", "parallel"),
        name=name,
    )(*ops)


def _mm_nt(a, b, *, epi, extra, out_dtype, name):
    M, N = a.shape
    K, Nb = b.shape
    assert N == Nb
    tm = _pick(M, MM_ROWS) if N <= 1024 else _pick(M, MM_ROWS[1:])
    tko = _pick(K)

    def body(*refs):
        if epi is None:
            a_ref, b_ref, o_ref = refs
        else:
            a_ref, b_ref, e_ref, o_ref = refs
        r = lax.dot_general(a_ref[...].astype(MXU), b_ref[...], NT, preferred_element_type=F32)
        if epi == "relu2grad":
            r = r * (2.0 * jnp.maximum(e_ref[...].astype(F32), 0.0))
        elif epi == "residual":
            r = ALPHA * e_ref[...] + r
        o_ref[...] = r.astype(out_dtype)

    in_specs = [pl.BlockSpec((tm, N), lambda i, j: (i, 0)), pl.BlockSpec((tko, N), lambda i, j: (j, 0))]
    ops = [a, b]
    if epi is not None:
        in_specs.append(pl.BlockSpec((tm, tko), lambda i, j: (i, j)))
        ops.append(extra)
    return pl.pallas_call(
        body,
        grid=(M // tm, K // tko),
        in_specs=in_specs,
        out_specs=pl.BlockSpec((tm, tko), lambda i, j: (i, j)),
        out_shape=jax.ShapeDtypeStruct((M, K), out_dtype),
        compiler_params=_params("parallel", "parallel"),
        name=name,
    )(*ops)


def _mm_tn(a, b, *, a_op, groups, name):
    M, Ka = a.shape
    Mb, N = b.shape
    assert M == Mb
    Nc = N // groups
    tka, tn, tm = _pick(Ka), _pick(Nc), _pick(M, (2048, 1024, 512, 256, 128))
    nbs, nm = Nc // tn, M // tm

    def body(a_ref, b_ref, o_ref, acc_ref):
        kk = pl.program_id(2)
        av = a_ref[...]
        if a_op == "relu2":
            av = jnp.square(jnp.maximum(av.astype(F32), 0.0))
        part = lax.dot_general(av.astype(MXU), b_ref[...].astype(MXU), TN, preferred_element_type=F32)

        @pl.when(kk == 0)
        def _():
            acc_ref[...] = part

        @pl.when(kk > 0)
        def _():
            acc_ref[...] += part

        @pl.when(kk == nm - 1)
        def _():
            o_ref[...] = acc_ref[...]

    return pl.pallas_call(
        body,
        grid=(Ka // tka, N // tn, nm),
        in_specs=[
            pl.BlockSpec((tm, tka), lambda i, j, k: (k, i)),
            pl.BlockSpec((tm, tn), lambda i, j, k: (k, j)),
        ],
        out_specs=pl.BlockSpec((None, tka, tn), lambda i, j, k: (j // nbs, i, j % nbs)),
        out_shape=jax.ShapeDtypeStruct((groups, Ka, Nc), F32),
        scratch_shapes=[pltpu.VMEM((tka, tn), F32)],
        compiler_params=_params("parallel", "parallel", "arbitrary"),
        name=name,
    )(a, b)


def _ln_stats(y):
    mu = jnp.mean(y, axis=-1, keepdims=True)
    yc = y - mu
    var = jnp.mean(yc * yc, axis=-1, keepdims=True)
    rstd = lax.rsqrt(var + LN_EPS)
    return yc * rstd, rstd


def _ln_bwd(dx, y, g, *, name):
    S, D = y.shape
    tm = _pick(S, (256, 128))

    def body(dx_ref, y_ref, g_ref, dy_ref, dyb_ref, dg_ref, db_ref):
        xhat, rstd = _ln_stats(y_ref[...])
        dx_v = dx_ref[...]
        dxh = dx_v * g_ref[...]
        m1 = jnp.mean(dxh, axis=-1, keepdims=True)
        m2 = jnp.mean(dxh * xhat, axis=-1, keepdims=True)
        dy = rstd * (dxh - m1 - xhat * m2)
        dy_ref[...] = dy
        dyb_ref[...] = dy.astype(MXU)

        @pl.when(pl.program_id(0) == 0)
        def _():
            dg_ref[...] = jnp.zeros_like(dg_ref)
            db_ref[...] = jnp.zeros_like(db_ref)

        dg_ref[...] += jnp.sum(dx_v * xhat, axis=0, keepdims=True)
        db_ref[...] += jnp.sum(dx_v, axis=0, keepdims=True)

    row = pl.BlockSpec((tm, D), lambda i: (i, 0))
    vec = pl.BlockSpec((1, D), lambda i: (0, 0))
    return pl.pallas_call(
        body,
        grid=(S // tm,),
        in_specs=[row, row, vec],
        out_specs=[row, row, vec, vec],
        out_shape=[jax.ShapeDtypeStruct((S, D), F32), jax.ShapeDtypeStruct((S, D), MXU)]
        + [jax.ShapeDtypeStruct((1, D), F32)] * 2,
        compiler_params=_params("arbitrary"),
        name=name,
    )(dx, y, g.reshape(1, D))


def _ln_bwd_loss(y, g, b, target, *, name):
    S, D = y.shape
    tm = _pick(S, (256, 128))

    def body(y_ref, g_ref, b_ref, t_ref, dy_ref, dyb_ref, dg_ref, db_ref, ls_ref):
        xhat, rstd = _ln_stats(y_ref[...])
        err = xhat * g_ref[...] + b_ref[...] - t_ref[...]
        dx_v = err * (1.0 / D)
        dxh = dx_v * g_ref[...]
        m1 = jnp.mean(dxh, axis=-1, keepdims=True)
        m2 = jnp.mean(dxh * xhat, axis=-1, keepdims=True)
        dy = rstd * (dxh - m1 - xhat * m2)
        dy_ref[...] = dy
        dyb_ref[...] = dy.astype(MXU)

        @pl.when(pl.program_id(0) == 0)
        def _():
            dg_ref[...] = jnp.zeros_like(dg_ref)
            db_ref[...] = jnp.zeros_like(db_ref)
            ls_ref[...] = jnp.zeros_like(ls_ref)

        dg_ref[...] += jnp.sum(dx_v * xhat, axis=0, keepdims=True)
        db_ref[...] += jnp.sum(dx_v, axis=0, keepdims=True)
        ls_ref[...] += jnp.sum(err * err, axis=0, keepdims=True)

    row = pl.BlockSpec((tm, D), lambda i: (i, 0))
    vec = pl.BlockSpec((1, D), lambda i: (0, 0))
    return pl.pallas_call(
        body,
        grid=(S // tm,),
        in_specs=[row, vec, vec, row],
        out_specs=[row, row, vec, vec, vec],
        out_shape=[jax.ShapeDtypeStruct((S, D), F32), jax.ShapeDtypeStruct((S, D), MXU)]
        + [jax.ShapeDtypeStruct((1, D), F32)] * 3,
        compiler_params=_params("arbitrary"),
        name=name,
    )(y, g.reshape(1, D), b.reshape(1, D), target)


def _rope_tables(S):
    inv_freq = ROPE_THETA ** (-jnp.arange(ROPE_HALF, dtype=F32) / ROPE_HALF)
    ang = jnp.arange(S, dtype=jnp.int32).astype(F32)[:, None] * inv_freq[None, :]
    cos, sin = jnp.cos(ang), jnp.sin(ang)
    ones = jnp.ones((S, HEAD_DIM - 2 * ROPE_HALF), F32)
    zeros8 = jnp.zeros((S, ROPE_HALF), F32)
    c = jnp.concatenate([cos, cos, ones], axis=1)
    s1 = jnp.concatenate([zeros8, sin, 0.0 * ones], axis=1)
    s2 = jnp.concatenate([-sin, zeros8, 0.0 * ones], axis=1)
    return tuple(jnp.concatenate([t, t], axis=1) for t in (c, s1, s2))


def _rope(xb, c, s1, s2):
    return xb * c + pltpu.roll(xb, ROPE_HALF, 1) * s1 + pltpu.roll(xb, LANES - ROPE_HALF, 1) * s2


def _rope_t(dyb, c, s1, s2):
    return dyb * c + pltpu.roll(dyb * s1, LANES - ROPE_HALF, 1) + pltpu.roll(dyb * s2, ROPE_HALF, 1)


def _qkv_bwd_post(dqs, dks, dvs, tables, *, name):
    S, D = dqs[0].shape
    tm = _pick(S, (256, 128))
    nb = len(dqs)
    rope = tables is not None

    def body(*refs):
        ins, o_ref = refs[:-1], refs[-1]
        groups = [ins[0:nb], ins[nb : 2 * nb], ins[2 * nb : 3 * nb]]
        tabs = [r[...] for r in ins[3 * nb :]]
        for j, grp in enumerate(groups):
            for blk in range(D // LANES):
                cols = slice(blk * LANES, (blk + 1) * LANES)
                v = grp[0][:, cols].astype(F32)
                for r in grp[1:]:
                    v = v + r[:, cols].astype(F32)
                if rope and j < 2:
                    v = _rope_t(v, *tabs)
                if j == 0:
                    v = v * QK_SCALE
                o_ref[:, j * D + blk * LANES : j * D + (blk + 1) * LANES] = v.astype(MXU)

    row = pl.BlockSpec((tm, D), lambda i: (i, 0))
    tab = pl.BlockSpec((tm, LANES), lambda i: (i, 0))
    return pl.pallas_call(
        body,
        grid=(S // tm,),
        in_specs=[row] * (3 * nb) + ([tab] * 3 if rope else []),
        out_specs=pl.BlockSpec((tm, 3 * D), lambda i: (i, 0)),
        out_shape=jax.ShapeDtypeStruct((S, 3 * D), MXU),
        compiler_params=_params("parallel"),
        name=name,
    )(*dqs, *dks, *dvs, *(tables if rope else ()))


def _log_fail(z):
    return -(jnp.maximum(z, 0.0) + jnp.log(1.0 + jnp.exp(-jnp.abs(z))))


def _sb_fwd(qkv, *, gather=None, name):
    S, D = qkv.shape[0], qkv.shape[1] // 3
    HP, H = D // LANES, D // HEAD_DIM
    tq = SB_TQ
    assert S % tq == 0
    cpb = tq // CHUNK
    assert S // CHUNK <= LANES
    steps = HP * (S // tq)

    def body(*refs):
        if gather is None:
            q_ref, k_ref, v_ref, o_ref, car_ref, acc_ref, carry0_ref, carry1_ref = refs
        else:
            q_ref, k_ref, v_ref, w_ref, o_ref, car_ref, g_ref, acc_ref, carry0_ref, carry1_ref = refs[:10]
            start, hand_on, finish = _gather_stages(w_ref, g_ref, *refs[10:])
            t = pl.program_id(0) * (S // tq) + pl.program_id(1)
            pl.when(t == 0)(start)
            pl.when(t == steps // 2)(hand_on)
        i = pl.program_id(1)
        lane = lax.broadcasted_iota(jnp.int32, (tq, LANES), 1)
        row = lax.broadcasted_iota(jnp.int32, (tq, LANES), 0)
        kl = lax.broadcasted_iota(jnp.int32, (CHUNK, LANES), 1)
        kr = lax.broadcasted_iota(jnp.int32, (CHUNK, LANES), 0)
        tri = (kr >= kl).astype(MXU)
        qv = q_ref[...]
        qh = (jnp.where(lane < HEAD_DIM, qv, jnp.zeros_like(qv)), jnp.where(lane >= HEAD_DIM, qv, jnp.zeros_like(qv)))
        carry_refs = (carry0_ref, carry1_ref)
        acc_ref[...] = jnp.zeros_like(acc_ref)
        car_ref[...] = jnp.full((2, tq, LANES), NEG, F32)
        for r in carry_refs:
            r[...] = jnp.zeros_like(r)

        def run(chunks, masked):
            offs = [pl.multiple_of(c * CHUNK, CHUNK) for c in chunks]
            kcs = [k_ref[pl.ds(o, CHUNK), :] for o in offs]
            vcs = [v_ref[pl.ds(o, CHUNK), :] for o in offs]
            jobs = [(n, hh) for n in range(len(chunks)) for hh in range(2)]
            zs = {j: lax.dot_general(qh[j[1]], kcs[j[0]], NT, preferred_element_type=F32) for j in jobs}
            if masked:
                cms = [(o + lane) < (i * tq + row) for o in offs]
            lfs = {}
            for j in jobs:
                lf = _log_fail(zs[j])
                lfs[j] = jnp.where(cms[j[0]], lf, 0.0) if masked else lf
            rs = {j: jnp.dot(lfs[j].astype(MXU), tri, preferred_element_type=F32) for j in jobs}
            ws = {}
            for hh in range(2):
                carry = carry_refs[hh][...]
                tile = car_ref[hh]
                for n, c in enumerate(chunks):
                    w = jnp.exp(zs[n, hh] + rs[n, hh] + carry)
                    if masked:
                        w = jnp.where(cms[n], w, 0.0)
                    ws[n, hh] = w.astype(MXU)
                    tile = jnp.where(lane == c, carry, tile)
                    carry = carry + jnp.sum(lfs[n, hh], axis=-1, keepdims=True)
                car_ref[hh] = tile
                carry_refs[hh][...] = carry
            for hh in range(2):
                pv = jnp.dot(ws[0, hh], vcs[0], preferred_element_type=F32)
                for n in range(1, len(chunks)):
                    pv = pv + jnp.dot(ws[n, hh], vcs[n], preferred_element_type=F32)
                acc_ref[hh] += pv

        def max_carry():
            return jnp.maximum(jnp.max(carry0_ref[...]), jnp.max(carry1_ref[...]))

        run([i * cpb + u for u in reversed(range(cpb))], True)

        def cond(st):
            return jnp.logical_and(st[0] >= 0, st[1] >= UNDERFLOW)

        def pair(st):
            c = st[0]
            run([c, c - 1], False)
            return c - 2, max_carry()

        lax.while_loop(cond, pair, (i * cpb - 1, max_carry()))
        o_ref[...] = jnp.where(lane < HEAD_DIM, acc_ref[0], acc_ref[1]).astype(o_ref.dtype)
        if gather is not None:
            pl.when(t == steps - 1)(finish)

    blk = pl.BlockSpec((tq, LANES), lambda h, i: (i, h))
    k_full = pl.BlockSpec((S, LANES), lambda h, i: (0, HP + h))
    v_full = pl.BlockSpec((S, LANES), lambda h, i: (0, 2 * HP + h))
    hbm = pl.BlockSpec(memory_space=pl.ANY)
    in_specs, ops = [blk, k_full, v_full], [qkv, qkv, qkv]
    out_specs = [blk, pl.BlockSpec((2, tq, LANES), lambda h, i: (h, i, 0))]
    out_shape = [jax.ShapeDtypeStruct((S, D), MXU), jax.ShapeDtypeStruct((H, S, LANES), F32)]
    scratch = [pltpu.VMEM((2, tq, LANES), F32), pltpu.VMEM((tq, 1), F32), pltpu.VMEM((tq, 1), F32)]
    if gather is not None:
        in_specs, ops = in_specs + [hbm], ops + [gather]
        out_specs = out_specs + [hbm]
        out_shape = out_shape + [jax.ShapeDtypeStruct((N_CHIPS,) + gather.shape, gather.dtype)]
        scratch = scratch + GATHER_SEMS
    return pl.pallas_call(
        body,
        grid=(HP, S // tq),
        in_specs=in_specs,
        out_specs=out_specs,
        out_shape=out_shape,
        scratch_shapes=scratch,
        compiler_params=_params("arbitrary", "arbitrary"),
        name=name,
    )(*ops)


def _sb_bwd(qkv, do, car, *, scatter=None, name):
    S, D = qkv.shape[0], qkv.shape[1] // 3
    HP = D // LANES
    tq = SB_TQ
    assert S % tq == 0
    cpb = tq // CHUNK
    nq = S // tq

    def body(*refs):
        if scatter is None:
            q_ref, k_ref, v_ref, do_ref, car_ref, dq_ref, dk_hbm, dv_hbm = refs[:8]
            dq_acc, dk_acc, dv_acc, gcar0_ref, gcar1_ref, sem = refs[8:]
        else:
            q_ref, k_ref, v_ref, do_ref, car_ref, g_ref, dq_ref, dk_hbm, dv_hbm, r_ref = refs[:10]
            dq_acc, dk_acc, dv_acc, gcar0_ref, gcar1_ref, sem = refs[10:16]
            start, finish = _scatter_stages(g_ref, r_ref, *refs[16:])
            t = pl.program_id(0) * nq + pl.program_id(1)
            pl.when(t == 0)(start)
        hp = pl.program_id(0)
        i = pl.program_id(1)
        lane = lax.broadcasted_iota(jnp.int32, (tq, LANES), 1)
        row = lax.broadcasted_iota(jnp.int32, (tq, LANES), 0)
        kl = lax.broadcasted_iota(jnp.int32, (CHUNK, LANES), 1)
        kr = lax.broadcasted_iota(jnp.int32, (CHUNK, LANES), 0)
        tri = (kr >= kl).astype(MXU)
        tri_prefix = (kr <= kl).astype(MXU)

        @pl.when(i == 0)
        def _():
            dk_acc[...] = jnp.zeros_like(dk_acc)
            dv_acc[...] = jnp.zeros_like(dv_acc)

        qv = q_ref[...]
        dov = do_ref[...]
        qh = (jnp.where(lane < HEAD_DIM, qv, jnp.zeros_like(qv)), jnp.where(lane >= HEAD_DIM, qv, jnp.zeros_like(qv)))
        doh = (jnp.where(lane < HEAD_DIM, dov, jnp.zeros_like(dov)), jnp.where(lane >= HEAD_DIM, dov, jnp.zeros_like(dov)))
        gcar_refs = (gcar0_ref, gcar1_ref)
        dq_acc[...] = jnp.zeros_like(dq_acc)
        for r in gcar_refs:
            r[...] = jnp.zeros_like(r)

        def run(chunks, masked):
            offs = [pl.multiple_of(c * CHUNK, CHUNK) for c in chunks]
            kcs = [k_ref[pl.ds(o, CHUNK), :] for o in offs]
            vcs = [v_ref[pl.ds(o, CHUNK), :] for o in offs]
            jobs = [(n, hh) for n in range(len(chunks)) for hh in range(2)]
            if masked:
                cms = [(o + lane) < (i * tq + row) for o in offs]
            zs = {j: lax.dot_general(qh[j[1]], kcs[j[0]], NT, preferred_element_type=F32) for j in jobs}
            dws = {j: lax.dot_general(doh[j[1]], vcs[j[0]], NT, preferred_element_type=F32) for j in jobs}
            lfs, sigs = {}, {}
            for j in jobs:
                lf = _log_fail(zs[j])
                sigs[j] = jnp.exp(zs[j] + lf)
                lfs[j] = jnp.where(cms[j[0]], lf, 0.0) if masked else lf
            rs = {j: jnp.dot(lfs[j].astype(MXU), tri, preferred_element_type=F32) for j in jobs}
            ws, gs = {}, {}
            for hh in range(2):
                tile = car_ref[hh]
                for n, c in enumerate(chunks):
                    carry = jnp.sum(jnp.where(lane == c, tile, 0.0), axis=-1, keepdims=True)
                    w = jnp.exp(zs[n, hh] + rs[n, hh] + carry)
                    if masked:
                        w = jnp.where(cms[n], w, 0.0)
                    ws[n, hh] = w.astype(MXU)
                    gs[n, hh] = w * dws[n, hh]
            big_gs = {j: jnp.dot(gs[j].astype(MXU), tri_prefix, preferred_element_type=F32) for j in jobs}
            dzs = {}
            for hh in range(2):
                gcar = gcar_refs[hh][...]
                for n in range(len(chunks)):
                    dz = gs[n, hh] - sigs[n, hh] * (big_gs[n, hh] + gcar)
                    if masked:
                        dz = jnp.where(cms[n], dz, 0.0)
                    dzs[n, hh] = dz.astype(MXU)
                    gcar = gcar + jnp.sum(gs[n, hh], axis=-1, keepdims=True)
                gcar_refs[hh][...] = gcar
            for hh in range(2):
                part = jnp.dot(dzs[0, hh], kcs[0], preferred_element_type=F32)
                for n in range(1, len(chunks)):
                    part = part + jnp.dot(dzs[n, hh], kcs[n], preferred_element_type=F32)
                dq_acc[hh] += part
            for n, o in enumerate(offs):
                dk_acc[pl.ds(o, CHUNK), :] += lax.dot_general(
                    dzs[n, 0], qh[0], TN, preferred_element_type=F32
                ) + lax.dot_general(dzs[n, 1], qh[1], TN, preferred_element_type=F32)
                dv_acc[pl.ds(o, CHUNK), :] += lax.dot_general(
                    ws[n, 0], doh[0], TN, preferred_element_type=F32
                ) + lax.dot_general(ws[n, 1], doh[1], TN, preferred_element_type=F32)

        colmax = jnp.maximum(jnp.max(car_ref[0], axis=0, keepdims=True), jnp.max(car_ref[1], axis=0, keepdims=True))
        lane1 = lax.broadcasted_iota(jnp.int32, (1, LANES), 1)
        dead = jnp.logical_and(colmax < UNDERFLOW, lane1 < i * cpb)
        first_pair = jnp.sum(jnp.where(dead, 1.0, 0.0)).astype(jnp.int32) // 2

        def loop_body(p, carry):
            run([2 * p, 2 * p + 1], False)
            return carry

        lax.fori_loop(first_pair, i * cpb // 2, loop_body, 0)
        run([i * cpb + u for u in range(cpb)], True)
        dq_ref[...] = jnp.where(lane < HEAD_DIM, dq_acc[0], dq_acc[1])

        @pl.when(i == nq - 1)
        def _():
            cols = pl.ds(pl.multiple_of(hp * LANES, LANES), LANES)
            for src, dst in ((dk_acc, dk_hbm), (dv_acc, dv_hbm)):
                cp = pltpu.make_async_copy(src, dst.at[:, cols], sem)
                cp.start()
                cp.wait()

        if scatter is not None:
            pl.when(t == HP * nq - 1)(finish)

    blk = pl.BlockSpec((tq, LANES), lambda h, i: (i, h))
    k_full = pl.BlockSpec((S, LANES), lambda h, i: (0, HP + h))
    v_full = pl.BlockSpec((S, LANES), lambda h, i: (0, 2 * HP + h))
    hbm = pl.BlockSpec(memory_space=pl.ANY)
    in_specs = [blk, k_full, v_full, blk, pl.BlockSpec((2, tq, LANES), lambda h, i: (h, i, 0))]
    ops = [qkv, qkv, qkv, do, car]
    out_specs, out_shape = [blk, hbm, hbm], [jax.ShapeDtypeStruct((S, D), F32)] * 3
    scratch = [
        pltpu.VMEM((2, tq, LANES), F32),
        pltpu.VMEM((S, LANES), F32),
        pltpu.VMEM((S, LANES), F32),
        pltpu.VMEM((tq, 1), F32),
        pltpu.VMEM((tq, 1), F32),
        pltpu.SemaphoreType.DMA,
    ]
    if scatter is not None:
        in_specs, ops = in_specs + [hbm], ops + [scatter]
        out_specs = out_specs + [hbm]
        out_shape = out_shape + [jax.ShapeDtypeStruct((3,) + scatter.shape[1:], scatter.dtype)]
        scratch = scratch + SCATTER_SEMS
    return pl.pallas_call(
        body,
        grid=(HP, nq),
        in_specs=in_specs,
        out_specs=out_specs,
        out_shape=out_shape,
        scratch_shapes=scratch,
        compiler_params=_params("arbitrary", "arbitrary"),
        name=name,
    )(*ops)


def _perm(x, d):
    if d == 1:
        return x
    S, D = x.shape
    return x.reshape(S // d, d, D).transpose(1, 0, 2).reshape(S, D)


def _unperm(x, d):
    if d == 1:
        return x
    S, D = x.shape
    return x.reshape(d, S // d, D).transpose(1, 0, 2).reshape(S, D)


def _dil_tb(S):
    return min(1024, S // DILATIONS[-1])


def _dil_masks():
    qi = lax.broadcasted_iota(jnp.int32, (CHUNK, CHUNK), 0)
    kj = lax.broadcasted_iota(jnp.int32, (CHUNK, CHUNK), 1)
    return qi, kj


def _dil_fwd(q, k, v, *, branch, name):
    S, D = q.shape
    HP = D // LANES
    tb = _dil_tb(S)
    nb, nsub = S // tb, tb // CHUNK
    bps = nb // DILATIONS[branch]

    def body(q_ref, k_ref, v_ref, kp_ref, vp_ref, o_ref, l_ref):
        n = pl.program_id(1)
        first = jnp.bitwise_and(n, bps - 1) == 0
        qi, kj = _dil_masks()
        hl = lax.broadcasted_iota(jnp.int32, (CHUNK, LANES), 1)
        m_cur = kj <= qi
        m_prev = (kj - qi) >= 0
        m_prev0 = (kj - qi) >= jnp.where(first, 2 * CHUNK, 0)
        hms = (hl < HEAD_DIM, hl >= HEAD_DIM)
        tiles, scores = [], {}
        for u in range(nsub):
            rows = slice(u * CHUNK, (u + 1) * CHUNK)
            qs, kc, vc = q_ref[rows, :], k_ref[rows, :], v_ref[rows, :]
            if u == 0:
                kp, vp, pm = kp_ref[...], vp_ref[...], m_prev0
            else:
                prev = slice((u - 1) * CHUNK, u * CHUNK)
                kp, vp, pm = k_ref[prev, :], v_ref[prev, :], m_prev
            tiles.append((rows, vp, vc, pm))
            for hh in range(2):
                qh = jnp.where(hms[hh], qs, jnp.zeros_like(qs))
                scores[u, hh] = (
                    lax.dot_general(qh, kp, NT, preferred_element_type=F32),
                    lax.dot_general(qh, kc, NT, preferred_element_type=F32),
                )
        probs = {}
        for (u, hh), (s_p, s_c) in scores.items():
            s_p = jnp.where(tiles[u][3], s_p, NEG)
            s_c = jnp.where(m_cur, s_c, NEG)
            m = jnp.max(jnp.maximum(s_p, s_c), axis=-1, keepdims=True)
            p_p, p_c = jnp.exp(s_p - m), jnp.exp(s_c - m)
            den = jnp.sum(p_p + p_c, axis=-1, keepdims=True)
            probs[u, hh] = (p_p.astype(MXU), p_c.astype(MXU), 1.0 / den, m + jnp.log(den))
        for u, (rows, vp, vc, _) in enumerate(tiles):
            o_t = l_t = None
            for hh in range(2):
                p_p, p_c, inv, lse = probs[u, hh]
                num = jnp.dot(p_p, vp, preferred_element_type=F32) + jnp.dot(p_c, vc, preferred_element_type=F32)
                o_h, l_h = num * inv, jnp.broadcast_to(lse, (CHUNK, LANES))
                o_t = o_h if hh == 0 else jnp.where(hms[1], o_h, o_t)
                l_t = l_h if hh == 0 else jnp.where(hms[1], l_h, l_t)
            o_ref[rows, :] = o_t
            l_ref[rows, :] = l_t

    blk = pl.BlockSpec((tb, LANES), lambda h, n: (n, h))
    tail = pl.BlockSpec((CHUNK, LANES), lambda h, n: (jnp.maximum(n * nsub - 1, 0), h))
    return pl.pallas_call(
        body,
        grid=(HP, nb),
        in_specs=[blk, blk, blk, tail, tail],
        out_specs=[blk, blk],
        out_shape=[jax.ShapeDtypeStruct((S, D), F32)] * 2,
        compiler_params=_params("parallel", "parallel"),
        name=name,
    )(q, k, v, k, v)


def _dil_combine(os, ls, *, name):
    S, D = os[0].shape
    tm = _pick(S, (256, 128))

    def body(o0_ref, o1_ref, o2_ref, l0_ref, l1_ref, l2_ref, out_ref, lse_ref):
        l0, l1, l2 = l0_ref[...], l1_ref[...], l2_ref[...]
        m = jnp.maximum(jnp.maximum(l0, l1), l2)
        a0, a1, a2 = jnp.exp(l0 - m), jnp.exp(l1 - m), jnp.exp(l2 - m)
        den = a0 + a1 + a2
        out_ref[...] = (a0 * o0_ref[...] + a1 * o1_ref[...] + a2 * o2_ref[...]) / den
        lse = m + jnp.log(den)
        lane = lax.broadcasted_iota(jnp.int32, (tm, LANES), 1)
        packed = jnp.zeros((tm, LANES), F32)
        for hp in range(D // LANES):
            blk = lse[:, hp * LANES : (hp + 1) * LANES]
            for hh in range(2):
                packed = jnp.where(lane == 2 * hp + hh, _lane_col(blk, lane, hh * HEAD_DIM), packed)
        lse_ref[...] = packed

    row = pl.BlockSpec((tm, D), lambda i: (i, 0))
    return pl.pallas_call(
        body,
        grid=(S // tm,),
        in_specs=[row] * 6,
        out_specs=[row, pl.BlockSpec((tm, LANES), lambda i: (i, 0))],
        out_shape=[jax.ShapeDtypeStruct((S, D), F32), jax.ShapeDtypeStruct((S, LANES), F32)],
        compiler_params=_params("parallel"),
        name=name,
    )(*os, *ls)


def _lane_col(tile, lane, at):
    return jnp.sum(jnp.where(lane == at, tile, 0.0), axis=-1, keepdims=True)


def _head_stats(do, out, lse_packed, *, name):
    S, D = out.shape
    H = D // HEAD_DIM
    assert 2 * H <= LANES
    tm = _pick(S, (256, 128))

    def body(do_ref, out_ref, l_ref, s_ref):
        lane = lax.broadcasted_iota(jnp.int32, (tm, LANES), 1)
        packed = l_ref[...]
        for hp in range(D // LANES):
            cols = slice(hp * LANES, (hp + 1) * LANES)
            prod = do_ref[:, cols].astype(F32) * out_ref[:, cols]
            for hh in range(2):
                head = (lane >= HEAD_DIM) if hh else (lane < HEAD_DIM)
                delta = jnp.sum(jnp.where(head, prod, 0.0), axis=-1, keepdims=True)
                packed = jnp.where(lane == H + 2 * hp + hh, delta, packed)
        s_ref[...] = packed

    row = pl.BlockSpec((tm, D), lambda i: (i, 0))
    vec = pl.BlockSpec((tm, LANES), lambda i: (i, 0))
    return pl.pallas_call(
        body,
        grid=(S // tm,),
        in_specs=[row, row, vec],
        out_specs=vec,
        out_shape=jax.ShapeDtypeStruct((S, LANES), F32),
        compiler_params=_params("parallel"),
        name=name,
    )(do, out, lse_packed)


def _dil_bwd_dq(q, k, v, do, stats, *, branch, name):
    S, D = q.shape
    HP, H = D // LANES, D // HEAD_DIM
    tb = _dil_tb(S)
    nb, nsub = S // tb, tb // CHUNK
    bps = nb // DILATIONS[branch]

    def body(q_ref, k_ref, v_ref, kp_ref, vp_ref, do_ref, st_ref, dq_ref):
        hp = pl.program_id(0)
        n = pl.program_id(1)
        first = jnp.bitwise_and(n, bps - 1) == 0
        qi, kj = _dil_masks()
        hl = lax.broadcasted_iota(jnp.int32, (CHUNK, LANES), 1)
        m_cur = kj <= qi
        m_prev = (kj - qi) >= 0
        m_prev0 = (kj - qi) >= jnp.where(first, 2 * CHUNK, 0)
        hms = (hl < HEAD_DIM, hl >= HEAD_DIM)
        jobs, cols = [], {}
        for u in range(nsub):
            rows = slice(u * CHUNK, (u + 1) * CHUNK)
            qs, kc, vc = q_ref[rows, :], k_ref[rows, :], v_ref[rows, :]
            if u == 0:
                kp, vp, pm = kp_ref[...], vp_ref[...], m_prev0
            else:
                prev = slice((u - 1) * CHUNK, u * CHUNK)
                kp, vp, pm = k_ref[prev, :], v_ref[prev, :], m_prev
            do_t, st_t = do_ref[rows, :], st_ref[rows, :]
            for hh in range(2):
                qh = jnp.where(hms[hh], qs, jnp.zeros_like(qs))
                dob = jnp.where(hms[hh], do_t, jnp.zeros_like(do_t))
                cols[u, hh] = (_lane_col(st_t, hl, H + 2 * hp + hh), _lane_col(st_t, hl, 2 * hp + hh))
                for kk, vv, msk in ((kp, vp, pm), (kc, vc, m_cur)):
                    s = lax.dot_general(qh, kk, NT, preferred_element_type=F32)
                    dp = lax.dot_general(dob, vv, NT, preferred_element_type=F32)
                    jobs.append((u, hh, s, dp, kk, msk))
        dss = []
        for u, hh, s, dp, kk, msk in jobs:
            delta, lse = cols[u, hh]
            p = jnp.exp(jnp.where(msk, s, NEG) - lse)
            dss.append((p * (dp - delta)).astype(MXU))
        dq_h = {}
        for (u, hh, _, _, kk, _), ds in zip(jobs, dss):
            part = jnp.dot(ds, kk, preferred_element_type=F32)
            dq_h[u, hh] = part if (u, hh) not in dq_h else dq_h[u, hh] + part
        for u in range(nsub):
            dq_ref[u * CHUNK : (u + 1) * CHUNK, :] = jnp.where(hms[0], dq_h[u, 0], dq_h[u, 1]).astype(MXU)

    blk = pl.BlockSpec((tb, LANES), lambda h, n: (n, h))
    tail = pl.BlockSpec((CHUNK, LANES), lambda h, n: (jnp.maximum(n * nsub - 1, 0), h))
    return pl.pallas_call(
        body,
        grid=(HP, nb),
        in_specs=[blk, blk, blk, tail, tail, blk, pl.BlockSpec((tb, LANES), lambda h, n: (n, 0))],
        out_specs=blk,
        out_shape=jax.ShapeDtypeStruct((S, D), MXU),
        compiler_params=_params("parallel", "parallel"),
        name=name,
    )(q, k, v, k, v, do, stats)


def _dil_bwd_dkv(q, k, v, do, stats, *, branch, name):
    S, D = q.shape
    HP, H = D // LANES, D // HEAD_DIM
    tb = _dil_tb(S)
    nb, nsub = S // tb, tb // CHUNK
    last_chunk = S // CHUNK - 1
    bps = nb // DILATIONS[branch]

    def body(q_ref, k_ref, v_ref, do_ref, st_ref, qn_ref, don_ref, stn_ref, dk_ref, dv_ref):
        hp = pl.program_id(0)
        n = pl.program_id(1)
        last =jnp.bitwise_and(n + 1, bps - 1) == 0
        qi, kj = _dil_masks()
        hl = lax.broadcasted_iota(jnp.int32, (CHUNK, LANES), 1)
        m_cur = kj <= qi
        m_next = (kj - qi) >= 0
        m_next_last = (kj - qi) >= jnp.where(last, 2 * CHUNK, 0)
        hms = (hl < HEAD_DIM, hl >= HEAD_DIM)
        qtiles = {}
        for t in range(nsub + 1):
            if t == nsub:
                qs, do_t, st_t = qn_ref[...], don_ref[...], stn_ref[...]
            else:
                rows = slice(t * CHUNK, (t + 1) * CHUNK)
                qs, do_t, st_t = q_ref[rows, :], do_ref[rows, :], st_ref[rows, :]
            for hh in range(2):
                qtiles[t, hh] = (
                    jnp.where(hms[hh], qs, jnp.zeros_like(qs)),
                    jnp.where(hms[hh], do_t, jnp.zeros_like(do_t)),
                    _lane_col(st_t, hl, H + 2 * hp + hh),
                    _lane_col(st_t, hl, 2 * hp + hh),
                )
        jobs = []
        for u in range(nsub):
            rows = slice(u * CHUNK, (u + 1) * CHUNK)
            kc, vc = k_ref[rows, :], v_ref[rows, :]
            for t, msk in ((u, m_cur), (u + 1, m_next_last if u == nsub - 1 else m_next)):
                for hh in range(2):
                    qh, dob, _, _ = qtiles[t, hh]
                    s = lax.dot_general(qh, kc, NT, preferred_element_type=F32)
                    dp = lax.dot_general(dob, vc, NT, preferred_element_type=F32)
                    jobs.append((u, t, hh, s, dp, msk))
        pds = []
        for u, t, hh, s, dp, msk in jobs:
            _, _, delta, lse = qtiles[t, hh]
            p = jnp.exp(jnp.where(msk, s, NEG) - lse)
            pds.append((p.astype(MXU), (p * (dp - delta)).astype(MXU)))
        dks, dvs = {}, {}
        for (u, t, hh, _, _, _), (pb, dsb) in zip(jobs, pds):
            qh, dob, _, _ = qtiles[t, hh]
            dv_p = lax.dot_general(pb, dob, TN, preferred_element_type=F32)
            dk_p = lax.dot_general(dsb, qh, TN, preferred_element_type=F32)
            dvs[u] = dv_p if u not in dvs else dvs[u] + dv_p
            dks[u] = dk_p if u not in dks else dks[u] + dk_p
        for u in range(nsub):
            dk_ref[u * CHUNK : (u + 1) * CHUNK, :] = dks[u].astype(MXU)
            dv_ref[u * CHUNK : (u + 1) * CHUNK, :] = dvs[u].astype(MXU)

    blk = pl.BlockSpec((tb, LANES), lambda h, n: (n, h))
    head = pl.BlockSpec((CHUNK, LANES), lambda h, n: (jnp.minimum((n + 1) * nsub, last_chunk), h))
    st_blk = pl.BlockSpec((tb, LANES), lambda h, n: (n, 0))
    st_head = pl.BlockSpec((CHUNK, LANES), lambda h, n: (jnp.minimum((n + 1) * nsub, last_chunk), 0))
    return pl.pallas_call(
        body,
        grid=(HP, nb),
        in_specs=[blk] * 4 + [st_blk, head, head, st_head],
        out_specs=[blk, blk],
        out_shape=[jax.ShapeDtypeStruct((S, D), MXU)] * 2,
        compiler_params=_params("parallel", "parallel"),
        name=name,
    )(q, k, v, do, stats, q, do, stats)


def _chip_peers():
    x, y, c = lax.axis_index("x"), lax.axis_index("y"), lax.axis_index("c")
    peers = [(1 - x, y, c), (x, 1 - y, c), (1 - x, 1 - y, c)]
    return x, y, c, peers


def _allgather_weights(wsh, *, name):
    R, D = wsh.shape

    def body(w_ref, out_ref, send_sems, recv_sems, local_sem):
        for stage in _gather_stages(w_ref, out_ref, send_sems, recv_sems, local_sem):
            stage()

    hbm = pl.BlockSpec(memory_space=pl.ANY)
    return pl.pallas_call(
        body,
        in_specs=[hbm],
        out_specs=hbm,
        out_shape=jax.ShapeDtypeStruct((N_CHIPS, R, D), wsh.dtype),
        scratch_shapes=GATHER_SEMS,
        name=name,
    )(wsh)


GATHER_SEMS = [pltpu.SemaphoreType.DMA((6,)), pltpu.SemaphoreType.DMA((6,)), pltpu.SemaphoreType.DMA]
SCATTER_SEMS = [pltpu.SemaphoreType.DMA((3,)), pltpu.SemaphoreType.DMA((3,))]


def _gather_stages(w_ref, out_ref, send_sems, recv_sems, local_sem):
    half = w_ref.shape[0] // 2
    x, y, c, peers = _chip_peers()
    me, sibling = 2 * x + y, (x, y, 1 - c)
    chips = [2 * p[0] + p[1] for p in peers]

    def rows(chip, hc):
        return out_ref.at[chip, pl.ds(hc * half, half), :]

    def copy(k, chip, hc, to, src=None):
        return pltpu.make_async_remote_copy(
            src_ref=rows(chip, hc) if src is None else src, dst_ref=rows(chip, hc),
            send_sem=send_sems.at[k], recv_sem=recv_sems.at[k], device_id=to, device_id_type=MESH,
        )

    mine = pltpu.make_async_copy(w_ref, out_ref.at[me], local_sem)
    first = [copy(j, me, c, p, src=w_ref.at[pl.ds(c * half, half), :]) for j, p in enumerate(peers)]
    passed = [copy(3 + j, chip, c, sibling) for j, chip in enumerate(chips)]

    def start():
        mine.start()
        for cp in first:
            cp.start()

    def hand_on():
        for j, chip in enumerate(chips):
            copy(j, chip, c, peers[j]).wait_recv()
            passed[j].start()

    def finish():
        for j, chip in enumerate(chips):
            copy(3 + j, chip, 1 - c, sibling).wait_recv()
        for cp in first + passed:
            cp.wait_send()
        mine.wait()

    return start, hand_on, finish


def _scatter_stages(g_ref, out_ref, send_sems, recv_sems):
    x, y, c, peers = _chip_peers()
    sends = [
        pltpu.make_async_remote_copy(
            src_ref=g_ref.at[2 * p[0] + p[1]], dst_ref=out_ref.at[j], send_sem=send_sems.at[j],
            recv_sem=recv_sems.at[j], device_id=p, device_id_type=MESH,
        )
        for j, p in enumerate(peers)
    ]

    def start():
        for cp in sends:
            cp.start()

    def finish():
        for j, p in enumerate(peers):
            pltpu.make_async_remote_copy(
                src_ref=out_ref.at[j], dst_ref=out_ref.at[j], send_sem=send_sems.at[j], recv_sem=recv_sems.at[j],
                device_id=p, device_id_type=MESH,
            ).wait_recv()
        for cp in sends:
            cp.wait_send()

    return start, finish


def _scatter_grads(gflat, *, name):
    _, R, D = gflat.shape

    def body(g_ref, out_ref, send_sems, recv_sems):
        for stage in _scatter_stages(g_ref, out_ref, send_sems, recv_sems):
            stage()

    hbm = pl.BlockSpec(memory_space=pl.ANY)
    return pl.pallas_call(
        body,
        in_specs=[hbm],
        out_specs=hbm,
        out_shape=jax.ShapeDtypeStruct((3, R, D), gflat.dtype),
        scratch_shapes=SCATTER_SEMS,
        name=name,
    )(gflat)


def _swap_with_sibling(part, *, name):
    def body(p_ref, out_ref, send_sem, recv_sem):
        x, y, c = lax.axis_index("x"), lax.axis_index("y"), lax.axis_index("c")
        cp = pltpu.make_async_remote_copy(
            src_ref=p_ref, dst_ref=out_ref, send_sem=send_sem, recv_sem=recv_sem,
            device_id=(x, y, 1 - c), device_id_type=MESH,
        )
        cp.start()
        cp.wait()

    hbm = pl.BlockSpec(memory_space=pl.ANY)
    return pl.pallas_call(
        body,
        in_specs=[hbm],
        out_specs=hbm,
        out_shape=jax.ShapeDtypeStruct(part.shape, part.dtype),
        scratch_shapes=[pltpu.SemaphoreType.DMA, pltpu.SemaphoreType.DMA],
        name=name,
    )(part)


def _gather_from_all(vec, *, name):
    R, D = vec.shape

    def body(v_ref, out_ref, send_sems, recv_sems):
        x, y, c = lax.axis_index("x"), lax.axis_index("y"), lax.axis_index("c")
        me = 4 * x + 2 * y + c
        out_ref[me] = v_ref[...]
        rel = [(j >> 2 & 1, j >> 1 & 1, j & 1) for j in range(1, 8)]
        peers = [((1 - x) if fx else x, (1 - y) if fy else y, (1 - c) if fc else c) for fx, fy, fc in rel]
        sends = [
            pltpu.make_async_remote_copy(
                src_ref=v_ref, dst_ref=out_ref.at[me], send_sem=send_sems.at[j], recv_sem=recv_sems.at[j],
                device_id=p, device_id_type=MESH,
            )
            for j, p in enumerate(peers)
        ]
        for cp in sends:
            cp.start()
        for j, p in enumerate(peers):
            theirs = out_ref.at[4 * p[0] + 2 * p[1] + p[2]]
            pltpu.make_async_remote_copy(
                src_ref=theirs, dst_ref=theirs, send_sem=send_sems.at[j], recv_sem=recv_sems.at[j],
                device_id=p, device_id_type=MESH,
            ).wait_recv()
        for cp in sends:
            cp.wait_send()

    vmem = pl.BlockSpec(memory_space=pltpu.VMEM)
    return pl.pallas_call(
        body,
        in_specs=[vmem],
        out_specs=vmem,
        out_shape=jax.ShapeDtypeStruct((8, R, D), vec.dtype),
        scratch_shapes=[pltpu.SemaphoreType.DMA((7,)), pltpu.SemaphoreType.DMA((7,))],
        name=name,
    )(vec)


def _sum4(a, rest, *, name):
    R, D = a.shape
    tm = _pick(R, (256, 128, 8))

    def body(a_ref, r_ref, o_ref):
        o_ref[...] = ((a_ref[...] + r_ref[0].astype(F32)) + r_ref[1].astype(F32)) + r_ref[2].astype(F32)

    row = pl.BlockSpec((tm, D), lambda i: (i, 0))
    return pl.pallas_call(
        body,
        grid=(R // tm,),
        in_specs=[row, pl.BlockSpec((3, tm, D), lambda i: (0, i, 0))],
        out_specs=row,
        out_shape=jax.ShapeDtypeStruct((R, D), F32),
        compiler_params=_params("parallel"),
        name=name,
    )(a, rest)


def _adamw(parts, w, m, v, *, name):
    P, R, D = parts.shape
    tm = _pick(R, (256, 128, 8))

    def body(p_ref, w_ref, m_ref, v_ref, g_ref, d_ref, nm_ref, nv_ref):
        g = p_ref[0]
        for k in range(1, P):
            g = g + p_ref[k]
        nm = ADAM_B1 * m_ref[...] + (1.0 - ADAM_B1) * g
        nv = ADAM_B2 * v_ref[...] + (1.0 - ADAM_B2) * jnp.square(g)
        m_hat = nm / (1.0 - ADAM_B1**ADAM_STEP)
        v_hat = nv / (1.0 - ADAM_B2**ADAM_STEP)
        g_ref[...] = g
        d_ref[...] = -ADAM_LR * (m_hat / (jnp.sqrt(v_hat) + ADAM_EPS) + ADAM_WD * w_ref[...])
        nm_ref[...] = nm
        nv_ref[...] = nv

    row = pl.BlockSpec((tm, D), lambda i: (i, 0))
    return pl.pallas_call(
        body,
        grid=(R // tm,),
        in_specs=[pl.BlockSpec((P, tm, D), lambda i: (0, i, 0)), row, row, row],
        out_specs=[row] * 4,
        out_shape=[jax.ShapeDtypeStruct((R, D), F32)] * 4,
        compiler_params=_params("parallel"),
        name=name,
    )(parts, w, m, v)


MATS = ("w_qkv", "w_o", "w_ff1", "w_ff2")
VECS = ("ln1_g", "ln1_b", "ln2_g", "ln2_b")
PARAM_ORDER = ("w_qkv", "w_o", "ln1_g", "ln1_b", "w_ff1", "w_ff2", "ln2_g", "ln2_b")


def _pack_rows(arrs, D):
    return jnp.concatenate([a.reshape(-1, D) for a in arrs], axis=0)


def kernel(x, w_qkv_0, w_o_0, ln1_g_0, ln1_b_0, w_ff1_0, w_ff2_0, ln2_g_0, ln2_b_0, w_qkv_1, w_o_1, ln1_g_1, ln1_b_1, w_ff1_1, w_ff2_1, ln2_g_1, ln2_b_1, loss_target, m_w_qkv_0, m_w_o_0, m_ln1_g_0, m_ln1_b_0, m_w_ff1_0, m_w_ff2_0, m_ln2_g_0, m_ln2_b_0, m_w_qkv_1, m_w_o_1, m_ln1_g_1, m_ln1_b_1, m_w_ff1_1, m_w_ff2_1, m_ln2_g_1, m_ln2_b_1, v_w_qkv_0, v_w_o_0, v_ln1_g_0, v_ln1_b_0, v_w_ff1_0, v_w_ff2_0, v_ln2_g_0, v_ln2_b_0, v_w_qkv_1, v_w_o_1, v_ln1_g_1, v_ln1_b_1, v_w_ff1_1, v_w_ff2_1, v_ln2_g_1, v_ln2_b_1):
    given = dict(locals())
    xs = x[0]
    target = loss_target[0]
    S, D = xs.shape
    C3 = 3 * D // N_CHIPS
    names = [f"{p}_{l}" for l in range(DEPTH) for p in PARAM_ORDER]
    mat_names = [f"{p}_{l}" for l in range(DEPTH) for p in MATS]
    vec_names = [f"{p}_{l}" for l in range(DEPTH) for p in VECS]
    rows = {"w_qkv": C3, "w_o": D // N_CHIPS, "w_ff1": D, "w_ff2": D}
    layer_rows = sum(rows.values())

    packed_w = _pack_rows([given[n] for n in mat_names], D).astype(MXU)
    offset, r0 = {}, 0
    for n in mat_names:
        offset[n] = r0
        r0 += rows[n[:-2]]

    def gathered_matrix(p, l, gathered, base):
        lo = offset[f"{p}_{l}"] - base
        t = gathered[:, lo : lo + rows[p], :]
        if p in ("w_qkv", "w_ff1"):
            return t.reshape(N_CHIPS, D, rows[p]).transpose(1, 0, 2).reshape(D, N_CHIPS * rows[p])
        return t.reshape(N_CHIPS * rows[p], D)

    short = {"w_qkv": "qkv", "w_o": "o", "w_ff1": "ff1", "w_ff2": "ff2"}
    first_gather = _allgather_weights(packed_w[:C3], name="allgather_w_qkv_0")
    W = [{"qkv": gathered_matrix("w_qkv", 0, first_gather, 0)}]
    tables = _rope_tables(S)

    h, hb = xs, xs.astype(MXU)
    saved = []
    for l in range(DEPTH):
        g1, b1, g2, b2 = (given[f"{p}_{l}"] for p in VECS)
        qkv = _mm_nn(hb, W[l]["qkv"], epi="qkv", extras=tables if l == 1 else (), name=f"qkv_{l}")
        st = dict(hb=hb, qkv=qkv)
        if l == 0:
            o, car, rest_gather = _sb_fwd(qkv, gather=packed_w[C3:], name="stickbreak_fwd")
            W.append({})
            for n in mat_names[1:]:
                W[int(n[-1])][short[n[:-2]]] = gathered_matrix(n[:-2], int(n[-1]), rest_gather, C3)
            st.update(car=car)
        else:
            q, k, v = qkv[:, :D], qkv[:, D : 2 * D], qkv[:, 2 * D :]
            qp, kp, vp = ([_perm(t, d) for d in DILATIONS] for t in (q, k, v))
            branches = [
                _dil_fwd(qp[i], kp[i], vp[i], branch=i, name=f"dilated_fwd_d{d}") for i, d in enumerate(DILATIONS)
            ]
            o, lse = _dil_combine(
                [_unperm(ob, d) for (ob, _), d in zip(branches, DILATIONS)],
                [_unperm(lb, d) for (_, lb), d in zip(branches, DILATIONS)],
                name="dilated_combine",
            )
            st.update(qp=qp, kp=kp, vp=vp, lse=lse)
        y1, x1, x1b = _mm_nn(o, W[l]["o"], epi="ln", extras=(h, g1, b1), name=f"attn_out_ln1_{l}")
        hp = _mm_nn(x1b, W[l]["ff1"], out_dtype=MXU, name=f"ff1_{l}")
        y2, x2, x2b = _mm_nn(hp, W[l]["ff2"], a_op="relu2", epi="ln", extras=(x1, g2, b2), name=f"ff2_ln2_{l}")
        st.update(o=o, y1=y1, x1b=x1b, hp=hp, y2=y2)
        saved.append(st)
        h, hb = x2, x2b

    grads = {}
    grad_blocks = lambda ns: jnp.concatenate([grads[n].reshape(N_CHIPS, -1, D) for n in ns], axis=1)
    top = saved[-1]
    dy2, dy2b, dg, db, loss_lanes = _ln_bwd_loss(
        top["y2"], given[f"ln2_g_{DEPTH - 1}"], given[f"ln2_b_{DEPTH - 1}"], target, name="loss_ln2_bwd"
    )
    loss = lax.psum(jnp.sum(loss_lanes) * (0.5 / D), ("x", "y", "c"))
    grads[f"ln2_g_{DEPTH - 1}"], grads[f"ln2_b_{DEPTH - 1}"] = dg, db
    grad_x = None
    for l in reversed(range(DEPTH)):
        st = saved[l]
        dhp = _mm_nt(dy2b, W[l]["ff2"], epi="relu2grad", extra=st["hp"], out_dtype=MXU, name=f"d_ff2_in_{l}")
        grads[f"w_ff2_{l}"] = _mm_tn(st["hp"], dy2b, a_op="relu2", groups=1, name=f"d_w_ff2_{l}")
        dx1 = _mm_nt(dhp, W[l]["ff1"], epi="residual", extra=dy2, out_dtype=F32, name=f"d_ff1_in_{l}")
        grads[f"w_ff1_{l}"] = _mm_tn(st["x1b"], dhp, a_op=None, groups=N_CHIPS, name=f"d_w_ff1_{l}")
        dy1, dy1b, grads[f"ln1_g_{l}"], grads[f"ln1_b_{l}"] = _ln_bwd(
            dx1, st["y1"], given[f"ln1_g_{l}"], name=f"ln1_bwd_{l}"
        )
        grads[f"w_o_{l}"] = _mm_tn(st["o"], dy1b, a_op=None, groups=1, name=f"d_w_o_{l}")
        if l == 0:
            do = _mm_nt(dy1b, W[l]["o"], epi=None, extra=None, out_dtype=MXU, name=f"d_attn_out_{l}")
            dq, dk, dv, from_chips_rest = _sb_bwd(
                st["qkv"], do, st["car"], scatter=grad_blocks(mat_names[1:]).astype(MXU), name="stickbreak_bwd"
            )
            dqkv = _qkv_bwd_post([dq], [dk], [dv], None, name=f"qkv_bwd_post_{l}")
        else:
            do = _mm_nt(dy1b, W[l]["o"], epi=None, extra=None, out_dtype=MXU, name=f"d_attn_out_{l}")
            stats = _head_stats(do, st["o"], st["lse"], name="dilated_head_stats")
            dqs, dks, dvs = [], [], []
            for i, d in enumerate(DILATIONS):
                ops = (st["qp"][i], st["kp"][i], st["vp"][i], _perm(do, d), _perm(stats, d))
                dqs.append(_unperm(_dil_bwd_dq(*ops, branch=i, name=f"dilated_bwd_dq_d{d}"), d))
                dk_d, dv_d = _dil_bwd_dkv(*ops, branch=i, name=f"dilated_bwd_dkv_d{d}")
                dks.append(_unperm(dk_d, d))
                dvs.append(_unperm(dv_d, d))
            dqkv = _qkv_bwd_post(dqs, dks, dvs, tables, name=f"qkv_bwd_post_{l}")
        dh = _mm_nt(dqkv, W[l]["qkv"], epi="residual", extra=dy1, out_dtype=F32, name=f"d_qkv_in_{l}")
        grads[f"w_qkv_{l}"] = _mm_tn(st["hb"], dqkv, a_op=None, groups=N_CHIPS, name=f"d_w_qkv_{l}")
        if l > 0:
            prev = saved[l - 1]
            dy2, dy2b, grads[f"ln2_g_{l - 1}"], grads[f"ln2_b_{l - 1}"] = _ln_bwd(
                dh, prev["y2"], given[f"ln2_g_{l - 1}"], name=f"ln2_bwd_{l - 1}"
            )
        else:
            grad_x = dh

    from_chips_first = _scatter_grads(grad_blocks(mat_names[:1]).astype(MXU), name="scatter_grads_w_qkv_0")
    me = 2 * lax.axis_index("x") + lax.axis_index("y")
    own = lax.dynamic_index_in_dim(grad_blocks(mat_names), me, 0, keepdims=False)
    from_chips = jnp.concatenate([from_chips_first, from_chips_rest], axis=1)
    part = _sum4(own, from_chips, name="sum_chips")
    sib = _swap_with_sibling(part, name="swap_sibling")
    pack = lambda prefix: _pack_rows([given[prefix + n] for n in mat_names], D)
    outs_m = _adamw(jnp.stack([part, sib]), pack(""), pack("m_"), pack("v_"), name="adamw_matrices")
    vflat = _pack_rows([grads[n] for n in vec_names], D)
    packv = lambda prefix: _pack_rows([given[prefix + n] for n in vec_names], D)
    outs_v = _adamw(_gather_from_all(vflat, name="gather_ln_grads"), packv(""), packv("m_"), packv("v_"), name="adamw_vectors")

    def unpack(flat_m, flat_v):
        out, r0 = {}, 0
        for n in mat_names:
            r = given[n].size // D
            out[n] = flat_m[r0 : r0 + r].reshape(given[n].shape)
            r0 += r
        for i, n in enumerate(vec_names):
            out[n] = flat_v[i]
        return [out[n] for n in names]

    result = [loss, grad_x[None]]
    for k in range(4):
        result += unpack(outs_m[k], outs_v[k])
    return tuple(result)
```

```python
import functools

import jax
import jax.numpy as jnp
from jax import lax
from jax.experimental import pallas as pl
from jax.experimental.pallas import tpu as pltpu

F32 = jnp.float32
MXU = jnp.bfloat16

HEAD_DIM = 64
LANES = 128
CHUNK = 128
ROPE_THETA = 500000.0
ROPE_HALF = 8
DILATIONS = (1, 4, 16)
DEPTH = 2
ALPHA = (2 * DEPTH) ** 0.25
LN_EPS = 1e-5
QK_SCALE = 0.125
ADAM_LR, ADAM_B1, ADAM_B2, ADAM_EPS, ADAM_WD, ADAM_STEP = 0.001, 0.9, 0.999, 1e-08, 0.01, 10
NEG = -1e30
UNDERFLOW = -104.0
SB_TQ = 2 * CHUNK
V7X_VMEM_LIMIT = 56 * 1024 * 1024
MESH = pl.DeviceIdType.MESH
NT = (((1,), (1,)), ((), ()))
TN = (((0,), (0,)), ((), ()))
N_CHIPS = 4
MM_ROWS = (1024, 512, 256, 128)


def _params(*sem):
    return pltpu.CompilerParams(dimension_semantics=sem, vmem_limit_bytes=V7X_VMEM_LIMIT)


def _pick(n, cands=(1024, 768, 512, 384, 256, 128)):
    for c in cands:
        if n % c == 0:
            return c
    raise ValueError(f"no tile for {n}")


def _mm_nn(a, b, *, a_op=None, epi=None, extras=(), out_dtype=F32, name):
    M, K = a.shape
    Kb, N = b.shape
    assert K == Kb
    tn = N // 3 if epi == "qkv" else (N if epi == "ln" else _pick(N))
    tm = _pick(M, MM_ROWS) if (K <= 1024 and epi != "ln") else _pick(M, MM_ROWS[1:])
    rope = epi == "qkv" and len(extras) == 3

    def body(a_ref, b_ref, *refs):
        av = a_ref[...]
        if a_op == "relu2":
            av = jnp.square(jnp.maximum(av.astype(F32), 0.0))
        part = jnp.dot(av.astype(MXU), b_ref[...], preferred_element_type=F32)
        if epi is None:
            refs[0][...] = part.astype(out_dtype)
        elif epi == "qkv":
            o_ref = refs[-1]
            tabs = tuple(r[...] for r in refs[:-1])
            j = pl.program_id(1)

            def write(rotate, scale):
                for blk in range(tn // LANES):
                    cols = slice(blk * LANES, (blk + 1) * LANES)
                    xb = part[:, cols]
                    if rotate:
                        xb = _rope(xb, *tabs)
                    if scale:
                        xb = xb * QK_SCALE
                    o_ref[:, cols] = xb.astype(MXU)

            pl.when(j == 0)(lambda: write(rope, True))
            pl.when(j == 1)(lambda: write(rope, False))
            pl.when(j == 2)(lambda: write(False, False))
        else:
            x_ref, g_ref, be_ref, y_ref, o_ref, ob_ref = refs
            y = ALPHA * x_ref[...] + part
            xhat, _ = _ln_stats(y)
            out = xhat * g_ref[...] + be_ref[...]
            y_ref[...] = y
            o_ref[...] = out
            ob_ref[...] = out.astype(MXU)

    in_specs = [pl.BlockSpec((tm, K), lambda i, j: (i, 0)), pl.BlockSpec((K, tn), lambda i, j: (0, j))]
    tile = pl.BlockSpec((tm, tn), lambda i, j: (i, j))
    ops = [a, b]
    if epi == "qkv":
        in_specs += [pl.BlockSpec((tm, LANES), lambda i, j: (i, 0))] * len(extras)
        ops += list(extras)
        out_specs, out_shape = tile, jax.ShapeDtypeStruct((M, N), MXU)
    elif epi == "ln":
        x, g, be = extras
        in_specs += [tile, pl.BlockSpec((1, N), lambda i, j: (0, 0)), pl.BlockSpec((1, N), lambda i, j: (0, 0))]
        ops += [x, g.reshape(1, N), be.reshape(1, N)]
        out_specs = [tile] * 3
        out_shape = [jax.ShapeDtypeStruct((M, N), F32)] * 2 + [jax.ShapeDtypeStruct((M, N), MXU)]
    else:
        out_specs, out_shape = tile, jax.ShapeDtypeStruct((M, N), out_dtype)
    return pl.pallas_call(
        body,
        grid=(M // tm, N // tn),
        in_specs=in_specs,
        out_specs=out_specs,
        out_shape=out_shape,
        compiler_params=_params("parallel", "parallel"),
        name=name,
    )(*ops)


def _mm_nt(a, b, *, epi, extra, out_dtype, name):
    M, N = a.shape
    K, Nb = b.shape
    assert N == Nb
    tm = _pick(M, MM_ROWS) if N <= 1024 else _pick(M, MM_ROWS[1:])
    tko = _pick(K)

    def body(*refs):
        if epi is None:
            a_ref, b_ref, o_ref = refs
        else:
            a_ref, b_ref, e_ref, o_ref = refs
        r = lax.dot_general(a_ref[...].astype(MXU), b_ref[...], NT, preferred_element_type=F32)
        if epi == "relu2grad":
            r = r * (2.0 * jnp.maximum(e_ref[...].astype(F32), 0.0))
        elif epi == "residual":
            r = ALPHA * e_ref[...] + r
        o_ref[...] = r.astype(out_dtype)

    in_specs = [pl.BlockSpec((tm, N), lambda i, j: (i, 0)), pl.BlockSpec((tko, N), lambda i, j: (j, 0))]
    ops = [a, b]
    if epi is not None:
        in_specs.append(pl.BlockSpec((tm, tko), lambda i, j: (i, j)))
        ops.append(extra)
    return pl.pallas_call(
        body,
        grid=(M // tm, K // tko),
        in_specs=in_specs,
        out_specs=pl.BlockSpec((tm, tko), lambda i, j: (i, j)),
        out_shape=jax.ShapeDtypeStruct((M, K), out_dtype),
        compiler_params=_params("parallel", "parallel"),
        name=name,
    )(*ops)


def _mm_tn(a, b, *, a_op, groups, name):
    M, Ka = a.shape
    Mb, N = b.shape
    assert M == Mb
    Nc = N // groups
    tka, tn, tm = _pick(Ka), _pick(Nc), _pick(M, (2048, 1024, 512, 256, 128))
    nbs, nm = Nc // tn, M // tm

    def body(a_ref, b_ref, o_ref, acc_ref):
        kk = pl.program_id(2)
        av = a_ref[...]
        if a_op == "relu2":
            av = jnp.square(jnp.maximum(av.astype(F32), 0.0))
        part = lax.dot_general(av.astype(MXU), b_ref[...].astype(MXU), TN, preferred_element_type=F32)

        @pl.when(kk == 0)
        def _():
            acc_ref[...] = part

        @pl.when(kk > 0)
        def _():
            acc_ref[...] += part

        @pl.when(kk == nm - 1)
        def _():
            o_ref[...] = acc_ref[...]

    return pl.pallas_call(
        body,
        grid=(Ka // tka, N // tn, nm),
        in_specs=[
            pl.BlockSpec((tm, tka), lambda i, j, k: (k, i)),
            pl.BlockSpec((tm, tn), lambda i, j, k: (k, j)),
        ],
        out_specs=pl.BlockSpec((None, tka, tn), lambda i, j, k: (j // nbs, i, j % nbs)),
        out_shape=jax.ShapeDtypeStruct((groups, Ka, Nc), F32),
        scratch_shapes=[pltpu.VMEM((tka, tn), F32)],
        compiler_params=_params("parallel", "parallel", "arbitrary"),
        name=name,
    )(a, b)


def _ln_stats(y):
    mu = jnp.mean(y, axis=-1, keepdims=True)
    yc = y - mu
    var = jnp.mean(yc * yc, axis=-1, keepdims=True)
    rstd = lax.rsqrt(var + LN_EPS)
    return yc * rstd, rstd


def _ln_bwd(dx, y, g, *, name):
    S, D = y.shape
    tm = _pick(S, (256, 128))

    def body(dx_ref, y_ref, g_ref, dy_ref, dyb_ref, dg_ref, db_ref):
        xhat, rstd = _ln_stats(y_ref[...])
        dx_v = dx_ref[...]
        dxh = dx_v * g_ref[...]
        m1 = jnp.mean(dxh, axis=-1, keepdims=True)
        m2 = jnp.mean(dxh * xhat, axis=-1, keepdims=True)
        dy = rstd * (dxh - m1 - xhat * m2)
        dy_ref[...] = dy
        dyb_ref[...] = dy.astype(MXU)

        @pl.when(pl.program_id(0) == 0)
        def _():
            dg_ref[...] = jnp.zeros_like(dg_ref)
            db_ref[...] = jnp.zeros_like(db_ref)

        dg_ref[...] += jnp.sum(dx_v * xhat, axis=0, keepdims=True)
        db_ref[...] += jnp.sum(dx_v, axis=0, keepdims=True)

    row = pl.BlockSpec((tm, D), lambda i: (i, 0))
    vec = pl.BlockSpec((1, D), lambda i: (0, 0))
    return pl.pallas_call(
        body,
        grid=(S // tm,),
        in_specs=[row, row, vec],
        out_specs=[row, row, vec, vec],
        out_shape=[jax.ShapeDtypeStruct((S, D), F32), jax.ShapeDtypeStruct((S, D), MXU)]
        + [jax.ShapeDtypeStruct((1, D), F32)] * 2,
        compiler_params=_params("arbitrary"),
        name=name,
    )(dx, y, g.reshape(1, D))


def _ln_bwd_loss(y, g, b, target, *, name):
    S, D = y.shape
    tm = _pick(S, (256, 128))

    def body(y_ref, g_ref, b_ref, t_ref, dy_ref, dyb_ref, dg_ref, db_ref, ls_ref):
        xhat, rstd = _ln_stats(y_ref[...])
        err = xhat * g_ref[...] + b_ref[...] - t_ref[...]
        dx_v = err * (1.0 / D)
        dxh = dx_v * g_ref[...]
        m1 = jnp.mean(dxh, axis=-1, keepdims=True)
        m2 = jnp.mean(dxh * xhat, axis=-1, keepdims=True)
        dy = rstd * (dxh - m1 - xhat * m2)
        dy_ref[...] = dy
        dyb_ref[...] = dy.astype(MXU)

        @pl.when(pl.program_id(0) == 0)
        def _():
            dg_ref[...] = jnp.zeros_like(dg_ref)
            db_ref[...] = jnp.zeros_like(db_ref)
            ls_ref[...] = jnp.zeros_like(ls_ref)

        dg_ref[...] += jnp.sum(dx_v * xhat, axis=0, keepdims=True)
        db_ref[...] += jnp.sum(dx_v, axis=0, keepdims=True)
        ls_ref[...] += jnp.sum(err * err, axis=0, keepdims=True)

    row = pl.BlockSpec((tm, D), lambda i: (i, 0))
    vec = pl.BlockSpec((1, D), lambda i: (0, 0))
    return pl.pallas_call(
        body,
        grid=(S // tm,),
        in_specs=[row, vec, vec, row],
        out_specs=[row, row, vec, vec, vec],
        out_shape=[jax.ShapeDtypeStruct((S, D), F32), jax.ShapeDtypeStruct((S, D), MXU)]
        + [jax.ShapeDtypeStruct((1, D), F32)] * 3,
        compiler_params=_params("arbitrary"),
        name=name,
    )(y, g.reshape(1, D), b.reshape(1, D), target)


def _rope_tables(S):
    inv_freq = ROPE_THETA ** (-jnp.arange(ROPE_HALF, dtype=F32) / ROPE_HALF)
    ang = jnp.arange(S, dtype=jnp.int32).astype(F32)[:, None] * inv_freq[None, :]
    cos, sin = jnp.cos(ang), jnp.sin(ang)
    ones = jnp.ones((S, HEAD_DIM - 2 * ROPE_HALF), F32)
    zeros8 = jnp.zeros((S, ROPE_HALF), F32)
    c = jnp.concatenate([cos, cos, ones], axis=1)
    s1 = jnp.concatenate([zeros8, sin, 0.0 * ones], axis=1)
    s2 = jnp.concatenate([-sin, zeros8, 0.0 * ones], axis=1)
    return tuple(jnp.concatenate([t, t], axis=1) for t in (c, s1, s2))


def _rope(xb, c, s1, s2):
    return xb * c + pltpu.roll(xb, ROPE_HALF, 1) * s1 + pltpu.roll(xb, LANES - ROPE_HALF, 1) * s2


def _rope_t(dyb, c, s1, s2):
    return dyb * c + pltpu.roll(dyb * s1, LANES - ROPE_HALF, 1) + pltpu.roll(dyb * s2, ROPE_HALF, 1)


def _qkv_bwd_post(dqs, dks, dvs, tables, *, name):
    S, D = dqs[0].shape
    tm = _pick(S, (256, 128))
    nb = len(dqs)
    rope = tables is not None

    def body(*refs):
        ins, o_ref = refs[:-1], refs[-1]
        groups = [ins[0:nb], ins[nb : 2 * nb], ins[2 * nb : 3 * nb]]
        tabs = [r[...] for r in ins[3 * nb :]]
        for j, grp in enumerate(groups):
            for blk in range(D // LANES):
                cols = slice(blk * LANES, (blk + 1) * LANES)
                v = grp[0][:, cols].astype(F32)
                for r in grp[1:]:
                    v = v + r[:, cols].astype(F32)
                if rope and j < 2:
                    v = _rope_t(v, *tabs)
                if j == 0:
                    v = v * QK_SCALE
                o_ref[:, j * D + blk * LANES : j * D + (blk + 1) * LANES] = v.astype(MXU)

    row = pl.BlockSpec((tm, D), lambda i: (i, 0))
    tab = pl.BlockSpec((tm, LANES), lambda i: (i, 0))
    return pl.pallas_call(
        body,
        grid=(S // tm,),
        in_specs=[row] * (3 * nb) + ([tab] * 3 if rope else []),
        out_specs=pl.BlockSpec((tm, 3 * D), lambda i: (i, 0)),
        out_shape=jax.ShapeDtypeStruct((S, 3 * D), MXU),
        compiler_params=_params("parallel"),
        name=name,
    )(*dqs, *dks, *dvs, *(tables if rope else ()))


def _log_fail(z):
    return -(jnp.maximum(z, 0.0) + jnp.log(1.0 + jnp.exp(-jnp.abs(z))))


def _sb_fwd(qkv, *, gather=None, name):
    S, D = qkv.shape[0], qkv.shape[1] // 3
    HP, H = D // LANES, D // HEAD_DIM
    tq = SB_TQ
    assert S % tq == 0
    cpb = tq // CHUNK
    assert S // CHUNK <= LANES
    steps = HP * (S // tq)

    def body(*refs):
        if gather is None:
            q_ref, k_ref, v_ref, o_ref, car_ref, acc_ref, carry0_ref, carry1_ref = refs
        else:
            q_ref, k_ref, v_ref, w_ref, o_ref, car_ref, g_ref, acc_ref, carry0_ref, carry1_ref = refs[:10]
            start, hand_on, finish = _gather_stages(w_ref, g_ref, *refs[10:])
            t = pl.program_id(0) * (S // tq) + pl.program_id(1)
            pl.when(t == 0)(start)
            pl.when(t == steps // 2)(hand_on)
        i = pl.program_id(1)
        lane = lax.broadcasted_iota(jnp.int32, (tq, LANES), 1)
        row = lax.broadcasted_iota(jnp.int32, (tq, LANES), 0)
        kl = lax.broadcasted_iota(jnp.int32, (CHUNK, LANES), 1)
        kr = lax.broadcasted_iota(jnp.int32, (CHUNK, LANES), 0)
        tri = (kr >= kl).astype(MXU)
        qv = q_ref[...]
        qh = (jnp.where(lane < HEAD_DIM, qv, jnp.zeros_like(qv)), jnp.where(lane >= HEAD_DIM, qv, jnp.zeros_like(qv)))
        carry_refs = (carry0_ref, carry1_ref)
        acc_ref[...] = jnp.zeros_like(acc_ref)
        car_ref[...] = jnp.full((2, tq, LANES), NEG, F32)
        for r in carry_refs:
            r[...] = jnp.zeros_like(r)

        def run(chunks, masked):
            offs = [pl.multiple_of(c * CHUNK, CHUNK) for c in chunks]
            kcs = [k_ref[pl.ds(o, CHUNK), :] for o in offs]
            vcs = [v_ref[pl.ds(o, CHUNK), :] for o in offs]
            jobs = [(n, hh) for n in range(len(chunks)) for hh in range(2)]
            zs = {j: lax.dot_general(qh[j[1]], kcs[j[0]], NT, preferred_element_type=F32) for j in jobs}
            if masked:
                cms = [(o + lane) < (i * tq + row) for o in offs]
            lfs = {}
            for j in jobs:
                lf = _log_fail(zs[j])
                lfs[j] = jnp.where(cms[j[0]], lf, 0.0) if masked else lf
            rs = {j: jnp.dot(lfs[j].astype(MXU), tri, preferred_element_type=F32) for j in jobs}
            ws = {}
            for hh in range(2):
                carry = carry_refs[hh][...]
                tile = car_ref[hh]
                for n, c in enumerate(chunks):
                    w = jnp.exp(zs[n, hh] + rs[n, hh] + carry)
                    if masked:
                        w = jnp.where(cms[n], w, 0.0)
                    ws[n, hh] = w.astype(MXU)
                    tile = jnp.where(lane == c, carry, tile)
                    carry = carry + jnp.sum(lfs[n, hh], axis=-1, keepdims=True)
                car_ref[hh] = tile
                carry_refs[hh][...] = carry
            for hh in range(2):
                pv = jnp.dot(ws[0, hh], vcs[0], preferred_element_type=F32)
                for n in range(1, len(chunks)):
                    pv = pv + jnp.dot(ws[n, hh], vcs[n], preferred_element_type=F32)
                acc_ref[hh] += pv

        def max_carry():
            return jnp.maximum(jnp.max(carry0_ref[...]), jnp.max(carry1_ref[...]))

        run([i * cpb + u for u in reversed(range(cpb))], True)

        def cond(st):
            return jnp.logical_and(st[0] >= 0, st[1] >= UNDERFLOW)

        def pair(st):
            c = st[0]
            run([c, c - 1], False)
            return c - 2, max_carry()

        lax.while_loop(cond, pair, (i * cpb - 1, max_carry()))
        o_ref[...] = jnp.where(lane < HEAD_DIM, acc_ref[0], acc_ref[1]).astype(o_ref.dtype)
        if gather is not None:
            pl.when(t == steps - 1)(finish)

    blk = pl.BlockSpec((tq, LANES), lambda h, i: (i, h))
    k_full = pl.BlockSpec((S, LANES), lambda h, i: (0, HP + h))
    v_full = pl.BlockSpec((S, LANES), lambda h, i: (0, 2 * HP + h))
    hbm = pl.BlockSpec(memory_space=pl.ANY)
    in_specs, ops = [blk, k_full, v_full], [qkv, qkv, qkv]
    out_specs = [blk, pl.BlockSpec((2, tq, LANES), lambda h, i: (h, i, 0))]
    out_shape = [jax.ShapeDtypeStruct((S, D), MXU), jax.ShapeDtypeStruct((H, S, LANES), F32)]
    scratch = [pltpu.VMEM((2, tq, LANES), F32), pltpu.VMEM((tq, 1), F32), pltpu.VMEM((tq, 1), F32)]
    if gather is not None:
        in_specs, ops = in_specs + [hbm], ops + [gather]
        out_specs = out_specs + [hbm]
        out_shape = out_shape + [jax.ShapeDtypeStruct((N_CHIPS,) + gather.shape, gather.dtype)]
        scratch = scratch + GATHER_SEMS
    return pl.pallas_call(
        body,
        grid=(HP, S // tq),
        in_specs=in_specs,
        out_specs=out_specs,
        out_shape=out_shape,
        scratch_shapes=scratch,
        compiler_params=_params("arbitrary", "arbitrary"),
        name=name,
    )(*ops)


def _sb_bwd(qkv, do, car, *, scatter=None, name):
    S, D = qkv.shape[0], qkv.shape[1] // 3
    HP = D // LANES
    tq = SB_TQ
    assert S % tq == 0
    cpb = tq // CHUNK
    nq = S // tq

    def body(*refs):
        if scatter is None:
            q_ref, k_ref, v_ref, do_ref, car_ref, dq_ref, dk_hbm, dv_hbm = refs[:8]
            dq_acc, dk_acc, dv_acc, gcar0_ref, gcar1_ref, sem = refs[8:]
        else:
            q_ref, k_ref, v_ref, do_ref, car_ref, g_ref, dq_ref, dk_hbm, dv_hbm, r_ref = refs[:10]
            dq_acc, dk_acc, dv_acc, gcar0_ref, gcar1_ref, sem = refs[10:16]
            start, finish = _scatter_stages(g_ref, r_ref, *refs[16:])
            t = pl.program_id(0) * nq + pl.program_id(1)
            pl.when(t == 0)(start)
        hp = pl.program_id(0)
        i = pl.program_id(1)
        lane = lax.broadcasted_iota(jnp.int32, (tq, LANES), 1)
        row = lax.broadcasted_iota(jnp.int32, (tq, LANES), 0)
        kl = lax.broadcasted_iota(jnp.int32, (CHUNK, LANES), 1)
        kr = lax.broadcasted_iota(jnp.int32, (CHUNK, LANES), 0)
        tri = (kr >= kl).astype(MXU)
        tri_prefix = (kr <= kl).astype(MXU)

        @pl.when(i == 0)
        def _():
            dk_acc[...] = jnp.zeros_like(dk_acc)
            dv_acc[...] = jnp.zeros_like(dv_acc)

        qv = q_ref[...]
        dov = do_ref[...]
        qh = (jnp.where(lane < HEAD_DIM, qv, jnp.zeros_like(qv)), jnp.where(lane >= HEAD_DIM, qv, jnp.zeros_like(qv)))
        doh = (jnp.where(lane < HEAD_DIM, dov, jnp.zeros_like(dov)), jnp.where(lane >= HEAD_DIM, dov, jnp.zeros_like(dov)))
        gcar_refs = (gcar0_ref, gcar1_ref)
        dq_acc[...] = jnp.zeros_like(dq_acc)
        for r in gcar_refs:
            r[...] = jnp.zeros_like(r)

        def run(chunks, masked):
            offs = [pl.multiple_of(c * CHUNK, CHUNK) for c in chunks]
            kcs = [k_ref[pl.ds(o, CHUNK), :] for o in offs]
            vcs = [v_ref[pl.ds(o, CHUNK), :] for o in offs]
            jobs = [(n, hh) for n in range(len(chunks)) for hh in range(2)]
            if masked:
                cms = [(o + lane) < (i * tq + row) for o in offs]
            zs = {j: lax.dot_general(qh[j[1]], kcs[j[0]], NT, preferred_element_type=F32) for j in jobs}
            dws = {j: lax.dot_general(doh[j[1]], vcs[j[0]], NT, preferred_element_type=F32) for j in jobs}
            lfs, sigs = {}, {}
            for j in jobs:
                lf = _log_fail(zs[j])
                sigs[j] = jnp.exp(zs[j] + lf)
                lfs[j] = jnp.where(cms[j[0]], lf, 0.0) if masked else lf
            rs = {j: jnp.dot(lfs[j].astype(MXU), tri, preferred_element_type=F32) for j in jobs}
            ws, gs = {}, {}
            for hh in range(2):
                tile = car_ref[hh]
                for n, c in enumerate(chunks):
                    carry = jnp.sum(jnp.where(lane == c, tile, 0.0), axis=-1, keepdims=True)
                    w = jnp.exp(zs[n, hh] + rs[n, hh] + carry)
                    if masked:
                        w = jnp.where(cms[n], w, 0.0)
                    ws[n, hh] = w.astype(MXU)
                    gs[n, hh] = w * dws[n, hh]
            big_gs = {j: jnp.dot(gs[j].astype(MXU), tri_prefix, preferred_element_type=F32) for j in jobs}
            dzs = {}
            for hh in range(2):
                gcar = gcar_refs[hh][...]
                for n in range(len(chunks)):
                    dz = gs[n, hh] - sigs[n, hh] * (big_gs[n, hh] + gcar)
                    if masked:
                        dz = jnp.where(cms[n], dz, 0.0)
                    dzs[n, hh] = dz.astype(MXU)
                    gcar = gcar + jnp.sum(gs[n, hh], axis=-1, keepdims=True)
                gcar_refs[hh][...] = gcar
            for hh in range(2):
                part = jnp.dot(dzs[0, hh], kcs[0], preferred_element_type=F32)
                for n in range(1, len(chunks)):
                    part = part + jnp.dot(dzs[n, hh], kcs[n], preferred_element_type=F32)
                dq_acc[hh] += part
            for n, o in enumerate(offs):
                dk_acc[pl.ds(o, CHUNK), :] += lax.dot_general(
                    dzs[n, 0], qh[0], TN, preferred_element_type=F32
                ) + lax.dot_general(dzs[n, 1], qh[1], TN, preferred_element_type=F32)
                dv_acc[pl.ds(o, CHUNK), :] += lax.dot_general(
                    ws[n, 0], doh[0], TN, preferred_element_type=F32
                ) + lax.dot_general(ws[n, 1], doh[1], TN, preferred_element_type=F32)

        colmax = jnp.maximum(jnp.max(car_ref[0], axis=0, keepdims=True), jnp.max(car_ref[1], axis=0, keepdims=True))
        lane1 = lax.broadcasted_iota(jnp.int32, (1, LANES), 1)
        dead = jnp.logical_and(colmax < UNDERFLOW, lane1 < i * cpb)
        first_pair = jnp.sum(jnp.where(dead, 1.0, 0.0)).astype(jnp.int32) // 2

        def loop_body(p, carry):
            run([2 * p, 2 * p + 1], False)
            return carry

        lax.fori_loop(first_pair, i * cpb // 2, loop_body, 0)
        run([i * cpb + u for u in range(cpb)], True)
        dq_ref[...] = jnp.where(lane < HEAD_DIM, dq_acc[0], dq_acc[1])

        @pl.when(i == nq - 1)
        def _():
            cols = pl.ds(pl.multiple_of(hp * LANES, LANES), LANES)
            for src, dst in ((dk_acc, dk_hbm), (dv_acc, dv_hbm)):
                cp = pltpu.make_async_copy(src, dst.at[:, cols], sem)
                cp.start()
                cp.wait()

        if scatter is not None:
            pl.when(t == HP * nq - 1)(finish)

    blk = pl.BlockSpec((tq, LANES), lambda h, i: (i, h))
    k_full = pl.BlockSpec((S, LANES), lambda h, i: (0, HP + h))
    v_full = pl.BlockSpec((S, LANES), lambda h, i: (0, 2 * HP + h))
    hbm = pl.BlockSpec(memory_space=pl.ANY)
    in_specs = [blk, k_full, v_full, blk, pl.BlockSpec((2, tq, LANES), lambda h, i: (h, i, 0))]
    ops = [qkv, qkv, qkv, do, car]
    out_specs, out_shape = [blk, hbm, hbm], [jax.ShapeDtypeStruct((S, D), F32)] * 3
    scratch = [
        pltpu.VMEM((2, tq, LANES), F32),
        pltpu.VMEM((S, LANES), F32),
        pltpu.VMEM((S, LANES), F32),
        pltpu.VMEM((tq, 1), F32),
        pltpu.VMEM((tq, 1), F32),
        pltpu.SemaphoreType.DMA,
    ]
    if scatter is not None:
        in_specs, ops = in_specs + [hbm], ops + [scatter]
        out_specs = out_specs + [hbm]
        out_shape = out_shape + [jax.ShapeDtypeStruct((3,) + scatter.shape[1:], scatter.dtype)]
        scratch = scratch + SCATTER_SEMS
    return pl.pallas_call(
        body,
        grid=(HP, nq),
        in_specs=in_specs,
        out_specs=out_specs,
        out_shape=out_shape,
        scratch_shapes=scratch,
        compiler_params=_params("arbitrary", "arbitrary"),
        name=name,
    )(*ops)


def _perm(x, d):
    if d == 1:
        return x
    S, D = x.shape
    return x.reshape(S // d, d, D).transpose(1, 0, 2).reshape(S, D)


def _unperm(x, d):
    if d == 1:
        return x
    S, D = x.shape
    return x.reshape(d, S // d, D).transpose(1, 0, 2).reshape(S, D)


def _dil_tb(S):
    return min(1024, S // DILATIONS[-1])


def _dil_masks():
    qi = lax.broadcasted_iota(jnp.int32, (CHUNK, CHUNK), 0)
    kj = lax.broadcasted_iota(jnp.int32, (CHUNK, CHUNK), 1)
    return qi, kj


def _dil_fwd(q, k, v, *, branch, name):
    S, D = q.shape
    HP = D // LANES
    tb = _dil_tb(S)
    nb, nsub = S // tb, tb // CHUNK
    bps = nb // DILATIONS[branch]

    def body(q_ref, k_ref, v_ref, kp_ref, vp_ref, o_ref, l_ref):
        n = pl.program_id(0)
        hp = pl.program_id(1)

        @pl.when(hp == 0)
        def _():
            l_ref[...] = jnp.zeros_like(l_ref)

        first = jnp.bitwise_and(n, bps - 1) == 0
        qi, kj = _dil_masks()
        hl = lax.broadcasted_iota(jnp.int32, (CHUNK, LANES), 1)
        m_cur = kj <= qi
        m_prev = (kj - qi) >= 0
        m_prev0 = (kj - qi) >= jnp.where(first, 2 * CHUNK, 0)
        hms = (hl < HEAD_DIM, hl >= HEAD_DIM)
        tiles, scores = [], {}
        for u in range(nsub):
            rows = slice(u * CHUNK, (u + 1) * CHUNK)
            qs, kc, vc = q_ref[rows, :], k_ref[rows, :], v_ref[rows, :]
            if u == 0:
                kp, vp, pm = kp_ref[...], vp_ref[...], m_prev0
            else:
                prev = slice((u - 1) * CHUNK, u * CHUNK)
                kp, vp, pm = k_ref[prev, :], v_ref[prev, :], m_prev
            tiles.append((rows, vp, vc, pm))
            for hh in range(2):
                qh = jnp.where(hms[hh], qs, jnp.zeros_like(qs))
                scores[u, hh] = (
                    lax.dot_general(qh, kp, NT, preferred_element_type=F32),
                    lax.dot_general(qh, kc, NT, preferred_element_type=F32),
                )
        probs = {}
        for (u, hh), (s_p, s_c) in scores.items():
            s_p = jnp.where(tiles[u][3], s_p, NEG)
            s_c = jnp.where(m_cur, s_c, NEG)
            m = jnp.max(jnp.maximum(s_p, s_c), axis=-1, keepdims=True)
            p_p, p_c = jnp.exp(s_p - m), jnp.exp(s_c - m)
            den = jnp.sum(p_p + p_c, axis=-1, keepdims=True)
            probs[u, hh] = (p_p.astype(MXU), p_c.astype(MXU), 1.0 / den, m + jnp.log(den))
        for u, (rows, vp, vc, _) in enumerate(tiles):
            o_t, l_t = None, l_ref[rows, :]
            for hh in range(2):
                p_p, p_c, inv, lse = probs[u, hh]
                num = jnp.dot(p_p, vp, preferred_element_type=F32) + jnp.dot(p_c, vc, preferred_element_type=F32)
                o_h = num * inv
                o_t = o_h if hh == 0 else jnp.where(hms[1], o_h, o_t)
                l_t = jnp.where(hl == 2 * hp + hh, lse, l_t)
            o_ref[rows, :] = o_t
            l_ref[rows, :] = l_t

    blk = pl.BlockSpec((tb, LANES), lambda n, h: (n, h))
    tail = pl.BlockSpec((CHUNK, LANES), lambda n, h: (jnp.maximum(n * nsub - 1, 0), h))
    return pl.pallas_call(
        body,
        grid=(nb, HP),
        in_specs=[blk, blk, blk, tail, tail],
        out_specs=[blk, pl.BlockSpec((tb, LANES), lambda n, h: (n, 0))],
        out_shape=[jax.ShapeDtypeStruct((S, D), F32), jax.ShapeDtypeStruct((S, LANES), F32)],
        compiler_params=_params("parallel", "arbitrary"),
        name=name,
    )(q, k, v, k, v)


def _dil_combine(os, ls, *, name):
    S, D = os[0].shape
    tm = _pick(S, (256, 128))

    def body(o0_ref, o1_ref, o2_ref, l0_ref, l1_ref, l2_ref, out_ref, lse_ref):
        l0, l1, l2 = l0_ref[...], l1_ref[...], l2_ref[...]
        m = jnp.maximum(jnp.maximum(l0, l1), l2)
        a0, a1, a2 = jnp.exp(l0 - m), jnp.exp(l1 - m), jnp.exp(l2 - m)
        den = a0 + a1 + a2
        lse_ref[...] = m + jnp.log(den)
        inv = 1.0 / den
        ws = (a0 * inv, a1 * inv, a2 * inv)
        lane = lax.broadcasted_iota(jnp.int32, (tm, LANES), 1)
        for hp in range(D // LANES):
            cols = slice(hp * LANES, (hp + 1) * LANES)
            acc = None
            for w, o_ref in zip(ws, (o0_ref, o1_ref, o2_ref)):
                spread = jnp.where(lane < HEAD_DIM, _lane_col(w, lane, 2 * hp), _lane_col(w, lane, 2 * hp + 1))
                term = spread * o_ref[:, cols]
                acc = term if acc is None else acc + term
            out_ref[:, cols] = acc

    row = pl.BlockSpec((tm, D), lambda i: (i, 0))
    vec = pl.BlockSpec((tm, LANES), lambda i: (i, 0))
    return pl.pallas_call(
        body,
        grid=(S // tm,),
        in_specs=[row] * 3 + [vec] * 3,
        out_specs=[row, vec],
        out_shape=[jax.ShapeDtypeStruct((S, D), F32), jax.ShapeDtypeStruct((S, LANES), F32)],
        compiler_params=_params("parallel"),
        name=name,
    )(*os, *ls)


def _lane_col(tile, lane, at):
    return jnp.sum(jnp.where(lane == at, tile, 0.0), axis=-1, keepdims=True)


def _head_stats(do, out, lse_packed, *, name):
    S, D = out.shape
    H = D // HEAD_DIM
    assert 2 * H <= LANES
    tm = _pick(S, (256, 128))

    def body(do_ref, out_ref, l_ref, s_ref):
        lane = lax.broadcasted_iota(jnp.int32, (tm, LANES), 1)
        packed = l_ref[...]
        for hp in range(D // LANES):
            cols = slice(hp * LANES, (hp + 1) * LANES)
            prod = do_ref[:, cols].astype(F32) * out_ref[:, cols]
            for hh in range(2):
                head = (lane >= HEAD_DIM) if hh else (lane < HEAD_DIM)
                delta = jnp.sum(jnp.where(head, prod, 0.0), axis=-1, keepdims=True)
                packed = jnp.where(lane == H + 2 * hp + hh, delta, packed)
        s_ref[...] = packed

    row = pl.BlockSpec((tm, D), lambda i: (i, 0))
    vec = pl.BlockSpec((tm, LANES), lambda i: (i, 0))
    return pl.pallas_call(
        body,
        grid=(S // tm,),
        in_specs=[row, row, vec],
        out_specs=vec,
        out_shape=jax.ShapeDtypeStruct((S, LANES), F32),
        compiler_params=_params("parallel"),
        name=name,
    )(do, out, lse_packed)


def _dil_bwd_dq(q, k, v, do, stats, *, branch, name):
    S, D = q.shape
    HP, H = D // LANES, D // HEAD_DIM
    tb = _dil_tb(S)
    nb, nsub = S // tb, tb // CHUNK
    bps = nb // DILATIONS[branch]

    def body(q_ref, k_ref, v_ref, kp_ref, vp_ref, do_ref, st_ref, dq_ref):
        hp = pl.program_id(0)
        n = pl.program_id(1)
        first = jnp.bitwise_and(n, bps - 1) == 0
        qi, kj = _dil_masks()
        hl = lax.broadcasted_iota(jnp.int32, (CHUNK, LANES), 1)
        m_cur = kj <= qi
        m_prev = (kj - qi) >= 0
        m_prev0 = (kj - qi) >= jnp.where(first, 2 * CHUNK, 0)
        hms = (hl < HEAD_DIM, hl >= HEAD_DIM)
        jobs, cols = [], {}
        for u in range(nsub):
            rows = slice(u * CHUNK, (u + 1) * CHUNK)
            qs, kc, vc = q_ref[rows, :], k_ref[rows, :], v_ref[rows, :]
            if u == 0:
                kp, vp, pm = kp_ref[...], vp_ref[...], m_prev0
            else:
                prev = slice((u - 1) * CHUNK, u * CHUNK)
                kp, vp, pm = k_ref[prev, :], v_ref[prev, :], m_prev
            do_t, st_t = do_ref[rows, :], st_ref[rows, :]
            for hh in range(2):
                qh = jnp.where(hms[hh], qs, jnp.zeros_like(qs))
                dob = jnp.where(hms[hh], do_t, jnp.zeros_like(do_t))
                cols[u, hh] = (_lane_col(st_t, hl, H + 2 * hp + hh), _lane_col(st_t, hl, 2 * hp + hh))
                for kk, vv, msk in ((kp, vp, pm), (kc, vc, m_cur)):
                    s = lax.dot_general(qh, kk, NT, preferred_element_type=F32)
                    dp = lax.dot_general(dob, vv, NT, preferred_element_type=F32)
                    jobs.append((u, hh, s, dp, kk, msk))
        dss = []
        for u, hh, s, dp, kk, msk in jobs:
            delta, lse = cols[u, hh]
            p = jnp.exp(jnp.where(msk, s, NEG) - lse)
            dss.append((p * (dp - delta)).astype(MXU))
        dq_h = {}
        for (u, hh, _, _, kk, _), ds in zip(jobs, dss):
            part = jnp.dot(ds, kk, preferred_element_type=F32)
            dq_h[u, hh] = part if (u, hh) not in dq_h else dq_h[u, hh] + part
        for u in range(nsub):
            dq_ref[u * CHUNK : (u + 1) * CHUNK, :] = jnp.where(hms[0], dq_h[u, 0], dq_h[u, 1]).astype(MXU)

    blk = pl.BlockSpec((tb, LANES), lambda h, n: (n, h))
    tail = pl.BlockSpec((CHUNK, LANES), lambda h, n: (jnp.maximum(n * nsub - 1, 0), h))
    return pl.pallas_call(
        body,
        grid=(HP, nb),
        in_specs=[blk, blk, blk, tail, tail, blk, pl.BlockSpec((tb, LANES), lambda h, n: (n, 0))],
        out_specs=blk,
        out_shape=jax.ShapeDtypeStruct((S, D), MXU),
        compiler_params=_params("parallel", "parallel"),
        name=name,
    )(q, k, v, k, v, do, stats)


def _dil_bwd_dkv(q, k, v, do, stats, *, branch, name):
    S, D = q.shape
    HP, H = D // LANES, D // HEAD_DIM
    tb = _dil_tb(S)
    nb, nsub = S // tb, tb // CHUNK
    last_chunk = S // CHUNK - 1
    bps = nb // DILATIONS[branch]

    def body(q_ref, k_ref, v_ref, do_ref, st_ref, qn_ref, don_ref, stn_ref, dk_ref, dv_ref):
        hp = pl.program_id(0)
        n = pl.program_id(1)
        last =jnp.bitwise_and(n + 1, bps - 1) == 0
        qi, kj = _dil_masks()
        hl = lax.broadcasted_iota(jnp.int32, (CHUNK, LANES), 1)
        m_cur = kj <= qi
        m_next = (kj - qi) >= 0
        m_next_last = (kj - qi) >= jnp.where(last, 2 * CHUNK, 0)
        hms = (hl < HEAD_DIM, hl >= HEAD_DIM)
        qtiles = {}
        for t in range(nsub + 1):
            if t == nsub:
                qs, do_t, st_t = qn_ref[...], don_ref[...], stn_ref[...]
            else:
                rows = slice(t * CHUNK, (t + 1) * CHUNK)
                qs, do_t, st_t = q_ref[rows, :], do_ref[rows, :], st_ref[rows, :]
            for hh in range(2):
                qtiles[t, hh] = (
                    jnp.where(hms[hh], qs, jnp.zeros_like(qs)),
                    jnp.where(hms[hh], do_t, jnp.zeros_like(do_t)),
                    _lane_col(st_t, hl, H + 2 * hp + hh),
                    _lane_col(st_t, hl, 2 * hp + hh),
                )
        jobs = []
        for u in range(nsub):
            rows = slice(u * CHUNK, (u + 1) * CHUNK)
            kc, vc = k_ref[rows, :], v_ref[rows, :]
            for t, msk in ((u, m_cur), (u + 1, m_next_last if u == nsub - 1 else m_next)):
                for hh in range(2):
                    qh, dob, _, _ = qtiles[t, hh]
                    s = lax.dot_general(qh, kc, NT, preferred_element_type=F32)
                    dp = lax.dot_general(dob, vc, NT, preferred_element_type=F32)
                    jobs.append((u, t, hh, s, dp, msk))
        pds = []
        for u, t, hh, s, dp, msk in jobs:
            _, _, delta, lse = qtiles[t, hh]
            p = jnp.exp(jnp.where(msk, s, NEG) - lse)
            pds.append((p.astype(MXU), (p * (dp - delta)).astype(MXU)))
        dks, dvs = {}, {}
        for (u, t, hh, _, _, _), (pb, dsb) in zip(jobs, pds):
            qh, dob, _, _ = qtiles[t, hh]
            dv_p = lax.dot_general(pb, dob, TN, preferred_element_type=F32)
            dk_p = lax.dot_general(dsb, qh, TN, preferred_element_type=F32)
            dvs[u] = dv_p if u not in dvs else dvs[u] + dv_p
            dks[u] = dk_p if u not in dks else dks[u] + dk_p
        for u in range(nsub):
            dk_ref[u * CHUNK : (u + 1) * CHUNK, :] = dks[u].astype(MXU)
            dv_ref[u * CHUNK : (u + 1) * CHUNK, :] = dvs[u].astype(MXU)

    blk = pl.BlockSpec((tb, LANES), lambda h, n: (n, h))
    head = pl.BlockSpec((CHUNK, LANES), lambda h, n: (jnp.minimum((n + 1) * nsub, last_chunk), h))
    st_blk = pl.BlockSpec((tb, LANES), lambda h, n: (n, 0))
    st_head = pl.BlockSpec((CHUNK, LANES), lambda h, n: (jnp.minimum((n + 1) * nsub, last_chunk), 0))
    return pl.pallas_call(
        body,
        grid=(HP, nb),
        in_specs=[blk] * 4 + [st_blk, head, head, st_head],
        out_specs=[blk, blk],
        out_shape=[jax.ShapeDtypeStruct((S, D), MXU)] * 2,
        compiler_params=_params("parallel", "parallel"),
        name=name,
    )(q, k, v, do, stats, q, do, stats)


def _chip_peers():
    x, y, c = lax.axis_index("x"), lax.axis_index("y"), lax.axis_index("c")
    peers = [(1 - x, y, c), (x, 1 - y, c), (1 - x, 1 - y, c)]
    return x, y, c, peers


def _allgather_weights(wsh, *, name):
    R, D = wsh.shape

    def body(w_ref, out_ref, send_sems, recv_sems, local_sem):
        for stage in _gather_stages(w_ref, out_ref, send_sems, recv_sems, local_sem):
            stage()

    hbm = pl.BlockSpec(memory_space=pl.ANY)
    return pl.pallas_call(
        body,
        in_specs=[hbm],
        out_specs=hbm,
        out_shape=jax.ShapeDtypeStruct((N_CHIPS, R, D), wsh.dtype),
        scratch_shapes=GATHER_SEMS,
        name=name,
    )(wsh)


GATHER_SEMS = [pltpu.SemaphoreType.DMA((6,)), pltpu.SemaphoreType.DMA((6,)), pltpu.SemaphoreType.DMA]
SCATTER_SEMS = [pltpu.SemaphoreType.DMA((3,)), pltpu.SemaphoreType.DMA((3,))]


def _gather_stages(w_ref, out_ref, send_sems, recv_sems, local_sem):
    half = w_ref.shape[0] // 2
    x, y, c, peers = _chip_peers()
    me, sibling = 2 * x + y, (x, y, 1 - c)
    chips = [2 * p[0] + p[1] for p in peers]

    def rows(chip, hc):
        return out_ref.at[chip, pl.ds(hc * half, half), :]

    def copy(k, chip, hc, to, src=None):
        return pltpu.make_async_remote_copy(
            src_ref=rows(chip, hc) if src is None else src, dst_ref=rows(chip, hc),
            send_sem=send_sems.at[k], recv_sem=recv_sems.at[k], device_id=to, device_id_type=MESH,
        )

    mine = pltpu.make_async_copy(w_ref, out_ref.at[me], local_sem)
    first = [copy(j, me, c, p, src=w_ref.at[pl.ds(c * half, half), :]) for j, p in enumerate(peers)]
    passed = [copy(3 + j, chip, c, sibling) for j, chip in enumerate(chips)]

    def start():
        mine.start()
        for cp in first:
            cp.start()

    def hand_on():
        for j, chip in enumerate(chips):
            copy(j, chip, c, peers[j]).wait_recv()
            passed[j].start()

    def finish():
        for j, chip in enumerate(chips):
            copy(3 + j, chip, 1 - c, sibling).wait_recv()
        for cp in first + passed:
            cp.wait_send()
        mine.wait()

    return start, hand_on, finish


def _scatter_stages(g_ref, out_ref, send_sems, recv_sems):
    x, y, c, peers = _chip_peers()
    sends = [
        pltpu.make_async_remote_copy(
            src_ref=g_ref.at[2 * p[0] + p[1]], dst_ref=out_ref.at[j], send_sem=send_sems.at[j],
            recv_sem=recv_sems.at[j], device_id=p, device_id_type=MESH,
        )
        for j, p in enumerate(peers)
    ]

    def start():
        for cp in sends:
            cp.start()

    def finish():
        for j, p in enumerate(peers):
            pltpu.make_async_remote_copy(
                src_ref=out_ref.at[j], dst_ref=out_ref.at[j], send_sem=send_sems.at[j], recv_sem=recv_sems.at[j],
                device_id=p, device_id_type=MESH,
            ).wait_recv()
        for cp in sends:
            cp.wait_send()

    return start, finish


def _scatter_grads(gflat, *, name):
    _, R, D = gflat.shape

    def body(g_ref, out_ref, send_sems, recv_sems):
        for stage in _scatter_stages(g_ref, out_ref, send_sems, recv_sems):
            stage()

    hbm = pl.BlockSpec(memory_space=pl.ANY)
    return pl.pallas_call(
        body,
        in_specs=[hbm],
        out_specs=hbm,
        out_shape=jax.ShapeDtypeStruct((3, R, D), gflat.dtype),
        scratch_shapes=SCATTER_SEMS,
        name=name,
    )(gflat)


def _swap_with_sibling(part, *, name):
    def body(p_ref, out_ref, send_sem, recv_sem):
        x, y, c = lax.axis_index("x"), lax.axis_index("y"), lax.axis_index("c")
        cp = pltpu.make_async_remote_copy(
            src_ref=p_ref, dst_ref=out_ref, send_sem=send_sem, recv_sem=recv_sem,
            device_id=(x, y, 1 - c), device_id_type=MESH,
        )
        cp.start()
        cp.wait()

    hbm = pl.BlockSpec(memory_space=pl.ANY)
    return pl.pallas_call(
        body,
        in_specs=[hbm],
        out_specs=hbm,
        out_shape=jax.ShapeDtypeStruct(part.shape, part.dtype),
        scratch_shapes=[pltpu.SemaphoreType.DMA, pltpu.SemaphoreType.DMA],
        name=name,
    )(part)


def _gather_from_all(vec, *, name):
    R, D = vec.shape

    def body(v_ref, out_ref, send_sems, recv_sems):
        x, y, c = lax.axis_index("x"), lax.axis_index("y"), lax.axis_index("c")
        me = 4 * x + 2 * y + c
        out_ref[me] = v_ref[...]
        rel = [(j >> 2 & 1, j >> 1 & 1, j & 1) for j in range(1, 8)]
        peers = [((1 - x) if fx else x, (1 - y) if fy else y, (1 - c) if fc else c) for fx, fy, fc in rel]
        sends = [
            pltpu.make_async_remote_copy(
                src_ref=v_ref, dst_ref=out_ref.at[me], send_sem=send_sems.at[j], recv_sem=recv_sems.at[j],
                device_id=p, device_id_type=MESH,
            )
            for j, p in enumerate(peers)
        ]
        for cp in sends:
            cp.start()
        for j, p in enumerate(peers):
            theirs = out_ref.at[4 * p[0] + 2 * p[1] + p[2]]
            pltpu.make_async_remote_copy(
                src_ref=theirs, dst_ref=theirs, send_sem=send_sems.at[j], recv_sem=recv_sems.at[j],
                device_id=p, device_id_type=MESH,
            ).wait_recv()
        for cp in sends:
            cp.wait_send()

    vmem = pl.BlockSpec(memory_space=pltpu.VMEM)
    return pl.pallas_call(
        body,
        in_specs=[vmem],
        out_specs=vmem,
        out_shape=jax.ShapeDtypeStruct((8, R, D), vec.dtype),
        scratch_shapes=[pltpu.SemaphoreType.DMA((7,)), pltpu.SemaphoreType.DMA((7,))],
        name=name,
    )(vec)


def _sum4(a, rest, *, name):
    R, D = a.shape
    tm = _pick(R, (256, 128, 8))

    def body(a_ref, r_ref, o_ref):
        o_ref[...] = ((a_ref[...] + r_ref[0].astype(F32)) + r_ref[1].astype(F32)) + r_ref[2].astype(F32)

    row = pl.BlockSpec((tm, D), lambda i: (i, 0))
    return pl.pallas_call(
        body,
        grid=(R // tm,),
        in_specs=[row, pl.BlockSpec((3, tm, D), lambda i: (0, i, 0))],
        out_specs=row,
        out_shape=jax.ShapeDtypeStruct((R, D), F32),
        compiler_params=_params("parallel"),
        name=name,
    )(a, rest)


def _adamw(parts, w, m, v, *, name):
    P, R, D = parts.shape
    tm = _pick(R, (256, 128, 8))

    def body(p_ref, w_ref, m_ref, v_ref, g_ref, d_ref, nm_ref, nv_ref):
        g = p_ref[0]
        for k in range(1, P):
            g = g + p_ref[k]
        nm = ADAM_B1 * m_ref[...] + (1.0 - ADAM_B1) * g
        nv = ADAM_B2 * v_ref[...] + (1.0 - ADAM_B2) * jnp.square(g)
        m_hat = nm / (1.0 - ADAM_B1**ADAM_STEP)
        v_hat = nv / (1.0 - ADAM_B2**ADAM_STEP)
        g_ref[...] = g
        d_ref[...] = -ADAM_LR * (m_hat / (jnp.sqrt(v_hat) + ADAM_EPS) + ADAM_WD * w_ref[...])
        nm_ref[...] = nm
        nv_ref[...] = nv

    row = pl.BlockSpec((tm, D), lambda i: (i, 0))
    return pl.pallas_call(
        body,
        grid=(R // tm,),
        in_specs=[pl.BlockSpec((P, tm, D), lambda i: (0, i, 0)), row, row, row],
        out_specs=[row] * 4,
        out_shape=[jax.ShapeDtypeStruct((R, D), F32)] * 4,
        compiler_params=_params("parallel"),
        name=name,
    )(parts, w, m, v)


MATS = ("w_qkv", "w_o", "w_ff1", "w_ff2")
VECS = ("ln1_g", "ln1_b", "ln2_g", "ln2_b")
PARAM_ORDER = ("w_qkv", "w_o", "ln1_g", "ln1_b", "w_ff1", "w_ff2", "ln2_g", "ln2_b")


def _pack_rows(arrs, D):
    return jnp.concatenate([a.reshape(-1, D) for a in arrs], axis=0)


def kernel(x, w_qkv_0, w_o_0, ln1_g_0, ln1_b_0, w_ff1_0, w_ff2_0, ln2_g_0, ln2_b_0, w_qkv_1, w_o_1, ln1_g_1, ln1_b_1, w_ff1_1, w_ff2_1, ln2_g_1, ln2_b_1, loss_target, m_w_qkv_0, m_w_o_0, m_ln1_g_0, m_ln1_b_0, m_w_ff1_0, m_w_ff2_0, m_ln2_g_0, m_ln2_b_0, m_w_qkv_1, m_w_o_1, m_ln1_g_1, m_ln1_b_1, m_w_ff1_1, m_w_ff2_1, m_ln2_g_1, m_ln2_b_1, v_w_qkv_0, v_w_o_0, v_ln1_g_0, v_ln1_b_0, v_w_ff1_0, v_w_ff2_0, v_ln2_g_0, v_ln2_b_0, v_w_qkv_1, v_w_o_1, v_ln1_g_1, v_ln1_b_1, v_w_ff1_1, v_w_ff2_1, v_ln2_g_1, v_ln2_b_1):
    given = dict(locals())
    xs = x[0]
    target = loss_target[0]
    S, D = xs.shape
    C3 = 3 * D // N_CHIPS
    names = [f"{p}_{l}" for l in range(DEPTH) for p in PARAM_ORDER]
    mat_names = [f"{p}_{l}" for l in range(DEPTH) for p in MATS]
    vec_names = [f"{p}_{l}" for l in range(DEPTH) for p in VECS]
    rows = {"w_qkv": C3, "w_o": D // N_CHIPS, "w_ff1": D, "w_ff2": D}
    layer_rows = sum(rows.values())

    packed_w = _pack_rows([given[n] for n in mat_names], D).astype(MXU)
    offset, r0 = {}, 0
    for n in mat_names:
        offset[n] = r0
        r0 += rows[n[:-2]]

    def gathered_matrix(p, l, gathered, base):
        lo = offset[f"{p}_{l}"] - base
        t = gathered[:, lo : lo + rows[p], :]
        if p in ("w_qkv", "w_ff1"):
            return t.reshape(N_CHIPS, D, rows[p]).transpose(1, 0, 2).reshape(D, N_CHIPS * rows[p])
        return t.reshape(N_CHIPS * rows[p], D)

    short = {"w_qkv": "qkv", "w_o": "o", "w_ff1": "ff1", "w_ff2": "ff2"}
    first_gather = _allgather_weights(packed_w[:C3], name="allgather_w_qkv_0")
    W = [{"qkv": gathered_matrix("w_qkv", 0, first_gather, 0)}]
    tables = _rope_tables(S)

    h, hb = xs, xs.astype(MXU)
    saved = []
    for l in range(DEPTH):
        g1, b1, g2, b2 = (given[f"{p}_{l}"] for p in VECS)
        qkv = _mm_nn(hb, W[l]["qkv"], epi="qkv", extras=tables if l == 1 else (), name=f"qkv_{l}")
        st = dict(hb=hb, qkv=qkv)
        if l == 0:
            o, car, rest_gather = _sb_fwd(qkv, gather=packed_w[C3:], name="stickbreak_fwd")
            W.append({})
            for n in mat_names[1:]:
                W[int(n[-1])][short[n[:-2]]] = gathered_matrix(n[:-2], int(n[-1]), rest_gather, C3)
            st.update(car=car)
        else:
            q, k, v = qkv[:, :D], qkv[:, D : 2 * D], qkv[:, 2 * D :]
            qp, kp, vp = ([_perm(t, d) for d in DILATIONS] for t in (q, k, v))
            branches = [
                _dil_fwd(qp[i], kp[i], vp[i], branch=i, name=f"dilated_fwd_d{d}") for i, d in enumerate(DILATIONS)
            ]
            o, lse = _dil_combine(
                [_unperm(ob, d) for (ob, _), d in zip(branches, DILATIONS)],
                [_unperm(lb, d) for (_, lb), d in zip(branches, DILATIONS)],
                name="dilated_combine",
            )
            st.update(qp=qp, kp=kp, vp=vp, lse=lse)
        y1, x1, x1b = _mm_nn(o, W[l]["o"], epi="ln", extras=(h, g1, b1), name=f"attn_out_ln1_{l}")
        hp = _mm_nn(x1b, W[l]["ff1"], out_dtype=MXU, name=f"ff1_{l}")
        y2, x2, x2b = _mm_nn(hp, W[l]["ff2"], a_op="relu2", epi="ln", extras=(x1, g2, b2), name=f"ff2_ln2_{l}")
        st.update(o=o, y1=y1, x1b=x1b, hp=hp, y2=y2)
        saved.append(st)
        h, hb = x2, x2b

    grads = {}
    grad_blocks = lambda ns: jnp.concatenate([grads[n].reshape(N_CHIPS, -1, D) for n in ns], axis=1)
    top = saved[-1]
    dy2, dy2b, dg, db, loss_lanes = _ln_bwd_loss(
        top["y2"], given[f"ln2_g_{DEPTH - 1}"], given[f"ln2_b_{DEPTH - 1}"], target, name="loss_ln2_bwd"
    )
    loss = lax.psum(jnp.sum(loss_lanes) * (0.5 / D), ("x", "y", "c"))
    grads[f"ln2_g_{DEPTH - 1}"], grads[f"ln2_b_{DEPTH - 1}"] = dg, db
    grad_x = None
    for l in reversed(range(DEPTH)):
        st = saved[l]
        dhp = _mm_nt(dy2b, W[l]["ff2"], epi="relu2grad", extra=st["hp"], out_dtype=MXU, name=f"d_ff2_in_{l}")
        grads[f"w_ff2_{l}"] = _mm_tn(st["hp"], dy2b, a_op="relu2", groups=1, name=f"d_w_ff2_{l}")
        dx1 = _mm_nt(dhp, W[l]["ff1"], epi="residual", extra=dy2, out_dtype=F32, name=f"d_ff1_in_{l}")
        grads[f"w_ff1_{l}"] = _mm_tn(st["x1b"], dhp, a_op=None, groups=N_CHIPS, name=f"d_w_ff1_{l}")
        dy1, dy1b, grads[f"ln1_g_{l}"], grads[f"ln1_b_{l}"] = _ln_bwd(
            dx1, st["y1"], given[f"ln1_g_{l}"], name=f"ln1_bwd_{l}"
        )
        grads[f"w_o_{l}"] = _mm_tn(st["o"], dy1b, a_op=None, groups=1, name=f"d_w_o_{l}")
        if l == 0:
            do = _mm_nt(dy1b, W[l]["o"], epi=None, extra=None, out_dtype=MXU, name=f"d_attn_out_{l}")
            dq, dk, dv, from_chips_rest = _sb_bwd(
                st["qkv"], do, st["car"], scatter=grad_blocks(mat_names[1:]).astype(MXU), name="stickbreak_bwd"
            )
            dqkv = _qkv_bwd_post([dq], [dk], [dv], None, name=f"qkv_bwd_post_{l}")
        else:
            do = _mm_nt(dy1b, W[l]["o"], epi=None, extra=None, out_dtype=MXU, name=f"d_attn_out_{l}")
            stats = _head_stats(do, st["o"], st["lse"], name="dilated_head_stats")
            dqs, dks, dvs = [], [], []
            for i, d in enumerate(DILATIONS):
                ops = (st["qp"][i], st["kp"][i], st["vp"][i], _perm(do, d), _perm(stats, d))
                dqs.append(_unperm(_dil_bwd_dq(*ops, branch=i, name=f"dilated_bwd_dq_d{d}"), d))
                dk_d, dv_d = _dil_bwd_dkv(*ops, branch=i, name=f"dilated_bwd_dkv_d{d}")
                dks.append(_unperm(dk_d, d))
                dvs.append(_unperm(dv_d, d))
            dqkv = _qkv_bwd_post(dqs, dks, dvs, tables, name=f"qkv_bwd_post_{l}")
        dh = _mm_nt(dqkv, W[l]["qkv"], epi="residual", extra=dy1, out_dtype=F32, name=f"d_qkv_in_{l}")
        grads[f"w_qkv_{l}"] = _mm_tn(st["hb"], dqkv, a_op=None, groups=N_CHIPS, name=f"d_w_qkv_{l}")
        if l > 0:
            prev = saved[l - 1]
            dy2, dy2b, grads[f"ln2_g_{l - 1}"], grads[f"ln2_b_{l - 1}"] = _ln_bwd(
                dh, prev["y2"], given[f"ln2_g_{l - 1}"], name=f"ln2_bwd_{l - 1}"
            )
        else:
            grad_x = dh

    from_chips_first = _scatter_grads(grad_blocks(mat_names[:1]).astype(MXU), name="scatter_grads_w_qkv_0")
    me = 2 * lax.axis_index("x") + lax.axis_index("y")
    own = lax.dynamic_index_in_dim(grad_blocks(mat_names), me, 0, keepdims=False)
    from_chips = jnp.concatenate([from_chips_first, from_chips_rest], axis=1)
    part = _sum4(own, from_chips, name="sum_chips")
    sib = _swap_with_sibling(part, name="swap_sibling")
    pack = lambda prefix: _pack_rows([given[prefix + n] for n in mat_names], D)
    outs_m = _adamw(jnp.stack([part, sib]), pack(""), pack("m_"), pack("v_"), name="adamw_matrices")
    vflat = _pack_rows([grads[n] for n in vec_names], D)
    packv = lambda prefix: _pack_rows([given[prefix + n] for n in vec_names], D)
    outs_v = _adamw(_gather_from_all(vflat, name="gather_ln_grads"), packv(""), packv("m_"), packv("v_"), name="adamw_vectors")

    def unpack(flat_m, flat_v):
        out, r0 = {}, 0
        for n in mat_names:
            r = given[n].size // D
            out[n] = flat_m[r0 : r0 + r].reshape(given[n].shape)
            r0 += r
        for i, n in enumerate(vec_names):
            out[n] = flat_v[i]
        return [out[n] for n in names]

    result = [loss, grad_x[None]]
    for k in range(4):
        result += unpack(outs_m[k], outs_v[k])
    return tuple(result)
```

```python
import functools

import jax
import jax.numpy as jnp
from jax import lax
from jax.experimental import pallas as pl
from jax.experimental.pallas import tpu as pltpu

F32 = jnp.float32
MXU = jnp.bfloat16

HEAD_DIM = 64
LANES = 128
CHUNK = 128
ROPE_THETA = 500000.0
ROPE_HALF = 8
DILATIONS = (1, 4, 16)
DEPTH = 2
ALPHA = (2 * DEPTH) ** 0.25
LN_EPS = 1e-5
QK_SCALE = 0.125
ADAM_LR, ADAM_B1, ADAM_B2, ADAM_EPS, ADAM_WD, ADAM_STEP = 0.001, 0.9, 0.999, 1e-08, 0.01, 10
NEG = -1e30
UNDERFLOW = -104.0
SB_TQ = 2 * CHUNK
V7X_VMEM_LIMIT = 56 * 1024 * 1024
MESH = pl.DeviceIdType.MESH
NT = (((1,), (1,)), ((), ()))
TN = (((0,), (0,)), ((), ()))
N_CHIPS = 4
MM_ROWS = (1024, 512, 256, 128)


def _params(*sem):
    return pltpu.CompilerParams(dimension_semantics=sem, vmem_limit_bytes=V7X_VMEM_LIMIT)


def _pick(n, cands=(1024, 768, 512, 384, 256, 128)):
    for c in cands:
        if n % c == 0:
            return c
    raise ValueError(f"no tile for {n}")


def _mm_nn(a, b, *, a_op=None, epi=None, extras=(), out_dtype=F32, name):
    M, K = a.shape
    Kb, N = b.shape
    assert K == Kb
    tn = N // 3 if epi == "qkv" else (N if epi == "ln" else _pick(N))
    tm = _pick(M, MM_ROWS) if (K <= 1024 and epi != "ln") else _pick(M, MM_ROWS[1:])
    rope = epi == "qkv" and len(extras) == 3

    def body(a_ref, b_ref, *refs):
        av = a_ref[...]
        if a_op == "relu2":
            av = jnp.square(jnp.maximum(av.astype(F32), 0.0))
        part = jnp.dot(av.astype(MXU), b_ref[...], preferred_element_type=F32)
        if epi is None:
            refs[0][...] = part.astype(out_dtype)
        elif epi == "qkv":
            o_ref = refs[-1]
            tabs = tuple(r[...] for r in refs[:-1])
            j = pl.program_id(1)

            def write(rotate, scale):
                for blk in range(tn // LANES):
                    cols = slice(blk * LANES, (blk + 1) * LANES)
                    xb = part[:, cols]
                    if rotate:
                        xb = _rope(xb, *tabs)
                    if scale:
                        xb = xb * QK_SCALE
                    o_ref[:, cols] = xb.astype(MXU)

            pl.when(j == 0)(lambda: write(rope, True))
            pl.when(j == 1)(lambda: write(rope, False))
            pl.when(j == 2)(lambda: write(False, False))
        else:
            x_ref, g_ref, be_ref, y_ref, o_ref, ob_ref = refs
            y = ALPHA * x_ref[...] + part
            xhat, _ = _ln_stats(y)
            out = xhat * g_ref[...] + be_ref[...]
            y_ref[...] = y
            o_ref[...] = out
            ob_ref[...] = out.astype(MXU)

    in_specs = [pl.BlockSpec((tm, K), lambda i, j: (i, 0)), pl.BlockSpec((K, tn), lambda i, j: (0, j))]
    tile = pl.BlockSpec((tm, tn), lambda i, j: (i, j))
    ops = [a, b]
    if epi == "qkv":
        in_specs += [pl.BlockSpec((tm, LANES), lambda i, j: (i, 0))] * len(extras)
        ops += list(extras)
        out_specs, out_shape = tile, jax.ShapeDtypeStruct((M, N), MXU)
    elif epi == "ln":
        x, g, be = extras
        in_specs += [tile, pl.BlockSpec((1, N), lambda i, j: (0, 0)), pl.BlockSpec((1, N), lambda i, j: (0, 0))]
        ops += [x, g.reshape(1, N), be.reshape(1, N)]
        out_specs = [tile] * 3
        out_shape = [jax.ShapeDtypeStruct((M, N), F32)] * 2 + [jax.ShapeDtypeStruct((M, N), MXU)]
    else:
        out_specs, out_shape = tile, jax.ShapeDtypeStruct((M, N), out_dtype)
    return pl.pallas_call(
        body,
        grid=(M // tm, N // tn),
        in_specs=in_specs,
        out_specs=out_specs,
        out_shape=out_shape,
        compiler_params=_params("parallel", "parallel"),
        name=name,
    )(*ops)


def _mm_nt(a, b, *, epi, extra, out_dtype, name):
    M, N = a.shape
    K, Nb = b.shape
    assert N == Nb
    tm = _pick(M, MM_ROWS) if N <= 1024 else _pick(M, MM_ROWS[1:])
    tko = _pick(K)

    def body(*refs):
        if epi is None:
            a_ref, b_ref, o_ref = refs
        else:
            a_ref, b_ref, e_ref, o_ref = refs
        r = lax.dot_general(a_ref[...].astype(MXU), b_ref[...], NT, preferred_element_type=F32)
        if epi == "relu2grad":
            r = r * (2.0 * jnp.maximum(e_ref[...].astype(F32), 0.0))
        elif epi == "residual":
            r = ALPHA * e_ref[...] + r
        o_ref[...] = r.astype(out_dtype)

    in_specs = [pl.BlockSpec((tm, N), lambda i, j: (i, 0)), pl.BlockSpec((tko, N), lambda i, j: (j, 0))]
    ops = [a, b]
    if epi is not None:
        in_specs.append(pl.BlockSpec((tm, tko), lambda i, j: (i, j)))
        ops.append(extra)
    return pl.pallas_call(
        body,
        grid=(M // tm, K // tko),
        in_specs=in_specs,
        out_specs=pl.BlockSpec((tm, tko), lambda i, j: (i, j)),
        out_shape=jax.ShapeDtypeStruct((M, K), out_dtype),
        compiler_params=_params("parallel", "parallel"),
        name=name,
    )(*ops)


def _mm_nt_ln_bwd(a, b, res, y, g, *, name):
    M, N = a.shape
    K, Nb = b.shape
    assert N == Nb and y.shape == (M, K)
    tm = _pick(M, MM_ROWS[1:])

    def body(a_ref, b_ref, r_ref, y_ref, g_ref, dy_ref, dyb_ref, dg_ref, db_ref):
        dx_v = ALPHA * r_ref[...] + lax.dot_general(
            a_ref[...].astype(MXU), b_ref[...], NT, preferred_element_type=F32
        )
        xhat, rstd = _ln_stats(y_ref[...])
        dxh = dx_v * g_ref[...]
        m1 = jnp.mean(dxh, axis=-1, keepdims=True)
        m2 = jnp.mean(dxh * xhat, axis=-1, keepdims=True)
        dy = rstd * (dxh - m1 - xhat * m2)
        dy_ref[...] = dy
        dyb_ref[...] = dy.astype(MXU)

        @pl.when(pl.program_id(0) == 0)
        def _():
            dg_ref[...] = jnp.zeros_like(dg_ref)
            db_ref[...] = jnp.zeros_like(db_ref)

        dg_ref[...] += jnp.sum(dx_v * xhat, axis=0, keepdims=True)
        db_ref[...] += jnp.sum(dx_v, axis=0, keepdims=True)

    row = pl.BlockSpec((tm, K), lambda i: (i, 0))
    vec = pl.BlockSpec((1, K), lambda i: (0, 0))
    return pl.pallas_call(
        body,
        grid=(M // tm,),
        in_specs=[pl.BlockSpec((tm, N), lambda i: (i, 0)), pl.BlockSpec((K, N), lambda i: (0, 0)), row, row, vec],
        out_specs=[row, row, vec, vec],
        out_shape=[jax.ShapeDtypeStruct((M, K), F32), jax.ShapeDtypeStruct((M, K), MXU)]
        + [jax.ShapeDtypeStruct((1, K), F32)] * 2,
        compiler_params=_params("arbitrary"),
        name=name,
    )(a, b, res, y, g.reshape(1, K))


def _mm_tn(a, b, *, a_op, groups, name):
    M, Ka = a.shape
    Mb, N = b.shape
    assert M == Mb
    Nc = N // groups
    tka, tn, tm = _pick(Ka), _pick(Nc), _pick(M, (2048, 1024, 512, 256, 128))
    nbs, nm = Nc // tn, M // tm

    def body(a_ref, b_ref, o_ref, acc_ref):
        kk = pl.program_id(2)
        av = a_ref[...]
        if a_op == "relu2":
            av = jnp.square(jnp.maximum(av.astype(F32), 0.0))
        part = lax.dot_general(av.astype(MXU), b_ref[...].astype(MXU), TN, preferred_element_type=F32)

        @pl.when(kk == 0)
        def _():
            acc_ref[...] = part

        @pl.when(kk > 0)
        def _():
            acc_ref[...] += part

        @pl.when(kk == nm - 1)
        def _():
            o_ref[...] = acc_ref[...]

    return pl.pallas_call(
        body,
        grid=(Ka // tka, N // tn, nm),
        in_specs=[
            pl.BlockSpec((tm, tka), lambda i, j, k: (k, i)),
            pl.BlockSpec((tm, tn), lambda i, j, k: (k, j)),
        ],
        out_specs=pl.BlockSpec((None, tka, tn), lambda i, j, k: (j // nbs, i, j % nbs)),
        out_shape=jax.ShapeDtypeStruct((groups, Ka, Nc), F32),
        scratch_shapes=[pltpu.VMEM((tka, tn), F32)],
        compiler_params=_params("parallel", "parallel", "arbitrary"),
        name=name,
    )(a, b)


def _ln_stats(y):
    mu = jnp.mean(y, axis=-1, keepdims=True)
    yc = y - mu
    var = jnp.mean(yc * yc, axis=-1, keepdims=True)
    rstd = lax.rsqrt(var + LN_EPS)
    return yc * rstd, rstd


def _ln_bwd_loss(y, g, b, target, *, name):
    S, D = y.shape
    tm = _pick(S, (256, 128))

    def body(y_ref, g_ref, b_ref, t_ref, dy_ref, dyb_ref, dg_ref, db_ref, ls_ref):
        xhat, rstd = _ln_stats(y_ref[...])
        err = xhat * g_ref[...] + b_ref[...] - t_ref[...]
        dx_v = err * (1.0 / D)
        dxh = dx_v * g_ref[...]
        m1 = jnp.mean(dxh, axis=-1, keepdims=True)
        m2 = jnp.mean(dxh * xhat, axis=-1, keepdims=True)
        dy = rstd * (dxh - m1 - xhat * m2)
        dy_ref[...] = dy
        dyb_ref[...] = dy.astype(MXU)

        @pl.when(pl.program_id(0) == 0)
        def _():
            dg_ref[...] = jnp.zeros_like(dg_ref)
            db_ref[...] = jnp.zeros_like(db_ref)
            ls_ref[...] = jnp.zeros_like(ls_ref)

        dg_ref[...] += jnp.sum(dx_v * xhat, axis=0, keepdims=True)
        db_ref[...] += jnp.sum(dx_v, axis=0, keepdims=True)
        ls_ref[...] += jnp.sum(err * err, axis=0, keepdims=True)

    row = pl.BlockSpec((tm, D), lambda i: (i, 0))
    vec = pl.BlockSpec((1, D), lambda i: (0, 0))
    return pl.pallas_call(
        body,
        grid=(S // tm,),
        in_specs=[row, vec, vec, row],
        out_specs=[row, row, vec, vec, vec],
        out_shape=[jax.ShapeDtypeStruct((S, D), F32), jax.ShapeDtypeStruct((S, D), MXU)]
        + [jax.ShapeDtypeStruct((1, D), F32)] * 3,
        compiler_params=_params("arbitrary"),
        name=name,
    )(y, g.reshape(1, D), b.reshape(1, D), target)


def _rope_tables(S):
    inv_freq = ROPE_THETA ** (-jnp.arange(ROPE_HALF, dtype=F32) / ROPE_HALF)
    ang = jnp.arange(S, dtype=jnp.int32).astype(F32)[:, None] * inv_freq[None, :]
    cos, sin = jnp.cos(ang), jnp.sin(ang)
    ones = jnp.ones((S, HEAD_DIM - 2 * ROPE_HALF), F32)
    zeros8 = jnp.zeros((S, ROPE_HALF), F32)
    c = jnp.concatenate([cos, cos, ones], axis=1)
    s1 = jnp.concatenate([zeros8, sin, 0.0 * ones], axis=1)
    s2 = jnp.concatenate([-sin, zeros8, 0.0 * ones], axis=1)
    return tuple(jnp.concatenate([t, t], axis=1) for t in (c, s1, s2))


def _rope(xb, c, s1, s2):
    return xb * c + pltpu.roll(xb, ROPE_HALF, 1) * s1 + pltpu.roll(xb, LANES - ROPE_HALF, 1) * s2


def _rope_t(dyb, c, s1, s2):
    return dyb * c + pltpu.roll(dyb * s1, LANES - ROPE_HALF, 1) + pltpu.roll(dyb * s2, ROPE_HALF, 1)


def _qkv_bwd_post(dqs, dks, dvs, tables, *, name):
    S, D = dqs[0].shape
    tm = _pick(S, (256, 128))
    nb = len(dqs)
    rope = tables is not None

    def body(*refs):
        ins, o_ref = refs[:-1], refs[-1]
        groups = [ins[0:nb], ins[nb : 2 * nb], ins[2 * nb : 3 * nb]]
        tabs = [r[...] for r in ins[3 * nb :]]
        for j, grp in enumerate(groups):
            for blk in range(D // LANES):
                cols = slice(blk * LANES, (blk + 1) * LANES)
                v = grp[0][:, cols].astype(F32)
                for r in grp[1:]:
                    v = v + r[:, cols].astype(F32)
                if rope and j < 2:
                    v = _rope_t(v, *tabs)
                if j == 0:
                    v = v * QK_SCALE
                o_ref[:, j * D + blk * LANES : j * D + (blk + 1) * LANES] = v.astype(MXU)

    row = pl.BlockSpec((tm, D), lambda i: (i, 0))
    tab = pl.BlockSpec((tm, LANES), lambda i: (i, 0))
    return pl.pallas_call(
        body,
        grid=(S // tm,),
        in_specs=[row] * (3 * nb) + ([tab] * 3 if rope else []),
        out_specs=pl.BlockSpec((tm, 3 * D), lambda i: (i, 0)),
        out_shape=jax.ShapeDtypeStruct((S, 3 * D), MXU),
        compiler_params=_params("parallel"),
        name=name,
    )(*dqs, *dks, *dvs, *(tables if rope else ()))


def _log_fail(z):
    return -(jnp.maximum(z, 0.0) + jnp.log(1.0 + jnp.exp(-jnp.abs(z))))


def _sb_fwd(qkv, *, gather=None, name):
    S, D = qkv.shape[0], qkv.shape[1] // 3
    HP, H = D // LANES, D // HEAD_DIM
    tq = SB_TQ
    assert S % tq == 0
    cpb = tq // CHUNK
    assert S // CHUNK <= LANES
    steps = HP * (S // tq)

    def body(*refs):
        if gather is None:
            q_ref, k_ref, v_ref, o_ref, car_ref, acc_ref, carry0_ref, carry1_ref = refs
        else:
            q_ref, k_ref, v_ref, w_ref, o_ref, car_ref, g_ref, acc_ref, carry0_ref, carry1_ref = refs[:10]
            start, hand_on, finish = _gather_stages(w_ref, g_ref, *refs[10:])
            t = pl.program_id(0) * (S // tq) + pl.program_id(1)
            pl.when(t == 0)(start)
            pl.when(t == steps // 2)(hand_on)
        i = pl.program_id(1)
        lane = lax.broadcasted_iota(jnp.int32, (tq, LANES), 1)
        row = lax.broadcasted_iota(jnp.int32, (tq, LANES), 0)
        kl = lax.broadcasted_iota(jnp.int32, (CHUNK, LANES), 1)
        kr = lax.broadcasted_iota(jnp.int32, (CHUNK, LANES), 0)
        tri = (kr >= kl).astype(MXU)
        qv = q_ref[...]
        qh = (jnp.where(lane < HEAD_DIM, qv, jnp.zeros_like(qv)), jnp.where(lane >= HEAD_DIM, qv, jnp.zeros_like(qv)))
        carry_refs = (carry0_ref, carry1_ref)
        acc_ref[...] = jnp.zeros_like(acc_ref)
        car_ref[...] = jnp.full((2, tq, LANES), NEG, F32)
        for r in carry_refs:
            r[...] = jnp.zeros_like(r)

        def run(chunks, masked):
            offs = [pl.multiple_of(c * CHUNK, CHUNK) for c in chunks]
            kcs = [k_ref[pl.ds(o, CHUNK), :] for o in offs]
            vcs = [v_ref[pl.ds(o, CHUNK), :] for o in offs]
            jobs = [(n, hh) for n in range(len(chunks)) for hh in range(2)]
            zs = {j: lax.dot_general(qh[j[1]], kcs[j[0]], NT, preferred_element_type=F32) for j in jobs}
            if masked:
                cms = [(o + lane) < (i * tq + row) for o in offs]
            lfs = {}
            for j in jobs:
                lf = _log_fail(zs[j])
                lfs[j] = jnp.where(cms[j[0]], lf, 0.0) if masked else lf
            rs = {j: jnp.dot(lfs[j].astype(MXU), tri, preferred_element_type=F32) for j in jobs}
            ws = {}
            for hh in range(2):
                carry = carry_refs[hh][...]
                tile = car_ref[hh]
                for n, c in enumerate(chunks):
                    w = jnp.exp(zs[n, hh] + rs[n, hh] + carry)
                    if masked:
                        w = jnp.where(cms[n], w, 0.0)
                    ws[n, hh] = w.astype(MXU)
                    tile = jnp.where(lane == c, carry, tile)
                    carry = carry + jnp.sum(lfs[n, hh], axis=-1, keepdims=True)
                car_ref[hh] = tile
                carry_refs[hh][...] = carry
            for hh in range(2):
                pv = jnp.dot(ws[0, hh], vcs[0], preferred_element_type=F32)
                for n in range(1, len(chunks)):
                    pv = pv + jnp.dot(ws[n, hh], vcs[n], preferred_element_type=F32)
                acc_ref[hh] += pv

        def max_carry():
            return jnp.maximum(jnp.max(carry0_ref[...]), jnp.max(carry1_ref[...]))

        run([i * cpb + u for u in reversed(range(cpb))], True)

        def cond(st):
            return jnp.logical_and(st[0] >= 0, st[1] >= UNDERFLOW)

        def pair(st):
            c = st[0]
            run([c, c - 1], False)
            return c - 2, max_carry()

        lax.while_loop(cond, pair, (i * cpb - 1, max_carry()))
        o_ref[...] = jnp.where(lane < HEAD_DIM, acc_ref[0], acc_ref[1]).astype(o_ref.dtype)
        if gather is not None:
            pl.when(t == steps - 1)(finish)

    blk = pl.BlockSpec((tq, LANES), lambda h, i: (i, h))
    k_full = pl.BlockSpec((S, LANES), lambda h, i: (0, HP + h))
    v_full = pl.BlockSpec((S, LANES), lambda h, i: (0, 2 * HP + h))
    hbm = pl.BlockSpec(memory_space=pl.ANY)
    in_specs, ops = [blk, k_full, v_full], [qkv, qkv, qkv]
    out_specs = [blk, pl.BlockSpec((2, tq, LANES), lambda h, i: (h, i, 0))]
    out_shape = [jax.ShapeDtypeStruct((S, D), MXU), jax.ShapeDtypeStruct((H, S, LANES), F32)]
    scratch = [pltpu.VMEM((2, tq, LANES), F32), pltpu.VMEM((tq, 1), F32), pltpu.VMEM((tq, 1), F32)]
    if gather is not None:
        in_specs, ops = in_specs + [hbm], ops + [gather]
        out_specs = out_specs + [hbm]
        out_shape = out_shape + [jax.ShapeDtypeStruct((N_CHIPS,) + gather.shape, gather.dtype)]
        scratch = scratch + GATHER_SEMS
    return pl.pallas_call(
        body,
        grid=(HP, S // tq),
        in_specs=in_specs,
        out_specs=out_specs,
        out_shape=out_shape,
        scratch_shapes=scratch,
        compiler_params=_params("arbitrary", "arbitrary"),
        name=name,
    )(*ops)


def _sb_bwd(qkv, do, car, *, scatter=None, name):
    S, D = qkv.shape[0], qkv.shape[1] // 3
    HP = D // LANES
    tq = SB_TQ
    assert S % tq == 0
    cpb = tq // CHUNK
    nq = S // tq

    def body(*refs):
        if scatter is None:
            q_ref, k_ref, v_ref, do_ref, car_ref, dq_ref, dk_hbm, dv_hbm = refs[:8]
            dq_acc, dk_acc, dv_acc, gcar0_ref, gcar1_ref, sem = refs[8:]
        else:
            q_ref, k_ref, v_ref, do_ref, car_ref, g_ref, dq_ref, dk_hbm, dv_hbm, r_ref = refs[:10]
            dq_acc, dk_acc, dv_acc, gcar0_ref, gcar1_ref, sem = refs[10:16]
            start, finish = _scatter_stages(g_ref, r_ref, *refs[16:])
            t = pl.program_id(0) * nq + pl.program_id(1)
            pl.when(t == 0)(start)
        hp = pl.program_id(0)
        i = pl.program_id(1)
        lane = lax.broadcasted_iota(jnp.int32, (tq, LANES), 1)
        row = lax.broadcasted_iota(jnp.int32, (tq, LANES), 0)
        kl = lax.broadcasted_iota(jnp.int32, (CHUNK, LANES), 1)
        kr = lax.broadcasted_iota(jnp.int32, (CHUNK, LANES), 0)
        tri = (kr >= kl).astype(MXU)
        tri_prefix = (kr <= kl).astype(MXU)

        @pl.when(i == 0)
        def _():
            dk_acc[...] = jnp.zeros_like(dk_acc)
            dv_acc[...] = jnp.zeros_like(dv_acc)

        qv = q_ref[...]
        dov = do_ref[...]
        qh = (jnp.where(lane < HEAD_DIM, qv, jnp.zeros_like(qv)), jnp.where(lane >= HEAD_DIM, qv, jnp.zeros_like(qv)))
        doh = (jnp.where(lane < HEAD_DIM, dov, jnp.zeros_like(dov)), jnp.where(lane >= HEAD_DIM, dov, jnp.zeros_like(dov)))
        gcar_refs = (gcar0_ref, gcar1_ref)
        dq_acc[...] = jnp.zeros_like(dq_acc)
        for r in gcar_refs:
            r[...] = jnp.zeros_like(r)

        def run(chunks, masked):
            offs = [pl.multiple_of(c * CHUNK, CHUNK) for c in chunks]
            kcs = [k_ref[pl.ds(o, CHUNK), :] for o in offs]
            vcs = [v_ref[pl.ds(o, CHUNK), :] for o in offs]
            jobs = [(n, hh) for n in range(len(chunks)) for hh in range(2)]
            if masked:
                cms = [(o + lane) < (i * tq + row) for o in offs]
            zs = {j: lax.dot_general(qh[j[1]], kcs[j[0]], NT, preferred_element_type=F32) for j in jobs}
            dws = {j: lax.dot_general(doh[j[1]], vcs[j[0]], NT, preferred_element_type=F32) for j in jobs}
            lfs, sigs = {}, {}
            for j in jobs:
                lf = _log_fail(zs[j])
                sigs[j] = jnp.exp(zs[j] + lf)
                lfs[j] = jnp.where(cms[j[0]], lf, 0.0) if masked else lf
            rs = {j: jnp.dot(lfs[j].astype(MXU), tri, preferred_element_type=F32) for j in jobs}
            ws, gs = {}, {}
            for hh in range(2):
                tile = car_ref[hh]
                for n, c in enumerate(chunks):
                    carry = jnp.sum(jnp.where(lane == c, tile, 0.0), axis=-1, keepdims=True)
                    w = jnp.exp(zs[n, hh] + rs[n, hh] + carry)
                    if masked:
                        w = jnp.where(cms[n], w, 0.0)
                    ws[n, hh] = w.astype(MXU)
                    gs[n, hh] = w * dws[n, hh]
            big_gs = {j: jnp.dot(gs[j].astype(MXU), tri_prefix, preferred_element_type=F32) for j in jobs}
            dzs = {}
            for hh in range(2):
                gcar = gcar_refs[hh][...]
                for n in range(len(chunks)):
                    dz = gs[n, hh] - sigs[n, hh] * (big_gs[n, hh] + gcar)
                    if masked:
                        dz = jnp.where(cms[n], dz, 0.0)
                    dzs[n, hh] = dz.astype(MXU)
                    gcar = gcar + jnp.sum(gs[n, hh], axis=-1, keepdims=True)
                gcar_refs[hh][...] = gcar
            for hh in range(2):
                part = jnp.dot(dzs[0, hh], kcs[0], preferred_element_type=F32)
                for n in range(1, len(chunks)):
                    part = part + jnp.dot(dzs[n, hh], kcs[n], preferred_element_type=F32)
                dq_acc[hh] += part
            for n, o in enumerate(offs):
                dk_acc[pl.ds(o, CHUNK), :] += lax.dot_general(
                    dzs[n, 0], qh[0], TN, preferred_element_type=F32
                ) + lax.dot_general(dzs[n, 1], qh[1], TN, preferred_element_type=F32)
                dv_acc[pl.ds(o, CHUNK), :] += lax.dot_general(
                    ws[n, 0], doh[0], TN, preferred_element_type=F32
                ) + lax.dot_general(ws[n, 1], doh[1], TN, preferred_element_type=F32)

        colmax = jnp.maximum(jnp.max(car_ref[0], axis=0, keepdims=True), jnp.max(car_ref[1], axis=0, keepdims=True))
        lane1 = lax.broadcasted_iota(jnp.int32, (1, LANES), 1)
        dead = jnp.logical_and(colmax < UNDERFLOW, lane1 < i * cpb)
        first_pair = jnp.sum(jnp.where(dead, 1.0, 0.0)).astype(jnp.int32) // 2

        def loop_body(p, carry):
            run([2 * p, 2 * p + 1], False)
            return carry

        lax.fori_loop(first_pair, i * cpb // 2, loop_body, 0)
        run([i * cpb + u for u in range(cpb)], True)
        dq_ref[...] = jnp.where(lane < HEAD_DIM, dq_acc[0], dq_acc[1])

        @pl.when(i == nq - 1)
        def _():
            cols = pl.ds(pl.multiple_of(hp * LANES, LANES), LANES)
            for src, dst in ((dk_acc, dk_hbm), (dv_acc, dv_hbm)):
                cp = pltpu.make_async_copy(src, dst.at[:, cols], sem)
                cp.start()
                cp.wait()

        if scatter is not None:
            pl.when(t == HP * nq - 1)(finish)

    blk = pl.BlockSpec((tq, LANES), lambda h, i: (i, h))
    k_full = pl.BlockSpec((S, LANES), lambda h, i: (0, HP + h))
    v_full = pl.BlockSpec((S, LANES), lambda h, i: (0, 2 * HP + h))
    hbm = pl.BlockSpec(memory_space=pl.ANY)
    in_specs = [blk, k_full, v_full, blk, pl.BlockSpec((2, tq, LANES), lambda h, i: (h, i, 0))]
    ops = [qkv, qkv, qkv, do, car]
    out_specs, out_shape = [blk, hbm, hbm], [jax.ShapeDtypeStruct((S, D), F32)] * 3
    scratch = [
        pltpu.VMEM((2, tq, LANES), F32),
        pltpu.VMEM((S, LANES), F32),
        pltpu.VMEM((S, LANES), F32),
        pltpu.VMEM((tq, 1), F32),
        pltpu.VMEM((tq, 1), F32),
        pltpu.SemaphoreType.DMA,
    ]
    if scatter is not None:
        in_specs, ops = in_specs + [hbm], ops + [scatter]
        out_specs = out_specs + [hbm]
        out_shape = out_shape + [jax.ShapeDtypeStruct((3,) + scatter.shape[1:], scatter.dtype)]
        scratch = scratch + SCATTER_SEMS
    return pl.pallas_call(
        body,
        grid=(HP, nq),
        in_specs=in_specs,
        out_specs=out_specs,
        out_shape=out_shape,
        scratch_shapes=scratch,
        compiler_params=_params("arbitrary", "arbitrary"),
        name=name,
    )(*ops)


def _perm(x, d):
    if d == 1:
        return x
    S, D = x.shape
    return x.reshape(S // d, d, D).transpose(1, 0, 2).reshape(S, D)


def _unperm(x, d):
    if d == 1:
        return x
    S, D = x.shape
    return x.reshape(d, S // d, D).transpose(1, 0, 2).reshape(S, D)


def _dil_tb(S):
    return min(1024, S // DILATIONS[-1])


def _dil_masks():
    qi = lax.broadcasted_iota(jnp.int32, (CHUNK, CHUNK), 0)
    kj = lax.broadcasted_iota(jnp.int32, (CHUNK, CHUNK), 1)
    return qi, kj


def _dil_fwd(q, k, v, *, branch, name):
    S, D = q.shape
    HP = D // LANES
    tb = _dil_tb(S)
    nb, nsub = S // tb, tb // CHUNK
    bps = nb // DILATIONS[branch]

    def body(q_ref, k_ref, v_ref, kp_ref, vp_ref, o_ref, l_ref):
        n = pl.program_id(0)
        hp = pl.program_id(1)

        @pl.when(hp == 0)
        def _():
            l_ref[...] = jnp.zeros_like(l_ref)

        first = jnp.bitwise_and(n, bps - 1) == 0
        qi, kj = _dil_masks()
        hl = lax.broadcasted_iota(jnp.int32, (CHUNK, LANES), 1)
        m_cur = kj <= qi
        m_prev = (kj - qi) >= 0
        m_prev0 = (kj - qi) >= jnp.where(first, 2 * CHUNK, 0)
        hms = (hl < HEAD_DIM, hl >= HEAD_DIM)
        tiles, scores = [], {}
        for u in range(nsub):
            rows = slice(u * CHUNK, (u + 1) * CHUNK)
            qs, kc, vc = q_ref[rows, :], k_ref[rows, :], v_ref[rows, :]
            if u == 0:
                kp, vp, pm = kp_ref[...], vp_ref[...], m_prev0
            else:
                prev = slice((u - 1) * CHUNK, u * CHUNK)
                kp, vp, pm = k_ref[prev, :], v_ref[prev, :], m_prev
            tiles.append((rows, vp, vc, pm))
            for hh in range(2):
                qh = jnp.where(hms[hh], qs, jnp.zeros_like(qs))
                scores[u, hh] = (
                    lax.dot_general(qh, kp, NT, preferred_element_type=F32),
                    lax.dot_general(qh, kc, NT, preferred_element_type=F32),
                )
        probs = {}
        for (u, hh), (s_p, s_c) in scores.items():
            s_p = jnp.where(tiles[u][3], s_p, NEG)
            s_c = jnp.where(m_cur, s_c, NEG)
            m = jnp.max(jnp.maximum(s_p, s_c), axis=-1, keepdims=True)
            p_p, p_c = jnp.exp(s_p - m), jnp.exp(s_c - m)
            den = jnp.sum(p_p + p_c, axis=-1, keepdims=True)
            probs[u, hh] = (p_p.astype(MXU), p_c.astype(MXU), 1.0 / den, m + jnp.log(den))
        for u, (rows, vp, vc, _) in enumerate(tiles):
            o_t, l_t = None, l_ref[rows, :]
            for hh in range(2):
                p_p, p_c, inv, lse = probs[u, hh]
                num = jnp.dot(p_p, vp, preferred_element_type=F32) + jnp.dot(p_c, vc, preferred_element_type=F32)
                o_h = num * inv
                o_t = o_h if hh == 0 else jnp.where(hms[1], o_h, o_t)
                l_t = jnp.where(hl == 2 * hp + hh, lse, l_t)
            o_ref[rows, :] = o_t
            l_ref[rows, :] = l_t

    blk = pl.BlockSpec((tb, LANES), lambda n, h: (n, h))
    tail = pl.BlockSpec((CHUNK, LANES), lambda n, h: (jnp.maximum(n * nsub - 1, 0), h))
    return pl.pallas_call(
        body,
        grid=(nb, HP),
        in_specs=[blk, blk, blk, tail, tail],
        out_specs=[blk, pl.BlockSpec((tb, LANES), lambda n, h: (n, 0))],
        out_shape=[jax.ShapeDtypeStruct((S, D), F32), jax.ShapeDtypeStruct((S, LANES), F32)],
        compiler_params=_params("parallel", "arbitrary"),
        name=name,
    )(q, k, v, k, v)


def _dil_combine(os, ls, *, name):
    S, D = os[0].shape
    tm = _pick(S, (256, 128))

    def body(o0_ref, o1_ref, o2_ref, l0_ref, l1_ref, l2_ref, out_ref, lse_ref):
        l0, l1, l2 = l0_ref[...], l1_ref[...], l2_ref[...]
        m = jnp.maximum(jnp.maximum(l0, l1), l2)
        a0, a1, a2 = jnp.exp(l0 - m), jnp.exp(l1 - m), jnp.exp(l2 - m)
        den = a0 + a1 + a2
        lse_ref[...] = m + jnp.log(den)
        inv = 1.0 / den
        ws = (a0 * inv, a1 * inv, a2 * inv)
        lane = lax.broadcasted_iota(jnp.int32, (tm, LANES), 1)
        for hp in range(D // LANES):
            cols = slice(hp * LANES, (hp + 1) * LANES)
            acc = None
            for w, o_ref in zip(ws, (o0_ref, o1_ref, o2_ref)):
                spread = jnp.where(lane < HEAD_DIM, _lane_col(w, lane, 2 * hp), _lane_col(w, lane, 2 * hp + 1))
                term = spread * o_ref[:, cols]
                acc = term if acc is None else acc + term
            out_ref[:, cols] = acc

    row = pl.BlockSpec((tm, D), lambda i: (i, 0))
    vec = pl.BlockSpec((tm, LANES), lambda i: (i, 0))
    return pl.pallas_call(
        body,
        grid=(S // tm,),
        in_specs=[row] * 3 + [vec] * 3,
        out_specs=[row, vec],
        out_shape=[jax.ShapeDtypeStruct((S, D), F32), jax.ShapeDtypeStruct((S, LANES), F32)],
        compiler_params=_params("parallel"),
        name=name,
    )(*os, *ls)


def _lane_col(tile, lane, at):
    return jnp.sum(jnp.where(lane == at, tile, 0.0), axis=-1, keepdims=True)


def _head_stats(do, out, lse_packed, *, name):
    S, D = out.shape
    H = D // HEAD_DIM
    assert 2 * H <= LANES
    tm = _pick(S, (256, 128))

    def body(do_ref, out_ref, l_ref, s_ref):
        lane = lax.broadcasted_iota(jnp.int32, (tm, LANES), 1)
        packed = l_ref[...]
        for hp in range(D // LANES):
            cols = slice(hp * LANES, (hp + 1) * LANES)
            prod = do_ref[:, cols].astype(F32) * out_ref[:, cols]
            for hh in range(2):
                head = (lane >= HEAD_DIM) if hh else (lane < HEAD_DIM)
                delta = jnp.sum(jnp.where(head, prod, 0.0), axis=-1, keepdims=True)
                packed = jnp.where(lane == H + 2 * hp + hh, delta, packed)
        s_ref[...] = packed

    row = pl.BlockSpec((tm, D), lambda i: (i, 0))
    vec = pl.BlockSpec((tm, LANES), lambda i: (i, 0))
    return pl.pallas_call(
        body,
        grid=(S // tm,),
        in_specs=[row, row, vec],
        out_specs=vec,
        out_shape=jax.ShapeDtypeStruct((S, LANES), F32),
        compiler_params=_params("parallel"),
        name=name,
    )(do, out, lse_packed)


def _dil_bwd_dq(q, k, v, do, stats, *, branch, name):
    S, D = q.shape
    HP, H = D // LANES, D // HEAD_DIM
    tb = _dil_tb(S)
    nb, nsub = S // tb, tb // CHUNK
    bps = nb // DILATIONS[branch]

    def body(q_ref, k_ref, v_ref, kp_ref, vp_ref, do_ref, st_ref, dq_ref):
        hp = pl.program_id(0)
        n = pl.program_id(1)
        first = jnp.bitwise_and(n, bps - 1) == 0
        qi, kj = _dil_masks()
        hl = lax.broadcasted_iota(jnp.int32, (CHUNK, LANES), 1)
        m_cur = kj <= qi
        m_prev = (kj - qi) >= 0
        m_prev0 = (kj - qi) >= jnp.where(first, 2 * CHUNK, 0)
        hms = (hl < HEAD_DIM, hl >= HEAD_DIM)
        jobs, cols = [], {}
        for u in range(nsub):
            rows = slice(u * CHUNK, (u + 1) * CHUNK)
            qs, kc, vc = q_ref[rows, :], k_ref[rows, :], v_ref[rows, :]
            if u == 0:
                kp, vp, pm = kp_ref[...], vp_ref[...], m_prev0
            else:
                prev = slice((u - 1) * CHUNK, u * CHUNK)
                kp, vp, pm = k_ref[prev, :], v_ref[prev, :], m_prev
            do_t, st_t = do_ref[rows, :], st_ref[rows, :]
            for hh in range(2):
                qh = jnp.where(hms[hh], qs, jnp.zeros_like(qs))
                dob = jnp.where(hms[hh], do_t, jnp.zeros_like(do_t))
                cols[u, hh] = (_lane_col(st_t, hl, H + 2 * hp + hh), _lane_col(st_t, hl, 2 * hp + hh))
                for kk, vv, msk in ((kp, vp, pm), (kc, vc, m_cur)):
                    s = lax.dot_general(qh, kk, NT, preferred_element_type=F32)
                    dp = lax.dot_general(dob, vv, NT, preferred_element_type=F32)
                    jobs.append((u, hh, s, dp, kk, msk))
        dss = []
        for u, hh, s, dp, kk, msk in jobs:
            delta, lse = cols[u, hh]
            p = jnp.exp(jnp.where(msk, s, NEG) - lse)
            dss.append((p * (dp - delta)).astype(MXU))
        dq_h = {}
        for (u, hh, _, _, kk, _), ds in zip(jobs, dss):
            part = jnp.dot(ds, kk, preferred_element_type=F32)
            dq_h[u, hh] = part if (u, hh) not in dq_h else dq_h[u, hh] + part
        for u in range(nsub):
            dq_ref[u * CHUNK : (u + 1) * CHUNK, :] = jnp.where(hms[0], dq_h[u, 0], dq_h[u, 1]).astype(MXU)

    blk = pl.BlockSpec((tb, LANES), lambda h, n: (n, h))
    tail = pl.BlockSpec((CHUNK, LANES), lambda h, n: (jnp.maximum(n * nsub - 1, 0), h))
    return pl.pallas_call(
        body,
        grid=(HP, nb),
        in_specs=[blk, blk, blk, tail, tail, blk, pl.BlockSpec((tb, LANES), lambda h, n: (n, 0))],
        out_specs=blk,
        out_shape=jax.ShapeDtypeStruct((S, D), MXU),
        compiler_params=_params("parallel", "parallel"),
        name=name,
    )(q, k, v, k, v, do, stats)


def _dil_bwd_dkv(q, k, v, do, stats, *, branch, name):
    S, D = q.shape
    HP, H = D // LANES, D // HEAD_DIM
    tb = _dil_tb(S)
    nb, nsub = S // tb, tb // CHUNK
    last_chunk = S // CHUNK - 1
    bps = nb // DILATIONS[branch]

    def body(q_ref, k_ref, v_ref, do_ref, st_ref, qn_ref, don_ref, stn_ref, dk_ref, dv_ref):
        hp = pl.program_id(0)
        n = pl.program_id(1)
        last =jnp.bitwise_and(n + 1, bps - 1) == 0
        qi, kj = _dil_masks()
        hl = lax.broadcasted_iota(jnp.int32, (CHUNK, LANES), 1)
        m_cur = kj <= qi
        m_next = (kj - qi) >= 0
        m_next_last = (kj - qi) >= jnp.where(last, 2 * CHUNK, 0)
        hms = (hl < HEAD_DIM, hl >= HEAD_DIM)
        qtiles = {}
        for t in range(nsub + 1):
            if t == nsub:
                qs, do_t, st_t = qn_ref[...], don_ref[...], stn_ref[...]
            else:
                rows = slice(t * CHUNK, (t + 1) * CHUNK)
                qs, do_t, st_t = q_ref[rows, :], do_ref[rows, :], st_ref[rows, :]
            for hh in range(2):
                qtiles[t, hh] = (
                    jnp.where(hms[hh], qs, jnp.zeros_like(qs)),
                    jnp.where(hms[hh], do_t, jnp.zeros_like(do_t)),
                    _lane_col(st_t, hl, H + 2 * hp + hh),
                    _lane_col(st_t, hl, 2 * hp + hh),
                )
        jobs = []
        for u in range(nsub):
            rows = slice(u * CHUNK, (u + 1) * CHUNK)
            kc, vc = k_ref[rows, :], v_ref[rows, :]
            for t, msk in ((u, m_cur), (u + 1, m_next_last if u == nsub - 1 else m_next)):
                for hh in range(2):
                    qh, dob, _, _ = qtiles[t, hh]
                    s = lax.dot_general(qh, kc, NT, preferred_element_type=F32)
                    dp = lax.dot_general(dob, vc, NT, preferred_element_type=F32)
                    jobs.append((u, t, hh, s, dp, msk))
        pds = []
        for u, t, hh, s, dp, msk in jobs:
            _, _, delta, lse = qtiles[t, hh]
            p = jnp.exp(jnp.where(msk, s, NEG) - lse)
            pds.append((p.astype(MXU), (p * (dp - delta)).astype(MXU)))
        dks, dvs = {}, {}
        for (u, t, hh, _, _, _), (pb, dsb) in zip(jobs, pds):
            qh, dob, _, _ = qtiles[t, hh]
            dv_p = lax.dot_general(pb, dob, TN, preferred_element_type=F32)
            dk_p = lax.dot_general(dsb, qh, TN, preferred_element_type=F32)
            dvs[u] = dv_p if u not in dvs else dvs[u] + dv_p
            dks[u] = dk_p if u not in dks else dks[u] + dk_p
        for u in range(nsub):
            dk_ref[u * CHUNK : (u + 1) * CHUNK, :] = dks[u].astype(MXU)
            dv_ref[u * CHUNK : (u + 1) * CHUNK, :] = dvs[u].astype(MXU)

    blk = pl.BlockSpec((tb, LANES), lambda h, n: (n, h))
    head = pl.BlockSpec((CHUNK, LANES), lambda h, n: (jnp.minimum((n + 1) * nsub, last_chunk), h))
    st_blk = pl.BlockSpec((tb, LANES), lambda h, n: (n, 0))
    st_head = pl.BlockSpec((CHUNK, LANES), lambda h, n: (jnp.minimum((n + 1) * nsub, last_chunk), 0))
    return pl.pallas_call(
        body,
        grid=(HP, nb),
        in_specs=[blk] * 4 + [st_blk, head, head, st_head],
        out_specs=[blk, blk],
        out_shape=[jax.ShapeDtypeStruct((S, D), MXU)] * 2,
        compiler_params=_params("parallel", "parallel"),
        name=name,
    )(q, k, v, do, stats, q, do, stats)


def _chip_peers():
    x, y, c = lax.axis_index("x"), lax.axis_index("y"), lax.axis_index("c")
    peers = [(1 - x, y, c), (x, 1 - y, c), (1 - x, 1 - y, c)]
    return x, y, c, peers


def _allgather_weights(wsh, *, name):
    R, D = wsh.shape

    def body(w_ref, out_ref, send_sems, recv_sems, local_sem):
        for stage in _gather_stages(w_ref, out_ref, send_sems, recv_sems, local_sem):
            stage()

    hbm = pl.BlockSpec(memory_space=pl.ANY)
    return pl.pallas_call(
        body,
        in_specs=[hbm],
        out_specs=hbm,
        out_shape=jax.ShapeDtypeStruct((N_CHIPS, R, D), wsh.dtype),
        scratch_shapes=GATHER_SEMS,
        name=name,
    )(wsh)


GATHER_SEMS = [pltpu.SemaphoreType.DMA((6,)), pltpu.SemaphoreType.DMA((6,)), pltpu.SemaphoreType.DMA]
SCATTER_SEMS = [pltpu.SemaphoreType.DMA((3,)), pltpu.SemaphoreType.DMA((3,))]


def _gather_stages(w_ref, out_ref, send_sems, recv_sems, local_sem):
    half = w_ref.shape[0] // 2
    x, y, c, peers = _chip_peers()
    me, sibling = 2 * x + y, (x, y, 1 - c)
    chips = [2 * p[0] + p[1] for p in peers]

    def rows(chip, hc):
        return out_ref.at[chip, pl.ds(hc * half, half), :]

    def copy(k, chip, hc, to, src=None):
        return pltpu.make_async_remote_copy(
            src_ref=rows(chip, hc) if src is None else src, dst_ref=rows(chip, hc),
            send_sem=send_sems.at[k], recv_sem=recv_sems.at[k], device_id=to, device_id_type=MESH,
        )

    mine = pltpu.make_async_copy(w_ref, out_ref.at[me], local_sem)
    first = [copy(j, me, c, p, src=w_ref.at[pl.ds(c * half, half), :]) for j, p in enumerate(peers)]
    passed = [copy(3 + j, chip, c, sibling) for j, chip in enumerate(chips)]

    def start():
        mine.start()
        for cp in first:
            cp.start()

    def hand_on():
        for j, chip in enumerate(chips):
            copy(j, chip, c, peers[j]).wait_recv()
            passed[j].start()

    def finish():
        for j, chip in enumerate(chips):
            copy(3 + j, chip, 1 - c, sibling).wait_recv()
        for cp in first + passed:
            cp.wait_send()
        mine.wait()

    return start, hand_on, finish


def _scatter_stages(g_ref, out_ref, send_sems, recv_sems):
    x, y, c, peers = _chip_peers()
    sends = [
        pltpu.make_async_remote_copy(
            src_ref=g_ref.at[2 * p[0] + p[1]], dst_ref=out_ref.at[j], send_sem=send_sems.at[j],
            recv_sem=recv_sems.at[j], device_id=p, device_id_type=MESH,
        )
        for j, p in enumerate(peers)
    ]

    def start():
        for cp in sends:
            cp.start()

    def finish():
        for j, p in enumerate(peers):
            pltpu.make_async_remote_copy(
                src_ref=out_ref.at[j], dst_ref=out_ref.at[j], send_sem=send_sems.at[j], recv_sem=recv_sems.at[j],
                device_id=p, device_id_type=MESH,
            ).wait_recv()
        for cp in sends:
            cp.wait_send()

    return start, finish


def _scatter_grads(gflat, *, name):
    _, R, D = gflat.shape

    def body(g_ref, out_ref, send_sems, recv_sems):
        for stage in _scatter_stages(g_ref, out_ref, send_sems, recv_sems):
            stage()

    hbm = pl.BlockSpec(memory_space=pl.ANY)
    return pl.pallas_call(
        body,
        in_specs=[hbm],
        out_specs=hbm,
        out_shape=jax.ShapeDtypeStruct((3, R, D), gflat.dtype),
        scratch_shapes=SCATTER_SEMS,
        name=name,
    )(gflat)


def _swap_with_sibling(part, *, name):
    def body(p_ref, out_ref, send_sem, recv_sem):
        x, y, c = lax.axis_index("x"), lax.axis_index("y"), lax.axis_index("c")
        cp = pltpu.make_async_remote_copy(
            src_ref=p_ref, dst_ref=out_ref, send_sem=send_sem, recv_sem=recv_sem,
            device_id=(x, y, 1 - c), device_id_type=MESH,
        )
        cp.start()
        cp.wait()

    hbm = pl.BlockSpec(memory_space=pl.ANY)
    return pl.pallas_call(
        body,
        in_specs=[hbm],
        out_specs=hbm,
        out_shape=jax.ShapeDtypeStruct(part.shape, part.dtype),
        scratch_shapes=[pltpu.SemaphoreType.DMA, pltpu.SemaphoreType.DMA],
        name=name,
    )(part)


def _gather_from_all(vec, *, name):
    R, D = vec.shape

    def body(v_ref, out_ref, send_sems, recv_sems):
        x, y, c = lax.axis_index("x"), lax.axis_index("y"), lax.axis_index("c")
        me = 4 * x + 2 * y + c
        out_ref[me] = v_ref[...]
        rel = [(j >> 2 & 1, j >> 1 & 1, j & 1) for j in range(1, 8)]
        peers = [((1 - x) if fx else x, (1 - y) if fy else y, (1 - c) if fc else c) for fx, fy, fc in rel]
        sends = [
            pltpu.make_async_remote_copy(
                src_ref=v_ref, dst_ref=out_ref.at[me], send_sem=send_sems.at[j], recv_sem=recv_sems.at[j],
                device_id=p, device_id_type=MESH,
            )
            for j, p in enumerate(peers)
        ]
        for cp in sends:
            cp.start()
        for j, p in enumerate(peers):
            theirs = out_ref.at[4 * p[0] + 2 * p[1] + p[2]]
            pltpu.make_async_remote_copy(
                src_ref=theirs, dst_ref=theirs, send_sem=send_sems.at[j], recv_sem=recv_sems.at[j],
                device_id=p, device_id_type=MESH,
            ).wait_recv()
        for cp in sends:
            cp.wait_send()

    vmem = pl.BlockSpec(memory_space=pltpu.VMEM)
    return pl.pallas_call(
        body,
        in_specs=[vmem],
        out_specs=vmem,
        out_shape=jax.ShapeDtypeStruct((8, R, D), vec.dtype),
        scratch_shapes=[pltpu.SemaphoreType.DMA((7,)), pltpu.SemaphoreType.DMA((7,))],
        name=name,
    )(vec)


def _sum4(a, rest, *, name):
    R, D = a.shape
    tm = _pick(R, (256, 128, 8))

    def body(a_ref, r_ref, o_ref):
        o_ref[...] = ((a_ref[...] + r_ref[0].astype(F32)) + r_ref[1].astype(F32)) + r_ref[2].astype(F32)

    row = pl.BlockSpec((tm, D), lambda i: (i, 0))
    return pl.pallas_call(
        body,
        grid=(R // tm,),
        in_specs=[row, pl.BlockSpec((3, tm, D), lambda i: (0, i, 0))],
        out_specs=row,
        out_shape=jax.ShapeDtypeStruct((R, D), F32),
        compiler_params=_params("parallel"),
        name=name,
    )(a, rest)


def _adamw(parts, w, m, v, *, name):
    P, R, D = parts.shape
    tm = _pick(R, (256, 128, 8))

    def body(p_ref, w_ref, m_ref, v_ref, g_ref, d_ref, nm_ref, nv_ref):
        g = p_ref[0]
        for k in range(1, P):
            g = g + p_ref[k]
        nm = ADAM_B1 * m_ref[...] + (1.0 - ADAM_B1) * g
        nv = ADAM_B2 * v_ref[...] + (1.0 - ADAM_B2) * jnp.square(g)
        m_hat = nm / (1.0 - ADAM_B1**ADAM_STEP)
        v_hat = nv / (1.0 - ADAM_B2**ADAM_STEP)
        g_ref[...] = g
        d_ref[...] = -ADAM_LR * (m_hat / (jnp.sqrt(v_hat) + ADAM_EPS) + ADAM_WD * w_ref[...])
        nm_ref[...] = nm
        nv_ref[...] = nv

    row = pl.BlockSpec((tm, D), lambda i: (i, 0))
    return pl.pallas_call(
        body,
        grid=(R // tm,),
        in_specs=[pl.BlockSpec((P, tm, D), lambda i: (0, i, 0)), row, row, row],
        out_specs=[row] * 4,
        out_shape=[jax.ShapeDtypeStruct((R, D), F32)] * 4,
        compiler_params=_params("parallel"),
        name=name,
    )(parts, w, m, v)


MATS = ("w_qkv", "w_o", "w_ff1", "w_ff2")
VECS = ("ln1_g", "ln1_b", "ln2_g", "ln2_b")
PARAM_ORDER = ("w_qkv", "w_o", "ln1_g", "ln1_b", "w_ff1", "w_ff2", "ln2_g", "ln2_b")


def _pack_rows(arrs, D):
    return jnp.concatenate([a.reshape(-1, D) for a in arrs], axis=0)


def kernel(x, w_qkv_0, w_o_0, ln1_g_0, ln1_b_0, w_ff1_0, w_ff2_0, ln2_g_0, ln2_b_0, w_qkv_1, w_o_1, ln1_g_1, ln1_b_1, w_ff1_1, w_ff2_1, ln2_g_1, ln2_b_1, loss_target, m_w_qkv_0, m_w_o_0, m_ln1_g_0, m_ln1_b_0, m_w_ff1_0, m_w_ff2_0, m_ln2_g_0, m_ln2_b_0, m_w_qkv_1, m_w_o_1, m_ln1_g_1, m_ln1_b_1, m_w_ff1_1, m_w_ff2_1, m_ln2_g_1, m_ln2_b_1, v_w_qkv_0, v_w_o_0, v_ln1_g_0, v_ln1_b_0, v_w_ff1_0, v_w_ff2_0, v_ln2_g_0, v_ln2_b_0, v_w_qkv_1, v_w_o_1, v_ln1_g_1, v_ln1_b_1, v_w_ff1_1, v_w_ff2_1, v_ln2_g_1, v_ln2_b_1):
    given = dict(locals())
    xs = x[0]
    target = loss_target[0]
    S, D = xs.shape
    C3 = 3 * D // N_CHIPS
    names = [f"{p}_{l}" for l in range(DEPTH) for p in PARAM_ORDER]
    mat_names = [f"{p}_{l}" for l in range(DEPTH) for p in MATS]
    vec_names = [f"{p}_{l}" for l in range(DEPTH) for p in VECS]
    rows = {"w_qkv": C3, "w_o": D // N_CHIPS, "w_ff1": D, "w_ff2": D}
    layer_rows = sum(rows.values())

    packed_w = _pack_rows([given[n] for n in mat_names], D).astype(MXU)
    offset, r0 = {}, 0
    for n in mat_names:
        offset[n] = r0
        r0 += rows[n[:-2]]

    def gathered_matrix(p, l, gathered, base):
        lo = offset[f"{p}_{l}"] - base
        t = gathered[:, lo : lo + rows[p], :]
        if p in ("w_qkv", "w_ff1"):
            return t.reshape(N_CHIPS, D, rows[p]).transpose(1, 0, 2).reshape(D, N_CHIPS * rows[p])
        return t.reshape(N_CHIPS * rows[p], D)

    short = {"w_qkv": "qkv", "w_o": "o", "w_ff1": "ff1", "w_ff2": "ff2"}
    first_gather = _allgather_weights(packed_w[:C3], name="allgather_w_qkv_0")
    W = [{"qkv": gathered_matrix("w_qkv", 0, first_gather, 0)}]
    tables = _rope_tables(S)

    h, hb = xs, xs.astype(MXU)
    saved = []
    for l in range(DEPTH):
        g1, b1, g2, b2 = (given[f"{p}_{l}"] for p in VECS)
        qkv = _mm_nn(hb, W[l]["qkv"], epi="qkv", extras=tables if l == 1 else (), name=f"qkv_{l}")
        st = dict(hb=hb, qkv=qkv)
        if l == 0:
            o, car, rest_gather = _sb_fwd(qkv, gather=packed_w[C3:], name="stickbreak_fwd")
            W.append({})
            for n in mat_names[1:]:
                W[int(n[-1])][short[n[:-2]]] = gathered_matrix(n[:-2], int(n[-1]), rest_gather, C3)
            st.update(car=car)
        else:
            q, k, v = qkv[:, :D], qkv[:, D : 2 * D], qkv[:, 2 * D :]
            qp, kp, vp = ([_perm(t, d) for d in DILATIONS] for t in (q, k, v))
            branches = [
                _dil_fwd(qp[i], kp[i], vp[i], branch=i, name=f"dilated_fwd_d{d}") for i, d in enumerate(DILATIONS)
            ]
            o, lse = _dil_combine(
                [_unperm(ob, d) for (ob, _), d in zip(branches, DILATIONS)],
                [_unperm(lb, d) for (_, lb), d in zip(branches, DILATIONS)],
                name="dilated_combine",
            )
            st.update(qp=qp, kp=kp, vp=vp, lse=lse)
        y1, x1, x1b = _mm_nn(o, W[l]["o"], epi="ln", extras=(h, g1, b1), name=f"attn_out_ln1_{l}")
        hp = _mm_nn(x1b, W[l]["ff1"], out_dtype=MXU, name=f"ff1_{l}")
        y2, x2, x2b = _mm_nn(hp, W[l]["ff2"], a_op="relu2", epi="ln", extras=(x1, g2, b2), name=f"ff2_ln2_{l}")
        st.update(o=o, y1=y1, x1b=x1b, hp=hp, y2=y2)
        saved.append(st)
        h, hb = x2, x2b

    grads = {}
    grad_blocks = lambda ns: jnp.concatenate([grads[n].reshape(N_CHIPS, -1, D) for n in ns], axis=1)
    top = saved[-1]
    dy2, dy2b, dg, db, loss_lanes = _ln_bwd_loss(
        top["y2"], given[f"ln2_g_{DEPTH - 1}"], given[f"ln2_b_{DEPTH - 1}"], target, name="loss_ln2_bwd"
    )
    loss = lax.psum(jnp.sum(loss_lanes) * (0.5 / D), ("x", "y", "c"))
    grads[f"ln2_g_{DEPTH - 1}"], grads[f"ln2_b_{DEPTH - 1}"] = dg, db
    grad_x = None
    for l in reversed(range(DEPTH)):
        st = saved[l]
        dhp = _mm_nt(dy2b, W[l]["ff2"], epi="relu2grad", extra=st["hp"], out_dtype=MXU, name=f"d_ff2_in_{l}")
        grads[f"w_ff2_{l}"] = _mm_tn(st["hp"], dy2b, a_op="relu2", groups=1, name=f"d_w_ff2_{l}")
        dy1, dy1b, grads[f"ln1_g_{l}"], grads[f"ln1_b_{l}"] = _mm_nt_ln_bwd(
            dhp, W[l]["ff1"], dy2, st["y1"], given[f"ln1_g_{l}"], name=f"d_ff1_in_ln1_bwd_{l}"
        )
        grads[f"w_ff1_{l}"] = _mm_tn(st["x1b"], dhp, a_op=None, groups=N_CHIPS, name=f"d_w_ff1_{l}")
        grads[f"w_o_{l}"] = _mm_tn(st["o"], dy1b, a_op=None, groups=1, name=f"d_w_o_{l}")
        if l == 0:
            do = _mm_nt(dy1b, W[l]["o"], epi=None, extra=None, out_dtype=MXU, name=f"d_attn_out_{l}")
            dq, dk, dv, from_chips_rest = _sb_bwd(
                st["qkv"], do, st["car"], scatter=grad_blocks(mat_names[1:]).astype(MXU), name="stickbreak_bwd"
            )
            dqkv = _qkv_bwd_post([dq], [dk], [dv], None, name=f"qkv_bwd_post_{l}")
        else:
            do = _mm_nt(dy1b, W[l]["o"], epi=None, extra=None, out_dtype=MXU, name=f"d_attn_out_{l}")
            stats = _head_stats(do, st["o"], st["lse"], name="dilated_head_stats")
            dqs, dks, dvs = [], [], []
            for i, d in enumerate(DILATIONS):
                ops = (st["qp"][i], st["kp"][i], st["vp"][i], _perm(do, d), _perm(stats, d))
                dqs.append(_unperm(_dil_bwd_dq(*ops, branch=i, name=f"dilated_bwd_dq_d{d}"), d))
                dk_d, dv_d = _dil_bwd_dkv(*ops, branch=i, name=f"dilated_bwd_dkv_d{d}")
                dks.append(_unperm(dk_d, d))
                dvs.append(_unperm(dv_d, d))
            dqkv = _qkv_bwd_post(dqs, dks, dvs, tables, name=f"qkv_bwd_post_{l}")
        grads[f"w_qkv_{l}"] = _mm_tn(st["hb"], dqkv, a_op=None, groups=N_CHIPS, name=f"d_w_qkv_{l}")
        if l > 0:
            prev = saved[l - 1]
            dy2, dy2b, grads[f"ln2_g_{l - 1}"], grads[f"ln2_b_{l - 1}"] = _mm_nt_ln_bwd(
                dqkv, W[l]["qkv"], dy1, prev["y2"], given[f"ln2_g_{l - 1}"], name=f"d_qkv_in_ln2_bwd_{l - 1}"
            )
        else:
            grad_x = _mm_nt(dqkv, W[l]["qkv"], epi="residual", extra=dy1, out_dtype=F32, name=f"d_qkv_in_{l}")

    from_chips_first = _scatter_grads(grad_blocks(mat_names[:1]).astype(MXU), name="scatter_grads_w_qkv_0")
    me = 2 * lax.axis_index("x") + lax.axis_index("y")
    own = lax.dynamic_index_in_dim(grad_blocks(mat_names), me, 0, keepdims=False)
    from_chips = jnp.concatenate([from_chips_first, from_chips_rest], axis=1)
    part = _sum4(own, from_chips, name="sum_chips")
    sib = _swap_with_sibling(part, name="swap_sibling")
    pack = lambda prefix: _pack_rows([given[prefix + n] for n in mat_names], D)
    outs_m = _adamw(jnp.stack([part, sib]), pack(""), pack("m_"), pack("v_"), name="adamw_matrices")
    vflat = _pack_rows([grads[n] for n in vec_names], D)
    packv = lambda prefix: _pack_rows([given[prefix + n] for n in vec_names], D)
    outs_v = _adamw(_gather_from_all(vflat, name="gather_ln_grads"), packv(""), packv("m_"), packv("v_"), name="adamw_vectors")

    def unpack(flat_m, flat_v):
        out, r0 = {}, 0
        for n in mat_names:
            r = given[n].size // D
            out[n] = flat_m[r0 : r0 + r].reshape(given[n].shape)
            r0 += r
        for i, n in enumerate(vec_names):
            out[n] = flat_v[i]
        return [out[n] for n in names]

    result = [loss, grad_x[None]]
    for k in range(4):
        result += unpack(outs_m[k], outs_v[k])
    return tuple(result)
```

```python
import functools

import jax
import jax.numpy as jnp
from jax import lax
from jax.experimental import pallas as pl
from jax.experimental.pallas import tpu as pltpu

F32 = jnp.float32
MXU = jnp.bfloat16

HEAD_DIM = 64
LANES = 128
CHUNK = 128
ROPE_THETA = 500000.0
ROPE_HALF = 8
DILATIONS = (1, 4, 16)
DEPTH = 2
ALPHA = (2 * DEPTH) ** 0.25
LN_EPS = 1e-5
QK_SCALE = 0.125
ADAM_LR, ADAM_B1, ADAM_B2, ADAM_EPS, ADAM_WD, ADAM_STEP = 0.001, 0.9, 0.999, 1e-08, 0.01, 10
NEG = -1e30
UNDERFLOW = -104.0
SB_TQ = 2 * CHUNK
V7X_VMEM_LIMIT = 56 * 1024 * 1024
MESH = pl.DeviceIdType.MESH
NT = (((1,), (1,)), ((), ()))
TN = (((0,), (0,)), ((), ()))
N_CHIPS = 4
MM_ROWS = (1024, 512, 256, 128)


def _params(*sem):
    return pltpu.CompilerParams(dimension_semantics=sem, vmem_limit_bytes=V7X_VMEM_LIMIT)


def _pick(n, cands=(1024, 768, 512, 384, 256, 128)):
    for c in cands:
        if n % c == 0:
            return c
    raise ValueError(f"no tile for {n}")


def _mm_nn(a, b, *, a_op=None, epi=None, extras=(), out_dtype=F32, name):
    M, K = a.shape
    Kb, N = b.shape
    assert K == Kb
    tn = N // 3 if epi == "qkv" else (N if epi == "ln" else _pick(N))
    tm = _pick(M, MM_ROWS) if (K <= 1024 and epi != "ln") else _pick(M, MM_ROWS[1:])
    rope = epi == "qkv" and len(extras) == 3

    def body(a_ref, b_ref, *refs):
        av = a_ref[...]
        if a_op == "relu2":
            av = jnp.square(jnp.maximum(av.astype(F32), 0.0))
        part = jnp.dot(av.astype(MXU), b_ref[...], preferred_element_type=F32)
        if epi is None:
            refs[0][...] = part.astype(out_dtype)
        elif epi == "qkv":
            o_ref = refs[-1]
            tabs = tuple(r[...] for r in refs[:-1])
            j = pl.program_id(1)

            def write(rotate, scale):
                for blk in range(tn // LANES):
                    cols = slice(blk * LANES, (blk + 1) * LANES)
                    xb = part[:, cols]
                    if rotate:
                        xb = _rope(xb, *tabs)
                    if scale:
                        xb = xb * QK_SCALE
                    o_ref[:, cols] = xb.astype(MXU)

            pl.when(j == 0)(lambda: write(rope, True))
            pl.when(j == 1)(lambda: write(rope, False))
            pl.when(j == 2)(lambda: write(False, False))
        else:
            x_ref, g_ref, be_ref, y_ref, o_ref, ob_ref = refs
            y = ALPHA * x_ref[...] + part
            xhat, _ = _ln_stats(y)
            out = xhat * g_ref[...] + be_ref[...]
            y_ref[...] = y
            o_ref[...] = out
            ob_ref[...] = out.astype(MXU)

    in_specs = [pl.BlockSpec((tm, K), lambda i, j: (i, 0)), pl.BlockSpec((K, tn), lambda i, j: (0, j))]
    tile = pl.BlockSpec((tm, tn), lambda i, j: (i, j))
    ops = [a, b]
    if epi == "qkv":
        in_specs += [pl.BlockSpec((tm, LANES), lambda i, j: (i, 0))] * len(extras)
        ops += list(extras)
        out_specs, out_shape = tile, jax.ShapeDtypeStruct((M, N), MXU)
    elif epi == "ln":
        x, g, be = extras
        in_specs += [tile, pl.BlockSpec((1, N), lambda i, j: (0, 0)), pl.BlockSpec((1, N), lambda i, j: (0, 0))]
        ops += [x, g.reshape(1, N), be.reshape(1, N)]
        out_specs = [tile] * 3
        out_shape = [jax.ShapeDtypeStruct((M, N), F32)] * 2 + [jax.ShapeDtypeStruct((M, N), MXU)]
    else:
        out_specs, out_shape = tile, jax.ShapeDtypeStruct((M, N), out_dtype)
    return pl.pallas_call(
        body,
        grid=(M // tm, N // tn),
        in_specs=in_specs,
        out_specs=out_specs,
        out_shape=out_shape,
        compiler_params=_params("parallel", "parallel"),
        name=name,
    )(*ops)


def _mm_nt(a, b, *, epi, extra, out_dtype, name):
    M, N = a.shape
    K, Nb = b.shape
    assert N == Nb
    tm = _pick(M, MM_ROWS) if N <= 1024 else _pick(M, MM_ROWS[1:])
    tko = _pick(K)

    def body(*refs):
        if epi is None:
            a_ref, b_ref, o_ref = refs
        else:
            a_ref, b_ref, e_ref, o_ref = refs
        r = lax.dot_general(a_ref[...].astype(MXU), b_ref[...], NT, preferred_element_type=F32)
        if epi == "relu2grad":
            r = r * (2.0 * jnp.maximum(e_ref[...].astype(F32), 0.0))
        elif epi == "residual":
            r = ALPHA * e_ref[...] + r
        o_ref[...] = r.astype(out_dtype)

    in_specs = [pl.BlockSpec((tm, N), lambda i, j: (i, 0)), pl.BlockSpec((tko, N), lambda i, j: (j, 0))]
    ops = [a, b]
    if epi is not None:
        in_specs.append(pl.BlockSpec((tm, tko), lambda i, j: (i, j)))
        ops.append(extra)
    return pl.pallas_call(
        body,
        grid=(M // tm, K // tko),
        in_specs=in_specs,
        out_specs=pl.BlockSpec((tm, tko), lambda i, j: (i, j)),
        out_shape=jax.ShapeDtypeStruct((M, K), out_dtype),
        compiler_params=_params("parallel", "parallel"),
        name=name,
    )(*ops)


def _mm_nt_ln_bwd(a, b, res, y, g, *, name):
    M, N = a.shape
    K, Nb = b.shape
    assert N == Nb and y.shape == (M, K)
    tm = _pick(M, MM_ROWS[1:])

    def body(a_ref, b_ref, r_ref, y_ref, g_ref, dy_ref, dyb_ref, dg_ref, db_ref):
        dx_v = ALPHA * r_ref[...] + lax.dot_general(
            a_ref[...].astype(MXU), b_ref[...], NT, preferred_element_type=F32
        )
        xhat, rstd = _ln_stats(y_ref[...])
        dxh = dx_v * g_ref[...]
        m1 = jnp.mean(dxh, axis=-1, keepdims=True)
        m2 = jnp.mean(dxh * xhat, axis=-1, keepdims=True)
        dy = rstd * (dxh - m1 - xhat * m2)
        dy_ref[...] = dy
        dyb_ref[...] = dy.astype(MXU)

        @pl.when(pl.program_id(0) == 0)
        def _():
            dg_ref[...] = jnp.zeros_like(dg_ref)
            db_ref[...] = jnp.zeros_like(db_ref)

        dg_ref[...] += jnp.sum(dx_v * xhat, axis=0, keepdims=True)
        db_ref[...] += jnp.sum(dx_v, axis=0, keepdims=True)

    row = pl.BlockSpec((tm, K), lambda i: (i, 0))
    vec = pl.BlockSpec((1, K), lambda i: (0, 0))
    return pl.pallas_call(
        body,
        grid=(M // tm,),
        in_specs=[pl.BlockSpec((tm, N), lambda i: (i, 0)), pl.BlockSpec((K, N), lambda i: (0, 0)), row, row, vec],
        out_specs=[row, row, vec, vec],
        out_shape=[jax.ShapeDtypeStruct((M, K), F32), jax.ShapeDtypeStruct((M, K), MXU)]
        + [jax.ShapeDtypeStruct((1, K), F32)] * 2,
        compiler_params=_params("arbitrary"),
        name=name,
    )(a, b, res, y, g.reshape(1, K))


def _mm_tn(a, b, *, a_op, groups, name):
    M, Ka = a.shape
    Mb, N = b.shape
    assert M == Mb
    Nc = N // groups
    tka, tn, tm = _pick(Ka), _pick(Nc), _pick(M, (2048, 1024, 512, 256, 128))
    nbs, nm = Nc // tn, M // tm

    def body(a_ref, b_ref, o_ref, acc_ref):
        kk = pl.program_id(2)
        av = a_ref[...]
        if a_op == "relu2":
            av = jnp.square(jnp.maximum(av.astype(F32), 0.0))
        part = lax.dot_general(av.astype(MXU), b_ref[...].astype(MXU), TN, preferred_element_type=F32)

        @pl.when(kk == 0)
        def _():
            acc_ref[...] = part

        @pl.when(kk > 0)
        def _():
            acc_ref[...] += part

        @pl.when(kk == nm - 1)
        def _():
            o_ref[...] = acc_ref[...]

    return pl.pallas_call(
        body,
        grid=(Ka // tka, N // tn, nm),
        in_specs=[
            pl.BlockSpec((tm, tka), lambda i, j, k: (k, i)),
            pl.BlockSpec((tm, tn), lambda i, j, k: (k, j)),
        ],
        out_specs=pl.BlockSpec((None, tka, tn), lambda i, j, k: (j // nbs, i, j % nbs)),
        out_shape=jax.ShapeDtypeStruct((groups, Ka, Nc), F32),
        scratch_shapes=[pltpu.VMEM((tka, tn), F32)],
        compiler_params=_params("parallel", "parallel", "arbitrary"),
        name=name,
    )(a, b)


def _ln_stats(y):
    mu = jnp.mean(y, axis=-1, keepdims=True)
    yc = y - mu
    var = jnp.mean(yc * yc, axis=-1, keepdims=True)
    rstd = lax.rsqrt(var + LN_EPS)
    return yc * rstd, rstd


def _ln_bwd_loss(y, g, b, target, *, name):
    S, D = y.shape
    tm = _pick(S, (256, 128))

    def body(y_ref, g_ref, b_ref, t_ref, dy_ref, dyb_ref, dg_ref, db_ref, ls_ref):
        xhat, rstd = _ln_stats(y_ref[...])
        err = xhat * g_ref[...] + b_ref[...] - t_ref[...]
        dx_v = err * (1.0 / D)
        dxh = dx_v * g_ref[...]
        m1 = jnp.mean(dxh, axis=-1, keepdims=True)
        m2 = jnp.mean(dxh * xhat, axis=-1, keepdims=True)
        dy = rstd * (dxh - m1 - xhat * m2)
        dy_ref[...] = dy
        dyb_ref[...] = dy.astype(MXU)

        @pl.when(pl.program_id(0) == 0)
        def _():
            dg_ref[...] = jnp.zeros_like(dg_ref)
            db_ref[...] = jnp.zeros_like(db_ref)
            ls_ref[...] = jnp.zeros_like(ls_ref)

        dg_ref[...] += jnp.sum(dx_v * xhat, axis=0, keepdims=True)
        db_ref[...] += jnp.sum(dx_v, axis=0, keepdims=True)
        ls_ref[...] += jnp.sum(err * err, axis=0, keepdims=True)

    row = pl.BlockSpec((tm, D), lambda i: (i, 0))
    vec = pl.BlockSpec((1, D), lambda i: (0, 0))
    return pl.pallas_call(
        body,
        grid=(S // tm,),
        in_specs=[row, vec, vec, row],
        out_specs=[row, row, vec, vec, vec],
        out_shape=[jax.ShapeDtypeStruct((S, D), F32), jax.ShapeDtypeStruct((S, D), MXU)]
        + [jax.ShapeDtypeStruct((1, D), F32)] * 3,
        compiler_params=_params("arbitrary"),
        name=name,
    )(y, g.reshape(1, D), b.reshape(1, D), target)


def _rope_tables(S):
    inv_freq = ROPE_THETA ** (-jnp.arange(ROPE_HALF, dtype=F32) / ROPE_HALF)
    ang = jnp.arange(S, dtype=jnp.int32).astype(F32)[:, None] * inv_freq[None, :]
    cos, sin = jnp.cos(ang), jnp.sin(ang)
    ones = jnp.ones((S, HEAD_DIM - 2 * ROPE_HALF), F32)
    zeros8 = jnp.zeros((S, ROPE_HALF), F32)
    c = jnp.concatenate([cos, cos, ones], axis=1)
    s1 = jnp.concatenate([zeros8, sin, 0.0 * ones], axis=1)
    s2 = jnp.concatenate([-sin, zeros8, 0.0 * ones], axis=1)
    return tuple(jnp.concatenate([t, t], axis=1) for t in (c, s1, s2))


def _rope(xb, c, s1, s2):
    return xb * c + pltpu.roll(xb, ROPE_HALF, 1) * s1 + pltpu.roll(xb, LANES - ROPE_HALF, 1) * s2


def _rope_t(dyb, c, s1, s2):
    return dyb * c + pltpu.roll(dyb * s1, LANES - ROPE_HALF, 1) + pltpu.roll(dyb * s2, ROPE_HALF, 1)


def _qkv_bwd_post(dqs, dks, dvs, tables, *, name):
    S, D = dqs[0].shape
    tm = _pick(S, (256, 128))
    nb = len(dqs)
    rope = tables is not None

    def body(*refs):
        ins, o_ref = refs[:-1], refs[-1]
        groups = [ins[0:nb], ins[nb : 2 * nb], ins[2 * nb : 3 * nb]]
        tabs = [r[...] for r in ins[3 * nb :]]
        for j, grp in enumerate(groups):
            for blk in range(D // LANES):
                cols = slice(blk * LANES, (blk + 1) * LANES)
                v = grp[0][:, cols].astype(F32)
                for r in grp[1:]:
                    v = v + r[:, cols].astype(F32)
                if rope and j < 2:
                    v = _rope_t(v, *tabs)
                if j == 0:
                    v = v * QK_SCALE
                o_ref[:, j * D + blk * LANES : j * D + (blk + 1) * LANES] = v.astype(MXU)

    row = pl.BlockSpec((tm, D), lambda i: (i, 0))
    tab = pl.BlockSpec((tm, LANES), lambda i: (i, 0))
    return pl.pallas_call(
        body,
        grid=(S // tm,),
        in_specs=[row] * (3 * nb) + ([tab] * 3 if rope else []),
        out_specs=pl.BlockSpec((tm, 3 * D), lambda i: (i, 0)),
        out_shape=jax.ShapeDtypeStruct((S, 3 * D), MXU),
        compiler_params=_params("parallel"),
        name=name,
    )(*dqs, *dks, *dvs, *(tables if rope else ()))


def _log_fail(z):
    return -(jnp.maximum(z, 0.0) + jnp.log(1.0 + jnp.exp(-jnp.abs(z))))


def _sb_fwd(qkv, *, gather=None, name):
    S, D = qkv.shape[0], qkv.shape[1] // 3
    HP, H = D // LANES, D // HEAD_DIM
    tq = SB_TQ
    assert S % tq == 0
    cpb = tq // CHUNK
    assert S // CHUNK <= LANES
    steps = HP * (S // tq)

    def body(*refs):
        if gather is None:
            q_ref, k_ref, v_ref, o_ref, car_ref, acc_ref, carry0_ref, carry1_ref = refs
        else:
            q_ref, k_ref, v_ref, w_ref, o_ref, car_ref, g_ref, acc_ref, carry0_ref, carry1_ref = refs[:10]
            start, hand_on, finish = _gather_stages(w_ref, g_ref, *refs[10:])
            t = pl.program_id(0) * (S // tq) + pl.program_id(1)
            pl.when(t == 0)(start)
            pl.when(t == steps // 2)(hand_on)
        i = pl.program_id(1)
        lane = lax.broadcasted_iota(jnp.int32, (tq, LANES), 1)
        row = lax.broadcasted_iota(jnp.int32, (tq, LANES), 0)
        kl = lax.broadcasted_iota(jnp.int32, (CHUNK, LANES), 1)
        kr = lax.broadcasted_iota(jnp.int32, (CHUNK, LANES), 0)
        tri = (kr >= kl).astype(MXU)
        qv = q_ref[...]
        qh = (jnp.where(lane < HEAD_DIM, qv, jnp.zeros_like(qv)), jnp.where(lane >= HEAD_DIM, qv, jnp.zeros_like(qv)))
        carry_refs = (carry0_ref, carry1_ref)
        acc_ref[...] = jnp.zeros_like(acc_ref)
        car_ref[...] = jnp.full((2, tq, LANES), NEG, F32)
        for r in carry_refs:
            r[...] = jnp.zeros_like(r)

        def run(chunks, masked):
            offs = [pl.multiple_of(c * CHUNK, CHUNK) for c in chunks]
            kcs = [k_ref[pl.ds(o, CHUNK), :] for o in offs]
            vcs = [v_ref[pl.ds(o, CHUNK), :] for o in offs]
            jobs = [(n, hh) for n in range(len(chunks)) for hh in range(2)]
            zs = {j: lax.dot_general(qh[j[1]], kcs[j[0]], NT, preferred_element_type=F32) for j in jobs}
            if masked:
                cms = [(o + lane) < (i * tq + row) for o in offs]
            lfs = {}
            for j in jobs:
                lf = _log_fail(zs[j])
                lfs[j] = jnp.where(cms[j[0]], lf, 0.0) if masked else lf
            rs = {j: jnp.dot(lfs[j].astype(MXU), tri, preferred_element_type=F32) for j in jobs}
            ws = {}
            for hh in range(2):
                carry = carry_refs[hh][...]
                tile = car_ref[hh]
                for n, c in enumerate(chunks):
                    w = jnp.exp(zs[n, hh] + rs[n, hh] + carry)
                    if masked:
                        w = jnp.where(cms[n], w, 0.0)
                    ws[n, hh] = w.astype(MXU)
                    tile = jnp.where(lane == c, carry, tile)
                    carry = carry + jnp.sum(lfs[n, hh], axis=-1, keepdims=True)
                car_ref[hh] = tile
                carry_refs[hh][...] = carry
            for hh in range(2):
                pv = jnp.dot(ws[0, hh], vcs[0], preferred_element_type=F32)
                for n in range(1, len(chunks)):
                    pv = pv + jnp.dot(ws[n, hh], vcs[n], preferred_element_type=F32)
                acc_ref[hh] += pv

        def max_carry():
            return jnp.maximum(jnp.max(carry0_ref[...]), jnp.max(carry1_ref[...]))

        run([i * cpb + u for u in reversed(range(cpb))], True)

        def cond(st):
            return jnp.logical_and(st[0] >= 0, st[1] >= UNDERFLOW)

        def pair(st):
            c = st[0]
            run([c, c - 1], False)
            return c - 2, max_carry()

        lax.while_loop(cond, pair, (i * cpb - 1, max_carry()))
        o_ref[...] = jnp.where(lane < HEAD_DIM, acc_ref[0], acc_ref[1]).astype(o_ref.dtype)
        if gather is not None:
            pl.when(t == steps - 1)(finish)

    blk = pl.BlockSpec((tq, LANES), lambda h, i: (i, h))
    k_full = pl.BlockSpec((S, LANES), lambda h, i: (0, HP + h))
    v_full = pl.BlockSpec((S, LANES), lambda h, i: (0, 2 * HP + h))
    hbm = pl.BlockSpec(memory_space=pl.ANY)
    in_specs, ops = [blk, k_full, v_full], [qkv, qkv, qkv]
    out_specs = [blk, pl.BlockSpec((2, tq, LANES), lambda h, i: (h, i, 0))]
    out_shape = [jax.ShapeDtypeStruct((S, D), MXU), jax.ShapeDtypeStruct((H, S, LANES), F32)]
    scratch = [pltpu.VMEM((2, tq, LANES), F32), pltpu.VMEM((tq, 1), F32), pltpu.VMEM((tq, 1), F32)]
    if gather is not None:
        in_specs, ops = in_specs + [hbm], ops + [gather]
        out_specs = out_specs + [hbm]
        out_shape = out_shape + [jax.ShapeDtypeStruct((N_CHIPS,) + gather.shape, gather.dtype)]
        scratch = scratch + GATHER_SEMS
    return pl.pallas_call(
        body,
        grid=(HP, S // tq),
        in_specs=in_specs,
        out_specs=out_specs,
        out_shape=out_shape,
        scratch_shapes=scratch,
        compiler_params=_params("arbitrary", "arbitrary"),
        name=name,
    )(*ops)


def _sb_bwd(qkv, do, car, *, scatter=None, name):
    S, D = qkv.shape[0], qkv.shape[1] // 3
    HP = D // LANES
    tq = SB_TQ
    assert S % tq == 0
    cpb = tq // CHUNK
    nq = S // tq

    def body(*refs):
        if scatter is None:
            q_ref, k_ref, v_ref, do_ref, car_ref, dq_ref, dk_hbm, dv_hbm = refs[:8]
            dq_acc, dk_acc, dv_acc, gcar0_ref, gcar1_ref, sem = refs[8:]
        else:
            q_ref, k_ref, v_ref, do_ref, car_ref, g_ref, dq_ref, dk_hbm, dv_hbm, r_ref = refs[:10]
            dq_acc, dk_acc, dv_acc, gcar0_ref, gcar1_ref, sem = refs[10:16]
            start, finish = _scatter_stages(g_ref, r_ref, *refs[16:])
            t = pl.program_id(0) * nq + pl.program_id(1)
            pl.when(t == 0)(start)
        hp = pl.program_id(0)
        i = pl.program_id(1)
        lane = lax.broadcasted_iota(jnp.int32, (tq, LANES), 1)
        row = lax.broadcasted_iota(jnp.int32, (tq, LANES), 0)
        kl = lax.broadcasted_iota(jnp.int32, (CHUNK, LANES), 1)
        kr = lax.broadcasted_iota(jnp.int32, (CHUNK, LANES), 0)
        tri = (kr >= kl).astype(MXU)
        tri_prefix = (kr <= kl).astype(MXU)

        @pl.when(i == 0)
        def _():
            dk_acc[...] = jnp.zeros_like(dk_acc)
            dv_acc[...] = jnp.zeros_like(dv_acc)

        qv = q_ref[...]
        dov = do_ref[...]
        qh = (jnp.where(lane < HEAD_DIM, qv, jnp.zeros_like(qv)), jnp.where(lane >= HEAD_DIM, qv, jnp.zeros_like(qv)))
        doh = (jnp.where(lane < HEAD_DIM, dov, jnp.zeros_like(dov)), jnp.where(lane >= HEAD_DIM, dov, jnp.zeros_like(dov)))
        gcar_refs = (gcar0_ref, gcar1_ref)
        dq_acc[...] = jnp.zeros_like(dq_acc)
        for r in gcar_refs:
            r[...] = jnp.zeros_like(r)

        def run(chunks, masked):
            offs = [pl.multiple_of(c * CHUNK, CHUNK) for c in chunks]
            kcs = [k_ref[pl.ds(o, CHUNK), :] for o in offs]
            vcs = [v_ref[pl.ds(o, CHUNK), :] for o in offs]
            jobs = [(n, hh) for n in range(len(chunks)) for hh in range(2)]
            if masked:
                cms = [(o + lane) < (i * tq + row) for o in offs]
            zs = {j: lax.dot_general(qh[j[1]], kcs[j[0]], NT, preferred_element_type=F32) for j in jobs}
            dws = {j: lax.dot_general(doh[j[1]], vcs[j[0]], NT, preferred_element_type=F32) for j in jobs}
            lfs, sigs = {}, {}
            for j in jobs:
                lf = _log_fail(zs[j])
                sigs[j] = jnp.exp(zs[j] + lf)
                lfs[j] = jnp.where(cms[j[0]], lf, 0.0) if masked else lf
            rs = {j: jnp.dot(lfs[j].astype(MXU), tri, preferred_element_type=F32) for j in jobs}
            ws, gs = {}, {}
            for hh in range(2):
                tile = car_ref[hh]
                for n, c in enumerate(chunks):
                    carry = jnp.sum(jnp.where(lane == c, tile, 0.0), axis=-1, keepdims=True)
                    w = jnp.exp(zs[n, hh] + rs[n, hh] + carry)
                    if masked:
                        w = jnp.where(cms[n], w, 0.0)
                    ws[n, hh] = w.astype(MXU)
                    gs[n, hh] = w * dws[n, hh]
            big_gs = {j: jnp.dot(gs[j].astype(MXU), tri_prefix, preferred_element_type=F32) for j in jobs}
            dzs = {}
            for hh in range(2):
                gcar = gcar_refs[hh][...]
                for n in range(len(chunks)):
                    dz = gs[n, hh] - sigs[n, hh] * (big_gs[n, hh] + gcar)
                    if masked:
                        dz = jnp.where(cms[n], dz, 0.0)
                    dzs[n, hh] = dz.astype(MXU)
                    gcar = gcar + jnp.sum(gs[n, hh], axis=-1, keepdims=True)
                gcar_refs[hh][...] = gcar
            for hh in range(2):
                part = jnp.dot(dzs[0, hh], kcs[0], preferred_element_type=F32)
                for n in range(1, len(chunks)):
                    part = part + jnp.dot(dzs[n, hh], kcs[n], preferred_element_type=F32)
                dq_acc[hh] += part
            for n, o in enumerate(offs):
                dk_acc[pl.ds(o, CHUNK), :] += lax.dot_general(
                    dzs[n, 0], qh[0], TN, preferred_element_type=F32
                ) + lax.dot_general(dzs[n, 1], qh[1], TN, preferred_element_type=F32)
                dv_acc[pl.ds(o, CHUNK), :] += lax.dot_general(
                    ws[n, 0], doh[0], TN, preferred_element_type=F32
                ) + lax.dot_general(ws[n, 1], doh[1], TN, preferred_element_type=F32)

        colmax = jnp.maximum(jnp.max(car_ref[0], axis=0, keepdims=True), jnp.max(car_ref[1], axis=0, keepdims=True))
        lane1 = lax.broadcasted_iota(jnp.int32, (1, LANES), 1)
        dead = jnp.logical_and(colmax < UNDERFLOW, lane1 < i * cpb)
        first_pair = jnp.sum(jnp.where(dead, 1.0, 0.0)).astype(jnp.int32) // 2

        def loop_body(p, carry):
            run([2 * p, 2 * p + 1], False)
            return carry

        lax.fori_loop(first_pair, i * cpb // 2, loop_body, 0)
        run([i * cpb + u for u in range(cpb)], True)
        dq_ref[...] = jnp.where(lane < HEAD_DIM, dq_acc[0], dq_acc[1])

        @pl.when(i == nq - 1)
        def _():
            cols = pl.ds(pl.multiple_of(hp * LANES, LANES), LANES)
            for src, dst in ((dk_acc, dk_hbm), (dv_acc, dv_hbm)):
                cp = pltpu.make_async_copy(src, dst.at[:, cols], sem)
                cp.start()
                cp.wait()

        if scatter is not None:
            pl.when(t == HP * nq - 1)(finish)

    blk = pl.BlockSpec((tq, LANES), lambda h, i: (i, h))
    k_full = pl.BlockSpec((S, LANES), lambda h, i: (0, HP + h))
    v_full = pl.BlockSpec((S, LANES), lambda h, i: (0, 2 * HP + h))
    hbm = pl.BlockSpec(memory_space=pl.ANY)
    in_specs = [blk, k_full, v_full, blk, pl.BlockSpec((2, tq, LANES), lambda h, i: (h, i, 0))]
    ops = [qkv, qkv, qkv, do, car]
    out_specs, out_shape = [blk, hbm, hbm], [jax.ShapeDtypeStruct((S, D), F32)] * 3
    scratch = [
        pltpu.VMEM((2, tq, LANES), F32),
        pltpu.VMEM((S, LANES), F32),
        pltpu.VMEM((S, LANES), F32),
        pltpu.VMEM((tq, 1), F32),
        pltpu.VMEM((tq, 1), F32),
        pltpu.SemaphoreType.DMA,
    ]
    if scatter is not None:
        in_specs, ops = in_specs + [hbm], ops + [scatter]
        out_specs = out_specs + [hbm]
        out_shape = out_shape + [jax.ShapeDtypeStruct((3,) + scatter.shape[1:], scatter.dtype)]
        scratch = scratch + SCATTER_SEMS
    return pl.pallas_call(
        body,
        grid=(HP, nq),
        in_specs=in_specs,
        out_specs=out_specs,
        out_shape=out_shape,
        scratch_shapes=scratch,
        compiler_params=_params("arbitrary", "arbitrary"),
        name=name,
    )(*ops)


def _perm(x, d):
    if d == 1:
        return x
    S, D = x.shape
    return x.reshape(S // d, d, D).transpose(1, 0, 2).reshape(S, D)


def _unperm(x, d):
    if d == 1:
        return x
    S, D = x.shape
    return x.reshape(d, S // d, D).transpose(1, 0, 2).reshape(S, D)


def _dil_tb(S, branch):
    return min(2048, S // DILATIONS[branch])


def _dil_masks():
    qi = lax.broadcasted_iota(jnp.int32, (CHUNK, CHUNK), 0)
    kj = lax.broadcasted_iota(jnp.int32, (CHUNK, CHUNK), 1)
    return qi, kj


def _dil_fwd(q, k, v, *, branch, name):
    S, D = q.shape
    HP = D // LANES
    tb = _dil_tb(S, branch)
    nb, nsub = S // tb, tb // CHUNK
    bps = nb // DILATIONS[branch]

    def body(q_ref, k_ref, v_ref, kp_ref, vp_ref, o_ref, l_ref):
        n = pl.program_id(0)
        hp = pl.program_id(1)

        @pl.when(hp == 0)
        def _():
            l_ref[...] = jnp.zeros_like(l_ref)

        first = jnp.bitwise_and(n, bps - 1) == 0
        qi, kj = _dil_masks()
        hl = lax.broadcasted_iota(jnp.int32, (CHUNK, LANES), 1)
        m_cur = kj <= qi
        m_prev = (kj - qi) >= 0
        m_prev0 = (kj - qi) >= jnp.where(first, 2 * CHUNK, 0)
        hms = (hl < HEAD_DIM, hl >= HEAD_DIM)
        tiles, scores = [], {}
        for u in range(nsub):
            rows = slice(u * CHUNK, (u + 1) * CHUNK)
            qs, kc, vc = q_ref[rows, :], k_ref[rows, :], v_ref[rows, :]
            if u == 0:
                kp, vp, pm = kp_ref[...], vp_ref[...], m_prev0
            else:
                prev = slice((u - 1) * CHUNK, u * CHUNK)
                kp, vp, pm = k_ref[prev, :], v_ref[prev, :], m_prev
            tiles.append((rows, vp, vc, pm))
            for hh in range(2):
                qh = jnp.where(hms[hh], qs, jnp.zeros_like(qs))
                scores[u, hh] = (
                    lax.dot_general(qh, kp, NT, preferred_element_type=F32),
                    lax.dot_general(qh, kc, NT, preferred_element_type=F32),
                )
        probs = {}
        for (u, hh), (s_p, s_c) in scores.items():
            s_p = jnp.where(tiles[u][3], s_p, NEG)
            s_c = jnp.where(m_cur, s_c, NEG)
            m = jnp.max(jnp.maximum(s_p, s_c), axis=-1, keepdims=True)
            p_p, p_c = jnp.exp(s_p - m), jnp.exp(s_c - m)
            den = jnp.sum(p_p + p_c, axis=-1, keepdims=True)
            probs[u, hh] = (p_p.astype(MXU), p_c.astype(MXU), 1.0 / den, m + jnp.log(den))
        for u, (rows, vp, vc, _) in enumerate(tiles):
            o_t, l_t = None, l_ref[rows, :]
            for hh in range(2):
                p_p, p_c, inv, lse = probs[u, hh]
                num = jnp.dot(p_p, vp, preferred_element_type=F32) + jnp.dot(p_c, vc, preferred_element_type=F32)
                o_h = num * inv
                o_t = o_h if hh == 0 else jnp.where(hms[1], o_h, o_t)
                l_t = jnp.where(hl == 2 * hp + hh, lse, l_t)
            o_ref[rows, :] = o_t
            l_ref[rows, :] = l_t

    blk = pl.BlockSpec((tb, LANES), lambda n, h: (n, h))
    tail = pl.BlockSpec((CHUNK, LANES), lambda n, h: (jnp.maximum(n * nsub - 1, 0), h))
    return pl.pallas_call(
        body,
        grid=(nb, HP),
        in_specs=[blk, blk, blk, tail, tail],
        out_specs=[blk, pl.BlockSpec((tb, LANES), lambda n, h: (n, 0))],
        out_shape=[jax.ShapeDtypeStruct((S, D), F32), jax.ShapeDtypeStruct((S, LANES), F32)],
        compiler_params=_params("parallel", "arbitrary"),
        name=name,
    )(q, k, v, k, v)


def _dil_combine(os, ls, *, name):
    S, D = os[0].shape
    tm = _pick(S, (256, 128))

    def body(o0_ref, o1_ref, o2_ref, l0_ref, l1_ref, l2_ref, out_ref, lse_ref):
        l0, l1, l2 = l0_ref[...], l1_ref[...], l2_ref[...]
        m = jnp.maximum(jnp.maximum(l0, l1), l2)
        a0, a1, a2 = jnp.exp(l0 - m), jnp.exp(l1 - m), jnp.exp(l2 - m)
        den = a0 + a1 + a2
        lse_ref[...] = m + jnp.log(den)
        inv = 1.0 / den
        ws = (a0 * inv, a1 * inv, a2 * inv)
        lane = lax.broadcasted_iota(jnp.int32, (tm, LANES), 1)
        for hp in range(D // LANES):
            cols = slice(hp * LANES, (hp + 1) * LANES)
            acc = None
            for w, o_ref in zip(ws, (o0_ref, o1_ref, o2_ref)):
                spread = jnp.where(lane < HEAD_DIM, _lane_col(w, lane, 2 * hp), _lane_col(w, lane, 2 * hp + 1))
                term = spread * o_ref[:, cols]
                acc = term if acc is None else acc + term
            out_ref[:, cols] = acc

    row = pl.BlockSpec((tm, D), lambda i: (i, 0))
    vec = pl.BlockSpec((tm, LANES), lambda i: (i, 0))
    return pl.pallas_call(
        body,
        grid=(S // tm,),
        in_specs=[row] * 3 + [vec] * 3,
        out_specs=[row, vec],
        out_shape=[jax.ShapeDtypeStruct((S, D), F32), jax.ShapeDtypeStruct((S, LANES), F32)],
        compiler_params=_params("parallel"),
        name=name,
    )(*os, *ls)


def _lane_col(tile, lane, at):
    return jnp.sum(jnp.where(lane == at, tile, 0.0), axis=-1, keepdims=True)


def _head_stats(do, out, lse_packed, *, name):
    S, D = out.shape
    H = D // HEAD_DIM
    assert 2 * H <= LANES
    tm = _pick(S, (256, 128))

    def body(do_ref, out_ref, l_ref, s_ref):
        lane = lax.broadcasted_iota(jnp.int32, (tm, LANES), 1)
        packed = l_ref[...]
        for hp in range(D // LANES):
            cols = slice(hp * LANES, (hp + 1) * LANES)
            prod = do_ref[:, cols].astype(F32) * out_ref[:, cols]
            for hh in range(2):
                head = (lane >= HEAD_DIM) if hh else (lane < HEAD_DIM)
                delta = jnp.sum(jnp.where(head, prod, 0.0), axis=-1, keepdims=True)
                packed = jnp.where(lane == H + 2 * hp + hh, delta, packed)
        s_ref[...] = packed

    row = pl.BlockSpec((tm, D), lambda i: (i, 0))
    vec = pl.BlockSpec((tm, LANES), lambda i: (i, 0))
    return pl.pallas_call(
        body,
        grid=(S // tm,),
        in_specs=[row, row, vec],
        out_specs=vec,
        out_shape=jax.ShapeDtypeStruct((S, LANES), F32),
        compiler_params=_params("parallel"),
        name=name,
    )(do, out, lse_packed)


def _dil_bwd_dq(q, k, v, do, stats, *, branch, name):
    S, D = q.shape
    HP, H = D // LANES, D // HEAD_DIM
    tb = _dil_tb(S, branch)
    nb, nsub = S // tb, tb // CHUNK
    bps = nb // DILATIONS[branch]

    def body(q_ref, k_ref, v_ref, kp_ref, vp_ref, do_ref, st_ref, dq_ref):
        hp = pl.program_id(0)
        n = pl.program_id(1)
        first = jnp.bitwise_and(n, bps - 1) == 0
        qi, kj = _dil_masks()
        hl = lax.broadcasted_iota(jnp.int32, (CHUNK, LANES), 1)
        m_cur = kj <= qi
        m_prev = (kj - qi) >= 0
        m_prev0 = (kj - qi) >= jnp.where(first, 2 * CHUNK, 0)
        hms = (hl < HEAD_DIM, hl >= HEAD_DIM)
        jobs, cols = [], {}
        for u in range(nsub):
            rows = slice(u * CHUNK, (u + 1) * CHUNK)
            qs, kc, vc = q_ref[rows, :], k_ref[rows, :], v_ref[rows, :]
            if u == 0:
                kp, vp, pm = kp_ref[...], vp_ref[...], m_prev0
            else:
                prev = slice((u - 1) * CHUNK, u * CHUNK)
                kp, vp, pm = k_ref[prev, :], v_ref[prev, :], m_prev
            do_t, st_t = do_ref[rows, :], st_ref[rows, :]
            for hh in range(2):
                qh = jnp.where(hms[hh], qs, jnp.zeros_like(qs))
                dob = jnp.where(hms[hh], do_t, jnp.zeros_like(do_t))
                cols[u, hh] = (_lane_col(st_t, hl, H + 2 * hp + hh), _lane_col(st_t, hl, 2 * hp + hh))
                for kk, vv, msk in ((kp, vp, pm), (kc, vc, m_cur)):
                    s = lax.dot_general(qh, kk, NT, preferred_element_type=F32)
                    dp = lax.dot_general(dob, vv, NT, preferred_element_type=F32)
                    jobs.append((u, hh, s, dp, kk, msk))
        dss = []
        for u, hh, s, dp, kk, msk in jobs:
            delta, lse = cols[u, hh]
            p = jnp.exp(jnp.where(msk, s, NEG) - lse)
            dss.append((p * (dp - delta)).astype(MXU))
        dq_h = {}
        for (u, hh, _, _, kk, _), ds in zip(jobs, dss):
            part = jnp.dot(ds, kk, preferred_element_type=F32)
            dq_h[u, hh] = part if (u, hh) not in dq_h else dq_h[u, hh] + part
        for u in range(nsub):
            dq_ref[u * CHUNK : (u + 1) * CHUNK, :] = jnp.where(hms[0], dq_h[u, 0], dq_h[u, 1]).astype(MXU)

    blk = pl.BlockSpec((tb, LANES), lambda h, n: (n, h))
    tail = pl.BlockSpec((CHUNK, LANES), lambda h, n: (jnp.maximum(n * nsub - 1, 0), h))
    return pl.pallas_call(
        body,
        grid=(HP, nb),
        in_specs=[blk, blk, blk, tail, tail, blk, pl.BlockSpec((tb, LANES), lambda h, n: (n, 0))],
        out_specs=blk,
        out_shape=jax.ShapeDtypeStruct((S, D), MXU),
        compiler_params=_params("parallel", "parallel"),
        name=name,
    )(q, k, v, k, v, do, stats)


def _dil_bwd_dkv(q, k, v, do, stats, *, branch, name):
    S, D = q.shape
    HP, H = D // LANES, D // HEAD_DIM
    tb = _dil_tb(S, branch)
    nb, nsub = S // tb, tb // CHUNK
    last_chunk = S // CHUNK - 1
    bps = nb // DILATIONS[branch]

    def body(q_ref, k_ref, v_ref, do_ref, st_ref, qn_ref, don_ref, stn_ref, dk_ref, dv_ref):
        hp = pl.program_id(0)
        n = pl.program_id(1)
        last =jnp.bitwise_and(n + 1, bps - 1) == 0
        qi, kj = _dil_masks()
        hl = lax.broadcasted_iota(jnp.int32, (CHUNK, LANES), 1)
        m_cur = kj <= qi
        m_next = (kj - qi) >= 0
        m_next_last = (kj - qi) >= jnp.where(last, 2 * CHUNK, 0)
        hms = (hl < HEAD_DIM, hl >= HEAD_DIM)
        qtiles = {}
        for t in range(nsub + 1):
            if t == nsub:
                qs, do_t, st_t = qn_ref[...], don_ref[...], stn_ref[...]
            else:
                rows = slice(t * CHUNK, (t + 1) * CHUNK)
                qs, do_t, st_t = q_ref[rows, :], do_ref[rows, :], st_ref[rows, :]
            for hh in range(2):
                qtiles[t, hh] = (
                    jnp.where(hms[hh], qs, jnp.zeros_like(qs)),
                    jnp.where(hms[hh], do_t, jnp.zeros_like(do_t)),
                    _lane_col(st_t, hl, H + 2 * hp + hh),
                    _lane_col(st_t, hl, 2 * hp + hh),
                )
        jobs = []
        for u in range(nsub):
            rows = slice(u * CHUNK, (u + 1) * CHUNK)
            kc, vc = k_ref[rows, :], v_ref[rows, :]
            for t, msk in ((u, m_cur), (u + 1, m_next_last if u == nsub - 1 else m_next)):
                for hh in range(2):
                    qh, dob, _, _ = qtiles[t, hh]
                    s = lax.dot_general(qh, kc, NT, preferred_element_type=F32)
                    dp = lax.dot_general(dob, vc, NT, preferred_element_type=F32)
                    jobs.append((u, t, hh, s, dp, msk))
        pds = []
        for u, t, hh, s, dp, msk in jobs:
            _, _, delta, lse = qtiles[t, hh]
            p = jnp.exp(jnp.where(msk, s, NEG) - lse)
            pds.append((p.astype(MXU), (p * (dp - delta)).astype(MXU)))
        dks, dvs = {}, {}
        for (u, t, hh, _, _, _), (pb, dsb) in zip(jobs, pds):
            qh, dob, _, _ = qtiles[t, hh]
            dv_p = lax.dot_general(pb, dob, TN, preferred_element_type=F32)
            dk_p = lax.dot_general(dsb, qh, TN, preferred_element_type=F32)
            dvs[u] = dv_p if u not in dvs else dvs[u] + dv_p
            dks[u] = dk_p if u not in dks else dks[u] + dk_p
        for u in range(nsub):
            dk_ref[u * CHUNK : (u + 1) * CHUNK, :] = dks[u].astype(MXU)
            dv_ref[u * CHUNK : (u + 1) * CHUNK, :] = dvs[u].astype(MXU)

    blk = pl.BlockSpec((tb, LANES), lambda h, n: (n, h))
    head = pl.BlockSpec((CHUNK, LANES), lambda h, n: (jnp.minimum((n + 1) * nsub, last_chunk), h))
    st_blk = pl.BlockSpec((tb, LANES), lambda h, n: (n, 0))
    st_head = pl.BlockSpec((CHUNK, LANES), lambda h, n: (jnp.minimum((n + 1) * nsub, last_chunk), 0))
    return pl.pallas_call(
        body,
        grid=(HP, nb),
        in_specs=[blk] * 4 + [st_blk, head, head, st_head],
        out_specs=[blk, blk],
        out_shape=[jax.ShapeDtypeStruct((S, D), MXU)] * 2,
        compiler_params=_params("parallel", "parallel"),
        name=name,
    )(q, k, v, do, stats, q, do, stats)


def _chip_peers():
    x, y, c = lax.axis_index("x"), lax.axis_index("y"), lax.axis_index("c")
    peers = [(1 - x, y, c), (x, 1 - y, c), (1 - x, 1 - y, c)]
    return x, y, c, peers


def _allgather_weights(wsh, *, name):
    R, D = wsh.shape

    def body(w_ref, out_ref, send_sems, recv_sems, local_sem):
        for stage in _gather_stages(w_ref, out_ref, send_sems, recv_sems, local_sem):
            stage()

    hbm = pl.BlockSpec(memory_space=pl.ANY)
    return pl.pallas_call(
        body,
        in_specs=[hbm],
        out_specs=hbm,
        out_shape=jax.ShapeDtypeStruct((N_CHIPS, R, D), wsh.dtype),
        scratch_shapes=GATHER_SEMS,
        name=name,
    )(wsh)


GATHER_SEMS = [pltpu.SemaphoreType.DMA((6,)), pltpu.SemaphoreType.DMA((6,)), pltpu.SemaphoreType.DMA]
SCATTER_SEMS = [pltpu.SemaphoreType.DMA((3,)), pltpu.SemaphoreType.DMA((3,))]


def _gather_stages(w_ref, out_ref, send_sems, recv_sems, local_sem):
    half = w_ref.shape[0] // 2
    x, y, c, peers = _chip_peers()
    me, sibling = 2 * x + y, (x, y, 1 - c)
    chips = [2 * p[0] + p[1] for p in peers]

    def rows(chip, hc):
        return out_ref.at[chip, pl.ds(hc * half, half), :]

    def copy(k, chip, hc, to, src=None):
        return pltpu.make_async_remote_copy(
            src_ref=rows(chip, hc) if src is None else src, dst_ref=rows(chip, hc),
            send_sem=send_sems.at[k], recv_sem=recv_sems.at[k], device_id=to, device_id_type=MESH,
        )

    mine = pltpu.make_async_copy(w_ref, out_ref.at[me], local_sem)
    first = [copy(j, me, c, p, src=w_ref.at[pl.ds(c * half, half), :]) for j, p in enumerate(peers)]
    passed = [copy(3 + j, chip, c, sibling) for j, chip in enumerate(chips)]

    def start():
        mine.start()
        for cp in first:
            cp.start()

    def hand_on():
        for j, chip in enumerate(chips):
            copy(j, chip, c, peers[j]).wait_recv()
            passed[j].start()

    def finish():
        for j, chip in enumerate(chips):
            copy(3 + j, chip, 1 - c, sibling).wait_recv()
        for cp in first + passed:
            cp.wait_send()
        mine.wait()

    return start, hand_on, finish


def _scatter_stages(g_ref, out_ref, send_sems, recv_sems):
    x, y, c, peers = _chip_peers()
    sends = [
        pltpu.make_async_remote_copy(
            src_ref=g_ref.at[2 * p[0] + p[1]], dst_ref=out_ref.at[j], send_sem=send_sems.at[j],
            recv_sem=recv_sems.at[j], device_id=p, device_id_type=MESH,
        )
        for j, p in enumerate(peers)
    ]

    def start():
        for cp in sends:
            cp.start()

    def finish():
        for j, p in enumerate(peers):
            pltpu.make_async_remote_copy(
                src_ref=out_ref.at[j], dst_ref=out_ref.at[j], send_sem=send_sems.at[j], recv_sem=recv_sems.at[j],
                device_id=p, device_id_type=MESH,
            ).wait_recv()
        for cp in sends:
            cp.wait_send()

    return start, finish


def _scatter_grads(gflat, *, name):
    _, R, D = gflat.shape

    def body(g_ref, out_ref, send_sems, recv_sems):
        for stage in _scatter_stages(g_ref, out_ref, send_sems, recv_sems):
            stage()

    hbm = pl.BlockSpec(memory_space=pl.ANY)
    return pl.pallas_call(
        body,
        in_specs=[hbm],
        out_specs=hbm,
        out_shape=jax.ShapeDtypeStruct((3, R, D), gflat.dtype),
        scratch_shapes=SCATTER_SEMS,
        name=name,
    )(gflat)


def _swap_with_sibling(part, *, name):
    def body(p_ref, out_ref, send_sem, recv_sem):
        x, y, c = lax.axis_index("x"), lax.axis_index("y"), lax.axis_index("c")
        cp = pltpu.make_async_remote_copy(
            src_ref=p_ref, dst_ref=out_ref, send_sem=send_sem, recv_sem=recv_sem,
            device_id=(x, y, 1 - c), device_id_type=MESH,
        )
        cp.start()
        cp.wait()

    hbm = pl.BlockSpec(memory_space=pl.ANY)
    return pl.pallas_call(
        body,
        in_specs=[hbm],
        out_specs=hbm,
        out_shape=jax.ShapeDtypeStruct(part.shape, part.dtype),
        scratch_shapes=[pltpu.SemaphoreType.DMA, pltpu.SemaphoreType.DMA],
        name=name,
    )(part)


def _gather_from_all(vec, *, name):
    R, D = vec.shape

    def body(v_ref, out_ref, send_sems, recv_sems):
        x, y, c = lax.axis_index("x"), lax.axis_index("y"), lax.axis_index("c")
        me = 4 * x + 2 * y + c
        out_ref[me] = v_ref[...]
        rel = [(j >> 2 & 1, j >> 1 & 1, j & 1) for j in range(1, 8)]
        peers = [((1 - x) if fx else x, (1 - y) if fy else y, (1 - c) if fc else c) for fx, fy, fc in rel]
        sends = [
            pltpu.make_async_remote_copy(
                src_ref=v_ref, dst_ref=out_ref.at[me], send_sem=send_sems.at[j], recv_sem=recv_sems.at[j],
                device_id=p, device_id_type=MESH,
            )
            for j, p in enumerate(peers)
        ]
        for cp in sends:
            cp.start()
        for j, p in enumerate(peers):
            theirs = out_ref.at[4 * p[0] + 2 * p[1] + p[2]]
            pltpu.make_async_remote_copy(
                src_ref=theirs, dst_ref=theirs, send_sem=send_sems.at[j], recv_sem=recv_sems.at[j],
                device_id=p, device_id_type=MESH,
            ).wait_recv()
        for cp in sends:
            cp.wait_send()

    vmem = pl.BlockSpec(memory_space=pltpu.VMEM)
    return pl.pallas_call(
        body,
        in_specs=[vmem],
        out_specs=vmem,
        out_shape=jax.ShapeDtypeStruct((8, R, D), vec.dtype),
        scratch_shapes=[pltpu.SemaphoreType.DMA((7,)), pltpu.SemaphoreType.DMA((7,))],
        name=name,
    )(vec)


def _sum4(a, rest, *, name):
    R, D = a.shape
    tm = _pick(R, (256, 128, 8))

    def body(a_ref, r_ref, o_ref):
        o_ref[...] = ((a_ref[...] + r_ref[0].astype(F32)) + r_ref[1].astype(F32)) + r_ref[2].astype(F32)

    row = pl.BlockSpec((tm, D), lambda i: (i, 0))
    return pl.pallas_call(
        body,
        grid=(R // tm,),
        in_specs=[row, pl.BlockSpec((3, tm, D), lambda i: (0, i, 0))],
        out_specs=row,
        out_shape=jax.ShapeDtypeStruct((R, D), F32),
        compiler_params=_params("parallel"),
        name=name,
    )(a, rest)


def _adamw(parts, w, m, v, *, name):
    P, R, D = parts.shape
    tm = _pick(R, (256, 128, 8))

    def body(p_ref, w_ref, m_ref, v_ref, g_ref, d_ref, nm_ref, nv_ref):
        g = p_ref[0]
        for k in range(1, P):
            g = g + p_ref[k]
        nm = ADAM_B1 * m_ref[...] + (1.0 - ADAM_B1) * g
        nv = ADAM_B2 * v_ref[...] + (1.0 - ADAM_B2) * jnp.square(g)
        m_hat = nm / (1.0 - ADAM_B1**ADAM_STEP)
        v_hat = nv / (1.0 - ADAM_B2**ADAM_STEP)
        g_ref[...] = g
        d_ref[...] = -ADAM_LR * (m_hat / (jnp.sqrt(v_hat) + ADAM_EPS) + ADAM_WD * w_ref[...])
        nm_ref[...] = nm
        nv_ref[...] = nv

    row = pl.BlockSpec((tm, D), lambda i: (i, 0))
    return pl.pallas_call(
        body,
        grid=(R // tm,),
        in_specs=[pl.BlockSpec((P, tm, D), lambda i: (0, i, 0)), row, row, row],
        out_specs=[row] * 4,
        out_shape=[jax.ShapeDtypeStruct((R, D), F32)] * 4,
        compiler_params=_params("parallel"),
        name=name,
    )(parts, w, m, v)


MATS = ("w_qkv", "w_o", "w_ff1", "w_ff2")
VECS = ("ln1_g", "ln1_b", "ln2_g", "ln2_b")
PARAM_ORDER = ("w_qkv", "w_o", "ln1_g", "ln1_b", "w_ff1", "w_ff2", "ln2_g", "ln2_b")


def _pack_rows(arrs, D):
    return jnp.concatenate([a.reshape(-1, D) for a in arrs], axis=0)


def kernel(x, w_qkv_0, w_o_0, ln1_g_0, ln1_b_0, w_ff1_0, w_ff2_0, ln2_g_0, ln2_b_0, w_qkv_1, w_o_1, ln1_g_1, ln1_b_1, w_ff1_1, w_ff2_1, ln2_g_1, ln2_b_1, loss_target, m_w_qkv_0, m_w_o_0, m_ln1_g_0, m_ln1_b_0, m_w_ff1_0, m_w_ff2_0, m_ln2_g_0, m_ln2_b_0, m_w_qkv_1, m_w_o_1, m_ln1_g_1, m_ln1_b_1, m_w_ff1_1, m_w_ff2_1, m_ln2_g_1, m_ln2_b_1, v_w_qkv_0, v_w_o_0, v_ln1_g_0, v_ln1_b_0, v_w_ff1_0, v_w_ff2_0, v_ln2_g_0, v_ln2_b_0, v_w_qkv_1, v_w_o_1, v_ln1_g_1, v_ln1_b_1, v_w_ff1_1, v_w_ff2_1, v_ln2_g_1, v_ln2_b_1):
    given = dict(locals())
    xs = x[0]
    target = loss_target[0]
    S, D = xs.shape
    C3 = 3 * D // N_CHIPS
    names = [f"{p}_{l}" for l in range(DEPTH) for p in PARAM_ORDER]
    mat_names = [f"{p}_{l}" for l in range(DEPTH) for p in MATS]
    vec_names = [f"{p}_{l}" for l in range(DEPTH) for p in VECS]
    rows = {"w_qkv": C3, "w_o": D // N_CHIPS, "w_ff1": D, "w_ff2": D}
    layer_rows = sum(rows.values())

    packed_w = _pack_rows([given[n] for n in mat_names], D).astype(MXU)
    offset, r0 = {}, 0
    for n in mat_names:
        offset[n] = r0
        r0 += rows[n[:-2]]

    def gathered_matrix(p, l, gathered, base):
        lo = offset[f"{p}_{l}"] - base
        t = gathered[:, lo : lo + rows[p], :]
        if p in ("w_qkv", "w_ff1"):
            return t.reshape(N_CHIPS, D, rows[p]).transpose(1, 0, 2).reshape(D, N_CHIPS * rows[p])
        return t.reshape(N_CHIPS * rows[p], D)

    short = {"w_qkv": "qkv", "w_o": "o", "w_ff1": "ff1", "w_ff2": "ff2"}
    first_gather = _allgather_weights(packed_w[:C3], name="allgather_w_qkv_0")
    W = [{"qkv": gathered_matrix("w_qkv", 0, first_gather, 0)}]
    tables = _rope_tables(S)

    h, hb = xs, xs.astype(MXU)
    saved = []
    for l in range(DEPTH):
        g1, b1, g2, b2 = (given[f"{p}_{l}"] for p in VECS)
        qkv = _mm_nn(hb, W[l]["qkv"], epi="qkv", extras=tables if l == 1 else (), name=f"qkv_{l}")
        st = dict(hb=hb, qkv=qkv)
        if l == 0:
            o, car, rest_gather = _sb_fwd(qkv, gather=packed_w[C3:], name="stickbreak_fwd")
            W.append({})
            for n in mat_names[1:]:
                W[int(n[-1])][short[n[:-2]]] = gathered_matrix(n[:-2], int(n[-1]), rest_gather, C3)
            st.update(car=car)
        else:
            q, k, v = qkv[:, :D], qkv[:, D : 2 * D], qkv[:, 2 * D :]
            qp, kp, vp = ([_perm(t, d) for d in DILATIONS] for t in (q, k, v))
            branches = [
                _dil_fwd(qp[i], kp[i], vp[i], branch=i, name=f"dilated_fwd_d{d}") for i, d in enumerate(DILATIONS)
            ]
            o, lse = _dil_combine(
                [_unperm(ob, d) for (ob, _), d in zip(branches, DILATIONS)],
                [_unperm(lb, d) for (_, lb), d in zip(branches, DILATIONS)],
                name="dilated_combine",
            )
            st.update(qp=qp, kp=kp, vp=vp, lse=lse)
        y1, x1, x1b = _mm_nn(o, W[l]["o"], epi="ln", extras=(h, g1, b1), name=f"attn_out_ln1_{l}")
        hp = _mm_nn(x1b, W[l]["ff1"], out_dtype=MXU, name=f"ff1_{l}")
        y2, x2, x2b = _mm_nn(hp, W[l]["ff2"], a_op="relu2", epi="ln", extras=(x1, g2, b2), name=f"ff2_ln2_{l}")
        st.update(o=o, y1=y1, x1b=x1b, hp=hp, y2=y2)
        saved.append(st)
        h, hb = x2, x2b

    grads = {}
    grad_blocks = lambda ns: jnp.concatenate([grads[n].reshape(N_CHIPS, -1, D) for n in ns], axis=1)
    top = saved[-1]
    dy2, dy2b, dg, db, loss_lanes = _ln_bwd_loss(
        top["y2"], given[f"ln2_g_{DEPTH - 1}"], given[f"ln2_b_{DEPTH - 1}"], target, name="loss_ln2_bwd"
    )
    loss = lax.psum(jnp.sum(loss_lanes) * (0.5 / D), ("x", "y", "c"))
    grads[f"ln2_g_{DEPTH - 1}"], grads[f"ln2_b_{DEPTH - 1}"] = dg, db
    grad_x = None
    for l in reversed(range(DEPTH)):
        st = saved[l]
        dhp = _mm_nt(dy2b, W[l]["ff2"], epi="relu2grad", extra=st["hp"], out_dtype=MXU, name=f"d_ff2_in_{l}")
        grads[f"w_ff2_{l}"] = _mm_tn(st["hp"], dy2b, a_op="relu2", groups=1, name=f"d_w_ff2_{l}")
        dy1, dy1b, grads[f"ln1_g_{l}"], grads[f"ln1_b_{l}"] = _mm_nt_ln_bwd(
            dhp, W[l]["ff1"], dy2, st["y1"], given[f"ln1_g_{l}"], name=f"d_ff1_in_ln1_bwd_{l}"
        )
        grads[f"w_ff1_{l}"] = _mm_tn(st["x1b"], dhp, a_op=None, groups=N_CHIPS, name=f"d_w_ff1_{l}")
        grads[f"w_o_{l}"] = _mm_tn(st["o"], dy1b, a_op=None, groups=1, name=f"d_w_o_{l}")
        if l == 0:
            do = _mm_nt(dy1b, W[l]["o"], epi=None, extra=None, out_dtype=MXU, name=f"d_attn_out_{l}")
            dq, dk, dv, from_chips_rest = _sb_bwd(
                st["qkv"], do, st["car"], scatter=grad_blocks(mat_names[1:]).astype(MXU), name="stickbreak_bwd"
            )
            dqkv = _qkv_bwd_post([dq], [dk], [dv], None, name=f"qkv_bwd_post_{l}")
        else:
            do = _mm_nt(dy1b, W[l]["o"], epi=None, extra=None, out_dtype=MXU, name=f"d_attn_out_{l}")
            stats = _head_stats(do, st["o"], st["lse"], name="dilated_head_stats")
            dqs, dks, dvs = [], [], []
            for i, d in enumerate(DILATIONS):
                ops = (st["qp"][i], st["kp"][i], st["vp"][i], _perm(do, d), _perm(stats, d))
                dqs.append(_unperm(_dil_bwd_dq(*ops, branch=i, name=f"dilated_bwd_dq_d{d}"), d))
                dk_d, dv_d = _dil_bwd_dkv(*ops, branch=i, name=f"dilated_bwd_dkv_d{d}")
                dks.append(_unperm(dk_d, d))
                dvs.append(_unperm(dv_d, d))
            dqkv = _qkv_bwd_post(dqs, dks, dvs, tables, name=f"qkv_bwd_post_{l}")
        grads[f"w_qkv_{l}"] = _mm_tn(st["hb"], dqkv, a_op=None, groups=N_CHIPS, name=f"d_w_qkv_{l}")
        if l > 0:
            prev = saved[l - 1]
            dy2, dy2b, grads[f"ln2_g_{l - 1}"], grads[f"ln2_b_{l - 1}"] = _mm_nt_ln_bwd(
                dqkv, W[l]["qkv"], dy1, prev["y2"], given[f"ln2_g_{l - 1}"], name=f"d_qkv_in_ln2_bwd_{l - 1}"
            )
        else:
            grad_x = _mm_nt(dqkv, W[l]["qkv"], epi="residual", extra=dy1, out_dtype=F32, name=f"d_qkv_in_{l}")

    from_chips_first = _scatter_grads(grad_blocks(mat_names[:1]).astype(MXU), name="scatter_grads_w_qkv_0")
    me = 2 * lax.axis_index("x") + lax.axis_index("y")
    own = lax.dynamic_index_in_dim(grad_blocks(mat_names), me, 0, keepdims=False)
    from_chips = jnp.concatenate([from_chips_first, from_chips_rest], axis=1)
    part = _sum4(own, from_chips, name="sum_chips")
    sib = _swap_with_sibling(part, name="swap_sibling")
    pack = lambda prefix: _pack_rows([given[prefix + n] for n in mat_names], D)
    outs_m = _adamw(jnp.stack([part, sib]), pack(""), pack("m_"), pack("v_"), name="adamw_matrices")
    vflat = _pack_rows([grads[n] for n in vec_names], D)
    packv = lambda prefix: _pack_rows([given[prefix + n] for n in vec_names], D)
    outs_v = _adamw(_gather_from_all(vflat, name="gather_ln_grads"), packv(""), packv("m_"), packv("v_"), name="adamw_vectors")

    def unpack(flat_m, flat_v):
        out, r0 = {}, 0
        for n in mat_names:
            r = given[n].size // D
            out[n] = flat_m[r0 : r0 + r].reshape(given[n].shape)
            r0 += r
        for i, n in enumerate(vec_names):
            out[n] = flat_v[i]
        return [out[n] for n in names]

    result = [loss, grad_x[None]]
    for k in range(4):
        result += unpack(outs_m[k], outs_v[k])
    return tuple(result)
```
